```python
import math
import jax, jax.numpy as jnp
from jax import lax
import numpy as np

D_MODEL = 1024
BATCH = 8
SEQ = 4096
DEPTH = 4

CHUNK = 64
EPS = 1e-6
CONV_WIDTH = 3
CONV_DIM = D_MODEL // 2
RET_HEADS = 4
RET_HEAD_DIM = (D_MODEL // 2) // RET_HEADS
RET_DIM = RET_HEADS * RET_HEAD_DIM
ROPE_THETA = 10000.0
SB_HEADS = 8
SB_HEAD_DIM = D_MODEL // SB_HEADS
SB_DIM = SB_HEADS * SB_HEAD_DIM
SB_BLOCK = 128
FFN_MULT_OF = 256
D_FF = -(-8 * D_MODEL // (3 * FFN_MULT_OF)) * FFN_MULT_OF
EVEN_IN = 3 * CONV_DIM + 4 * RET_DIM
N_EVEN = (DEPTH + 1) // 2
N_ODD = DEPTH // 2

kernel_name = 'hybrid_conv_retention_stickbreaking_adaln_trunk'


def rms_norm(x, g):
    xf = x.astype(jnp.float32)
    y = xf * lax.rsqrt(jnp.mean(xf * xf, axis=-1, keepdims=True) + EPS)
    return (y * g.astype(jnp.float32)).astype(x.dtype)


def rotary(x, pos):
    dh = x.shape[-1]
    inv_freq = 1.0 / (ROPE_THETA ** (jnp.arange(0, dh, 2, dtype=jnp.float32) / dh))
    ang = pos.astype(jnp.float32)[:, None] * inv_freq[None, :]
    cos = jnp.cos(ang)[None, :, None, :].astype(x.dtype)
    sin = jnp.sin(ang)[None, :, None, :].astype(x.dtype)
    x1, x2 = jnp.split(x, 2, axis=-1)
    return jnp.concatenate([x1 * cos - x2 * sin, x1 * sin + x2 * cos], axis=-1)


def short_gated_conv(b_gate, c_gate, u, conv_w):
    z = c_gate * u
    w = conv_w[:, None, :].astype(z.dtype)
    y = lax.conv_general_dilated(z, w, window_strides=(1,), padding=[(CONV_WIDTH - 1, 0)],
                                 dimension_numbers=('NWC', 'WIO', 'NWC'),
                                 feature_group_count=CONV_DIM)
    return b_gate * y


def retention(q, k, v):
    bsz, s_len, h, dh = q.shape
    n = s_len // CHUNK
    dt = q.dtype
    log_g = jnp.log1p(-jnp.exp2(-5.0 - jnp.arange(h, dtype=jnp.float32)))
    idx = jnp.arange(CHUNK, dtype=jnp.float32)
    intra_dec = jnp.exp(jnp.abs(idx[:, None] - idx[None, :])[None] * log_g[:, None, None]).astype(dt)
    q_dec = jnp.exp((idx + 1.0)[None, :] * log_g[:, None]).astype(dt)
    k_dec = jnp.exp((CHUNK - 1.0 - idx)[None, :] * log_g[:, None]).astype(dt)
    chunk_dec = jnp.exp(CHUNK * log_g).astype(dt)
    qc = q.reshape(bsz, n, CHUNK, h, dh)
    kc = k.reshape(bsz, n, CHUNK, h, dh)
    vc = v.reshape(bsz, n, CHUNK, h, dh)
    scores = jnp.einsum('bnihd,bnjhd->bnhij', qc, kc) * intra_dec
    o_intra = jnp.einsum('bnhij,bnjhd->bnihd', scores, vc)
    kv = jnp.einsum('bnjhd,hj,bnjhe->bnhde', kc, k_dec, vc)

    def step(state, kv_i):
        return state * chunk_dec[None, :, None, None] + kv_i, state

    _, prev = lax.scan(step, jnp.zeros_like(kv[:, 0]), jnp.moveaxis(kv, 1, 0))
    prev = jnp.moveaxis(prev, 0, 1)
    o_inter = jnp.einsum('bnihd,hi,bnhde->bnihe', qc, q_dec, prev)
    return (o_intra + o_inter).reshape(bsz, s_len, h, dh)


def stick_breaking(q, k, v):
    s_len, dh = q.shape[2], q.shape[3]
    scale = dh ** -0.5
    outs = []
    for i in range(s_len // SB_BLOCK):
        q0 = i * SB_BLOCK
        kl = q0 + SB_BLOCK
        qb, kb, vb = q[:, :, q0:kl], k[:, :, :kl], v[:, :, :kl]
        z = jnp.einsum('bhqd,bhkd->bhqk', qb, kb).astype(jnp.float32) * scale
        qpos = q0 + jnp.arange(SB_BLOCK)
        kpos = jnp.arange(kl)
        mask = kpos[None, :] < qpos[:, None]
        log_beta = jax.nn.log_sigmoid(z)
        log_keep = jnp.where(mask, jax.nn.log_sigmoid(-z), 0.0)
        acc = lax.cumsum(log_keep, axis=3, reverse=True) - log_keep
        w = jnp.where(mask, jnp.exp(log_beta + acc), 0.0)
        outs.append(jnp.einsum('bhqk,bhkd->bhqd', w.astype(vb.dtype), vb))
    return jnp.concatenate(outs, axis=2)


def even_mixer(h, w_in, conv_w, ret_norm_g, w_out):
    bsz, s_len, _ = h.shape
    proj = h @ w_in
    cuts = [CONV_DIM, 2 * CONV_DIM, 3 * CONV_DIM, 3 * CONV_DIM + RET_DIM,
            3 * CONV_DIM + 2 * RET_DIM, 3 * CONV_DIM + 3 * RET_DIM]
    b_gate, c_gate, u, q, k, v, g = jnp.split(proj, cuts, axis=-1)
    a_out = short_gated_conv(b_gate, c_gate, u, conv_w)
    pos = jnp.arange(s_len)
    q = rotary(q.reshape(bsz, s_len, RET_HEADS, RET_HEAD_DIM), pos)
    k = rotary(k.reshape(bsz, s_len, RET_HEADS, RET_HEAD_DIM), pos) * (RET_HEAD_DIM ** -0.5)
    v = v.reshape(bsz, s_len, RET_HEADS, RET_HEAD_DIM)
    r = retention(q, k, v)
    r = rms_norm(r, ret_norm_g.reshape(RET_HEADS, RET_HEAD_DIM))
    r = jax.nn.silu(g) * r.reshape(bsz, s_len, RET_DIM)
    return jnp.concatenate([a_out, r], axis=-1) @ w_out


def odd_mixer(h, w_qkv, q_norm_g, k_norm_g, w_out):
    bsz, s_len, _ = h.shape
    q, k, v = jnp.split(h @ w_qkv, 3, axis=-1)
    q = rms_norm(q.reshape(bsz, s_len, SB_HEADS, SB_HEAD_DIM), q_norm_g)
    k = rms_norm(k.reshape(bsz, s_len, SB_HEADS, SB_HEAD_DIM), k_norm_g)
    v = v.reshape(bsz, s_len, SB_HEADS, SB_HEAD_DIM)
    o = stick_breaking(q.transpose(0, 2, 1, 3), k.transpose(0, 2, 1, 3), v.transpose(0, 2, 1, 3))
    return o.transpose(0, 2, 1, 3).reshape(bsz, s_len, SB_DIM) @ w_out


def swiglu(h, w_gate, w_up, w_down):
    return (jax.nn.silu(h @ w_gate) * (h @ w_up)) @ w_down


def _fwd_setup_inputs(seed: int = 0) -> dict:
    key = jax.random.key(seed)
    ks = jax.random.split(key, 17)

    def nrm(k, shape, fan_in, mult=1.0):
        return jax.random.normal(k, shape, jnp.float32) * (mult * fan_in ** -0.5)

    def gain(k, shape):
        return 1.0 + 0.02 * jax.random.normal(k, shape, jnp.float32)

    return {
        'x': jax.random.normal(ks[0], (BATCH, SEQ, D_MODEL), jnp.float32),
        'c': jax.random.normal(ks[1], (BATCH, D_MODEL), jnp.float32),
        'ada_w': nrm(ks[2], (DEPTH, D_MODEL, 6 * D_MODEL), D_MODEL, 0.5),
        'ada_b': 0.02 * jax.random.normal(ks[3], (DEPTH, 6 * D_MODEL), jnp.float32),
        'norm_mix_g': gain(ks[4], (DEPTH, D_MODEL)),
        'norm_ffn_g': gain(ks[5], (DEPTH, D_MODEL)),
        'ev_w_in': nrm(ks[6], (N_EVEN, D_MODEL, EVEN_IN), D_MODEL),
        'ev_conv_w': nrm(ks[7], (N_EVEN, CONV_WIDTH, CONV_DIM), CONV_WIDTH),
        'ev_ret_norm_g': gain(ks[8], (N_EVEN, RET_DIM)),
        'ev_w_out': nrm(ks[9], (N_EVEN, CONV_DIM + RET_DIM, D_MODEL), CONV_DIM + RET_DIM),
        'od_w_qkv': nrm(ks[10], (N_ODD, D_MODEL, 3 * SB_DIM), D_MODEL),
        'od_q_norm_g': gain(ks[11], (N_ODD, SB_HEAD_DIM)),
        'od_k_norm_g': gain(ks[12], (N_ODD, SB_HEAD_DIM)),
        'od_w_out': nrm(ks[13], (N_ODD, SB_DIM, D_MODEL), SB_DIM),
        'ffn_w_gate': nrm(ks[14], (DEPTH, D_MODEL, D_FF), D_MODEL),
        'ffn_w_up': nrm(ks[15], (DEPTH, D_MODEL, D_FF), D_MODEL),
        'ffn_w_down': nrm(ks[16], (DEPTH, D_FF, D_MODEL), D_FF),
    }


def _fwd_reference(x, c, ada_w, ada_b, norm_mix_g, norm_ffn_g, ev_w_in, ev_conv_w, ev_ret_norm_g,
              ev_w_out, od_w_qkv, od_q_norm_g, od_k_norm_g, od_w_out, ffn_w_gate, ffn_w_up,
              ffn_w_down):
    c_act = jax.nn.silu(c)
    for l in range(DEPTH):
        mod = c_act @ ada_w[l] + ada_b[l]
        sh1, sc1, g1, sh2, sc2, g2 = [m[:, None, :] for m in jnp.split(mod, 6, axis=-1)]
        h = rms_norm(x, norm_mix_g[l]) * (1 + sc1) + sh1
        j = l // 2
        if l % 2 == 0:
            y = even_mixer(h, ev_w_in[j], ev_conv_w[j], ev_ret_norm_g[j], ev_w_out[j])
        else:
            y = odd_mixer(h, od_w_qkv[j], od_q_norm_g[j], od_k_norm_g[j], od_w_out[j])
        x = x + g1 * y
        h = rms_norm(x, norm_ffn_g[l]) * (1 + sc2) + sh2
        x = x + g2 * swiglu(h, ffn_w_gate[l], ffn_w_up[l], ffn_w_down[l])
    return x


import jax as _jax
import jax.numpy as _jnp

TWIN_FORMAT = 'train_step'
FWD_PARAMS = ['x', 'c', 'ada_w', 'ada_b', 'norm_mix_g', 'norm_ffn_g', 'ev_w_in', 'ev_conv_w', 'ev_ret_norm_g', 'ev_w_out', 'od_w_qkv', 'od_q_norm_g', 'od_k_norm_g', 'od_w_out', 'ffn_w_gate', 'ffn_w_up', 'ffn_w_down']
TWIN_WEIGHTS = ['ada_w', 'ada_b', 'norm_mix_g', 'norm_ffn_g', 'ev_w_in', 'ev_conv_w', 'ev_ret_norm_g', 'ev_w_out', 'od_w_qkv', 'od_q_norm_g', 'od_k_norm_g', 'od_w_out', 'ffn_w_gate', 'ffn_w_up', 'ffn_w_down']
TWIN_DIFF_INPUT = 'x'
TWIN_INPUTS = ['x', 'c', 'ada_w', 'ada_b', 'norm_mix_g', 'norm_ffn_g', 'ev_w_in', 'ev_conv_w', 'ev_ret_norm_g', 'ev_w_out', 'od_w_qkv', 'od_q_norm_g', 'od_k_norm_g', 'od_w_out', 'ffn_w_gate', 'ffn_w_up', 'ffn_w_down', 'loss_target', 'm_ada_w', 'm_ada_b', 'm_norm_mix_g', 'm_norm_ffn_g', 'm_ev_w_in', 'm_ev_conv_w', 'm_ev_ret_norm_g', 'm_ev_w_out', 'm_od_w_qkv', 'm_od_q_norm_g', 'm_od_k_norm_g', 'm_od_w_out', 'm_ffn_w_gate', 'm_ffn_w_up', 'm_ffn_w_down', 'v_ada_w', 'v_ada_b', 'v_norm_mix_g', 'v_norm_ffn_g', 'v_ev_w_in', 'v_ev_conv_w', 'v_ev_ret_norm_g', 'v_ev_w_out', 'v_od_w_qkv', 'v_od_q_norm_g', 'v_od_k_norm_g', 'v_od_w_out', 'v_ffn_w_gate', 'v_ffn_w_up', 'v_ffn_w_down']
TWIN_OUTPUTS = ['loss', 'grad_x', 'grad_ada_w', 'grad_ada_b', 'grad_norm_mix_g', 'grad_norm_ffn_g', 'grad_ev_w_in', 'grad_ev_conv_w', 'grad_ev_ret_norm_g', 'grad_ev_w_out', 'grad_od_w_qkv', 'grad_od_q_norm_g', 'grad_od_k_norm_g', 'grad_od_w_out', 'grad_ffn_w_gate', 'grad_ffn_w_up', 'grad_ffn_w_down', 'delta_ada_w', 'delta_ada_b', 'delta_norm_mix_g', 'delta_norm_ffn_g', 'delta_ev_w_in', 'delta_ev_conv_w', 'delta_ev_ret_norm_g', 'delta_ev_w_out', 'delta_od_w_qkv', 'delta_od_q_norm_g', 'delta_od_k_norm_g', 'delta_od_w_out', 'delta_ffn_w_gate', 'delta_ffn_w_up', 'delta_ffn_w_down', 'new_m_ada_w', 'new_m_ada_b', 'new_m_norm_mix_g', 'new_m_norm_ffn_g', 'new_m_ev_w_in', 'new_m_ev_conv_w', 'new_m_ev_ret_norm_g', 'new_m_ev_w_out', 'new_m_od_w_qkv', 'new_m_od_q_norm_g', 'new_m_od_k_norm_g', 'new_m_od_w_out', 'new_m_ffn_w_gate', 'new_m_ffn_w_up', 'new_m_ffn_w_down', 'new_v_ada_w', 'new_v_ada_b', 'new_v_norm_mix_g', 'new_v_norm_ffn_g', 'new_v_ev_w_in', 'new_v_ev_conv_w', 'new_v_ev_ret_norm_g', 'new_v_ev_w_out', 'new_v_od_w_qkv', 'new_v_od_q_norm_g', 'new_v_od_k_norm_g', 'new_v_od_w_out', 'new_v_ffn_w_gate', 'new_v_ffn_w_up', 'new_v_ffn_w_down']
TWIN_LEAF_KINDS = {'loss': 'loss', 'grad_x': 'grad_x', 'grad_ada_w': 'grad_w', 'grad_ada_b': 'grad_w', 'grad_norm_mix_g': 'grad_w', 'grad_norm_ffn_g': 'grad_w', 'grad_ev_w_in': 'grad_w', 'grad_ev_conv_w': 'grad_w', 'grad_ev_ret_norm_g': 'grad_w', 'grad_ev_w_out': 'grad_w', 'grad_od_w_qkv': 'grad_w', 'grad_od_q_norm_g': 'grad_w', 'grad_od_k_norm_g': 'grad_w', 'grad_od_w_out': 'grad_w', 'grad_ffn_w_gate': 'grad_w', 'grad_ffn_w_up': 'grad_w', 'grad_ffn_w_down': 'grad_w', 'delta_ada_w': 'delta_w', 'delta_ada_b': 'delta_w', 'delta_norm_mix_g': 'delta_w', 'delta_norm_ffn_g': 'delta_w', 'delta_ev_w_in': 'delta_w', 'delta_ev_conv_w': 'delta_w', 'delta_ev_ret_norm_g': 'delta_w', 'delta_ev_w_out': 'delta_w', 'delta_od_w_qkv': 'delta_w', 'delta_od_q_norm_g': 'delta_w', 'delta_od_k_norm_g': 'delta_w', 'delta_od_w_out': 'delta_w', 'delta_ffn_w_gate': 'delta_w', 'delta_ffn_w_up': 'delta_w', 'delta_ffn_w_down': 'delta_w', 'new_m_ada_w': 'new_m', 'new_m_ada_b': 'new_m', 'new_m_norm_mix_g': 'new_m', 'new_m_norm_ffn_g': 'new_m', 'new_m_ev_w_in': 'new_m', 'new_m_ev_conv_w': 'new_m', 'new_m_ev_ret_norm_g': 'new_m', 'new_m_ev_w_out': 'new_m', 'new_m_od_w_qkv': 'new_m', 'new_m_od_q_norm_g': 'new_m', 'new_m_od_k_norm_g': 'new_m', 'new_m_od_w_out': 'new_m', 'new_m_ffn_w_gate': 'new_m', 'new_m_ffn_w_up': 'new_m', 'new_m_ffn_w_down': 'new_m', 'new_v_ada_w': 'new_v', 'new_v_ada_b': 'new_v', 'new_v_norm_mix_g': 'new_v', 'new_v_norm_ffn_g': 'new_v', 'new_v_ev_w_in': 'new_v', 'new_v_ev_conv_w': 'new_v', 'new_v_ev_ret_norm_g': 'new_v', 'new_v_ev_w_out': 'new_v', 'new_v_od_w_qkv': 'new_v', 'new_v_od_q_norm_g': 'new_v', 'new_v_od_k_norm_g': 'new_v', 'new_v_od_w_out': 'new_v', 'new_v_ffn_w_gate': 'new_v', 'new_v_ffn_w_up': 'new_v', 'new_v_ffn_w_down': 'new_v'}


def _forward(args):
    return _fwd_reference(*[args[k] for k in FWD_PARAMS])


def _output_shape():
    def fwd():
        inp = _fwd_setup_inputs(0)
        return _fwd_reference(*[inp[k] for k in FWD_PARAMS])
    out = _jax.eval_shape(fwd)
    return out.shape, out.dtype

N_MICROBATCH = 1
ADAM_LR = 0.001
ADAM_B1 = 0.9
ADAM_B2 = 0.999
ADAM_EPS = 1e-08
ADAM_WD = 0.01
ADAM_STEP = 10
PER_EXAMPLE_BATCH_AXIS = {'x': 0, 'c': 0, 'loss_target': 0}
SHARED_INPUTS = []
_WEIGHT_DTYPES = {'ada_w': _jnp.float32, 'ada_b': _jnp.float32, 'norm_mix_g': _jnp.float32, 'norm_ffn_g': _jnp.float32, 'ev_w_in': _jnp.float32, 'ev_conv_w': _jnp.float32, 'ev_ret_norm_g': _jnp.float32, 'ev_w_out': _jnp.float32, 'od_w_qkv': _jnp.float32, 'od_q_norm_g': _jnp.float32, 'od_k_norm_g': _jnp.float32, 'od_w_out': _jnp.float32, 'ffn_w_gate': _jnp.float32, 'ffn_w_up': _jnp.float32, 'ffn_w_down': _jnp.float32}
MOMENT_SCALE = {'ada_w': 1.136431e+00, 'ada_b': 3.042388e+00, 'norm_mix_g': 5.472522e+00, 'norm_ffn_g': 3.165437e+00, 'ev_w_in': 1.947249e-01, 'ev_conv_w': 2.792222e+00, 'ev_ret_norm_g': 1.295459e+00, 'ev_w_out': 1.653372e-01, 'od_w_qkv': 1.603268e-01, 'od_q_norm_g': 1.554087e+00, 'od_k_norm_g': 1.549153e+00, 'od_w_out': 2.266339e-01, 'ffn_w_gate': 7.221252e-02, 'ffn_w_up': 6.029230e-02, 'ffn_w_down': 9.664996e-02}


def _to_microbatches(a, axis):
    t = _jnp.moveaxis(a, axis, 0)
    t = t.reshape((N_MICROBATCH, t.shape[0] // N_MICROBATCH) + t.shape[1:])
    return _jnp.moveaxis(t, 1, axis + 1)


def setup_inputs(seed: int = 0) -> dict:
    inp = _fwd_setup_inputs(seed)
    key = _jax.random.fold_in(_jax.random.key(seed), 7919)
    shape, _ = _output_shape()
    out = dict(inp)
    out["loss_target"] = _jax.random.normal(_jax.random.fold_in(key, 0), shape, _jnp.float32)
    for i, name in enumerate(TWIN_WEIGHTS):
        w = inp[name].astype(_jnp.float32)
        if MOMENT_SCALE is None:
            s = _jnp.sqrt(_jnp.mean(_jnp.square(w)) + 1e-30)
        else:
            s = MOMENT_SCALE[name]
        km, kv = _jax.random.split(_jax.random.fold_in(key, i + 1))
        out[name] = w
        out["m_" + name] = s * _jax.random.normal(km, w.shape, _jnp.float32)
        out["v_" + name] = (s * s) * _jax.random.uniform(kv, w.shape, _jnp.float32, 0.5, 1.5)
    if N_MICROBATCH > 1:
        for name, axis in PER_EXAMPLE_BATCH_AXIS.items():
            out[name] = _to_microbatches(out[name], axis)
    return {'x': out['x'], 'c': out['c'], 'ada_w': out['ada_w'], 'ada_b': out['ada_b'], 'norm_mix_g': out['norm_mix_g'], 'norm_ffn_g': out['norm_ffn_g'], 'ev_w_in': out['ev_w_in'], 'ev_conv_w': out['ev_conv_w'], 'ev_ret_norm_g': out['ev_ret_norm_g'], 'ev_w_out': out['ev_w_out'], 'od_w_qkv': out['od_w_qkv'], 'od_q_norm_g': out['od_q_norm_g'], 'od_k_norm_g': out['od_k_norm_g'], 'od_w_out': out['od_w_out'], 'ffn_w_gate': out['ffn_w_gate'], 'ffn_w_up': out['ffn_w_up'], 'ffn_w_down': out['ffn_w_down'], 'loss_target': out['loss_target'], 'm_ada_w': out['m_ada_w'], 'm_ada_b': out['m_ada_b'], 'm_norm_mix_g': out['m_norm_mix_g'], 'm_norm_ffn_g': out['m_norm_ffn_g'], 'm_ev_w_in': out['m_ev_w_in'], 'm_ev_conv_w': out['m_ev_conv_w'], 'm_ev_ret_norm_g': out['m_ev_ret_norm_g'], 'm_ev_w_out': out['m_ev_w_out'], 'm_od_w_qkv': out['m_od_w_qkv'], 'm_od_q_norm_g': out['m_od_q_norm_g'], 'm_od_k_norm_g': out['m_od_k_norm_g'], 'm_od_w_out': out['m_od_w_out'], 'm_ffn_w_gate': out['m_ffn_w_gate'], 'm_ffn_w_up': out['m_ffn_w_up'], 'm_ffn_w_down': out['m_ffn_w_down'], 'v_ada_w': out['v_ada_w'], 'v_ada_b': out['v_ada_b'], 'v_norm_mix_g': out['v_norm_mix_g'], 'v_norm_ffn_g': out['v_norm_ffn_g'], 'v_ev_w_in': out['v_ev_w_in'], 'v_ev_conv_w': out['v_ev_conv_w'], 'v_ev_ret_norm_g': out['v_ev_ret_norm_g'], 'v_ev_w_out': out['v_ev_w_out'], 'v_od_w_qkv': out['v_od_w_qkv'], 'v_od_q_norm_g': out['v_od_q_norm_g'], 'v_od_k_norm_g': out['v_od_k_norm_g'], 'v_od_w_out': out['v_od_w_out'], 'v_ffn_w_gate': out['v_ffn_w_gate'], 'v_ffn_w_up': out['v_ffn_w_up'], 'v_ffn_w_down': out['v_ffn_w_down']}


def _loss(weights, diff, rest, loss_target):
    with _jax.named_scope("forward"):
        args = {**rest, TWIN_DIFF_INPUT: diff, **{k: w.astype(_WEIGHT_DTYPES[k]) for k, w in weights.items()}}
        y = _forward(args)
    with _jax.named_scope("loss_head"):
        err = _jnp.square(y.astype(_jnp.float32) - loss_target)
        return 0.5 * _jnp.sum(_jnp.mean(err, axis=-1)) if err.ndim else 0.5 * err


def _adamw(w, g, m, v):
    m = ADAM_B1 * m + (1.0 - ADAM_B1) * g
    v = ADAM_B2 * v + (1.0 - ADAM_B2) * _jnp.square(g)
    m_hat = m / (1.0 - ADAM_B1 ** ADAM_STEP)
    v_hat = v / (1.0 - ADAM_B2 ** ADAM_STEP)
    delta = -ADAM_LR * (m_hat / (_jnp.sqrt(v_hat) + ADAM_EPS) + ADAM_WD * w)
    return delta, m, v


def reference(x, c, ada_w, ada_b, norm_mix_g, norm_ffn_g, ev_w_in, ev_conv_w, ev_ret_norm_g, ev_w_out, od_w_qkv, od_q_norm_g, od_k_norm_g, od_w_out, ffn_w_gate, ffn_w_up, ffn_w_down, loss_target, m_ada_w, m_ada_b, m_norm_mix_g, m_norm_ffn_g, m_ev_w_in, m_ev_conv_w, m_ev_ret_norm_g, m_ev_w_out, m_od_w_qkv, m_od_q_norm_g, m_od_k_norm_g, m_od_w_out, m_ffn_w_gate, m_ffn_w_up, m_ffn_w_down, v_ada_w, v_ada_b, v_norm_mix_g, v_norm_ffn_g, v_ev_w_in, v_ev_conv_w, v_ev_ret_norm_g, v_ev_w_out, v_od_w_qkv, v_od_q_norm_g, v_od_k_norm_g, v_od_w_out, v_ffn_w_gate, v_ffn_w_up, v_ffn_w_down):
    given = dict(x=x, c=c, ada_w=ada_w, ada_b=ada_b, norm_mix_g=norm_mix_g, norm_ffn_g=norm_ffn_g, ev_w_in=ev_w_in, ev_conv_w=ev_conv_w, ev_ret_norm_g=ev_ret_norm_g, ev_w_out=ev_w_out, od_w_qkv=od_w_qkv, od_q_norm_g=od_q_norm_g, od_k_norm_g=od_k_norm_g, od_w_out=od_w_out, ffn_w_gate=ffn_w_gate, ffn_w_up=ffn_w_up, ffn_w_down=ffn_w_down, loss_target=loss_target, m_ada_w=m_ada_w, m_ada_b=m_ada_b, m_norm_mix_g=m_norm_mix_g, m_norm_ffn_g=m_norm_ffn_g, m_ev_w_in=m_ev_w_in, m_ev_conv_w=m_ev_conv_w, m_ev_ret_norm_g=m_ev_ret_norm_g, m_ev_w_out=m_ev_w_out, m_od_w_qkv=m_od_w_qkv, m_od_q_norm_g=m_od_q_norm_g, m_od_k_norm_g=m_od_k_norm_g, m_od_w_out=m_od_w_out, m_ffn_w_gate=m_ffn_w_gate, m_ffn_w_up=m_ffn_w_up, m_ffn_w_down=m_ffn_w_down, v_ada_w=v_ada_w, v_ada_b=v_ada_b, v_norm_mix_g=v_norm_mix_g, v_norm_ffn_g=v_norm_ffn_g, v_ev_w_in=v_ev_w_in, v_ev_conv_w=v_ev_conv_w, v_ev_ret_norm_g=v_ev_ret_norm_g, v_ev_w_out=v_ev_w_out, v_od_w_qkv=v_od_w_qkv, v_od_q_norm_g=v_od_q_norm_g, v_od_k_norm_g=v_od_k_norm_g, v_od_w_out=v_od_w_out, v_ffn_w_gate=v_ffn_w_gate, v_ffn_w_up=v_ffn_w_up, v_ffn_w_down=v_ffn_w_down)
    weights = {n: given[n] for n in TWIN_WEIGHTS}
    shared = {n: given[n] for n in SHARED_INPUTS}
    per_example = {n: given[n] for n in ['x', 'c']}
    grad_fn = _jax.value_and_grad(_loss, argnums=(0, 1))

    def one_microbatch(ex, loss_target):
        ex = dict(ex)
        diff = ex.pop(TWIN_DIFF_INPUT)
        return grad_fn(weights, diff, {**shared, **ex}, loss_target)

    if N_MICROBATCH == 1:
        loss, (grad_w, grad_x) = one_microbatch(per_example, given["loss_target"])
    else:
        def body(carry, xs):
            loss_sum, grad_sum = carry
            l_k, (gw_k, gx_k) = one_microbatch(xs[0], xs[1])
            with _jax.named_scope("update"):
                return (loss_sum + l_k, _jax.tree.map(_jnp.add, grad_sum, gw_k)), gx_k

        init = (_jnp.zeros((), _jnp.float32), _jax.tree.map(_jnp.zeros_like, weights))
        (loss, grad_w), grad_x = _jax.lax.scan(body, init, (per_example, given["loss_target"]))
    with _jax.named_scope("update"):
        delta_w, new_m, new_v = {}, {}, {}
        for n in TWIN_WEIGHTS:
            delta_w[n], new_m[n], new_v[n] = _adamw(weights[n], grad_w[n], given["m_" + n], given["v_" + n])
    return (loss, grad_x, *[grad_w[n] for n in TWIN_WEIGHTS], *[delta_w[n] for n in TWIN_WEIGHTS],
            *[new_m[n] for n in TWIN_WEIGHTS], *[new_v[n] for n in TWIN_WEIGHTS])
```

```python
import functools

import numpy as np
import jax
import jax.numpy as jnp
from jax import lax
from jax.experimental import pallas as pl
from jax.experimental.pallas import tpu as pltpu

F32 = jnp.float32
BF16 = jnp.bfloat16
MESH = pl.DeviceIdType.MESH

D_MODEL = 1024
DEPTH = 4
N_CHIP = 4
N_DEV = 8
HEAD = 128
RET_HEADS = 4
SB_HEADS = 8
CONV_DIM = 512
RET_DIM = 512
CONV_WIDTH = 3
RET_CHUNK = 64
RET_BLOCK = 256
SB_BLOCK = 128
EPS = 1e-6
ROPE_THETA = 10000.0
QK_SCALE = HEAD ** -0.5
LANE = 128
ROW_CHUNK = 512
VMEM_LIMIT_V7X = 56 * 1024 * 1024

ADAM_LR, ADAM_B1, ADAM_B2, ADAM_EPS, ADAM_WD, ADAM_STEP = 0.001, 0.9, 0.999, 1e-08, 0.01, 10


def _cparams(*sem):
    return pltpu.CompilerParams(dimension_semantics=sem or None, vmem_limit_bytes=VMEM_LIMIT_V7X)


def _tile(n, pref):
    if n <= pref:
        return n
    for t in range(pref - pref % 8, 7, -8):
        if n % t == 0:
            return t
    return n


def _sig(v):
    return 1.0 / (1.0 + jnp.exp(-v))


def _dot(a, b, ca, cb):
    return lax.dot_general(a.astype(BF16), b.astype(BF16), (((ca,), (cb,)), ((), ())),
                           preferred_element_type=F32)


def _position():
    x, y, c = lax.axis_index("x"), lax.axis_index("y"), lax.axis_index("c")
    return x, y, c


def _other_chips(x, y):
    return [(1 - x, y), (x, 1 - y), (1 - x, 1 - y)]


def _mm(name, pairs, out_sds, out_specs, grid, contract, red_axis=None, post=None,
        extras=(), extra_specs=(), sum_pairs=True, aliases=None):
    n_p, n_ex, n_out = len(pairs), len(extras), len(out_sds)
    n_acc = 1 if sum_pairs else n_p
    n_red = grid[red_axis] if red_axis is not None else 1

    def default_post(accs, ex, outs):
        outs[0][...] = accs[0].astype(outs[0].dtype)

    post_fn = post or default_post

    def kern(*refs):
        ab = refs[:2 * n_p]
        ex = refs[2 * n_p:2 * n_p + n_ex]
        outs = refs[2 * n_p + n_ex:2 * n_p + n_ex + n_out]
        accs = refs[2 * n_p + n_ex + n_out:]
        prods = [_dot(ab[2 * p][...], ab[2 * p + 1][...], contract[0], contract[1]) for p in range(n_p)]
        if sum_pairs:
            tot = prods[0]
            for p_ in prods[1:]:
                tot = tot + p_
            prods = [tot]
        if red_axis is None:
            post_fn(prods, ex, outs)
        else:
            k = pl.program_id(red_axis)

            @pl.when(k == 0)
            def _():
                for a_, p_ in zip(accs, prods):
                    a_[...] = p_

            @pl.when(k > 0)
            def _():
                for a_, p_ in zip(accs, prods):
                    a_[...] += p_

            @pl.when(k == n_red - 1)
            def _():
                post_fn([a_[...] for a_ in accs], ex, outs)

    ins, in_specs = [], []
    for a, b, sa, sb in pairs:
        ins += [a, b]
        in_specs += [sa, sb]
    ins += list(extras)
    in_specs += list(extra_specs)
    scratch = []
    if red_axis is not None:
        scratch = [pltpu.VMEM(tuple(acc_shape), F32) for acc_shape in [_acc_shape(pairs[0], contract)] * n_acc]
    sem = tuple("arbitrary" if ax == red_axis else "parallel" for ax in range(len(grid)))
    res = pl.pallas_call(kern, name=name, grid=grid, in_specs=in_specs, out_specs=list(out_specs),
                         out_shape=list(out_sds), scratch_shapes=scratch,
                         input_output_aliases=aliases or {}, compiler_params=_cparams(*sem))(*ins)
    return res


def _acc_shape(pair, contract):
    sa, sb = pair[2], pair[3]
    da = [d for d in sa.block_shape if d is not None]
    db = [d for d in sb.block_shape if d is not None]
    return (da[1 - contract[0]], db[1 - contract[1]])


def _sds(shape, dtype):
    return jax.ShapeDtypeStruct(tuple(shape), dtype)


def _proj_cols(name, h, w4, l):
    T, K = h.shape
    n = w4.shape[-1]
    tm = _tile(T, 512)
    return _mm(name, [(h, w4, pl.BlockSpec((tm, K), lambda i, m: (m, 0)),
                       pl.BlockSpec((None, None, K, n), lambda i, m: (i, l, 0, 0)))],
               [_sds((T, N_CHIP * n), F32)], [pl.BlockSpec((tm, n), lambda i, m: (m, i))],
               (N_CHIP, T // tm), (1, 0))[0]


def _out_proj(name, a, a_stacked, w4, l, xres, mod, lm, kg):
    T = xres.shape[0]
    k = w4.shape[-2]
    tm = _tile(T, 512)
    if a_stacked:
        sa = pl.BlockSpec((None, tm, k), lambda m, i: (i, m, 0))
    else:
        sa = pl.BlockSpec((tm, k), lambda m, i: (m, i))

    def post(accs, ex, outs):
        y = accs[0]
        outs[0][...] = y
        outs[1][...] = ex[0][...] + ex[1][...] * y

    row = pl.BlockSpec((tm, D_MODEL), lambda m, i: (m, 0))
    return _mm(name, [(a, w4, sa, pl.BlockSpec((None, None, k, D_MODEL), lambda m, i: (i, l, 0, 0)))],
               [_sds((T, D_MODEL), F32)] * 2, [row, row], (T // tm, N_CHIP), (1, 0), red_axis=1, post=post,
               extras=[xres, mod], extra_specs=[row, pl.BlockSpec((None, None, 1, D_MODEL), lambda m, i: (lm, kg, 0, 0))])


def _ffn_up(name, h2, wg4, wu4, l):
    T, K = h2.shape
    n = wg4.shape[-1]
    tm = _tile(T, 512)

    def post(accs, ex, outs):
        g, u = accs
        outs[0][...] = g
        outs[1][...] = u
        outs[2][...] = (g * _sig(g) * u).astype(BF16)

    sa = pl.BlockSpec((tm, K), lambda i, m: (m, 0))
    sw = pl.BlockSpec((None, None, K, n), lambda i, m: (i, l, 0, 0))
    so = pl.BlockSpec((None, tm, n), lambda i, m: (i, m, 0))
    return _mm(name, [(h2, wg4, sa, sw), (h2, wu4, sa, sw)],
               [_sds((N_CHIP, T, n), F32), _sds((N_CHIP, T, n), F32), _sds((N_CHIP, T, n), BF16)], [so, so, so],
               (N_CHIP, T // tm), (1, 0), post=post, sum_pairs=False)


def _bwd_cols(name, dproj, w4, l):
    T = dproj.shape[0]
    K, n = w4.shape[-2:]
    tm = _tile(T, 512)
    return _mm(name, [(dproj, w4, pl.BlockSpec((tm, n), lambda m, i: (m, i)),
                       pl.BlockSpec((None, None, K, n), lambda m, i: (i, l, 0, 0)))],
               [_sds((T, K), F32)], [pl.BlockSpec((tm, K), lambda m, i: (m, 0))],
               (T // tm, N_CHIP), (1, 1), red_axis=1)[0]


def _bwd_rows(name, dy, w4, l):
    T, N = dy.shape
    k = w4.shape[-2]
    tm = _tile(T, 512)
    return _mm(name, [(dy, w4, pl.BlockSpec((tm, N), lambda i, m: (m, 0)),
                       pl.BlockSpec((None, None, k, N), lambda i, m: (i, l, 0, 0)))],
               [_sds((T, N_CHIP * k), F32)], [pl.BlockSpec((tm, k), lambda i, m: (m, i))],
               (N_CHIP, T // tm), (1, 1))[0]


def _ffn_down_bwd(name, dy, wd4, l, gate, up):
    T, N = dy.shape
    k = wd4.shape[-2]
    tm = _tile(T, 512)

    def post(accs, ex, outs):
        da = accs[0]
        g = ex[0][...]
        u = ex[1][...]
        sg = _sig(g)
        outs[0][...] = (da * u * (sg * (1.0 + g * (1.0 - sg)))).astype(BF16)
        outs[1][...] = (da * (g * sg)).astype(BF16)

    so = pl.BlockSpec((None, tm, k), lambda i, m: (i, m, 0))
    return _mm(name, [(dy, wd4, pl.BlockSpec((tm, N), lambda i, m: (m, 0)),
                       pl.BlockSpec((None, None, k, N), lambda i, m: (i, l, 0, 0)))],
               [_sds((N_CHIP, T, k), BF16)] * 2, [so, so], (N_CHIP, T // tm), (1, 1), post=post,
               extras=[gate, up], extra_specs=[so, so])


def _ffn_up_bwd(name, dgate, dup, wg4, wu4, l):
    _, T, n = dgate.shape
    K = wg4.shape[-2]
    tm = _tile(T, 512)
    sa = pl.BlockSpec((None, tm, n), lambda m, i: (i, m, 0))
    sw = pl.BlockSpec((None, None, K, n), lambda m, i: (i, l, 0, 0))
    return _mm(name, [(dgate, wg4, sa, sw), (dup, wu4, sa, sw)],
               [_sds((T, K), F32)], [pl.BlockSpec((tm, K), lambda m, i: (m, 0))],
               (T // tm, N_CHIP), (1, 1), red_axis=1)[0]


def _wgrad(name, a, a_kind, b, b_kind, gbuf, l):
    r, cdim = gbuf.shape[-2:]
    T = a.shape[-2]
    tt = _tile(T, 512)

    def spec(kind, w):
        if kind == "full":
            return pl.BlockSpec((tt, w), lambda i, t: (t, 0))
        if kind == "cols":
            return pl.BlockSpec((tt, w), lambda i, t: (t, i))
        return pl.BlockSpec((None, tt, w), lambda i, t: (i, t, 0))

    def post(accs, ex, outs):
        outs[0][...] = accs[0]

    return _mm(name, [(a, b, spec(a_kind, r), spec(b_kind, cdim))], [_sds(gbuf.shape, F32)],
               [pl.BlockSpec((None, None, r, cdim), lambda i, t: (i, l, 0, 0))], (N_CHIP, T // tt), (0, 0),
               red_axis=1, post=post, extras=[gbuf], extra_specs=[pl.BlockSpec(memory_space=pl.ANY)],
               aliases={2: 0})[0]


def _vec_spec(*idx):
    return pl.BlockSpec((None,) * len(idx) + (1, D_MODEL), lambda m: tuple(idx) + (0, 0))


def _normmod(name, x, gain3, l, mod, ksc, ksh):
    T = x.shape[0]
    tm = _tile(T, 512)

    def kern(x_ref, g_ref, sc_ref, sh_ref, h_ref):
        xv = x_ref[...]
        r = lax.rsqrt(jnp.mean(xv * xv, axis=-1, keepdims=True) + EPS)
        h = (xv * r) * g_ref[...]
        h_ref[...] = (h * (1.0 + sc_ref[...]) + sh_ref[...]).astype(BF16)

    row = pl.BlockSpec((tm, D_MODEL), lambda m: (m, 0))
    return pl.pallas_call(kern, name=name, grid=(T // tm,),
                          in_specs=[row, _vec_spec(l), _vec_spec(l, ksc), _vec_spec(l, ksh)], out_specs=row,
                          out_shape=_sds((T, D_MODEL), BF16), compiler_params=_cparams("parallel"))(x, gain3, mod, mod)


def _normmod_bwd(name, x, dh, dres, gain3, l, mod, ksc):
    T = x.shape[0]
    tm = _tile(T, 512)
    nt = T // tm

    def kern(x_ref, dh_ref, dres_ref, g_ref, sc_ref, dx_ref, st_ref):
        m = pl.program_id(0)

        @pl.when(m == 0)
        def _():
            st_ref[...] = jnp.zeros_like(st_ref)

        xv = x_ref[...]
        dhv = dh_ref[...]
        r = lax.rsqrt(jnp.mean(xv * xv, axis=-1, keepdims=True) + EPS)
        xh = xv * r
        wv = g_ref[...] * (1.0 + sc_ref[...])
        dxh = dhv * wv
        dx_ref[...] = dres_ref[...] + r * (dxh - xh * jnp.mean(dxh * xh, axis=-1, keepdims=True))
        st_ref[0:1, :] += jnp.sum(dhv, axis=0, keepdims=True)
        st_ref[1:2, :] += jnp.sum(dhv * xh, axis=0, keepdims=True)

        @pl.when(m == nt - 1)
        def _():
            dw = st_ref[1:2, :]
            st_ref[2:3, :] = dw * (1.0 + sc_ref[...])
            st_ref[1:2, :] = dw * g_ref[...]

    row = pl.BlockSpec((tm, D_MODEL), lambda m: (m, 0))
    return pl.pallas_call(kern, name=name, grid=(nt,),
                          in_specs=[row, row, row, _vec_spec(l), _vec_spec(l, ksc)],
                          out_specs=[row, pl.BlockSpec((8, D_MODEL), lambda m: (0, 0))],
                          out_shape=[_sds((T, D_MODEL), F32), _sds((8, D_MODEL), F32)],
                          compiler_params=_cparams("arbitrary"))(x, dh, dres, gain3, mod)


def _gate_bwd(name, dxn, y, mod, l, kg):
    T = dxn.shape[0]
    tm = _tile(T, 512)

    def kern(d_ref, y_ref, g_ref, dy_ref, st_ref):
        @pl.when(pl.program_id(0) == 0)
        def _():
            st_ref[...] = jnp.zeros_like(st_ref)

        dv = d_ref[...]
        dy_ref[...] = (dv * g_ref[...]).astype(BF16)
        st_ref[0:1, :] += jnp.sum(dv * y_ref[...], axis=0, keepdims=True)

    row = pl.BlockSpec((tm, D_MODEL), lambda m: (m, 0))
    return pl.pallas_call(kern, name=name, grid=(T // tm,), in_specs=[row, row, _vec_spec(l, kg)],
                          out_specs=[row, pl.BlockSpec((8, D_MODEL), lambda m: (0, 0))],
                          out_shape=[_sds((T, D_MODEL), BF16), _sds((8, D_MODEL), F32)],
                          compiler_params=_cparams("arbitrary"))(dxn, y, mod)


def _loss_head(y, target):
    T = y.shape[0]
    tm = _tile(T, 512)

    def kern(y_ref, t_ref, dy_ref, acc_ref):
        @pl.when(pl.program_id(0) == 0)
        def _():
            acc_ref[...] = jnp.zeros_like(acc_ref)

        e = y_ref[...] - t_ref[...]
        dy_ref[...] = e * (1.0 / D_MODEL)
        s = jnp.sum(jnp.sum(e * e, axis=-1, keepdims=True), axis=0, keepdims=True)
        acc_ref[...] += s * (0.5 / D_MODEL)

    row = pl.BlockSpec((tm, D_MODEL), lambda m: (m, 0))
    return pl.pallas_call(kern, name="loss_head", grid=(T // tm,), in_specs=[row, row],
                          out_specs=[row, pl.BlockSpec((8, LANE), lambda m: (0, 0))],
                          out_shape=[_sds((T, D_MODEL), F32), _sds((8, LANE), F32)],
                          compiler_params=_cparams("arbitrary"))(y, target)


def _row_chunks(T):
    rc = min(ROW_CHUNK, T)
    return [(r * rc, rc) for r in range(T // rc)]


def _conv_fwd(name, proj, conv_w):
    T = proj.shape[0]
    nblk = CONV_DIM // LANE

    def kern(b_ref, c_ref, u_ref, w_ref, a_ref, zs):
        zs[0:8, :] = jnp.zeros((8, LANE), F32)
        for r0, rc in _row_chunks(T):
            zs[8 + r0:8 + r0 + rc, :] = c_ref[r0:r0 + rc, :] * u_ref[r0:r0 + rc, :]
        w0, w1, w2 = w_ref[0:1, :], w_ref[1:2, :], w_ref[2:3, :]
        for r0, rc in _row_chunks(T):
            yc = w2 * zs[8 + r0:8 + r0 + rc, :] + w1 * zs[7 + r0:7 + r0 + rc, :] + w0 * zs[6 + r0:6 + r0 + rc, :]
            a_ref[r0:r0 + rc, :] = (b_ref[r0:r0 + rc, :] * yc).astype(BF16)

    col = lambda p: pl.BlockSpec((T, LANE), lambda cb: (0, p * nblk + cb))
    return pl.pallas_call(kern, name=name, grid=(nblk,),
                          in_specs=[col(0), col(1), col(2), pl.BlockSpec((CONV_WIDTH, LANE), lambda cb: (0, cb))],
                          out_specs=pl.BlockSpec((T, LANE), lambda cb: (0, cb)),
                          out_shape=_sds((T, CONV_DIM), BF16), scratch_shapes=[pltpu.VMEM((T + 8, LANE), F32)],
                          compiler_params=_cparams("parallel"))(proj, proj, proj, conv_w)


def _conv_bwd(name, proj, conv_w, dcat):
    T = proj.shape[0]
    nblk = CONV_DIM // LANE

    def kern(b_ref, c_ref, u_ref, w_ref, da_ref, db_ref, dc_ref, du_ref, dw_ref, zs, ds):
        zs[0:8, :] = jnp.zeros((8, LANE), F32)
        ds[T:T + 8, :] = jnp.zeros((8, LANE), F32)
        for r0, rc in _row_chunks(T):
            zs[8 + r0:8 + r0 + rc, :] = c_ref[r0:r0 + rc, :] * u_ref[r0:r0 + rc, :]
        w0, w1, w2 = w_ref[0:1, :], w_ref[1:2, :], w_ref[2:3, :]
        acc = [jnp.zeros((1, LANE), F32) for _ in range(3)]
        for r0, rc in _row_chunks(T):
            z0 = zs[8 + r0:8 + r0 + rc, :]
            z1 = zs[7 + r0:7 + r0 + rc, :]
            z2 = zs[6 + r0:6 + r0 + rc, :]
            da = da_ref[r0:r0 + rc, :]
            db_ref[r0:r0 + rc, :] = (da * (w2 * z0 + w1 * z1 + w0 * z2)).astype(BF16)
            dyc = da * b_ref[r0:r0 + rc, :]
            ds[r0:r0 + rc, :] = dyc
            acc[2] = acc[2] + jnp.sum(dyc * z0, axis=0, keepdims=True)
            acc[1] = acc[1] + jnp.sum(dyc * z1, axis=0, keepdims=True)
            acc[0] = acc[0] + jnp.sum(dyc * z2, axis=0, keepdims=True)
        dw_ref[...] = jnp.zeros_like(dw_ref)
        for k in range(3):
            dw_ref[k:k + 1, :] = acc[k]
        for r0, rc in _row_chunks(T):
            dz = w2 * ds[r0:r0 + rc, :] + w1 * ds[r0 + 1:r0 + 1 + rc, :] + w0 * ds[r0 + 2:r0 + 2 + rc, :]
            dc_ref[r0:r0 + rc, :] = (dz * u_ref[r0:r0 + rc, :]).astype(BF16)
            du_ref[r0:r0 + rc, :] = (dz * c_ref[r0:r0 + rc, :]).astype(BF16)

    col = lambda p: pl.BlockSpec((T, LANE), lambda cb: (0, p * nblk + cb))
    out = pl.BlockSpec((T, LANE), lambda cb: (0, cb))
    return pl.pallas_call(kern, name=name, grid=(nblk,),
                          in_specs=[col(0), col(1), col(2), pl.BlockSpec((CONV_WIDTH, LANE), lambda cb: (0, cb)), out],
                          out_specs=[out, out, out, pl.BlockSpec((8, LANE), lambda cb: (0, cb))],
                          out_shape=[_sds((T, CONV_DIM), BF16)] * 3 + [_sds((8, CONV_DIM), F32)],
                          scratch_shapes=[pltpu.VMEM((T + 8, LANE), F32), pltpu.VMEM((T + 8, LANE), F32)],
                          compiler_params=_cparams("parallel"))(proj, proj, proj, conv_w, dcat)


_Q0, _K0, _V0, _G0 = 3 * CONV_DIM // LANE, (3 * CONV_DIM + RET_DIM) // LANE, (3 * CONV_DIM + 2 * RET_DIM) // LANE, \
    (3 * CONV_DIM + 3 * RET_DIM) // LANE


def _ret_tables(B, lg1):
    ti = lax.broadcasted_iota(jnp.int32, (B, B), 0)
    si = lax.broadcasted_iota(jnp.int32, (B, B), 1)
    dist = jnp.abs(ti - si).astype(F32)
    shift = RET_CHUNK.bit_length() - 1
    dmat = jnp.where((si >> shift) <= (ti >> shift), jnp.exp(dist * lg1), 0.0)
    tcol = lax.broadcasted_iota(jnp.int32, (B, 1), 0).astype(F32)
    qdec = jnp.exp((tcol + 1.0) * lg1)
    kdec = jnp.exp((B - 1.0 - tcol) * lg1)
    bdec = jnp.exp(float(B) * lg1)
    return dmat, qdec, kdec, bdec


def _retention_fwd(name, proj, cosf, sinf, gr, lgt):
    T = proj.shape[0]
    B = min(RET_BLOCK, T)
    nb = T // B

    def kern(q_ref, k_ref, v_ref, g_ref, cos_ref, sin_ref, gr_ref, lg_ref, r_ref, o_ref, st_ref, S):
        @pl.when(pl.program_id(1) == 0)
        def _():
            S[...] = jnp.zeros_like(S)

        cosv, sinv = cos_ref[...], sin_ref[...]
        rot = lambda a: a * cosv + pltpu.roll(a, HEAD // 2, 1) * sinv
        qr = rot(q_ref[...])
        kr = rot(k_ref[...]) * QK_SCALE
        v = v_ref[...]
        dmat, qdec, kdec, bdec = _ret_tables(B, lg_ref[0:1, 0:1])
        sv = S[...]
        st_ref[...] = sv
        pd = _dot(qr, kr, 1, 1) * dmat
        o = _dot(pd, v, 1, 0) + _dot(qr * qdec, sv, 1, 0)
        S[...] = bdec * sv + _dot(kr * kdec, v, 0, 0)
        o_ref[...] = o
        rs = lax.rsqrt(jnp.mean(o * o, axis=-1, keepdims=True) + EPS)
        g = g_ref[...]
        r_ref[...] = (g * _sig(g) * (o * rs * gr_ref[...])).astype(BF16)

    col = lambda c0: pl.BlockSpec((B, HEAD), lambda h, i: (i, c0 + h))
    tab = pl.BlockSpec((B, HEAD), lambda h, i: (i, 0))
    outc = pl.BlockSpec((B, HEAD), lambda h, i: (i, h))
    return pl.pallas_call(
        kern, name=name, grid=(RET_HEADS, nb),
        in_specs=[col(_Q0), col(_K0), col(_V0), col(_G0), tab, tab, pl.BlockSpec((1, HEAD), lambda h, i: (0, h)),
                  pl.BlockSpec((None, 1, LANE), lambda h, i: (h, 0, 0))],
        out_specs=[outc, outc, pl.BlockSpec((None, None, HEAD, HEAD), lambda h, i: (h, i, 0, 0))],
        out_shape=[_sds((T, RET_DIM), BF16), _sds((T, RET_DIM), F32), _sds((RET_HEADS, nb, HEAD, HEAD), F32)],
        scratch_shapes=[pltpu.VMEM((HEAD, HEAD), F32)],
        compiler_params=_cparams("parallel", "arbitrary"))(proj, proj, proj, proj, cosf, sinf, gr, lgt)


def _retention_bwd(name, proj, oraw, states, dcat, cosf, sinf, gr, lgt):
    T = proj.shape[0]
    B = min(RET_BLOCK, T)
    nb = T // B

    def kern(q_ref, k_ref, v_ref, g_ref, o_ref, dr_ref, st_ref, cos_ref, sin_ref, gr_ref, lg_ref,
             dq_ref, dk_ref, dv_ref, dg_ref, dgr_ref, dS):
        @pl.when(pl.program_id(1) == 0)
        def _():
            dS[...] = jnp.zeros_like(dS)
            dgr_ref[...] = jnp.zeros_like(dgr_ref)

        cosv, sinv = cos_ref[...], sin_ref[...]
        rot = lambda a: a * cosv + pltpu.roll(a, HEAD // 2, 1) * sinv
        rot_t = lambda a: a * cosv + pltpu.roll(a * sinv, HEAD // 2, 1)
        qr = rot(q_ref[...])
        kr = rot(k_ref[...]) * QK_SCALE
        v = v_ref[...]
        dmat, qdec, kdec, bdec = _ret_tables(B, lg_ref[0:1, 0:1])
        o = o_ref[...]
        rs = lax.rsqrt(jnp.mean(o * o, axis=-1, keepdims=True) + EPS)
        xh = o * rs
        g = g_ref[...]
        sg = _sig(g)
        grv = gr_ref[...]
        dr = dr_ref[...]
        dn = dr * (g * sg)
        dg_ref[...] = (dr * (xh * grv) * (sg * (1.0 + g * (1.0 - sg)))).astype(BF16)
        dgr_ref[0:1, :] += jnp.sum(dn * xh, axis=0, keepdims=True)
        dxh = dn * grv
        do = rs * (dxh - xh * jnp.mean(dxh * xh, axis=-1, keepdims=True))
        sp = st_ref[...]
        dsv = dS[...]
        pd = _dot(qr, kr, 1, 1) * dmat
        dp = _dot(do, v, 1, 1) * dmat
        dv_ref[...] = (_dot(pd, do, 0, 0) + _dot(kr * kdec, dsv, 1, 0)).astype(BF16)
        dqr = _dot(dp, kr, 1, 0) + _dot(do, sp, 1, 1) * qdec
        dkr = _dot(dp, qr, 0, 0) + _dot(v, dsv, 1, 1) * kdec
        dS[...] = bdec * dsv + _dot(qr * qdec, do, 0, 0)
        dq_ref[...] = rot_t(dqr).astype(BF16)
        dk_ref[...] = (rot_t(dkr) * QK_SCALE).astype(BF16)

    rev = lambda i: nb - 1 - i
    col = lambda c0: pl.BlockSpec((B, HEAD), lambda h, i: (rev(i), c0 + h))
    tab = pl.BlockSpec((B, HEAD), lambda h, i: (rev(i), 0))
    outc = pl.BlockSpec((B, HEAD), lambda h, i: (rev(i), h))
    return pl.pallas_call(
        kern, name=name, grid=(RET_HEADS, nb),
        in_specs=[col(_Q0), col(_K0), col(_V0), col(_G0), outc,
                  pl.BlockSpec((B, HEAD), lambda h, i: (rev(i), CONV_DIM // LANE + h)),
                  pl.BlockSpec((None, None, HEAD, HEAD), lambda h, i: (h, rev(i), 0, 0)), tab, tab,
                  pl.BlockSpec((1, HEAD), lambda h, i: (0, h)), pl.BlockSpec((None, 1, LANE), lambda h, i: (h, 0, 0))],
        out_specs=[outc, outc, outc, outc, pl.BlockSpec((8, HEAD), lambda h, i: (0, h))],
        out_shape=[_sds((T, RET_DIM), BF16)] * 4 + [_sds((8, RET_DIM), F32)],
        scratch_shapes=[pltpu.VMEM((HEAD, HEAD), F32)],
        compiler_params=_cparams("parallel", "arbitrary"))(proj, proj, proj, proj, oraw, dcat, states, cosf, sinf, gr, lgt)


def _strict_upper():
    r = lax.broadcasted_iota(jnp.int32, (SB_BLOCK, SB_BLOCK), 0)
    c = lax.broadcasted_iota(jnp.int32, (SB_BLOCK, SB_BLOCK), 1)
    return (r > c).astype(BF16), c - r


def _suffix_sum(vals, tri):
    hi = vals.astype(BF16)
    lo = (vals - hi.astype(F32)).astype(BF16)
    dn = (((1,), (0,)), ((), ()))
    return lax.dot_general(hi, tri, dn, preferred_element_type=F32) + lax.dot_general(lo, tri, dn, preferred_element_type=F32)


def _sb_scores(qi, kj, tri, col_minus_row, on_diag, carry):
    z = _dot(qi, kj, 1, 1) * QK_SCALE
    lb = jnp.minimum(z, 0.0) - jnp.log(1.0 + jnp.exp(-jnp.abs(z)))
    valid = col_minus_row < jnp.where(on_diag, 0, SB_BLOCK)
    lk = jnp.where(valid, lb - z, 0.0)
    acc = _suffix_sum(lk, tri) + carry
    w = jnp.where(valid, jnp.exp(lb + acc), 0.0)
    return lb, lk, valid, w


def _head_norm_rows(src, gain, dst, T):
    for r0, rc in _row_chunks(T):
        a = src[r0:r0 + rc, :]
        r = lax.rsqrt(jnp.mean(a * a, axis=-1, keepdims=True) + EPS)
        dst[r0:r0 + rc, :] = (a * r * gain).astype(BF16)


def _sb_fwd(name, qkv, gq, gk):
    T = qkv.shape[0]
    nq = T // SB_BLOCK

    def kern(q_ref, k_ref, v_ref, gq_ref, gk_ref, o_ref, o32_ref, qn, kn, vb):
        _head_norm_rows(q_ref, gq_ref[...], qn, T)
        _head_norm_rows(k_ref, gk_ref[...], kn, T)
        for r0, rc in _row_chunks(T):
            vb[r0:r0 + rc, :] = v_ref[r0:r0 + rc, :].astype(BF16)
        tri, diag_mask = _strict_upper()

        def qblock(i, _):
            rows_i = pl.ds(pl.multiple_of(i * SB_BLOCK, SB_BLOCK), SB_BLOCK)
            qi = qn[rows_i, :]

            def kblock(jj, st):
                acc, car = st
                j = i - jj
                rows_j = pl.ds(pl.multiple_of(j * SB_BLOCK, SB_BLOCK), SB_BLOCK)
                _, lk, _, w = _sb_scores(qi, kn[rows_j, :], tri, diag_mask, jj == 0, car)
                w_hi = w.astype(BF16)
                w_lo = (w - w_hi.astype(F32)).astype(BF16)
                vj = vb[rows_j, :]
                acc = acc + _dot(w_hi, vj, 1, 0) + _dot(w_lo, vj, 1, 0)
                return acc, car + jnp.sum(lk, axis=-1, keepdims=True)

            acc, _ = lax.fori_loop(0, i + 1, kblock, (jnp.zeros((SB_BLOCK, HEAD), F32), jnp.zeros((SB_BLOCK, 1), F32)))
            o_ref[rows_i, :] = acc.astype(BF16)
            o32_ref[rows_i, :] = acc
            return 0

        lax.fori_loop(0, nq, qblock, 0)

    col = lambda c0: pl.BlockSpec((T, HEAD), lambda h: (0, c0 + h))
    vec = pl.BlockSpec((1, HEAD), lambda h: (0, 0))
    out = pl.BlockSpec((T, HEAD), lambda h: (0, h))
    return pl.pallas_call(kern, name=name, grid=(SB_HEADS,),
                          in_specs=[col(0), col(SB_HEADS), col(2 * SB_HEADS), vec, vec], out_specs=[out, out],
                          out_shape=[_sds((T, D_MODEL), BF16), _sds((T, D_MODEL), F32)],
                          scratch_shapes=[pltpu.VMEM((T, HEAD), BF16)] * 3,
                          compiler_params=_cparams("parallel"))(qkv, qkv, qkv, gq, gk)


def _sb_bwd(name, qkv, gq, gk, o32, dcat):
    T = qkv.shape[0]
    nq = T // SB_BLOCK

    def kern(q_ref, k_ref, v_ref, gq_ref, gk_ref, o_ref, do_ref, dq_ref, dk_ref, dv_ref, dgq_ref, dgk_ref,
             qn, kn, vb, dqn, dkn, dvv):
        @pl.when(pl.program_id(0) == 0)
        def _():
            dgq_ref[...] = jnp.zeros_like(dgq_ref)
            dgk_ref[...] = jnp.zeros_like(dgk_ref)

        _head_norm_rows(q_ref, gq_ref[...], qn, T)
        _head_norm_rows(k_ref, gk_ref[...], kn, T)
        for r0, rc in _row_chunks(T):
            vb[r0:r0 + rc, :] = v_ref[r0:r0 + rc, :].astype(BF16)
            dkn[r0:r0 + rc, :] = jnp.zeros((rc, HEAD), F32)
            dvv[r0:r0 + rc, :] = jnp.zeros((rc, HEAD), F32)
        tri, diag_mask = _strict_upper()

        def qblock(i, _):
            rows_i = pl.ds(pl.multiple_of(i * SB_BLOCK, SB_BLOCK), SB_BLOCK)
            qi = qn[rows_i, :]
            doi = do_ref[rows_i, :]
            dob = doi.astype(BF16)
            etot = jnp.sum(dob.astype(F32) * o_ref[rows_i, :], axis=-1, keepdims=True)

            def kblock(jj, st):
                dq_acc, car, ecar = st
                j = i - jj
                rows_j = pl.ds(pl.multiple_of(j * SB_BLOCK, SB_BLOCK), SB_BLOCK)
                kj = kn[rows_j, :]
                vj = vb[rows_j, :]
                lb, lk, valid, w = _sb_scores(qi, kj, tri, diag_mask, jj == 0, car)
                e = w * _dot(dob, vj, 1, 1)
                suff = _suffix_sum(e, tri) + e + ecar
                sig = jnp.exp(lb)
                dz = jnp.where(valid, e * (1.0 - sig) - sig * (etot - suff), 0.0) * QK_SCALE
                dzb = dz.astype(BF16)
                dkn[rows_j, :] += _dot(dzb, qi, 0, 0)
                dvv[rows_j, :] += _dot(w, dob, 0, 0)
                return (dq_acc + _dot(dzb, kj, 1, 0), car + jnp.sum(lk, axis=-1, keepdims=True),
                        ecar + jnp.sum(e, axis=-1, keepdims=True))

            zcol = jnp.zeros((SB_BLOCK, 1), F32)
            dq_acc, _, _ = lax.fori_loop(0, i + 1, kblock, (jnp.zeros((SB_BLOCK, HEAD), F32), zcol, zcol))
            dqn[rows_i, :] = dq_acc
            return 0

        lax.fori_loop(0, nq, qblock, 0)

        def norm_bwd(src, gain, dnorm, dst, dgain):
            tot = jnp.zeros((1, HEAD), F32)
            for r0, rc in _row_chunks(T):
                a = src[r0:r0 + rc, :]
                r = lax.rsqrt(jnp.mean(a * a, axis=-1, keepdims=True) + EPS)
                xh = a * r
                dn = dnorm[r0:r0 + rc, :]
                tot = tot + jnp.sum(dn * xh, axis=0, keepdims=True)
                dxh = dn * gain
                dst[r0:r0 + rc, :] = (r * (dxh - xh * jnp.mean(dxh * xh, axis=-1, keepdims=True))).astype(BF16)
            dgain[0:1, :] += tot

        norm_bwd(q_ref, gq_ref[...], dqn, dq_ref, dgq_ref)
        norm_bwd(k_ref, gk_ref[...], dkn, dk_ref, dgk_ref)
        for r0, rc in _row_chunks(T):
            dv_ref[r0:r0 + rc, :] = dvv[r0:r0 + rc, :].astype(BF16)

    col = lambda c0: pl.BlockSpec((T, HEAD), lambda h: (0, c0 + h))
    vec = pl.BlockSpec((1, HEAD), lambda h: (0, 0))
    out = pl.BlockSpec((T, HEAD), lambda h: (0, h))
    st = pl.BlockSpec((8, HEAD), lambda h: (0, 0))
    return pl.pallas_call(kern, name=name, grid=(SB_HEADS,),
                          in_specs=[col(0), col(SB_HEADS), col(2 * SB_HEADS), vec, vec, out, out],
                          out_specs=[out, out, out, st, st],
                          out_shape=[_sds((T, D_MODEL), BF16)] * 3 + [_sds((8, HEAD), F32)] * 2,
                          scratch_shapes=[pltpu.VMEM((T, HEAD), BF16)] * 3 + [pltpu.VMEM((T, HEAD), F32)] * 3,
                          compiler_params=_cparams("arbitrary"))(qkv, qkv, qkv, gq, gk, o32, dcat)


def _ada_fwd(c_all, ada_w, ada_b_cols):
    L, K, n = ada_w.shape

    def kern(c_ref, w_ref, b_ref, o_ref):
        cv = c_ref[...]
        o_ref[...] = _dot(cv * _sig(cv), w_ref[...], 1, 0) + b_ref[...]

    return pl.pallas_call(kern, name="ada_fwd", grid=(L,),
                          in_specs=[pl.BlockSpec((N_DEV, K), lambda l: (0, 0)), pl.BlockSpec((None, K, n), lambda l: (l, 0, 0)),
                                    pl.BlockSpec((None, 1, n), lambda l: (l, 0, 0))],
                          out_specs=pl.BlockSpec((None, N_DEV, n), lambda l: (l, 0, 0)),
                          out_shape=_sds((L, N_DEV, n), F32), compiler_params=_cparams("parallel"))(c_all, ada_w, ada_b_cols)


def _ada_wgrad(c_all_t, dmod_cols):
    K = c_all_t.shape[0]
    L, _, n = dmod_cols.shape
    tk = 128

    def kern(c_ref, d_ref, o_ref):
        cv = c_ref[...]
        ca = cv * _sig(cv)
        acc = ca[:, 0:1] * d_ref[0:1, :]
        for b in range(1, N_DEV):
            acc = acc + ca[:, b:b + 1] * d_ref[b:b + 1, :]
        o_ref[...] = acc

    return pl.pallas_call(kern, name="ada_wgrad", grid=(L, K // tk),
                          in_specs=[pl.BlockSpec((tk, N_DEV), lambda l, m: (m, 0)), pl.BlockSpec((None, N_DEV, n), lambda l, m: (l, 0, 0))],
                          out_specs=pl.BlockSpec((None, tk, n), lambda l, m: (l, m, 0)),
                          out_shape=_sds((L, K, n), F32), compiler_params=_cparams("parallel", "parallel"))(c_all_t, dmod_cols)


def _sum_devices(g):
    _, R, n = g.shape

    def kern(g_ref, o_ref):
        acc = g_ref[0]
        for d in range(1, N_DEV):
            acc = acc + g_ref[d]
        o_ref[...] = acc

    return pl.pallas_call(kern, name="sum_devices", out_shape=_sds((R, n), F32))(g)


def _pair_sum(name, g, recv, c_idx):
    n4, L, r, cdim = g.shape
    rows = (L // 2) * r
    tr = _tile(rows, 512)
    gv = g.reshape(n4, 2, rows, cdim)
    rv = recv.reshape(n4, rows, cdim)

    def kern(c_ref, g_ref, r_ref, o_ref):
        o_ref[...] = g_ref[...] + r_ref[...]

    gs = pltpu.PrefetchScalarGridSpec(
        num_scalar_prefetch=1, grid=(n4, rows // tr),
        in_specs=[pl.BlockSpec((None, None, tr, cdim), lambda j, m, cr: (j, cr[0], m, 0)),
                  pl.BlockSpec((None, tr, cdim), lambda j, m, cr: (j, m, 0))],
        out_specs=pl.BlockSpec((None, tr, cdim), lambda j, m, cr: (j, m, 0)))
    out = pl.pallas_call(kern, name=name, grid_spec=gs, out_shape=_sds((n4, rows, cdim), F32),
                         compiler_params=_cparams("parallel", "parallel"))(c_idx, gv, rv)
    return out.reshape(n4, L // 2, r, cdim)


def _chip_sum(name, recv):
    n4, hl, r, cdim = recv.shape
    rows = hl * r
    tr = _tile(rows, 512)

    def kern(r_ref, o_ref):
        o_ref[...] = ((r_ref[0] + r_ref[1]) + r_ref[2]) + r_ref[3]

    out = pl.pallas_call(kern, name=name, grid=(rows // tr,),
                         in_specs=[pl.BlockSpec((n4, tr, cdim), lambda m: (0, m, 0))],
                         out_specs=pl.BlockSpec((tr, cdim), lambda m: (m, 0)), out_shape=_sds((rows, cdim), F32),
                         compiler_params=_cparams("parallel"))(recv.reshape(n4, rows, cdim))
    return out.reshape(hl, r, cdim)


def _adamw(name, w, g, m, v):
    shape = w.shape
    cols = shape[-1]
    rows = int(np.prod(shape[:-1]))
    tr = _tile(rows, 512) if rows % 8 == 0 else rows
    c1 = 1.0 - ADAM_B1 ** ADAM_STEP
    c2 = 1.0 - ADAM_B2 ** ADAM_STEP

    def kern(w_ref, g_ref, m_ref, v_ref, d_ref, mo_ref, vo_ref):
        gv = g_ref[...]
        mn = ADAM_B1 * m_ref[...] + (1.0 - ADAM_B1) * gv
        vn = ADAM_B2 * v_ref[...] + (1.0 - ADAM_B2) * (gv * gv)
        mo_ref[...] = mn
        vo_ref[...] = vn
        d_ref[...] = -ADAM_LR * ((mn / c1) / (jnp.sqrt(vn / c2) + ADAM_EPS) + ADAM_WD * w_ref[...])

    blk = pl.BlockSpec((tr, cols), lambda i: (i, 0))
    outs = pl.pallas_call(kern, name=name, grid=(rows // tr,), in_specs=[blk] * 4, out_specs=[blk] * 3,
                          out_shape=[_sds((rows, cols), F32)] * 3, compiler_params=_cparams("parallel"))(
        *[a.reshape(rows, cols) for a in (w, g, m, v)])
    return tuple(o.reshape(shape) for o in outs)


def _rcopy(src, dst, ssem, rsem, dev):
    return pltpu.make_async_remote_copy(src_ref=src, dst_ref=dst, send_sem=ssem, recv_sem=rsem, device_id=dev,
                                        device_id_type=MESH)


def _gather8(name, blk):
    m_per, n = blk.shape

    def body(x_ref, out_ref, send_sems, recv_sems, local_sem):
        x, y, c = _position()
        me, sibling = (x, y, c), (x, y, 1 - c)
        chips = _other_chips(x, y)

        def rows(px, py, pc):
            return out_ref.at[pl.ds((4 * px + 2 * py + pc) * m_per, m_per), :]

        def copy(k, block, to, src=None):
            return _rcopy(rows(*block) if src is None else src, rows(*block), send_sems.at[k], recv_sems.at[k], to)

        mine = pltpu.make_async_copy(x_ref, rows(*me), local_sem)
        mine.start()
        first = [copy(0, me, sibling, src=x_ref)]
        first += [copy(1 + j, me, (*chip, c), src=x_ref) for j, chip in enumerate(chips)]
        for cp in first:
            cp.start()
        passed = [copy(4 + j, (*chip, c), sibling) for j, chip in enumerate(chips)]
        for j, chip in enumerate(chips):
            copy(1 + j, (*chip, c), me).wait_recv()
            passed[j].start()
        copy(0, sibling, me).wait_recv()
        for j, chip in enumerate(chips):
            copy(4 + j, (*chip, 1 - c), me).wait_recv()
        for cp in first + passed:
            cp.wait_send()
        mine.wait()

    return pl.pallas_call(body, name=name, out_shape=_sds((N_DEV * m_per, n), blk.dtype),
                          in_specs=[pl.BlockSpec(memory_space=pltpu.VMEM)], out_specs=pl.BlockSpec(memory_space=pltpu.VMEM),
                          scratch_shapes=[pltpu.SemaphoreType.DMA((7,)), pltpu.SemaphoreType.DMA((7,)), pltpu.SemaphoreType.DMA],
                          compiler_params=pltpu.CompilerParams(vmem_limit_bytes=VMEM_LIMIT_V7X))(blk)


_ANY = pl.BlockSpec(memory_space=pl.ANY)


def _gather_weights(shards):
    n = len(shards)

    def body(*refs):
        src, out = refs[:n], refs[n:2 * n]
        send_sems, recv_sems, fsend_sems, frecv_sems, local_sems = refs[2 * n:]
        x, y, c = _position()
        sibling = (x, y, 1 - c)
        chips = _other_chips(x, y)
        me_chip = 2 * x + y
        started = []
        local = []
        for t in range(n):
            hl = src[t].shape[0] // 2
            mine_half = pl.ds(c * hl, hl)
            lc = pltpu.make_async_copy(src[t], out[t].at[me_chip], local_sems.at[t])
            lc.start()
            local.append(lc)
            for j, (px, py) in enumerate(chips):
                cp = _rcopy(src[t].at[mine_half], out[t].at[me_chip, mine_half], send_sems.at[3 * t + j],
                            recv_sems.at[3 * t + j], (px, py, c))
                cp.start()
                started.append(cp)
        for t in range(n):
            hl = src[t].shape[0] // 2
            mine_half = pl.ds(c * hl, hl)
            for j, (px, py) in enumerate(chips):
                landed = out[t].at[2 * px + py, mine_half]
                _rcopy(landed, landed, send_sems.at[3 * t + j], recv_sems.at[3 * t + j], (px, py, c)).wait_recv()
                fw = _rcopy(landed, landed, fsend_sems.at[3 * t + j], frecv_sems.at[3 * t + j], sibling)
                fw.start()
                started.append(fw)
        for t in range(n):
            hl = src[t].shape[0] // 2
            other_half = pl.ds((1 - c) * hl, hl)
            for j, (px, py) in enumerate(chips):
                landed = out[t].at[2 * px + py, other_half]
                _rcopy(landed, landed, fsend_sems.at[3 * t + j], frecv_sems.at[3 * t + j], sibling).wait_recv()
        for cp in started:
            cp.wait_send()
        for lc in local:
            lc.wait()

    sems = [pltpu.SemaphoreType.DMA((3 * n,))] * 4 + [pltpu.SemaphoreType.DMA((n,))]
    return pl.pallas_call(body, name="gather_weights", out_shape=[_sds((N_CHIP,) + s.shape, s.dtype) for s in shards],
                          in_specs=[_ANY] * n, out_specs=[_ANY] * n, scratch_shapes=sems)(*shards)


def _exchange_pair(grads):
    n = len(grads)

    def body(*refs):
        src, out = refs[:n], refs[n:2 * n]
        send_sems, recv_sems = refs[2 * n:]
        x, y, c = _position()
        sibling = (x, y, 1 - c)
        cps = []
        for t in range(n):
            hl = src[t].shape[1] // 2
            cp = _rcopy(src[t].at[:, pl.ds((1 - c) * hl, hl)], out[t], send_sems.at[t], recv_sems.at[t], sibling)
            cp.start()
            cps.append(cp)
        for cp in cps:
            cp.wait_recv()
        for cp in cps:
            cp.wait_send()

    sems = [pltpu.SemaphoreType.DMA((n,))] * 2
    return pl.pallas_call(body, name="exchange_pair",
                          out_shape=[_sds((g.shape[0], g.shape[1] // 2) + g.shape[2:], g.dtype) for g in grads],
                          in_specs=[_ANY] * n, out_specs=[_ANY] * n, scratch_shapes=sems)(*grads)


def _scatter_chips(psums):
    n = len(psums)

    def body(*refs):
        src, out = refs[:n], refs[n:2 * n]
        send_sems, recv_sems, local_sems = refs[2 * n:]
        x, y, c = _position()
        chips = _other_chips(x, y)
        me_chip = 2 * x + y
        cps, local = [], []
        for t in range(n):
            lc = pltpu.make_async_copy(src[t].at[me_chip], out[t].at[me_chip], local_sems.at[t])
            lc.start()
            local.append(lc)
            for j, (px, py) in enumerate(chips):
                cp = _rcopy(src[t].at[2 * px + py], out[t].at[me_chip], send_sems.at[3 * t + j], recv_sems.at[3 * t + j],
                            (px, py, c))
                cp.start()
                cps.append(cp)
        for t in range(n):
            for j, (px, py) in enumerate(chips):
                slot = out[t].at[2 * px + py]
                _rcopy(slot, slot, send_sems.at[3 * t + j], recv_sems.at[3 * t + j], (px, py, c)).wait_recv()
        for cp in cps:
            cp.wait_send()
        for lc in local:
            lc.wait()

    sems = [pltpu.SemaphoreType.DMA((3 * n,))] * 2 + [pltpu.SemaphoreType.DMA((n,))]
    return pl.pallas_call(body, name="scatter_chips", out_shape=[_sds(p.shape, p.dtype) for p in psums],
                          in_specs=[_ANY] * n, out_specs=[_ANY] * n, scratch_shapes=sems)(*psums)


def _share_halves(halves):
    n = len(halves)

    def body(*refs):
        src, out = refs[:n], refs[n:2 * n]
        send_sems, recv_sems, local_sems = refs[2 * n:]
        x, y, c = _position()
        sibling = (x, y, 1 - c)
        cps, local = [], []
        for t in range(n):
            hl = src[t].shape[0]
            mine = out[t].at[pl.ds(c * hl, hl)]
            lc = pltpu.make_async_copy(src[t], mine, local_sems.at[t])
            lc.start()
            local.append(lc)
            cp = _rcopy(src[t], mine, send_sems.at[t], recv_sems.at[t], sibling)
            cp.start()
            cps.append(cp)
        for t in range(n):
            hl = src[t].shape[0]
            theirs = out[t].at[pl.ds((1 - c) * hl, hl)]
            _rcopy(theirs, theirs, send_sems.at[t], recv_sems.at[t], sibling).wait_recv()
        for cp in cps:
            cp.wait_send()
        for lc in local:
            lc.wait()

    sems = [pltpu.SemaphoreType.DMA((n,))] * 3
    return pl.pallas_call(body, name="share_halves",
                          out_shape=[_sds((2 * h.shape[0],) + h.shape[1:], h.dtype) for h in halves],
                          in_specs=[_ANY] * n, out_specs=[_ANY] * n, scratch_shapes=sems)(*halves)


def _rope_tables(T):
    inv_freq = 1.0 / (ROPE_THETA ** (jnp.arange(0, HEAD, 2, dtype=F32) / HEAD))
    ang = jnp.arange(T, dtype=F32)[:, None] * inv_freq[None, :]
    cos, sin = jnp.cos(ang), jnp.sin(ang)
    return jnp.concatenate([cos, cos], axis=-1), jnp.concatenate([-sin, sin], axis=-1)


def _decay_table():
    lg = np.log1p(-np.exp2(-5.0 - np.arange(RET_HEADS, dtype=np.float32))).astype(np.float32)
    return jnp.asarray(np.broadcast_to(lg[:, None, None], (RET_HEADS, 1, LANE)).copy())


def _local_step(x0, target, mod, W, G, norm_mix_g, norm_ffn_g, conv_full, ev_ret_norm_g, od_q_norm_g, od_k_norm_g):
    T = x0.shape[0]
    KSH1, KSC1, KG1, KSH2, KSC2, KG2 = range(6)
    gain_mix = norm_mix_g.reshape(DEPTH, 1, D_MODEL)
    gain_ffn = norm_ffn_g.reshape(DEPTH, 1, D_MODEL)
    cosf, sinf = _rope_tables(T)
    lgt = _decay_table()

    saved = []
    xcur = x0
    for l in range(DEPTH):
        j = l // 2
        s = dict(x_in=xcur)
        h = _normmod(f"norm_mix_{l}", xcur, gain_mix, l, mod, KSC1, KSH1)
        s["h"] = h
        if l % 2 == 0:
            proj = _proj_cols(f"ev_in_{l}", h, W["ev_w_in"], j)
            a = _conv_fwd(f"conv_{l}", proj, conv_full[j])
            r, oraw, states = _retention_fwd(f"ret_{l}", proj, cosf, sinf, ev_ret_norm_g[j].reshape(1, RET_DIM), lgt)
            cat = jnp.concatenate([a, r], axis=1)
            s.update(proj=proj, oraw=oraw, states=states, cat=cat)
            y, xmid = _out_proj(f"ev_out_{l}", cat, False, W["ev_w_out"], j, xcur, mod, l, KG1)
        else:
            qkv = _proj_cols(f"od_in_{l}", h, W["od_w_qkv"], j)
            o, o32 = _sb_fwd(f"sb_{l}", qkv, od_q_norm_g[j].reshape(1, HEAD), od_k_norm_g[j].reshape(1, HEAD))
            s.update(qkv=qkv, cat=o, o32=o32)
            y, xmid = _out_proj(f"od_out_{l}", o, False, W["od_w_out"], j, xcur, mod, l, KG1)
        s.update(y1=y, x_mid=xmid)
        h2 = _normmod(f"norm_ffn_{l}", xmid, gain_ffn, l, mod, KSC2, KSH2)
        gate, up, act = _ffn_up(f"ffn_up_{l}", h2, W["ffn_w_gate"], W["ffn_w_up"], l)
        y2, xcur = _out_proj(f"ffn_down_{l}", act, True, W["ffn_w_down"], l, xmid, mod, l, KG2)
        s.update(h2=h2, gate=gate, up=up, act=act, y2=y2)
        saved.append(s)

    dy, lacc = _loss_head(xcur, target)

    dmod_rows = [None] * DEPTH
    d_mix = [None] * DEPTH
    d_ffn = [None] * DEPTH
    d_conv = [None] * 2
    d_ret = [None] * 2
    d_gq = [None] * 2
    d_gk = [None] * 2
    dx = dy
    for l in reversed(range(DEPTH)):
        j = l // 2
        s = saved[l]
        dyg, st_g2 = _gate_bwd(f"gate2_bwd_{l}", dx, s["y2"], mod, l, KG2)
        G["ffn_w_down"] = _wgrad(f"wg_down_{l}", s["act"], "stack", dyg, "full", G["ffn_w_down"], l)
        dgate, dup = _ffn_down_bwd(f"ffn_down_bwd_{l}", dyg, W["ffn_w_down"], l, s["gate"], s["up"])
        G["ffn_w_gate"] = _wgrad(f"wg_gate_{l}", s["h2"], "full", dgate, "stack", G["ffn_w_gate"], l)
        G["ffn_w_up"] = _wgrad(f"wg_up_{l}", s["h2"], "full", dup, "stack", G["ffn_w_up"], l)
        dh2 = _ffn_up_bwd(f"ffn_up_bwd_{l}", dgate, dup, W["ffn_w_gate"], W["ffn_w_up"], l)
        dxm, st_n2 = _normmod_bwd(f"norm_ffn_bwd_{l}", s["x_mid"], dh2, dx, gain_ffn, l, mod, KSC2)
        dyg1, st_g1 = _gate_bwd(f"gate1_bwd_{l}", dxm, s["y1"], mod, l, KG1)
        if l % 2 == 0:
            G["ev_w_out"] = _wgrad(f"wg_evout_{l}", s["cat"], "cols", dyg1, "full", G["ev_w_out"], j)
            dcat = _bwd_rows(f"ev_out_bwd_{l}", dyg1, W["ev_w_out"], j)
            db, dcg, du, dwc = _conv_bwd(f"conv_bwd_{l}", s["proj"], conv_full[j], dcat)
            dq, dk, dv, dg, dgr = _retention_bwd(f"ret_bwd_{l}", s["proj"], s["oraw"], s["states"], dcat, cosf, sinf,
                                                 ev_ret_norm_g[j].reshape(1, RET_DIM), lgt)
            dproj = jnp.concatenate([db, dcg, du, dq, dk, dv, dg], axis=1)
            d_conv[j], d_ret[j] = dwc[:CONV_WIDTH], dgr[0]
            G["ev_w_in"] = _wgrad(f"wg_evin_{l}", s["h"], "full", dproj, "cols", G["ev_w_in"], j)
            dh = _bwd_cols(f"ev_in_bwd_{l}", dproj, W["ev_w_in"], j)
        else:
            G["od_w_out"] = _wgrad(f"wg_odout_{l}", s["cat"], "cols", dyg1, "full", G["od_w_out"], j)
            dcat = _bwd_rows(f"od_out_bwd_{l}", dyg1, W["od_w_out"], j)
            dq, dk, dv, dgq, dgk = _sb_bwd(f"sb_bwd_{l}", s["qkv"], od_q_norm_g[j].reshape(1, HEAD),
                                           od_k_norm_g[j].reshape(1, HEAD), s["o32"], dcat)
            dproj = jnp.concatenate([dq, dk, dv], axis=1)
            d_gq[j], d_gk[j] = dgq[0], dgk[0]
            G["od_w_qkv"] = _wgrad(f"wg_odin_{l}", s["h"], "full", dproj, "cols", G["od_w_qkv"], j)
            dh = _bwd_cols(f"od_in_bwd_{l}", dproj, W["od_w_qkv"], j)
        dx, st_n1 = _normmod_bwd(f"norm_mix_bwd_{l}", s["x_in"], dh, dxm, gain_mix, l, mod, KSC1)
        dmod_rows[l] = jnp.stack([st_n1[0], st_n1[1], st_g1[0], st_n2[0], st_n2[1], st_g2[0]]).reshape(6 * D_MODEL)
        d_mix[l], d_ffn[l] = st_n1[2], st_n2[2]
    return lacc, dx, G, (dmod_rows, d_mix, d_ffn, d_ret, d_gq, d_gk, d_conv)


def kernel(x, c, ada_w, ada_b, norm_mix_g, norm_ffn_g, ev_w_in, ev_conv_w, ev_ret_norm_g, ev_w_out, od_w_qkv, od_q_norm_g, od_k_norm_g, od_w_out, ffn_w_gate, ffn_w_up, ffn_w_down, loss_target, m_ada_w, m_ada_b, m_norm_mix_g, m_norm_ffn_g, m_ev_w_in, m_ev_conv_w, m_ev_ret_norm_g, m_ev_w_out, m_od_w_qkv, m_od_q_norm_g, m_od_k_norm_g, m_od_w_out, m_ffn_w_gate, m_ffn_w_up, m_ffn_w_down, v_ada_w, v_ada_b, v_norm_mix_g, v_norm_ffn_g, v_ev_w_in, v_ev_conv_w, v_ev_ret_norm_g, v_ev_w_out, v_od_w_qkv, v_od_q_norm_g, v_od_k_norm_g, v_od_w_out, v_ffn_w_gate, v_ffn_w_up, v_ffn_w_down):
    xi, yi, ci = _position()
    chip = 2 * xi + yi
    dev = 4 * xi + 2 * yi + ci
    x0 = x[0]
    target = loss_target[0]

    n_small = D_MODEL + 2 * CONV_WIDTH * LANE
    small = jnp.concatenate([c.reshape(1, D_MODEL), ev_conv_w.reshape(1, 2 * CONV_WIDTH * LANE)], axis=1)
    small = jnp.broadcast_to(small, (8, n_small))
    g1 = _gather8("gather_cond", small).reshape(N_DEV, 8, n_small)[:, 0, :]
    c_all = g1[:, :D_MODEL]
    conv_all = g1[0::2, D_MODEL:].reshape(N_CHIP, 2, CONV_WIDTH, LANE)
    conv_full = conv_all.transpose(1, 2, 0, 3).reshape(2, CONV_WIDTH, CONV_DIM)

    n_ada = ada_w.shape[-1]
    ada_b_cols = lax.dynamic_slice_in_dim(ada_b, chip * n_ada, n_ada, axis=1).reshape(DEPTH, 1, n_ada)
    mod_cols = _ada_fwd(c_all, ada_w, ada_b_cols)
    g2 = _gather8("gather_mod", mod_cols.reshape(DEPTH * N_DEV, n_ada)).reshape(N_DEV, DEPTH, N_DEV, n_ada)
    mod_mine = lax.dynamic_index_in_dim(g2[0::2], dev, axis=2, keepdims=False)
    mod = mod_mine.transpose(1, 0, 2).reshape(DEPTH, 6, 1, D_MODEL)

    big_names = ["ev_w_in", "ev_w_out", "od_w_qkv", "od_w_out", "ffn_w_gate", "ffn_w_up", "ffn_w_down"]
    big = dict(ev_w_in=ev_w_in, ev_w_out=ev_w_out, od_w_qkv=od_w_qkv, od_w_out=od_w_out, ffn_w_gate=ffn_w_gate,
               ffn_w_up=ffn_w_up, ffn_w_down=ffn_w_down)
    W = dict(zip(big_names, _gather_weights([big[k].astype(BF16) for k in big_names])))
    G = {k: lax.empty((N_CHIP,) + big[k].shape, F32) for k in big_names}

    lacc, dx, G, small_grads = _local_step(x0, target, mod, W, G, norm_mix_g, norm_ffn_g, conv_full, ev_ret_norm_g,
                                           od_q_norm_g, od_k_norm_g)
    loss = lax.psum(lacc[0, 0], ("x", "y", "c"))
    grad_x = dx[None]
    dmod_rows, d_mix, d_ffn, d_ret, d_gq, d_gk, d_conv = small_grads

    c_idx = jnp.reshape(ci, (1,)).astype(jnp.int32)
    glist = [G[k] for k in big_names]
    recv_a = _exchange_pair(glist)
    psums = [_pair_sum(f"pair_sum_{k}", g, r, c_idx) for k, g, r in zip(big_names, glist, recv_a)]
    recv_b = _scatter_chips(psums)
    halves = [_chip_sum(f"chip_sum_{k}", r) for k, r in zip(big_names, recv_b)]
    grads = dict(zip(big_names, _share_halves(halves)))

    pieces = [jnp.stack(dmod_rows).reshape(-1), jnp.stack(d_mix).reshape(-1), jnp.stack(d_ffn).reshape(-1),
              jnp.stack(d_ret).reshape(-1), jnp.stack(d_gq).reshape(-1), jnp.stack(d_gk).reshape(-1),
              jnp.stack(d_conv).reshape(-1)]
    sizes = [int(p.shape[0]) for p in pieces]
    n_pack = sum(sizes)
    n_cols = -(-n_pack // (8 * LANE)) * LANE
    packed = jnp.concatenate(pieces + [jnp.zeros((8 * n_cols - n_pack,), F32)]).reshape(8, n_cols)
    g3 = _gather8("gather_small", packed).reshape(N_DEV, 8, n_cols)
    tot = _sum_devices(g3).reshape(-1)
    offs = np.cumsum([0] + sizes)
    part = [tot[offs[i]:offs[i + 1]] for i in range(len(sizes))]
    grads["ada_b"] = part[0].reshape(DEPTH, 6 * D_MODEL)
    grads["norm_mix_g"] = part[1].reshape(DEPTH, D_MODEL)
    grads["norm_ffn_g"] = part[2].reshape(DEPTH, D_MODEL)
    grads["ev_ret_norm_g"] = part[3].reshape(2, RET_DIM)
    grads["od_q_norm_g"] = part[4].reshape(2, HEAD)
    grads["od_k_norm_g"] = part[5].reshape(2, HEAD)
    conv_g = part[6].reshape(2, CONV_WIDTH, CONV_DIM)
    grads["ev_conv_w"] = lax.dynamic_slice_in_dim(conv_g, chip * LANE, LANE, axis=2)
    dmod_all = g3.reshape(N_DEV, -1)[:, :DEPTH * 6 * D_MODEL].reshape(N_DEV, DEPTH, 6 * D_MODEL)
    dmod_cols = lax.dynamic_slice_in_dim(dmod_all, chip * n_ada, n_ada, axis=2).transpose(1, 0, 2)
    grads["ada_w"] = _ada_wgrad(c_all.T, dmod_cols)

    weights = dict(ada_w=ada_w, ada_b=ada_b, norm_mix_g=norm_mix_g, norm_ffn_g=norm_ffn_g, ev_w_in=ev_w_in,
                   ev_conv_w=ev_conv_w, ev_ret_norm_g=ev_ret_norm_g, ev_w_out=ev_w_out, od_w_qkv=od_w_qkv,
                   od_q_norm_g=od_q_norm_g, od_k_norm_g=od_k_norm_g, od_w_out=od_w_out, ffn_w_gate=ffn_w_gate,
                   ffn_w_up=ffn_w_up, ffn_w_down=ffn_w_down)
    m_in = dict(ada_w=m_ada_w, ada_b=m_ada_b, norm_mix_g=m_norm_mix_g, norm_ffn_g=m_norm_ffn_g, ev_w_in=m_ev_w_in,
                ev_conv_w=m_ev_conv_w, ev_ret_norm_g=m_ev_ret_norm_g, ev_w_out=m_ev_w_out, od_w_qkv=m_od_w_qkv,
                od_q_norm_g=m_od_q_norm_g, od_k_norm_g=m_od_k_norm_g, od_w_out=m_od_w_out, ffn_w_gate=m_ffn_w_gate,
                ffn_w_up=m_ffn_w_up, ffn_w_down=m_ffn_w_down)
    v_in = dict(ada_w=v_ada_w, ada_b=v_ada_b, norm_mix_g=v_norm_mix_g, norm_ffn_g=v_norm_ffn_g, ev_w_in=v_ev_w_in,
                ev_conv_w=v_ev_conv_w, ev_ret_norm_g=v_ev_ret_norm_g, ev_w_out=v_ev_w_out, od_w_qkv=v_od_w_qkv,
                od_q_norm_g=v_od_q_norm_g, od_k_norm_g=v_od_k_norm_g, od_w_out=v_od_w_out, ffn_w_gate=v_ffn_w_gate,
                ffn_w_up=v_ffn_w_up, ffn_w_down=v_ffn_w_down)
    order = list(weights)
    deltas, new_m, new_v = {}, {}, {}
    for k in order:
        deltas[k], new_m[k], new_v[k] = _adamw(f"adamw_{k}", weights[k], grads[k], m_in[k], v_in[k])
    return (loss, grad_x, *[grads[k] for k in order], *[deltas[k] for k in order], *[new_m[k] for k in order],
            *[new_v[k] for k in order])
```

```python
import functools

import numpy as np
import jax
import jax.numpy as jnp
from jax import lax
from jax.experimental import pallas as pl
from jax.experimental.pallas import tpu as pltpu

F32 = jnp.float32
BF16 = jnp.bfloat16
MESH = pl.DeviceIdType.MESH

D_MODEL = 1024
DEPTH = 4
N_CHIP = 4
N_DEV = 8
HEAD = 128
RET_HEADS = 4
SB_HEADS = 8
CONV_DIM = 512
RET_DIM = 512
CONV_WIDTH = 3
RET_CHUNK = 64
RET_BLOCK = 256
SB_BLOCK = 256
EPS = 1e-6
ROPE_THETA = 10000.0
QK_SCALE = HEAD ** -0.5
LANE = 128
ROW_CHUNK = 512
VMEM_LIMIT_V7X = 56 * 1024 * 1024

ADAM_LR, ADAM_B1, ADAM_B2, ADAM_EPS, ADAM_WD, ADAM_STEP = 0.001, 0.9, 0.999, 1e-08, 0.01, 10


def _cparams(*sem):
    return pltpu.CompilerParams(dimension_semantics=sem or None, vmem_limit_bytes=VMEM_LIMIT_V7X)


def _tile(n, pref):
    if n <= pref:
        return n
    for t in range(pref - pref % 8, 7, -8):
        if n % t == 0:
            return t
    return n


def _sig(v):
    return 1.0 / (1.0 + jnp.exp(-v))


def _dot(a, b, ca, cb):
    return lax.dot_general(a.astype(BF16), b.astype(BF16), (((ca,), (cb,)), ((), ())),
                           preferred_element_type=F32)


def _position():
    x, y, c = lax.axis_index("x"), lax.axis_index("y"), lax.axis_index("c")
    return x, y, c


def _other_chips(x, y):
    return [(1 - x, y), (x, 1 - y), (1 - x, 1 - y)]


def _mm(name, pairs, out_sds, out_specs, grid, contract, red_axis=None, post=None,
        extras=(), extra_specs=(), sum_pairs=True, aliases=None):
    n_p, n_ex, n_out = len(pairs), len(extras), len(out_sds)
    n_acc = 1 if sum_pairs else n_p
    n_red = grid[red_axis] if red_axis is not None else 1

    def default_post(accs, ex, outs):
        outs[0][...] = accs[0].astype(outs[0].dtype)

    post_fn = post or default_post

    def kern(*refs):
        ab = refs[:2 * n_p]
        ex = refs[2 * n_p:2 * n_p + n_ex]
        outs = refs[2 * n_p + n_ex:2 * n_p + n_ex + n_out]
        accs = refs[2 * n_p + n_ex + n_out:]
        prods = [_dot(ab[2 * p][...], ab[2 * p + 1][...], contract[0], contract[1]) for p in range(n_p)]
        if sum_pairs:
            tot = prods[0]
            for p_ in prods[1:]:
                tot = tot + p_
            prods = [tot]
        if red_axis is None:
            post_fn(prods, ex, outs)
        else:
            k = pl.program_id(red_axis)

            @pl.when(k == 0)
            def _():
                for a_, p_ in zip(accs, prods):
                    a_[...] = p_

            @pl.when(k > 0)
            def _():
                for a_, p_ in zip(accs, prods):
                    a_[...] += p_

            @pl.when(k == n_red - 1)
            def _():
                post_fn([a_[...] for a_ in accs], ex, outs)

    ins, in_specs = [], []
    for a, b, sa, sb in pairs:
        ins += [a, b]
        in_specs += [sa, sb]
    ins += list(extras)
    in_specs += list(extra_specs)
    scratch = []
    if red_axis is not None:
        scratch = [pltpu.VMEM(tuple(acc_shape), F32) for acc_shape in [_acc_shape(pairs[0], contract)] * n_acc]
    sem = tuple("arbitrary" if ax == red_axis else "parallel" for ax in range(len(grid)))
    res = pl.pallas_call(kern, name=name, grid=grid, in_specs=in_specs, out_specs=list(out_specs),
                         out_shape=list(out_sds), scratch_shapes=scratch,
                         input_output_aliases=aliases or {}, compiler_params=_cparams(*sem))(*ins)
    return res


def _acc_shape(pair, contract):
    sa, sb = pair[2], pair[3]
    da = [d for d in sa.block_shape if d is not None]
    db = [d for d in sb.block_shape if d is not None]
    return (da[1 - contract[0]], db[1 - contract[1]])


def _sds(shape, dtype):
    return jax.ShapeDtypeStruct(tuple(shape), dtype)


def _proj_cols(name, h, w4, l):
    T, K = h.shape
    n = w4.shape[-1]
    tm = _tile(T, 512)
    return _mm(name, [(h, w4, pl.BlockSpec((tm, K), lambda i, m: (m, 0)),
                       pl.BlockSpec((None, None, K, n), lambda i, m: (i, l, 0, 0)))],
               [_sds((T, N_CHIP * n), F32)], [pl.BlockSpec((tm, n), lambda i, m: (m, i))],
               (N_CHIP, T // tm), (1, 0))[0]


def _out_proj(name, a, a_stacked, w4, l, xres, mod, lm, kg):
    T = xres.shape[0]
    k = w4.shape[-2]
    tm = _tile(T, 512)
    if a_stacked:
        sa = pl.BlockSpec((None, tm, k), lambda m, i: (i, m, 0))
    else:
        sa = pl.BlockSpec((tm, k), lambda m, i: (m, i))

    def post(accs, ex, outs):
        y = accs[0]
        outs[0][...] = y
        outs[1][...] = ex[0][...] + ex[1][...] * y

    row = pl.BlockSpec((tm, D_MODEL), lambda m, i: (m, 0))
    return _mm(name, [(a, w4, sa, pl.BlockSpec((None, None, k, D_MODEL), lambda m, i: (i, l, 0, 0)))],
               [_sds((T, D_MODEL), F32)] * 2, [row, row], (T // tm, N_CHIP), (1, 0), red_axis=1, post=post,
               extras=[xres, mod], extra_specs=[row, pl.BlockSpec((None, None, 1, D_MODEL), lambda m, i: (lm, kg, 0, 0))])


def _ffn_up(name, h2, wg4, wu4, l):
    T, K = h2.shape
    n = wg4.shape[-1]
    tm = _tile(T, 512)

    def post(accs, ex, outs):
        g, u = accs
        outs[0][...] = g
        outs[1][...] = u
        outs[2][...] = (g * _sig(g) * u).astype(BF16)

    sa = pl.BlockSpec((tm, K), lambda i, m: (m, 0))
    sw = pl.BlockSpec((None, None, K, n), lambda i, m: (i, l, 0, 0))
    so = pl.BlockSpec((None, tm, n), lambda i, m: (i, m, 0))
    return _mm(name, [(h2, wg4, sa, sw), (h2, wu4, sa, sw)],
               [_sds((N_CHIP, T, n), F32), _sds((N_CHIP, T, n), F32), _sds((N_CHIP, T, n), BF16)], [so, so, so],
               (N_CHIP, T // tm), (1, 0), post=post, sum_pairs=False)


def _bwd_cols(name, dproj, w4, l):
    T = dproj.shape[0]
    K, n = w4.shape[-2:]
    tm = _tile(T, 512)
    return _mm(name, [(dproj, w4, pl.BlockSpec((tm, n), lambda m, i: (m, i)),
                       pl.BlockSpec((None, None, K, n), lambda m, i: (i, l, 0, 0)))],
               [_sds((T, K), F32)], [pl.BlockSpec((tm, K), lambda m, i: (m, 0))],
               (T // tm, N_CHIP), (1, 1), red_axis=1)[0]


def _bwd_rows(name, dy, w4, l):
    T, N = dy.shape
    k = w4.shape[-2]
    tm = _tile(T, 512)
    return _mm(name, [(dy, w4, pl.BlockSpec((tm, N), lambda i, m: (m, 0)),
                       pl.BlockSpec((None, None, k, N), lambda i, m: (i, l, 0, 0)))],
               [_sds((T, N_CHIP * k), F32)], [pl.BlockSpec((tm, k), lambda i, m: (m, i))],
               (N_CHIP, T // tm), (1, 1))[0]


def _ffn_down_bwd(name, dy, wd4, l, gate, up):
    T, N = dy.shape
    k = wd4.shape[-2]
    tm = _tile(T, 512)

    def post(accs, ex, outs):
        da = accs[0]
        g = ex[0][...]
        u = ex[1][...]
        sg = _sig(g)
        outs[0][...] = (da * u * (sg * (1.0 + g * (1.0 - sg)))).astype(BF16)
        outs[1][...] = (da * (g * sg)).astype(BF16)

    so = pl.BlockSpec((None, tm, k), lambda i, m: (i, m, 0))
    return _mm(name, [(dy, wd4, pl.BlockSpec((tm, N), lambda i, m: (m, 0)),
                       pl.BlockSpec((None, None, k, N), lambda i, m: (i, l, 0, 0)))],
               [_sds((N_CHIP, T, k), BF16)] * 2, [so, so], (N_CHIP, T // tm), (1, 1), post=post,
               extras=[gate, up], extra_specs=[so, so])


def _ffn_up_bwd(name, dgate, dup, wg4, wu4, l):
    _, T, n = dgate.shape
    K = wg4.shape[-2]
    tm = _tile(T, 512)
    sa = pl.BlockSpec((None, tm, n), lambda m, i: (i, m, 0))
    sw = pl.BlockSpec((None, None, K, n), lambda m, i: (i, l, 0, 0))
    return _mm(name, [(dgate, wg4, sa, sw), (dup, wu4, sa, sw)],
               [_sds((T, K), F32)], [pl.BlockSpec((tm, K), lambda m, i: (m, 0))],
               (T // tm, N_CHIP), (1, 1), red_axis=1)[0]


def _wgrad(name, a, a_kind, b, b_kind, gbuf, l):
    r, cdim = gbuf.shape[-2:]
    T = a.shape[-2]
    tt = _tile(T, 512)

    def spec(kind, w):
        if kind == "full":
            return pl.BlockSpec((tt, w), lambda i, t: (t, 0))
        if kind == "cols":
            return pl.BlockSpec((tt, w), lambda i, t: (t, i))
        return pl.BlockSpec((None, tt, w), lambda i, t: (i, t, 0))

    def post(accs, ex, outs):
        outs[0][...] = accs[0]

    return _mm(name, [(a, b, spec(a_kind, r), spec(b_kind, cdim))], [_sds(gbuf.shape, F32)],
               [pl.BlockSpec((None, None, r, cdim), lambda i, t: (i, l, 0, 0))], (N_CHIP, T // tt), (0, 0),
               red_axis=1, post=post, extras=[gbuf], extra_specs=[pl.BlockSpec(memory_space=pl.ANY)],
               aliases={2: 0})[0]


def _vec_spec(*idx):
    return pl.BlockSpec((None,) * len(idx) + (1, D_MODEL), lambda m: tuple(idx) + (0, 0))


def _normmod(name, x, gain3, l, mod, ksc, ksh):
    T = x.shape[0]
    tm = _tile(T, 512)

    def kern(x_ref, g_ref, sc_ref, sh_ref, h_ref):
        xv = x_ref[...]
        r = lax.rsqrt(jnp.mean(xv * xv, axis=-1, keepdims=True) + EPS)
        h = (xv * r) * g_ref[...]
        h_ref[...] = (h * (1.0 + sc_ref[...]) + sh_ref[...]).astype(BF16)

    row = pl.BlockSpec((tm, D_MODEL), lambda m: (m, 0))
    return pl.pallas_call(kern, name=name, grid=(T // tm,),
                          in_specs=[row, _vec_spec(l), _vec_spec(l, ksc), _vec_spec(l, ksh)], out_specs=row,
                          out_shape=_sds((T, D_MODEL), BF16), compiler_params=_cparams("parallel"))(x, gain3, mod, mod)


def _normmod_bwd(name, x, dh, dres, gain3, l, mod, ksc):
    T = x.shape[0]
    tm = _tile(T, 512)
    nt = T // tm

    def kern(x_ref, dh_ref, dres_ref, g_ref, sc_ref, dx_ref, st_ref):
        m = pl.program_id(0)

        @pl.when(m == 0)
        def _():
            st_ref[...] = jnp.zeros_like(st_ref)

        xv = x_ref[...]
        dhv = dh_ref[...]
        r = lax.rsqrt(jnp.mean(xv * xv, axis=-1, keepdims=True) + EPS)
        xh = xv * r
        wv = g_ref[...] * (1.0 + sc_ref[...])
        dxh = dhv * wv
        dx_ref[...] = dres_ref[...] + r * (dxh - xh * jnp.mean(dxh * xh, axis=-1, keepdims=True))
        st_ref[0:1, :] += jnp.sum(dhv, axis=0, keepdims=True)
        st_ref[1:2, :] += jnp.sum(dhv * xh, axis=0, keepdims=True)

        @pl.when(m == nt - 1)
        def _():
            dw = st_ref[1:2, :]
            st_ref[2:3, :] = dw * (1.0 + sc_ref[...])
            st_ref[1:2, :] = dw * g_ref[...]

    row = pl.BlockSpec((tm, D_MODEL), lambda m: (m, 0))
    return pl.pallas_call(kern, name=name, grid=(nt,),
                          in_specs=[row, row, row, _vec_spec(l), _vec_spec(l, ksc)],
                          out_specs=[row, pl.BlockSpec((8, D_MODEL), lambda m: (0, 0))],
                          out_shape=[_sds((T, D_MODEL), F32), _sds((8, D_MODEL), F32)],
                          compiler_params=_cparams("arbitrary"))(x, dh, dres, gain3, mod)


def _gate_bwd(name, dxn, y, mod, l, kg):
    T = dxn.shape[0]
    tm = _tile(T, 512)

    def kern(d_ref, y_ref, g_ref, dy_ref, st_ref):
        @pl.when(pl.program_id(0) == 0)
        def _():
            st_ref[...] = jnp.zeros_like(st_ref)

        dv = d_ref[...]
        dy_ref[...] = (dv * g_ref[...]).astype(BF16)
        st_ref[0:1, :] += jnp.sum(dv * y_ref[...], axis=0, keepdims=True)

    row = pl.BlockSpec((tm, D_MODEL), lambda m: (m, 0))
    return pl.pallas_call(kern, name=name, grid=(T // tm,), in_specs=[row, row, _vec_spec(l, kg)],
                          out_specs=[row, pl.BlockSpec((8, D_MODEL), lambda m: (0, 0))],
                          out_shape=[_sds((T, D_MODEL), BF16), _sds((8, D_MODEL), F32)],
                          compiler_params=_cparams("arbitrary"))(dxn, y, mod)


def _loss_head(y, target):
    T = y.shape[0]
    tm = _tile(T, 512)

    def kern(y_ref, t_ref, dy_ref, acc_ref):
        @pl.when(pl.program_id(0) == 0)
        def _():
            acc_ref[...] = jnp.zeros_like(acc_ref)

        e = y_ref[...] - t_ref[...]
        dy_ref[...] = e * (1.0 / D_MODEL)
        s = jnp.sum(jnp.sum(e * e, axis=-1, keepdims=True), axis=0, keepdims=True)
        acc_ref[...] += s * (0.5 / D_MODEL)

    row = pl.BlockSpec((tm, D_MODEL), lambda m: (m, 0))
    return pl.pallas_call(kern, name="loss_head", grid=(T // tm,), in_specs=[row, row],
                          out_specs=[row, pl.BlockSpec((8, LANE), lambda m: (0, 0))],
                          out_shape=[_sds((T, D_MODEL), F32), _sds((8, LANE), F32)],
                          compiler_params=_cparams("arbitrary"))(y, target)


def _row_chunks(T):
    rc = min(ROW_CHUNK, T)
    return [(r * rc, rc) for r in range(T // rc)]


def _conv_fwd(name, proj, conv_w):
    T = proj.shape[0]
    nblk = CONV_DIM // LANE

    def kern(b_ref, c_ref, u_ref, w_ref, a_ref, zs):
        zs[0:8, :] = jnp.zeros((8, LANE), F32)
        for r0, rc in _row_chunks(T):
            zs[8 + r0:8 + r0 + rc, :] = c_ref[r0:r0 + rc, :] * u_ref[r0:r0 + rc, :]
        w0, w1, w2 = w_ref[0:1, :], w_ref[1:2, :], w_ref[2:3, :]
        for r0, rc in _row_chunks(T):
            yc = w2 * zs[8 + r0:8 + r0 + rc, :] + w1 * zs[7 + r0:7 + r0 + rc, :] + w0 * zs[6 + r0:6 + r0 + rc, :]
            a_ref[r0:r0 + rc, :] = (b_ref[r0:r0 + rc, :] * yc).astype(BF16)

    col = lambda p: pl.BlockSpec((T, LANE), lambda cb: (0, p * nblk + cb))
    return pl.pallas_call(kern, name=name, grid=(nblk,),
                          in_specs=[col(0), col(1), col(2), pl.BlockSpec((CONV_WIDTH, LANE), lambda cb: (0, cb))],
                          out_specs=pl.BlockSpec((T, LANE), lambda cb: (0, cb)),
                          out_shape=_sds((T, CONV_DIM), BF16), scratch_shapes=[pltpu.VMEM((T + 8, LANE), F32)],
                          compiler_params=_cparams("parallel"))(proj, proj, proj, conv_w)


def _conv_bwd(name, proj, conv_w, dcat):
    T = proj.shape[0]
    nblk = CONV_DIM // LANE

    def kern(b_ref, c_ref, u_ref, w_ref, da_ref, db_ref, dc_ref, du_ref, dw_ref, zs, ds):
        zs[0:8, :] = jnp.zeros((8, LANE), F32)
        ds[T:T + 8, :] = jnp.zeros((8, LANE), F32)
        for r0, rc in _row_chunks(T):
            zs[8 + r0:8 + r0 + rc, :] = c_ref[r0:r0 + rc, :] * u_ref[r0:r0 + rc, :]
        w0, w1, w2 = w_ref[0:1, :], w_ref[1:2, :], w_ref[2:3, :]
        acc = [jnp.zeros((1, LANE), F32) for _ in range(3)]
        for r0, rc in _row_chunks(T):
            z0 = zs[8 + r0:8 + r0 + rc, :]
            z1 = zs[7 + r0:7 + r0 + rc, :]
            z2 = zs[6 + r0:6 + r0 + rc, :]
            da = da_ref[r0:r0 + rc, :]
            db_ref[r0:r0 + rc, :] = (da * (w2 * z0 + w1 * z1 + w0 * z2)).astype(BF16)
            dyc = da * b_ref[r0:r0 + rc, :]
            ds[r0:r0 + rc, :] = dyc
            acc[2] = acc[2] + jnp.sum(dyc * z0, axis=0, keepdims=True)
            acc[1] = acc[1] + jnp.sum(dyc * z1, axis=0, keepdims=True)
            acc[0] = acc[0] + jnp.sum(dyc * z2, axis=0, keepdims=True)
        dw_ref[...] = jnp.zeros_like(dw_ref)
        for k in range(3):
            dw_ref[k:k + 1, :] = acc[k]
        for r0, rc in _row_chunks(T):
            dz = w2 * ds[r0:r0 + rc, :] + w1 * ds[r0 + 1:r0 + 1 + rc, :] + w0 * ds[r0 + 2:r0 + 2 + rc, :]
            dc_ref[r0:r0 + rc, :] = (dz * u_ref[r0:r0 + rc, :]).astype(BF16)
            du_ref[r0:r0 + rc, :] = (dz * c_ref[r0:r0 + rc, :]).astype(BF16)

    col = lambda p: pl.BlockSpec((T, LANE), lambda cb: (0, p * nblk + cb))
    out = pl.BlockSpec((T, LANE), lambda cb: (0, cb))
    return pl.pallas_call(kern, name=name, grid=(nblk,),
                          in_specs=[col(0), col(1), col(2), pl.BlockSpec((CONV_WIDTH, LANE), lambda cb: (0, cb)), out],
                          out_specs=[out, out, out, pl.BlockSpec((8, LANE), lambda cb: (0, cb))],
                          out_shape=[_sds((T, CONV_DIM), BF16)] * 3 + [_sds((8, CONV_DIM), F32)],
                          scratch_shapes=[pltpu.VMEM((T + 8, LANE), F32), pltpu.VMEM((T + 8, LANE), F32)],
                          compiler_params=_cparams("parallel"))(proj, proj, proj, conv_w, dcat)


_Q0, _K0, _V0, _G0 = 3 * CONV_DIM // LANE, (3 * CONV_DIM + RET_DIM) // LANE, (3 * CONV_DIM + 2 * RET_DIM) // LANE, \
    (3 * CONV_DIM + 3 * RET_DIM) // LANE


def _ret_tables(B, lg1):
    ti = lax.broadcasted_iota(jnp.int32, (B, B), 0)
    si = lax.broadcasted_iota(jnp.int32, (B, B), 1)
    dist = jnp.abs(ti - si).astype(F32)
    shift = RET_CHUNK.bit_length() - 1
    dmat = jnp.where((si >> shift) <= (ti >> shift), jnp.exp(dist * lg1), 0.0)
    tcol = lax.broadcasted_iota(jnp.int32, (B, 1), 0).astype(F32)
    qdec = jnp.exp((tcol + 1.0) * lg1)
    kdec = jnp.exp((B - 1.0 - tcol) * lg1)
    bdec = jnp.exp(float(B) * lg1)
    return dmat, qdec, kdec, bdec


def _retention_fwd(name, proj, cosf, sinf, gr, lgt):
    T = proj.shape[0]
    B = min(RET_BLOCK, T)
    nb = T // B

    def kern(q_ref, k_ref, v_ref, g_ref, cos_ref, sin_ref, gr_ref, lg_ref, r_ref, o_ref, st_ref, S):
        @pl.when(pl.program_id(1) == 0)
        def _():
            S[...] = jnp.zeros_like(S)

        cosv, sinv = cos_ref[...], sin_ref[...]
        rot = lambda a: a * cosv + pltpu.roll(a, HEAD // 2, 1) * sinv
        qr = rot(q_ref[...])
        kr = rot(k_ref[...]) * QK_SCALE
        v = v_ref[...]
        dmat, qdec, kdec, bdec = _ret_tables(B, lg_ref[0:1, 0:1])
        sv = S[...]
        st_ref[...] = sv
        pd = _dot(qr, kr, 1, 1) * dmat
        o = _dot(pd, v, 1, 0) + _dot(qr * qdec, sv, 1, 0)
        S[...] = bdec * sv + _dot(kr * kdec, v, 0, 0)
        o_ref[...] = o
        rs = lax.rsqrt(jnp.mean(o * o, axis=-1, keepdims=True) + EPS)
        g = g_ref[...]
        r_ref[...] = (g * _sig(g) * (o * rs * gr_ref[...])).astype(BF16)

    col = lambda c0: pl.BlockSpec((B, HEAD), lambda h, i: (i, c0 + h))
    tab = pl.BlockSpec((B, HEAD), lambda h, i: (i, 0))
    outc = pl.BlockSpec((B, HEAD), lambda h, i: (i, h))
    return pl.pallas_call(
        kern, name=name, grid=(RET_HEADS, nb),
        in_specs=[col(_Q0), col(_K0), col(_V0), col(_G0), tab, tab, pl.BlockSpec((1, HEAD), lambda h, i: (0, h)),
                  pl.BlockSpec((None, 1, LANE), lambda h, i: (h, 0, 0))],
        out_specs=[outc, outc, pl.BlockSpec((None, None, HEAD, HEAD), lambda h, i: (h, i, 0, 0))],
        out_shape=[_sds((T, RET_DIM), BF16), _sds((T, RET_DIM), F32), _sds((RET_HEADS, nb, HEAD, HEAD), F32)],
        scratch_shapes=[pltpu.VMEM((HEAD, HEAD), F32)],
        compiler_params=_cparams("parallel", "arbitrary"))(proj, proj, proj, proj, cosf, sinf, gr, lgt)


def _retention_bwd(name, proj, oraw, states, dcat, cosf, sinf, gr, lgt):
    T = proj.shape[0]
    B = min(RET_BLOCK, T)
    nb = T // B

    def kern(q_ref, k_ref, v_ref, g_ref, o_ref, dr_ref, st_ref, cos_ref, sin_ref, gr_ref, lg_ref,
             dq_ref, dk_ref, dv_ref, dg_ref, dgr_ref, dS):
        @pl.when(pl.program_id(1) == 0)
        def _():
            dS[...] = jnp.zeros_like(dS)
            dgr_ref[...] = jnp.zeros_like(dgr_ref)

        cosv, sinv = cos_ref[...], sin_ref[...]
        rot = lambda a: a * cosv + pltpu.roll(a, HEAD // 2, 1) * sinv
        rot_t = lambda a: a * cosv + pltpu.roll(a * sinv, HEAD // 2, 1)
        qr = rot(q_ref[...])
        kr = rot(k_ref[...]) * QK_SCALE
        v = v_ref[...]
        dmat, qdec, kdec, bdec = _ret_tables(B, lg_ref[0:1, 0:1])
        o = o_ref[...]
        rs = lax.rsqrt(jnp.mean(o * o, axis=-1, keepdims=True) + EPS)
        xh = o * rs
        g = g_ref[...]
        sg = _sig(g)
        grv = gr_ref[...]
        dr = dr_ref[...]
        dn = dr * (g * sg)
        dg_ref[...] = (dr * (xh * grv) * (sg * (1.0 + g * (1.0 - sg)))).astype(BF16)
        dgr_ref[0:1, :] += jnp.sum(dn * xh, axis=0, keepdims=True)
        dxh = dn * grv
        do = rs * (dxh - xh * jnp.mean(dxh * xh, axis=-1, keepdims=True))
        sp = st_ref[...]
        dsv = dS[...]
        pd = _dot(qr, kr, 1, 1) * dmat
        dp = _dot(do, v, 1, 1) * dmat
        dv_ref[...] = (_dot(pd, do, 0, 0) + _dot(kr * kdec, dsv, 1, 0)).astype(BF16)
        dqr = _dot(dp, kr, 1, 0) + _dot(do, sp, 1, 1) * qdec
        dkr = _dot(dp, qr, 0, 0) + _dot(v, dsv, 1, 1) * kdec
        dS[...] = bdec * dsv + _dot(qr * qdec, do, 0, 0)
        dq_ref[...] = rot_t(dqr).astype(BF16)
        dk_ref[...] = (rot_t(dkr) * QK_SCALE).astype(BF16)

    rev = lambda i: nb - 1 - i
    col = lambda c0: pl.BlockSpec((B, HEAD), lambda h, i: (rev(i), c0 + h))
    tab = pl.BlockSpec((B, HEAD), lambda h, i: (rev(i), 0))
    outc = pl.BlockSpec((B, HEAD), lambda h, i: (rev(i), h))
    return pl.pallas_call(
        kern, name=name, grid=(RET_HEADS, nb),
        in_specs=[col(_Q0), col(_K0), col(_V0), col(_G0), outc,
                  pl.BlockSpec((B, HEAD), lambda h, i: (rev(i), CONV_DIM // LANE + h)),
                  pl.BlockSpec((None, None, HEAD, HEAD), lambda h, i: (h, rev(i), 0, 0)), tab, tab,
                  pl.BlockSpec((1, HEAD), lambda h, i: (0, h)), pl.BlockSpec((None, 1, LANE), lambda h, i: (h, 0, 0))],
        out_specs=[outc, outc, outc, outc, pl.BlockSpec((8, HEAD), lambda h, i: (0, h))],
        out_shape=[_sds((T, RET_DIM), BF16)] * 4 + [_sds((8, RET_DIM), F32)],
        scratch_shapes=[pltpu.VMEM((HEAD, HEAD), F32)],
        compiler_params=_cparams("parallel", "arbitrary"))(proj, proj, proj, proj, oraw, dcat, states, cosf, sinf, gr, lgt)


def _strict_upper():
    r = lax.broadcasted_iota(jnp.int32, (SB_BLOCK, SB_BLOCK), 0)
    c = lax.broadcasted_iota(jnp.int32, (SB_BLOCK, SB_BLOCK), 1)
    return (r > c).astype(BF16), c - r


def _suffix_sum(vals, tri):
    hi = vals.astype(BF16)
    lo = (vals - hi.astype(F32)).astype(BF16)
    dn = (((1,), (0,)), ((), ()))
    return lax.dot_general(hi, tri, dn, preferred_element_type=F32) + lax.dot_general(lo, tri, dn, preferred_element_type=F32)


def _sb_scores(qi, kj, tri, col_minus_row, on_diag):
    z = _dot(qi, kj, 1, 1) * QK_SCALE
    lb = jnp.minimum(z, 0.0) - jnp.log(1.0 + jnp.exp(-jnp.abs(z)))
    valid = col_minus_row < jnp.where(on_diag, 0, SB_BLOCK)
    lk = jnp.where(valid, lb - z, 0.0)
    w_loc = jnp.where(valid, jnp.exp(lb + _suffix_sum(lk, tri)), 0.0)
    return lb, lk, valid, w_loc


def _head_norm_rows(src, gain, dst, T):
    for r0, rc in _row_chunks(T):
        a = src[r0:r0 + rc, :]
        r = lax.rsqrt(jnp.mean(a * a, axis=-1, keepdims=True) + EPS)
        dst[r0:r0 + rc, :] = (a * r * gain).astype(BF16)


def _sb_fwd(name, qkv, gq, gk):
    T = qkv.shape[0]
    nq = T // SB_BLOCK

    def kern(q_ref, k_ref, v_ref, gq_ref, gk_ref, o_ref, o32_ref, qn, kn, vb):
        _head_norm_rows(q_ref, gq_ref[...], qn, T)
        _head_norm_rows(k_ref, gk_ref[...], kn, T)
        for r0, rc in _row_chunks(T):
            vb[r0:r0 + rc, :] = v_ref[r0:r0 + rc, :].astype(BF16)
        tri, diag_mask = _strict_upper()

        def qblock(i, _):
            rows_i = pl.ds(pl.multiple_of(i * SB_BLOCK, SB_BLOCK), SB_BLOCK)
            qi = qn[rows_i, :]

            def kblock(jj, st):
                acc, car = st
                j = i - jj
                rows_j = pl.ds(pl.multiple_of(j * SB_BLOCK, SB_BLOCK), SB_BLOCK)
                _, lk, _, w = _sb_scores(qi, kn[rows_j, :], tri, diag_mask, jj == 0)
                w_hi = w.astype(BF16)
                w_lo = (w - w_hi.astype(F32)).astype(BF16)
                vj = vb[rows_j, :]
                acc = acc + jnp.exp(car) * (_dot(w_hi, vj, 1, 0) + _dot(w_lo, vj, 1, 0))
                return acc, car + jnp.sum(lk, axis=-1, keepdims=True)

            acc, _ = lax.fori_loop(0, i + 1, kblock, (jnp.zeros((SB_BLOCK, HEAD), F32), jnp.zeros((SB_BLOCK, 1), F32)))
            o_ref[rows_i, :] = acc.astype(BF16)
            o32_ref[rows_i, :] = acc
            return 0

        lax.fori_loop(0, nq, qblock, 0)

    col = lambda c0: pl.BlockSpec((T, HEAD), lambda h: (0, c0 + h))
    vec = pl.BlockSpec((1, HEAD), lambda h: (0, 0))
    out = pl.BlockSpec((T, HEAD), lambda h: (0, h))
    return pl.pallas_call(kern, name=name, grid=(SB_HEADS,),
                          in_specs=[col(0), col(SB_HEADS), col(2 * SB_HEADS), vec, vec], out_specs=[out, out],
                          out_shape=[_sds((T, D_MODEL), BF16), _sds((T, D_MODEL), F32)],
                          scratch_shapes=[pltpu.VMEM((T, HEAD), BF16)] * 3,
                          compiler_params=_cparams("parallel"))(qkv, qkv, qkv, gq, gk)


def _sb_bwd(name, qkv, gq, gk, o32, dcat):
    T = qkv.shape[0]
    nq = T // SB_BLOCK

    def kern(q_ref, k_ref, v_ref, gq_ref, gk_ref, o_ref, do_ref, dq_ref, dk_ref, dv_ref, dgq_ref, dgk_ref,
             qn, kn, vb, dqn, dkn, dvv):
        @pl.when(pl.program_id(0) == 0)
        def _():
            dgq_ref[...] = jnp.zeros_like(dgq_ref)
            dgk_ref[...] = jnp.zeros_like(dgk_ref)

        _head_norm_rows(q_ref, gq_ref[...], qn, T)
        _head_norm_rows(k_ref, gk_ref[...], kn, T)
        for r0, rc in _row_chunks(T):
            vb[r0:r0 + rc, :] = v_ref[r0:r0 + rc, :].astype(BF16)
            dkn[r0:r0 + rc, :] = jnp.zeros((rc, HEAD), F32)
            dvv[r0:r0 + rc, :] = jnp.zeros((rc, HEAD), F32)
        tri, diag_mask = _strict_upper()

        def qblock(i, _):
            rows_i = pl.ds(pl.multiple_of(i * SB_BLOCK, SB_BLOCK), SB_BLOCK)
            qi = qn[rows_i, :]
            doi = do_ref[rows_i, :]
            dob = doi.astype(BF16)
            etot = jnp.sum(dob.astype(F32) * o_ref[rows_i, :], axis=-1, keepdims=True)

            def kblock(jj, st):
                dq_acc, car, ecar = st
                j = i - jj
                rows_j = pl.ds(pl.multiple_of(j * SB_BLOCK, SB_BLOCK), SB_BLOCK)
                kj = kn[rows_j, :]
                vj = vb[rows_j, :]
                lb, lk, valid, w_loc = _sb_scores(qi, kj, tri, diag_mask, jj == 0)
                w = w_loc * jnp.exp(car)
                e = w * _dot(dob, vj, 1, 1)
                suff = _suffix_sum(e, tri) + e + ecar
                sig = jnp.exp(lb)
                dz = jnp.where(valid, e * (1.0 - sig) - sig * (etot - suff), 0.0) * QK_SCALE
                dzb = dz.astype(BF16)
                dkn[rows_j, :] += _dot(dzb, qi, 0, 0)
                dvv[rows_j, :] += _dot(w, dob, 0, 0)
                return (dq_acc + _dot(dzb, kj, 1, 0), car + jnp.sum(lk, axis=-1, keepdims=True),
                        ecar + jnp.sum(e, axis=-1, keepdims=True))

            zcol = jnp.zeros((SB_BLOCK, 1), F32)
            dq_acc, _, _ = lax.fori_loop(0, i + 1, kblock, (jnp.zeros((SB_BLOCK, HEAD), F32), zcol, zcol))
            dqn[rows_i, :] = dq_acc
            return 0

        lax.fori_loop(0, nq, qblock, 0)

        def norm_bwd(src, gain, dnorm, dst, dgain):
            tot = jnp.zeros((1, HEAD), F32)
            for r0, rc in _row_chunks(T):
                a = src[r0:r0 + rc, :]
                r = lax.rsqrt(jnp.mean(a * a, axis=-1, keepdims=True) + EPS)
                xh = a * r
                dn = dnorm[r0:r0 + rc, :]
                tot = tot + jnp.sum(dn * xh, axis=0, keepdims=True)
                dxh = dn * gain
                dst[r0:r0 + rc, :] = (r * (dxh - xh * jnp.mean(dxh * xh, axis=-1, keepdims=True))).astype(BF16)
            dgain[0:1, :] += tot

        norm_bwd(q_ref, gq_ref[...], dqn, dq_ref, dgq_ref)
        norm_bwd(k_ref, gk_ref[...], dkn, dk_ref, dgk_ref)
        for r0, rc in _row_chunks(T):
            dv_ref[r0:r0 + rc, :] = dvv[r0:r0 + rc, :].astype(BF16)

    col = lambda c0: pl.BlockSpec((T, HEAD), lambda h: (0, c0 + h))
    vec = pl.BlockSpec((1, HEAD), lambda h: (0, 0))
    out = pl.BlockSpec((T, HEAD), lambda h: (0, h))
    st = pl.BlockSpec((8, HEAD), lambda h: (0, 0))
    return pl.pallas_call(kern, name=name, grid=(SB_HEADS,),
                          in_specs=[col(0), col(SB_HEADS), col(2 * SB_HEADS), vec, vec, out, out],
                          out_specs=[out, out, out, st, st],
                          out_shape=[_sds((T, D_MODEL), BF16)] * 3 + [_sds((8, HEAD), F32)] * 2,
                          scratch_shapes=[pltpu.VMEM((T, HEAD), BF16)] * 3 + [pltpu.VMEM((T, HEAD), F32)] * 3,
                          compiler_params=_cparams("arbitrary"))(qkv, qkv, qkv, gq, gk, o32, dcat)


def _ada_fwd(c_all, ada_w, ada_b_cols):
    L, K, n = ada_w.shape

    def kern(c_ref, w_ref, b_ref, o_ref):
        cv = c_ref[...]
        o_ref[...] = _dot(cv * _sig(cv), w_ref[...], 1, 0) + b_ref[...]

    return pl.pallas_call(kern, name="ada_fwd", grid=(L,),
                          in_specs=[pl.BlockSpec((N_DEV, K), lambda l: (0, 0)), pl.BlockSpec((None, K, n), lambda l: (l, 0, 0)),
                                    pl.BlockSpec((None, 1, n), lambda l: (l, 0, 0))],
                          out_specs=pl.BlockSpec((None, N_DEV, n), lambda l: (l, 0, 0)),
                          out_shape=_sds((L, N_DEV, n), F32), compiler_params=_cparams("parallel"))(c_all, ada_w, ada_b_cols)


def _ada_wgrad(c_all_t, dmod_cols):
    K = c_all_t.shape[0]
    L, _, n = dmod_cols.shape
    tk = 128

    def kern(c_ref, d_ref, o_ref):
        cv = c_ref[...]
        ca = cv * _sig(cv)
        acc = ca[:, 0:1] * d_ref[0:1, :]
        for b in range(1, N_DEV):
            acc = acc + ca[:, b:b + 1] * d_ref[b:b + 1, :]
        o_ref[...] = acc

    return pl.pallas_call(kern, name="ada_wgrad", grid=(L, K // tk),
                          in_specs=[pl.BlockSpec((tk, N_DEV), lambda l, m: (m, 0)), pl.BlockSpec((None, N_DEV, n), lambda l, m: (l, 0, 0))],
                          out_specs=pl.BlockSpec((None, tk, n), lambda l, m: (l, m, 0)),
                          out_shape=_sds((L, K, n), F32), compiler_params=_cparams("parallel", "parallel"))(c_all_t, dmod_cols)


def _sum_devices(g):
    _, R, n = g.shape

    def kern(g_ref, o_ref):
        acc = g_ref[0]
        for d in range(1, N_DEV):
            acc = acc + g_ref[d]
        o_ref[...] = acc

    return pl.pallas_call(kern, name="sum_devices", out_shape=_sds((R, n), F32))(g)


def _pair_sum(name, g, recv, c_idx):
    n4, L, r, cdim = g.shape
    rows = (L // 2) * r
    tr = _tile(rows, 512)
    gv = g.reshape(n4, 2, rows, cdim)
    rv = recv.reshape(n4, rows, cdim)

    def kern(c_ref, g_ref, r_ref, o_ref):
        o_ref[...] = (g_ref[...] + r_ref[...]).astype(BF16)

    gs = pltpu.PrefetchScalarGridSpec(
        num_scalar_prefetch=1, grid=(n4, rows // tr),
        in_specs=[pl.BlockSpec((None, None, tr, cdim), lambda j, m, cr: (j, cr[0], m, 0)),
                  pl.BlockSpec((None, tr, cdim), lambda j, m, cr: (j, m, 0))],
        out_specs=pl.BlockSpec((None, tr, cdim), lambda j, m, cr: (j, m, 0)))
    out = pl.pallas_call(kern, name=name, grid_spec=gs, out_shape=_sds((n4, rows, cdim), BF16),
                         compiler_params=_cparams("parallel", "parallel"))(c_idx, gv, rv)
    return out.reshape(n4, L // 2, r, cdim)


def _chip_sum(name, recv, psum, pos_idx):
    n4, hl, r, cdim = recv.shape
    rows = hl * r
    tr = _tile(rows, 512)

    def kern(pos_ref, r0, r1, r2, r3, own_ref, o_ref):
        me = pos_ref[0]
        own = own_ref[...].astype(F32)
        terms = [jnp.where(me == s, own, rr[...].astype(F32)) for s, rr in enumerate((r0, r1, r2, r3))]
        o_ref[...] = ((terms[0] + terms[1]) + terms[2]) + terms[3]

    def slot(s):
        return pl.BlockSpec((None, tr, cdim), lambda m, pos: (jnp.where(pos[0] == s, (s + 1) % n4, s), m, 0))

    gs = pltpu.PrefetchScalarGridSpec(
        num_scalar_prefetch=1, grid=(rows // tr,),
        in_specs=[slot(s) for s in range(n4)] + [pl.BlockSpec((None, tr, cdim), lambda m, pos: (pos[0], m, 0))],
        out_specs=pl.BlockSpec((None, tr, cdim), lambda m, pos: (pos[1], m, 0)))
    rv = recv.reshape(n4, rows, cdim)
    out = pl.pallas_call(kern, name=name, grid_spec=gs, out_shape=_sds((2, rows, cdim), F32),
                         compiler_params=_cparams("parallel"))(pos_idx, rv, rv, rv, rv, psum.reshape(n4, rows, cdim))
    return out.reshape(2 * hl, r, cdim)


def _adamw(name, w, g, m, v):
    shape = w.shape
    cols = shape[-1]
    rows = int(np.prod(shape[:-1]))
    tr = _tile(rows, 512) if rows % 8 == 0 else rows
    c1 = 1.0 - ADAM_B1 ** ADAM_STEP
    c2 = 1.0 - ADAM_B2 ** ADAM_STEP

    def kern(w_ref, g_ref, m_ref, v_ref, d_ref, mo_ref, vo_ref):
        gv = g_ref[...]
        mn = ADAM_B1 * m_ref[...] + (1.0 - ADAM_B1) * gv
        vn = ADAM_B2 * v_ref[...] + (1.0 - ADAM_B2) * (gv * gv)
        mo_ref[...] = mn
        vo_ref[...] = vn
        d_ref[...] = -ADAM_LR * ((mn / c1) / (jnp.sqrt(vn / c2) + ADAM_EPS) + ADAM_WD * w_ref[...])

    blk = pl.BlockSpec((tr, cols), lambda i: (i, 0))
    outs = pl.pallas_call(kern, name=name, grid=(rows // tr,), in_specs=[blk] * 4, out_specs=[blk] * 3,
                          out_shape=[_sds((rows, cols), F32)] * 3, compiler_params=_cparams("parallel"))(
        *[a.reshape(rows, cols) for a in (w, g, m, v)])
    return tuple(o.reshape(shape) for o in outs)


def _rcopy(src, dst, ssem, rsem, dev):
    return pltpu.make_async_remote_copy(src_ref=src, dst_ref=dst, send_sem=ssem, recv_sem=rsem, device_id=dev,
                                        device_id_type=MESH)


def _gather8(name, blk):
    m_per, n = blk.shape

    def body(x_ref, out_ref, send_sems, recv_sems, local_sem):
        x, y, c = _position()
        me, sibling = (x, y, c), (x, y, 1 - c)
        chips = _other_chips(x, y)

        def rows(px, py, pc):
            return out_ref.at[pl.ds((4 * px + 2 * py + pc) * m_per, m_per), :]

        def copy(k, block, to, src=None):
            return _rcopy(rows(*block) if src is None else src, rows(*block), send_sems.at[k], recv_sems.at[k], to)

        mine = pltpu.make_async_copy(x_ref, rows(*me), local_sem)
        mine.start()
        first = [copy(0, me, sibling, src=x_ref)]
        first += [copy(1 + j, me, (*chip, c), src=x_ref) for j, chip in enumerate(chips)]
        for cp in first:
            cp.start()
        passed = [copy(4 + j, (*chip, c), sibling) for j, chip in enumerate(chips)]
        for j, chip in enumerate(chips):
            copy(1 + j, (*chip, c), me).wait_recv()
            passed[j].start()
        copy(0, sibling, me).wait_recv()
        for j, chip in enumerate(chips):
            copy(4 + j, (*chip, 1 - c), me).wait_recv()
        for cp in first + passed:
            cp.wait_send()
        mine.wait()

    return pl.pallas_call(body, name=name, out_shape=_sds((N_DEV * m_per, n), blk.dtype),
                          in_specs=[pl.BlockSpec(memory_space=pltpu.VMEM)], out_specs=pl.BlockSpec(memory_space=pltpu.VMEM),
                          scratch_shapes=[pltpu.SemaphoreType.DMA((7,)), pltpu.SemaphoreType.DMA((7,)), pltpu.SemaphoreType.DMA],
                          compiler_params=pltpu.CompilerParams(vmem_limit_bytes=VMEM_LIMIT_V7X))(blk)


_ANY = pl.BlockSpec(memory_space=pl.ANY)


def _gather_weights(shards):
    n = len(shards)

    def body(*refs):
        src, out = refs[:n], refs[n:2 * n]
        send_sems, recv_sems, fsend_sems, frecv_sems = refs[2 * n:]
        x, y, c = _position()
        sibling = (x, y, 1 - c)
        chips = _other_chips(x, y)
        me_chip = 2 * x + y
        started = []
        for t in range(n):
            hl = src[t].shape[0] // 2
            mine_half = pl.ds(c * hl, hl)
            for j, (px, py) in enumerate(chips):
                cp = _rcopy(src[t].at[mine_half], out[t].at[me_chip, mine_half], send_sems.at[3 * t + j],
                            recv_sems.at[3 * t + j], (px, py, c))
                cp.start()
                started.append(cp)
        for t in range(n):
            hl = src[t].shape[0] // 2
            mine_half = pl.ds(c * hl, hl)
            for j, (px, py) in enumerate(chips):
                landed = out[t].at[2 * px + py, mine_half]
                _rcopy(landed, landed, send_sems.at[3 * t + j], recv_sems.at[3 * t + j], (px, py, c)).wait_recv()
                fw = _rcopy(landed, landed, fsend_sems.at[3 * t + j], frecv_sems.at[3 * t + j], sibling)
                fw.start()
                started.append(fw)
        for t in range(n):
            hl = src[t].shape[0] // 2
            other_half = pl.ds((1 - c) * hl, hl)
            for j, (px, py) in enumerate(chips):
                landed = out[t].at[2 * px + py, other_half]
                _rcopy(landed, landed, fsend_sems.at[3 * t + j], frecv_sems.at[3 * t + j], sibling).wait_recv()
        for cp in started:
            cp.wait_send()

    sems = [pltpu.SemaphoreType.DMA((3 * n,))] * 4
    return pl.pallas_call(body, name="gather_weights", out_shape=[_sds((N_CHIP,) + s.shape, s.dtype) for s in shards],
                          in_specs=[_ANY] * n, out_specs=[_ANY] * n, scratch_shapes=sems)(*shards)


def _place_own(name, w4, shard, pos_idx):
    n4, L, r, cdim = w4.shape
    rows = L * r
    tr = _tile(rows, 1024)

    def kern(pos_ref, s_ref, w_ref, o_ref):
        o_ref[...] = s_ref[...]

    gs = pltpu.PrefetchScalarGridSpec(
        num_scalar_prefetch=1, grid=(rows // tr,),
        in_specs=[pl.BlockSpec((tr, cdim), lambda m, pos: (m, 0)), _ANY],
        out_specs=pl.BlockSpec((None, tr, cdim), lambda m, pos: (pos[0], m, 0)))
    out = pl.pallas_call(kern, name=name, grid_spec=gs, out_shape=_sds((n4, rows, cdim), w4.dtype),
                         input_output_aliases={2: 0}, compiler_params=_cparams("parallel"))(
        pos_idx, shard.reshape(rows, cdim), w4.reshape(n4, rows, cdim))
    return out.reshape(w4.shape)


def _exchange_pair(grads):
    n = len(grads)

    def body(*refs):
        src, out = refs[:n], refs[n:2 * n]
        send_sems, recv_sems = refs[2 * n:]
        x, y, c = _position()
        sibling = (x, y, 1 - c)
        cps = []
        for t in range(n):
            hl = src[t].shape[1] // 2
            cp = _rcopy(src[t].at[:, pl.ds((1 - c) * hl, hl)], out[t], send_sems.at[t], recv_sems.at[t], sibling)
            cp.start()
            cps.append(cp)
        for cp in cps:
            cp.wait_recv()
        for cp in cps:
            cp.wait_send()

    sems = [pltpu.SemaphoreType.DMA((n,))] * 2
    return pl.pallas_call(body, name="exchange_pair",
                          out_shape=[_sds((g.shape[0], g.shape[1] // 2) + g.shape[2:], g.dtype) for g in grads],
                          in_specs=[_ANY] * n, out_specs=[_ANY] * n, scratch_shapes=sems)(*grads)


def _scatter_chips(psums):
    n = len(psums)

    def body(*refs):
        src, out = refs[:n], refs[n:2 * n]
        send_sems, recv_sems = refs[2 * n:]
        x, y, c = _position()
        chips = _other_chips(x, y)
        me_chip = 2 * x + y
        cps = []
        for t in range(n):
            for j, (px, py) in enumerate(chips):
                cp = _rcopy(src[t].at[2 * px + py], out[t].at[me_chip], send_sems.at[3 * t + j], recv_sems.at[3 * t + j],
                            (px, py, c))
                cp.start()
                cps.append(cp)
        for t in range(n):
            for j, (px, py) in enumerate(chips):
                slot = out[t].at[2 * px + py]
                _rcopy(slot, slot, send_sems.at[3 * t + j], recv_sems.at[3 * t + j], (px, py, c)).wait_recv()
        for cp in cps:
            cp.wait_send()

    sems = [pltpu.SemaphoreType.DMA((3 * n,))] * 2
    return pl.pallas_call(body, name="scatter_chips", out_shape=[_sds(p.shape, p.dtype) for p in psums],
                          in_specs=[_ANY] * n, out_specs=[_ANY] * n, scratch_shapes=sems)(*psums)


def _share_halves(full):
    n = len(full)

    def body(*refs):
        out = refs[n:2 * n]
        send_sems, recv_sems = refs[2 * n:]
        x, y, c = _position()
        sibling = (x, y, 1 - c)
        cps = []
        for t in range(n):
            hl = out[t].shape[0] // 2
            mine = out[t].at[pl.ds(c * hl, hl)]
            cp = _rcopy(mine, mine, send_sems.at[t], recv_sems.at[t], sibling)
            cp.start()
            cps.append(cp)
        for t in range(n):
            hl = out[t].shape[0] // 2
            theirs = out[t].at[pl.ds((1 - c) * hl, hl)]
            _rcopy(theirs, theirs, send_sems.at[t], recv_sems.at[t], sibling).wait_recv()
        for cp in cps:
            cp.wait_send()

    sems = [pltpu.SemaphoreType.DMA((n,))] * 2
    return pl.pallas_call(body, name="share_halves", out_shape=[_sds(h.shape, h.dtype) for h in full],
                          in_specs=[_ANY] * n, out_specs=[_ANY] * n, scratch_shapes=sems,
                          input_output_aliases={t: t for t in range(n)})(*full)


def _rope_tables(T):
    inv_freq = 1.0 / (ROPE_THETA ** (jnp.arange(0, HEAD, 2, dtype=F32) / HEAD))
    ang = jnp.arange(T, dtype=F32)[:, None] * inv_freq[None, :]
    cos, sin = jnp.cos(ang), jnp.sin(ang)
    return jnp.concatenate([cos, cos], axis=-1), jnp.concatenate([-sin, sin], axis=-1)


def _decay_table():
    lg = np.log1p(-np.exp2(-5.0 - np.arange(RET_HEADS, dtype=np.float32))).astype(np.float32)
    return jnp.asarray(np.broadcast_to(lg[:, None, None], (RET_HEADS, 1, LANE)).copy())


def _local_step(x0, target, mod, W, G, norm_mix_g, norm_ffn_g, conv_full, ev_ret_norm_g, od_q_norm_g, od_k_norm_g):
    T = x0.shape[0]
    KSH1, KSC1, KG1, KSH2, KSC2, KG2 = range(6)
    gain_mix = norm_mix_g.reshape(DEPTH, 1, D_MODEL)
    gain_ffn = norm_ffn_g.reshape(DEPTH, 1, D_MODEL)
    cosf, sinf = _rope_tables(T)
    lgt = _decay_table()

    saved = []
    xcur = x0
    for l in range(DEPTH):
        j = l // 2
        s = dict(x_in=xcur)
        h = _normmod(f"norm_mix_{l}", xcur, gain_mix, l, mod, KSC1, KSH1)
        s["h"] = h
        if l % 2 == 0:
            proj = _proj_cols(f"ev_in_{l}", h, W["ev_w_in"], j)
            a = _conv_fwd(f"conv_{l}", proj, conv_full[j])
            r, oraw, states = _retention_fwd(f"ret_{l}", proj, cosf, sinf, ev_ret_norm_g[j].reshape(1, RET_DIM), lgt)
            cat = jnp.concatenate([a, r], axis=1)
            s.update(proj=proj, oraw=oraw, states=states, cat=cat)
            y, xmid = _out_proj(f"ev_out_{l}", cat, False, W["ev_w_out"], j, xcur, mod, l, KG1)
        else:
            qkv = _proj_cols(f"od_in_{l}", h, W["od_w_qkv"], j)
            o, o32 = _sb_fwd(f"sb_{l}", qkv, od_q_norm_g[j].reshape(1, HEAD), od_k_norm_g[j].reshape(1, HEAD))
            s.update(qkv=qkv, cat=o, o32=o32)
            y, xmid = _out_proj(f"od_out_{l}", o, False, W["od_w_out"], j, xcur, mod, l, KG1)
        s.update(y1=y, x_mid=xmid)
        h2 = _normmod(f"norm_ffn_{l}", xmid, gain_ffn, l, mod, KSC2, KSH2)
        gate, up, act = _ffn_up(f"ffn_up_{l}", h2, W["ffn_w_gate"], W["ffn_w_up"], l)
        y2, xcur = _out_proj(f"ffn_down_{l}", act, True, W["ffn_w_down"], l, xmid, mod, l, KG2)
        s.update(h2=h2, gate=gate, up=up, act=act, y2=y2)
        saved.append(s)

    dy, lacc = _loss_head(xcur, target)

    dmod_rows = [None] * DEPTH
    d_mix = [None] * DEPTH
    d_ffn = [None] * DEPTH
    d_conv = [None] * 2
    d_ret = [None] * 2
    d_gq = [None] * 2
    d_gk = [None] * 2
    dx = dy
    for l in reversed(range(DEPTH)):
        j = l // 2
        s = saved[l]
        dyg, st_g2 = _gate_bwd(f"gate2_bwd_{l}", dx, s["y2"], mod, l, KG2)
        G["ffn_w_down"] = _wgrad(f"wg_down_{l}", s["act"], "stack", dyg, "full", G["ffn_w_down"], l)
        dgate, dup = _ffn_down_bwd(f"ffn_down_bwd_{l}", dyg, W["ffn_w_down"], l, s["gate"], s["up"])
        G["ffn_w_gate"] = _wgrad(f"wg_gate_{l}", s["h2"], "full", dgate, "stack", G["ffn_w_gate"], l)
        G["ffn_w_up"] = _wgrad(f"wg_up_{l}", s["h2"], "full", dup, "stack", G["ffn_w_up"], l)
        dh2 = _ffn_up_bwd(f"ffn_up_bwd_{l}", dgate, dup, W["ffn_w_gate"], W["ffn_w_up"], l)
        dxm, st_n2 = _normmod_bwd(f"norm_ffn_bwd_{l}", s["x_mid"], dh2, dx, gain_ffn, l, mod, KSC2)
        dyg1, st_g1 = _gate_bwd(f"gate1_bwd_{l}", dxm, s["y1"], mod, l, KG1)
        if l % 2 == 0:
            G["ev_w_out"] = _wgrad(f"wg_evout_{l}", s["cat"], "cols", dyg1, "full", G["ev_w_out"], j)
            dcat = _bwd_rows(f"ev_out_bwd_{l}", dyg1, W["ev_w_out"], j)
            db, dcg, du, dwc = _conv_bwd(f"conv_bwd_{l}", s["proj"], conv_full[j], dcat)
            dq, dk, dv, dg, dgr = _retention_bwd(f"ret_bwd_{l}", s["proj"], s["oraw"], s["states"], dcat, cosf, sinf,
                                                 ev_ret_norm_g[j].reshape(1, RET_DIM), lgt)
            dproj = jnp.concatenate([db, dcg, du, dq, dk, dv, dg], axis=1)
            d_conv[j], d_ret[j] = dwc[:CONV_WIDTH], dgr[0]
            G["ev_w_in"] = _wgrad(f"wg_evin_{l}", s["h"], "full", dproj, "cols", G["ev_w_in"], j)
            dh = _bwd_cols(f"ev_in_bwd_{l}", dproj, W["ev_w_in"], j)
        else:
            G["od_w_out"] = _wgrad(f"wg_odout_{l}", s["cat"], "cols", dyg1, "full", G["od_w_out"], j)
            dcat = _bwd_rows(f"od_out_bwd_{l}", dyg1, W["od_w_out"], j)
            dq, dk, dv, dgq, dgk = _sb_bwd(f"sb_bwd_{l}", s["qkv"], od_q_norm_g[j].reshape(1, HEAD),
                                           od_k_norm_g[j].reshape(1, HEAD), s["o32"], dcat)
            dproj = jnp.concatenate([dq, dk, dv], axis=1)
            d_gq[j], d_gk[j] = dgq[0], dgk[0]
            G["od_w_qkv"] = _wgrad(f"wg_odin_{l}", s["h"], "full", dproj, "cols", G["od_w_qkv"], j)
            dh = _bwd_cols(f"od_in_bwd_{l}", dproj, W["od_w_qkv"], j)
        dx, st_n1 = _normmod_bwd(f"norm_mix_bwd_{l}", s["x_in"], dh, dxm, gain_mix, l, mod, KSC1)
        dmod_rows[l] = jnp.stack([st_n1[0], st_n1[1], st_g1[0], st_n2[0], st_n2[1], st_g2[0]]).reshape(6 * D_MODEL)
        d_mix[l], d_ffn[l] = st_n1[2], st_n2[2]
    return lacc, dx, G, (dmod_rows, d_mix, d_ffn, d_ret, d_gq, d_gk, d_conv)


def kernel(x, c, ada_w, ada_b, norm_mix_g, norm_ffn_g, ev_w_in, ev_conv_w, ev_ret_norm_g, ev_w_out, od_w_qkv, od_q_norm_g, od_k_norm_g, od_w_out, ffn_w_gate, ffn_w_up, ffn_w_down, loss_target, m_ada_w, m_ada_b, m_norm_mix_g, m_norm_ffn_g, m_ev_w_in, m_ev_conv_w, m_ev_ret_norm_g, m_ev_w_out, m_od_w_qkv, m_od_q_norm_g, m_od_k_norm_g, m_od_w_out, m_ffn_w_gate, m_ffn_w_up, m_ffn_w_down, v_ada_w, v_ada_b, v_norm_mix_g, v_norm_ffn_g, v_ev_w_in, v_ev_conv_w, v_ev_ret_norm_g, v_ev_w_out, v_od_w_qkv, v_od_q_norm_g, v_od_k_norm_g, v_od_w_out, v_ffn_w_gate, v_ffn_w_up, v_ffn_w_down):
    xi, yi, ci = _position()
    chip = 2 * xi + yi
    dev = 4 * xi + 2 * yi + ci
    x0 = x[0]
    target = loss_target[0]

    n_small = D_MODEL + 2 * CONV_WIDTH * LANE
    small = jnp.concatenate([c.reshape(1, D_MODEL), ev_conv_w.reshape(1, 2 * CONV_WIDTH * LANE)], axis=1)
    small = jnp.broadcast_to(small, (8, n_small))
    g1 = _gather8("gather_cond", small).reshape(N_DEV, 8, n_small)[:, 0, :]
    c_all = g1[:, :D_MODEL]
    conv_all = g1[0::2, D_MODEL:].reshape(N_CHIP, 2, CONV_WIDTH, LANE)
    conv_full = conv_all.transpose(1, 2, 0, 3).reshape(2, CONV_WIDTH, CONV_DIM)

    n_ada = ada_w.shape[-1]
    ada_b_cols = lax.dynamic_slice_in_dim(ada_b, chip * n_ada, n_ada, axis=1).reshape(DEPTH, 1, n_ada)
    mod_cols = _ada_fwd(c_all, ada_w, ada_b_cols)
    g2 = _gather8("gather_mod", mod_cols.reshape(DEPTH * N_DEV, n_ada)).reshape(N_DEV, DEPTH, N_DEV, n_ada)
    mod_mine = lax.dynamic_index_in_dim(g2[0::2], dev, axis=2, keepdims=False)
    mod = mod_mine.transpose(1, 0, 2).reshape(DEPTH, 6, 1, D_MODEL)

    big_names = ["ev_w_in", "ev_w_out", "od_w_qkv", "od_w_out", "ffn_w_gate", "ffn_w_up", "ffn_w_down"]
    big = dict(ev_w_in=ev_w_in, ev_w_out=ev_w_out, od_w_qkv=od_w_qkv, od_w_out=od_w_out, ffn_w_gate=ffn_w_gate,
               ffn_w_up=ffn_w_up, ffn_w_down=ffn_w_down)
    pos_idx = jnp.stack([chip, ci]).astype(jnp.int32)
    shards = [big[k].astype(BF16) for k in big_names]
    W = {k: _place_own(f"place_{k}", w4, s, pos_idx) for k, w4, s in zip(big_names, _gather_weights(shards), shards)}
    G = {k: lax.empty((N_CHIP,) + big[k].shape, F32) for k in big_names}

    lacc, dx, G, small_grads = _local_step(x0, target, mod, W, G, norm_mix_g, norm_ffn_g, conv_full, ev_ret_norm_g,
                                           od_q_norm_g, od_k_norm_g)
    loss = lax.psum(lacc[0, 0], ("x", "y", "c"))
    grad_x = dx[None]
    dmod_rows, d_mix, d_ffn, d_ret, d_gq, d_gk, d_conv = small_grads

    c_idx = jnp.reshape(ci, (1,)).astype(jnp.int32)
    glist = [G[k] for k in big_names]
    recv_a = _exchange_pair(glist)
    psums = [_pair_sum(f"pair_sum_{k}", g, r, c_idx) for k, g, r in zip(big_names, glist, recv_a)]
    recv_b = _scatter_chips(psums)
    halves = [_chip_sum(f"chip_sum_{k}", r, p, pos_idx) for k, r, p in zip(big_names, recv_b, psums)]
    grads = dict(zip(big_names, _share_halves(halves)))

    pieces = [jnp.stack(dmod_rows).reshape(-1), jnp.stack(d_mix).reshape(-1), jnp.stack(d_ffn).reshape(-1),
              jnp.stack(d_ret).reshape(-1), jnp.stack(d_gq).reshape(-1), jnp.stack(d_gk).reshape(-1),
              jnp.stack(d_conv).reshape(-1)]
    sizes = [int(p.shape[0]) for p in pieces]
    n_pack = sum(sizes)
    n_cols = -(-n_pack // (8 * LANE)) * LANE
    packed = jnp.concatenate(pieces + [jnp.zeros((8 * n_cols - n_pack,), F32)]).reshape(8, n_cols)
    g3 = _gather8("gather_small", packed).reshape(N_DEV, 8, n_cols)
    tot = _sum_devices(g3).reshape(-1)
    offs = np.cumsum([0] + sizes)
    part = [tot[offs[i]:offs[i + 1]] for i in range(len(sizes))]
    grads["ada_b"] = part[0].reshape(DEPTH, 6 * D_MODEL)
    grads["norm_mix_g"] = part[1].reshape(DEPTH, D_MODEL)
    grads["norm_ffn_g"] = part[2].reshape(DEPTH, D_MODEL)
    grads["ev_ret_norm_g"] = part[3].reshape(2, RET_DIM)
    grads["od_q_norm_g"] = part[4].reshape(2, HEAD)
    grads["od_k_norm_g"] = part[5].reshape(2, HEAD)
    conv_g = part[6].reshape(2, CONV_WIDTH, CONV_DIM)
    grads["ev_conv_w"] = lax.dynamic_slice_in_dim(conv_g, chip * LANE, LANE, axis=2)
    dmod_all = g3.reshape(N_DEV, -1)[:, :DEPTH * 6 * D_MODEL].reshape(N_DEV, DEPTH, 6 * D_MODEL)
    dmod_cols = lax.dynamic_slice_in_dim(dmod_all, chip * n_ada, n_ada, axis=2).transpose(1, 0, 2)
    grads["ada_w"] = _ada_wgrad(c_all.T, dmod_cols)

    weights = dict(ada_w=ada_w, ada_b=ada_b, norm_mix_g=norm_mix_g, norm_ffn_g=norm_ffn_g, ev_w_in=ev_w_in,
                   ev_conv_w=ev_conv_w, ev_ret_norm_g=ev_ret_norm_g, ev_w_out=ev_w_out, od_w_qkv=od_w_qkv,
                   od_q_norm_g=od_q_norm_g, od_k_norm_g=od_k_norm_g, od_w_out=od_w_out, ffn_w_gate=ffn_w_gate,
                   ffn_w_up=ffn_w_up, ffn_w_down=ffn_w_down)
    m_in = dict(ada_w=m_ada_w, ada_b=m_ada_b, norm_mix_g=m_norm_mix_g, norm_ffn_g=m_norm_ffn_g, ev_w_in=m_ev_w_in,
                ev_conv_w=m_ev_conv_w, ev_ret_norm_g=m_ev_ret_norm_g, ev_w_out=m_ev_w_out, od_w_qkv=m_od_w_qkv,
                od_q_norm_g=m_od_q_norm_g, od_k_norm_g=m_od_k_norm_g, od_w_out=m_od_w_out, ffn_w_gate=m_ffn_w_gate,
                ffn_w_up=m_ffn_w_up, ffn_w_down=m_ffn_w_down)
    v_in = dict(ada_w=v_ada_w, ada_b=v_ada_b, norm_mix_g=v_norm_mix_g, norm_ffn_g=v_norm_ffn_g, ev_w_in=v_ev_w_in,
                ev_conv_w=v_ev_conv_w, ev_ret_norm_g=v_ev_ret_norm_g, ev_w_out=v_ev_w_out, od_w_qkv=v_od_w_qkv,
                od_q_norm_g=v_od_q_norm_g, od_k_norm_g=v_od_k_norm_g, od_w_out=v_od_w_out, ffn_w_gate=v_ffn_w_gate,
                ffn_w_up=v_ffn_w_up, ffn_w_down=v_ffn_w_down)
    order = list(weights)
    deltas, new_m, new_v = {}, {}, {}
    for k in order:
        deltas[k], new_m[k], new_v[k] = _adamw(f"adamw_{k}", weights[k], grads[k], m_in[k], v_in[k])
    return (loss, grad_x, *[grads[k] for k in order], *[deltas[k] for k in order], *[new_m[k] for k in order],
            *[new_v[k] for k in order])
```

```python
import functools

import numpy as np
import jax
import jax.numpy as jnp
from jax import lax
from jax.experimental import pallas as pl
from jax.experimental.pallas import tpu as pltpu

F32 = jnp.float32
BF16 = jnp.bfloat16
MESH = pl.DeviceIdType.MESH

D_MODEL = 1024
DEPTH = 4
N_CHIP = 4
N_DEV = 8
HEAD = 128
RET_HEADS = 4
SB_HEADS = 8
CONV_DIM = 512
RET_DIM = 512
CONV_WIDTH = 3
RET_CHUNK = 64
RET_BLOCK = 256
SB_BLOCK = 256
EPS = 1e-6
ROPE_THETA = 10000.0
QK_SCALE = HEAD ** -0.5
LANE = 128
ROW_CHUNK = 512
ROWS_STREAMED = 1024
VMEM_LIMIT_V7X = 56 * 1024 * 1024

ADAM_LR, ADAM_B1, ADAM_B2, ADAM_EPS, ADAM_WD, ADAM_STEP = 0.001, 0.9, 0.999, 1e-08, 0.01, 10


def _cparams(*sem):
    return pltpu.CompilerParams(dimension_semantics=sem or None, vmem_limit_bytes=VMEM_LIMIT_V7X)


def _tile(n, pref):
    if n <= pref:
        return n
    for t in range(pref - pref % 8, 7, -8):
        if n % t == 0:
            return t
    return n


def _sig(v):
    return 1.0 / (1.0 + jnp.exp(-v))


def _dot(a, b, ca, cb):
    return lax.dot_general(a.astype(BF16), b.astype(BF16), (((ca,), (cb,)), ((), ())),
                           preferred_element_type=F32)


def _position():
    x, y, c = lax.axis_index("x"), lax.axis_index("y"), lax.axis_index("c")
    return x, y, c


def _other_chips(x, y):
    return [(1 - x, y), (x, 1 - y), (1 - x, 1 - y)]


def _mm(name, pairs, out_sds, out_specs, grid, contract, red_axis=None, post=None,
        extras=(), extra_specs=(), sum_pairs=True, aliases=None, inner=None):
    n_p, n_ex, n_out = len(pairs), len(extras), len(out_sds)
    n_acc = 1 if sum_pairs else n_p
    n_red = grid[red_axis] if red_axis is not None else 1

    def default_post(accs, ex, outs):
        outs[0][...] = accs[0].astype(outs[0].dtype)

    post_fn = post or default_post

    def kern(*refs):
        ab = refs[:2 * n_p]
        ex = refs[2 * n_p:2 * n_p + n_ex]
        outs = refs[2 * n_p + n_ex:2 * n_p + n_ex + n_out]
        accs = refs[2 * n_p + n_ex + n_out:]
        if inner is None:
            prods = [_dot(ab[2 * p][...], ab[2 * p + 1][...], contract[0], contract[1]) for p in range(n_p)]
        else:
            n_in, a_get, b_get = inner
            prods = []
            for p in range(n_p):
                tot = _dot(a_get(ab[2 * p], 0), b_get(ab[2 * p + 1], 0), contract[0], contract[1])
                for i in range(1, n_in):
                    tot = tot + _dot(a_get(ab[2 * p], i), b_get(ab[2 * p + 1], i), contract[0], contract[1])
                prods.append(tot)
        if sum_pairs:
            tot = prods[0]
            for p_ in prods[1:]:
                tot = tot + p_
            prods = [tot]
        if red_axis is None:
            post_fn(prods, ex, outs)
        else:
            k = pl.program_id(red_axis)

            @pl.when(k == 0)
            def _():
                for a_, p_ in zip(accs, prods):
                    a_[...] = p_

            @pl.when(k > 0)
            def _():
                for a_, p_ in zip(accs, prods):
                    a_[...] += p_

            @pl.when(k == n_red - 1)
            def _():
                post_fn([a_[...] for a_ in accs], ex, outs)

    ins, in_specs = [], []
    for a, b, sa, sb in pairs:
        ins += [a, b]
        in_specs += [sa, sb]
    ins += list(extras)
    in_specs += list(extra_specs)
    scratch = []
    if red_axis is not None:
        scratch = [pltpu.VMEM(tuple(acc_shape), F32) for acc_shape in [_acc_shape(pairs[0], contract)] * n_acc]
    sem = tuple("arbitrary" if ax == red_axis else "parallel" for ax in range(len(grid)))
    res = pl.pallas_call(kern, name=name, grid=grid, in_specs=in_specs, out_specs=list(out_specs),
                         out_shape=list(out_sds), scratch_shapes=scratch,
                         input_output_aliases=aliases or {}, compiler_params=_cparams(*sem))(*ins)
    return res


def _acc_shape(pair, contract):
    sa, sb = pair[2], pair[3]
    da = [d for d in sa.block_shape if d is not None]
    db = [d for d in sb.block_shape if d is not None]
    return (da[1 - contract[0]], db[1 - contract[1]])


def _sds(shape, dtype):
    return jax.ShapeDtypeStruct(tuple(shape), dtype)


def _proj_cols(name, h, w4, l):
    T, K = h.shape
    n = w4.shape[-1]
    tm = _tile(T, ROWS_STREAMED)
    return _mm(name, [(h, w4, pl.BlockSpec((tm, K), lambda i, m: (m, 0)),
                       pl.BlockSpec((None, None, K, n), lambda i, m: (i, l, 0, 0)))],
               [_sds((T, N_CHIP * n), F32)], [pl.BlockSpec((tm, n), lambda i, m: (m, i))],
               (N_CHIP, T // tm), (1, 0))[0]


def _out_proj(name, a, a_stacked, w4, l, xres, mod, lm, kg):
    T = xres.shape[0]
    k = w4.shape[-2]
    tm = _tile(T, 512)
    if a_stacked:
        sa = pl.BlockSpec((N_CHIP, tm, k), lambda m: (0, m, 0))
        a_get = lambda ref, i: ref[i]
    else:
        sa = pl.BlockSpec((tm, N_CHIP * k), lambda m: (m, 0))
        a_get = lambda ref, i: ref[:, i * k:(i + 1) * k]

    def post(accs, ex, outs):
        y = accs[0]
        outs[0][...] = y
        outs[1][...] = ex[0][...] + ex[1][...] * y

    row = pl.BlockSpec((tm, D_MODEL), lambda m: (m, 0))
    return _mm(name, [(a, w4, sa, pl.BlockSpec((N_CHIP, None, k, D_MODEL), lambda m: (0, l, 0, 0)))],
               [_sds((T, D_MODEL), F32)] * 2, [row, row], (T // tm,), (1, 0), post=post,
               extras=[xres, mod], extra_specs=[row, pl.BlockSpec((None, None, 1, D_MODEL), lambda m: (lm, kg, 0, 0))],
               inner=(N_CHIP, a_get, lambda ref, i: ref[i]))


def _ffn_up(name, h2, wg4, wu4, l):
    T, K = h2.shape
    n = wg4.shape[-1]
    tm = _tile(T, ROWS_STREAMED)

    def post(accs, ex, outs):
        g, u = accs
        outs[0][...] = g
        outs[1][...] = u
        outs[2][...] = (g * _sig(g) * u).astype(BF16)

    sa = pl.BlockSpec((tm, K), lambda i, m: (m, 0))
    sw = pl.BlockSpec((None, None, K, n), lambda i, m: (i, l, 0, 0))
    so = pl.BlockSpec((None, tm, n), lambda i, m: (i, m, 0))
    return _mm(name, [(h2, wg4, sa, sw), (h2, wu4, sa, sw)],
               [_sds((N_CHIP, T, n), F32), _sds((N_CHIP, T, n), F32), _sds((N_CHIP, T, n), BF16)], [so, so, so],
               (N_CHIP, T // tm), (1, 0), post=post, sum_pairs=False)


def _bwd_cols(name, dproj, w4, l):
    T = dproj.shape[0]
    K, n = w4.shape[-2:]
    tm = _tile(T, 512)
    return _mm(name, [(dproj, w4, pl.BlockSpec((tm, N_CHIP * n), lambda m: (m, 0)),
                       pl.BlockSpec((N_CHIP, None, K, n), lambda m: (0, l, 0, 0)))],
               [_sds((T, K), F32)], [pl.BlockSpec((tm, K), lambda m: (m, 0))], (T // tm,), (1, 1),
               inner=(N_CHIP, lambda ref, i: ref[:, i * n:(i + 1) * n], lambda ref, i: ref[i]))[0]


def _bwd_rows(name, dy, w4, l):
    T, N = dy.shape
    k = w4.shape[-2]
    tm = _tile(T, ROWS_STREAMED)
    return _mm(name, [(dy, w4, pl.BlockSpec((tm, N), lambda i, m: (m, 0)),
                       pl.BlockSpec((None, None, k, N), lambda i, m: (i, l, 0, 0)))],
               [_sds((T, N_CHIP * k), F32)], [pl.BlockSpec((tm, k), lambda i, m: (m, i))],
               (N_CHIP, T // tm), (1, 1))[0]


def _ffn_down_bwd(name, dy, wd4, l, gate, up):
    T, N = dy.shape
    k = wd4.shape[-2]
    tm = _tile(T, ROWS_STREAMED)

    def post(accs, ex, outs):
        da = accs[0]
        g = ex[0][...]
        u = ex[1][...]
        sg = _sig(g)
        outs[0][...] = (da * u * (sg * (1.0 + g * (1.0 - sg)))).astype(BF16)
        outs[1][...] = (da * (g * sg)).astype(BF16)

    so = pl.BlockSpec((None, tm, k), lambda i, m: (i, m, 0))
    return _mm(name, [(dy, wd4, pl.BlockSpec((tm, N), lambda i, m: (m, 0)),
                       pl.BlockSpec((None, None, k, N), lambda i, m: (i, l, 0, 0)))],
               [_sds((N_CHIP, T, k), BF16)] * 2, [so, so], (N_CHIP, T // tm), (1, 1), post=post,
               extras=[gate, up], extra_specs=[so, so])


def _ffn_up_bwd(name, dgate, dup, wg4, wu4, l):
    _, T, n = dgate.shape
    K = wg4.shape[-2]
    tm = _tile(T, 512)
    sa = pl.BlockSpec((N_CHIP, tm, n), lambda m: (0, m, 0))
    sw = pl.BlockSpec((N_CHIP, None, K, n), lambda m: (0, l, 0, 0))
    pick = lambda ref, i: ref[i]
    return _mm(name, [(dgate, wg4, sa, sw), (dup, wu4, sa, sw)],
               [_sds((T, K), F32)], [pl.BlockSpec((tm, K), lambda m: (m, 0))], (T // tm,), (1, 1),
               inner=(N_CHIP, pick, pick))[0]


def _wgrad(name, a, a_kind, b, b_kind, gbuf, l):
    r, cdim = gbuf.shape[-2:]
    T = a.shape[-2]
    tt = _tile(T, ROWS_STREAMED)

    def spec(kind, w):
        if kind == "full":
            return pl.BlockSpec((tt, w), lambda i, t: (t, 0))
        if kind == "cols":
            return pl.BlockSpec((tt, w), lambda i, t: (t, i))
        return pl.BlockSpec((None, tt, w), lambda i, t: (i, t, 0))

    def post(accs, ex, outs):
        outs[0][...] = accs[0]

    return _mm(name, [(a, b, spec(a_kind, r), spec(b_kind, cdim))], [_sds(gbuf.shape, F32)],
               [pl.BlockSpec((None, None, r, cdim), lambda i, t: (i, l, 0, 0))], (N_CHIP, T // tt), (0, 0),
               red_axis=1, post=post, extras=[gbuf], extra_specs=[pl.BlockSpec(memory_space=pl.ANY)],
               aliases={2: 0})[0]


def _vec_spec(*idx):
    return pl.BlockSpec((None,) * len(idx) + (1, D_MODEL), lambda m: tuple(idx) + (0, 0))


def _normmod(name, x, gain3, l, mod, ksc, ksh):
    T = x.shape[0]
    tm = _tile(T, 512)

    def kern(x_ref, g_ref, sc_ref, sh_ref, h_ref):
        xv = x_ref[...]
        r = lax.rsqrt(jnp.mean(xv * xv, axis=-1, keepdims=True) + EPS)
        h = (xv * r) * g_ref[...]
        h_ref[...] = (h * (1.0 + sc_ref[...]) + sh_ref[...]).astype(BF16)

    row = pl.BlockSpec((tm, D_MODEL), lambda m: (m, 0))
    return pl.pallas_call(kern, name=name, grid=(T // tm,),
                          in_specs=[row, _vec_spec(l), _vec_spec(l, ksc), _vec_spec(l, ksh)], out_specs=row,
                          out_shape=_sds((T, D_MODEL), BF16), compiler_params=_cparams("parallel"))(x, gain3, mod, mod)


def _normmod_bwd(name, x, dh, dres, gain3, l, mod, ksc):
    T = x.shape[0]
    tm = _tile(T, 512)
    nt = T // tm

    def kern(x_ref, dh_ref, dres_ref, g_ref, sc_ref, dx_ref, st_ref):
        m = pl.program_id(0)

        @pl.when(m == 0)
        def _():
            st_ref[...] = jnp.zeros_like(st_ref)

        xv = x_ref[...]
        dhv = dh_ref[...]
        r = lax.rsqrt(jnp.mean(xv * xv, axis=-1, keepdims=True) + EPS)
        xh = xv * r
        wv = g_ref[...] * (1.0 + sc_ref[...])
        dxh = dhv * wv
        dx_ref[...] = dres_ref[...] + r * (dxh - xh * jnp.mean(dxh * xh, axis=-1, keepdims=True))
        st_ref[0:1, :] += jnp.sum(dhv, axis=0, keepdims=True)
        st_ref[1:2, :] += jnp.sum(dhv * xh, axis=0, keepdims=True)

        @pl.when(m == nt - 1)
        def _():
            dw = st_ref[1:2, :]
            st_ref[2:3, :] = dw * (1.0 + sc_ref[...])
            st_ref[1:2, :] = dw * g_ref[...]

    row = pl.BlockSpec((tm, D_MODEL), lambda m: (m, 0))
    return pl.pallas_call(kern, name=name, grid=(nt,),
                          in_specs=[row, row, row, _vec_spec(l), _vec_spec(l, ksc)],
                          out_specs=[row, pl.BlockSpec((8, D_MODEL), lambda m: (0, 0))],
                          out_shape=[_sds((T, D_MODEL), F32), _sds((8, D_MODEL), F32)],
                          compiler_params=_cparams("arbitrary"))(x, dh, dres, gain3, mod)


def _gate_bwd(name, dxn, y, mod, l, kg):
    T = dxn.shape[0]
    tm = _tile(T, 512)

    def kern(d_ref, y_ref, g_ref, dy_ref, st_ref):
        @pl.when(pl.program_id(0) == 0)
        def _():
            st_ref[...] = jnp.zeros_like(st_ref)

        dv = d_ref[...]
        dy_ref[...] = (dv * g_ref[...]).astype(BF16)
        st_ref[0:1, :] += jnp.sum(dv * y_ref[...], axis=0, keepdims=True)

    row = pl.BlockSpec((tm, D_MODEL), lambda m: (m, 0))
    return pl.pallas_call(kern, name=name, grid=(T // tm,), in_specs=[row, row, _vec_spec(l, kg)],
                          out_specs=[row, pl.BlockSpec((8, D_MODEL), lambda m: (0, 0))],
                          out_shape=[_sds((T, D_MODEL), BF16), _sds((8, D_MODEL), F32)],
                          compiler_params=_cparams("arbitrary"))(dxn, y, mod)


def _loss_head(y, target):
    T = y.shape[0]
    tm = _tile(T, 512)

    def kern(y_ref, t_ref, dy_ref, acc_ref):
        @pl.when(pl.program_id(0) == 0)
        def _():
            acc_ref[...] = jnp.zeros_like(acc_ref)

        e = y_ref[...] - t_ref[...]
        dy_ref[...] = e * (1.0 / D_MODEL)
        s = jnp.sum(jnp.sum(e * e, axis=-1, keepdims=True), axis=0, keepdims=True)
        acc_ref[...] += s * (0.5 / D_MODEL)

    row = pl.BlockSpec((tm, D_MODEL), lambda m: (m, 0))
    return pl.pallas_call(kern, name="loss_head", grid=(T // tm,), in_specs=[row, row],
                          out_specs=[row, pl.BlockSpec((8, LANE), lambda m: (0, 0))],
                          out_shape=[_sds((T, D_MODEL), F32), _sds((8, LANE), F32)],
                          compiler_params=_cparams("arbitrary"))(y, target)


def _row_chunks(T):
    rc = min(ROW_CHUNK, T)
    return [(r * rc, rc) for r in range(T // rc)]


def _conv_fwd(name, proj, conv_w):
    T = proj.shape[0]
    nblk = CONV_DIM // LANE

    def kern(b_ref, c_ref, u_ref, w_ref, a_ref, zs):
        zs[0:8, :] = jnp.zeros((8, LANE), F32)
        for r0, rc in _row_chunks(T):
            zs[8 + r0:8 + r0 + rc, :] = c_ref[r0:r0 + rc, :] * u_ref[r0:r0 + rc, :]
        w0, w1, w2 = w_ref[0:1, :], w_ref[1:2, :], w_ref[2:3, :]
        for r0, rc in _row_chunks(T):
            yc = w2 * zs[8 + r0:8 + r0 + rc, :] + w1 * zs[7 + r0:7 + r0 + rc, :] + w0 * zs[6 + r0:6 + r0 + rc, :]
            a_ref[r0:r0 + rc, :] = (b_ref[r0:r0 + rc, :] * yc).astype(BF16)

    col = lambda p: pl.BlockSpec((T, LANE), lambda cb: (0, p * nblk + cb))
    return pl.pallas_call(kern, name=name, grid=(nblk,),
                          in_specs=[col(0), col(1), col(2), pl.BlockSpec((CONV_WIDTH, LANE), lambda cb: (0, cb))],
                          out_specs=pl.BlockSpec((T, LANE), lambda cb: (0, cb)),
                          out_shape=_sds((T, CONV_DIM), BF16), scratch_shapes=[pltpu.VMEM((T + 8, LANE), F32)],
                          compiler_params=_cparams("parallel"))(proj, proj, proj, conv_w)


def _conv_bwd(name, proj, conv_w, dcat):
    T = proj.shape[0]
    nblk = CONV_DIM // LANE

    def kern(b_ref, c_ref, u_ref, w_ref, da_ref, db_ref, dc_ref, du_ref, dw_ref, zs, ds):
        zs[0:8, :] = jnp.zeros((8, LANE), F32)
        ds[T:T + 8, :] = jnp.zeros((8, LANE), F32)
        for r0, rc in _row_chunks(T):
            zs[8 + r0:8 + r0 + rc, :] = c_ref[r0:r0 + rc, :] * u_ref[r0:r0 + rc, :]
        w0, w1, w2 = w_ref[0:1, :], w_ref[1:2, :], w_ref[2:3, :]
        acc = [jnp.zeros((1, LANE), F32) for _ in range(3)]
        for r0, rc in _row_chunks(T):
            z0 = zs[8 + r0:8 + r0 + rc, :]
            z1 = zs[7 + r0:7 + r0 + rc, :]
            z2 = zs[6 + r0:6 + r0 + rc, :]
            da = da_ref[r0:r0 + rc, :]
            db_ref[r0:r0 + rc, :] = (da * (w2 * z0 + w1 * z1 + w0 * z2)).astype(BF16)
            dyc = da * b_ref[r0:r0 + rc, :]
            ds[r0:r0 + rc, :] = dyc
            acc[2] = acc[2] + jnp.sum(dyc * z0, axis=0, keepdims=True)
            acc[1] = acc[1] + jnp.sum(dyc * z1, axis=0, keepdims=True)
            acc[0] = acc[0] + jnp.sum(dyc * z2, axis=0, keepdims=True)
        dw_ref[...] = jnp.zeros_like(dw_ref)
        for k in range(3):
            dw_ref[k:k + 1, :] = acc[k]
        for r0, rc in _row_chunks(T):
            dz = w2 * ds[r0:r0 + rc, :] + w1 * ds[r0 + 1:r0 + 1 + rc, :] + w0 * ds[r0 + 2:r0 + 2 + rc, :]
            dc_ref[r0:r0 + rc, :] = (dz * u_ref[r0:r0 + rc, :]).astype(BF16)
            du_ref[r0:r0 + rc, :] = (dz * c_ref[r0:r0 + rc, :]).astype(BF16)

    col = lambda p: pl.BlockSpec((T, LANE), lambda cb: (0, p * nblk + cb))
    out = pl.BlockSpec((T, LANE), lambda cb: (0, cb))
    return pl.pallas_call(kern, name=name, grid=(nblk,),
                          in_specs=[col(0), col(1), col(2), pl.BlockSpec((CONV_WIDTH, LANE), lambda cb: (0, cb)), out],
                          out_specs=[out, out, out, pl.BlockSpec((8, LANE), lambda cb: (0, cb))],
                          out_shape=[_sds((T, CONV_DIM), BF16)] * 3 + [_sds((8, CONV_DIM), F32)],
                          scratch_shapes=[pltpu.VMEM((T + 8, LANE), F32), pltpu.VMEM((T + 8, LANE), F32)],
                          compiler_params=_cparams("parallel"))(proj, proj, proj, conv_w, dcat)


_Q0, _K0, _V0, _G0 = 3 * CONV_DIM // LANE, (3 * CONV_DIM + RET_DIM) // LANE, (3 * CONV_DIM + 2 * RET_DIM) // LANE, \
    (3 * CONV_DIM + 3 * RET_DIM) // LANE


def _ret_tables(B, lg1):
    ti = lax.broadcasted_iota(jnp.int32, (B, B), 0)
    si = lax.broadcasted_iota(jnp.int32, (B, B), 1)
    dist = jnp.abs(ti - si).astype(F32)
    shift = RET_CHUNK.bit_length() - 1
    dmat = jnp.where((si >> shift) <= (ti >> shift), jnp.exp(dist * lg1), 0.0)
    tcol = lax.broadcasted_iota(jnp.int32, (B, 1), 0).astype(F32)
    qdec = jnp.exp((tcol + 1.0) * lg1)
    kdec = jnp.exp((B - 1.0 - tcol) * lg1)
    bdec = jnp.exp(float(B) * lg1)
    return dmat, qdec, kdec, bdec


def _retention_fwd(name, proj, cosf, sinf, gr, lgt):
    T = proj.shape[0]
    B = min(RET_BLOCK, T)
    nb = T // B

    def kern(q_ref, k_ref, v_ref, g_ref, cos_ref, sin_ref, gr_ref, lg_ref, r_ref, o_ref, st_ref, S):
        @pl.when(pl.program_id(1) == 0)
        def _():
            S[...] = jnp.zeros_like(S)

        cosv, sinv = cos_ref[...], sin_ref[...]
        rot = lambda a: a * cosv + pltpu.roll(a, HEAD // 2, 1) * sinv
        qr = rot(q_ref[...])
        kr = rot(k_ref[...]) * QK_SCALE
        v = v_ref[...]
        dmat, qdec, kdec, bdec = _ret_tables(B, lg_ref[0:1, 0:1])
        sv = S[...]
        st_ref[...] = sv
        pd = _dot(qr, kr, 1, 1) * dmat
        o = _dot(pd, v, 1, 0) + _dot(qr * qdec, sv, 1, 0)
        S[...] = bdec * sv + _dot(kr * kdec, v, 0, 0)
        o_ref[...] = o
        rs = lax.rsqrt(jnp.mean(o * o, axis=-1, keepdims=True) + EPS)
        g = g_ref[...]
        r_ref[...] = (g * _sig(g) * (o * rs * gr_ref[...])).astype(BF16)

    col = lambda c0: pl.BlockSpec((B, HEAD), lambda h, i: (i, c0 + h))
    tab = pl.BlockSpec((B, HEAD), lambda h, i: (i, 0))
    outc = pl.BlockSpec((B, HEAD), lambda h, i: (i, h))
    return pl.pallas_call(
        kern, name=name, grid=(RET_HEADS, nb),
        in_specs=[col(_Q0), col(_K0), col(_V0), col(_G0), tab, tab, pl.BlockSpec((1, HEAD), lambda h, i: (0, h)),
                  pl.BlockSpec((None, 1, LANE), lambda h, i: (h, 0, 0))],
        out_specs=[outc, outc, pl.BlockSpec((None, None, HEAD, HEAD), lambda h, i: (h, i, 0, 0))],
        out_shape=[_sds((T, RET_DIM), BF16), _sds((T, RET_DIM), F32), _sds((RET_HEADS, nb, HEAD, HEAD), F32)],
        scratch_shapes=[pltpu.VMEM((HEAD, HEAD), F32)],
        compiler_params=_cparams("parallel", "arbitrary"))(proj, proj, proj, proj, cosf, sinf, gr, lgt)


def _retention_bwd(name, proj, oraw, states, dcat, cosf, sinf, gr, lgt):
    T = proj.shape[0]
    B = min(RET_BLOCK, T)
    nb = T // B

    def kern(q_ref, k_ref, v_ref, g_ref, o_ref, dr_ref, st_ref, cos_ref, sin_ref, gr_ref, lg_ref,
             dq_ref, dk_ref, dv_ref, dg_ref, dgr_ref, dS):
        @pl.when(pl.program_id(1) == 0)
        def _():
            dS[...] = jnp.zeros_like(dS)
            dgr_ref[...] = jnp.zeros_like(dgr_ref)

        cosv, sinv = cos_ref[...], sin_ref[...]
        rot = lambda a: a * cosv + pltpu.roll(a, HEAD // 2, 1) * sinv
        rot_t = lambda a: a * cosv + pltpu.roll(a * sinv, HEAD // 2, 1)
        qr = rot(q_ref[...])
        kr = rot(k_ref[...]) * QK_SCALE
        v = v_ref[...]
        dmat, qdec, kdec, bdec = _ret_tables(B, lg_ref[0:1, 0:1])
        o = o_ref[...]
        rs = lax.rsqrt(jnp.mean(o * o, axis=-1, keepdims=True) + EPS)
        xh = o * rs
        g = g_ref[...]
        sg = _sig(g)
        grv = gr_ref[...]
        dr = dr_ref[...]
        dn = dr * (g * sg)
        dg_ref[...] = (dr * (xh * grv) * (sg * (1.0 + g * (1.0 - sg)))).astype(BF16)
        dgr_ref[0:1, :] += jnp.sum(dn * xh, axis=0, keepdims=True)
        dxh = dn * grv
        do = rs * (dxh - xh * jnp.mean(dxh * xh, axis=-1, keepdims=True))
        sp = st_ref[...]
        dsv = dS[...]
        pd = _dot(qr, kr, 1, 1) * dmat
        dp = _dot(do, v, 1, 1) * dmat
        dv_ref[...] = (_dot(pd, do, 0, 0) + _dot(kr * kdec, dsv, 1, 0)).astype(BF16)
        dqr = _dot(dp, kr, 1, 0) + _dot(do, sp, 1, 1) * qdec
        dkr = _dot(dp, qr, 0, 0) + _dot(v, dsv, 1, 1) * kdec
        dS[...] = bdec * dsv + _dot(qr * qdec, do, 0, 0)
        dq_ref[...] = rot_t(dqr).astype(BF16)
        dk_ref[...] = (rot_t(dkr) * QK_SCALE).astype(BF16)

    rev = lambda i: nb - 1 - i
    col = lambda c0: pl.BlockSpec((B, HEAD), lambda h, i: (rev(i), c0 + h))
    tab = pl.BlockSpec((B, HEAD), lambda h, i: (rev(i), 0))
    outc = pl.BlockSpec((B, HEAD), lambda h, i: (rev(i), h))
    return pl.pallas_call(
        kern, name=name, grid=(RET_HEADS, nb),
        in_specs=[col(_Q0), col(_K0), col(_V0), col(_G0), outc,
                  pl.BlockSpec((B, HEAD), lambda h, i: (rev(i), CONV_DIM // LANE + h)),
                  pl.BlockSpec((None, None, HEAD, HEAD), lambda h, i: (h, rev(i), 0, 0)), tab, tab,
                  pl.BlockSpec((1, HEAD), lambda h, i: (0, h)), pl.BlockSpec((None, 1, LANE), lambda h, i: (h, 0, 0))],
        out_specs=[outc, outc, outc, outc, pl.BlockSpec((8, HEAD), lambda h, i: (0, h))],
        out_shape=[_sds((T, RET_DIM), BF16)] * 4 + [_sds((8, RET_DIM), F32)],
        scratch_shapes=[pltpu.VMEM((HEAD, HEAD), F32)],
        compiler_params=_cparams("parallel", "arbitrary"))(proj, proj, proj, proj, oraw, dcat, states, cosf, sinf, gr, lgt)


def _strict_upper():
    r = lax.broadcasted_iota(jnp.int32, (SB_BLOCK, SB_BLOCK), 0)
    c = lax.broadcasted_iota(jnp.int32, (SB_BLOCK, SB_BLOCK), 1)
    return (r > c).astype(BF16), c - r


def _suffix_sum(vals, tri):
    hi = vals.astype(BF16)
    lo = (vals - hi.astype(F32)).astype(BF16)
    dn = (((1,), (0,)), ((), ()))
    return lax.dot_general(hi, tri, dn, preferred_element_type=F32) + lax.dot_general(lo, tri, dn, preferred_element_type=F32)


def _sb_scores(qi, kj, tri, col_minus_row, on_diag):
    z = _dot(qi, kj, 1, 1) * QK_SCALE
    lb = jnp.minimum(z, 0.0) - jnp.log(1.0 + jnp.exp(-jnp.abs(z)))
    if on_diag:
        valid = col_minus_row < 0
        lk = jnp.where(valid, lb - z, 0.0)
        w_loc = jnp.where(valid, jnp.exp(lb + _suffix_sum(lk, tri)), 0.0)
        return lb, lk, valid, w_loc
    lk = lb - z
    return lb, lk, None, jnp.exp(lb + _suffix_sum(lk, tri))


def _sb_walk(i, tile, st):
    st = tile(i, True, st)

    def pair(p, s):
        j = i - 1 - 2 * p
        return tile(j - 1, False, tile(j, False, s))

    st = lax.fori_loop(0, i // 2, pair, st)
    return lax.fori_loop(0, i % 2, lambda _, s: tile(0, False, s), st)


def _head_norm_rows(src, gain, dst, T):
    for r0, rc in _row_chunks(T):
        a = src[r0:r0 + rc, :]
        r = lax.rsqrt(jnp.mean(a * a, axis=-1, keepdims=True) + EPS)
        dst[r0:r0 + rc, :] = (a * r * gain).astype(BF16)


def _sb_fwd(name, qkv, gq, gk):
    T = qkv.shape[0]
    nq = T // SB_BLOCK

    def kern(q_ref, k_ref, v_ref, gq_ref, gk_ref, o_ref, o32_ref, qn, kn, vb):
        _head_norm_rows(q_ref, gq_ref[...], qn, T)
        _head_norm_rows(k_ref, gk_ref[...], kn, T)
        for r0, rc in _row_chunks(T):
            vb[r0:r0 + rc, :] = v_ref[r0:r0 + rc, :].astype(BF16)
        tri, diag_mask = _strict_upper()

        def qblock(i, _):
            rows_i = pl.ds(pl.multiple_of(i * SB_BLOCK, SB_BLOCK), SB_BLOCK)
            qi = qn[rows_i, :]

            def tile(j, on_diag, st):
                acc, car = st
                rows_j = pl.ds(pl.multiple_of(j * SB_BLOCK, SB_BLOCK), SB_BLOCK)
                _, lk, _, w = _sb_scores(qi, kn[rows_j, :], tri, diag_mask, on_diag)
                w_hi = w.astype(BF16)
                w_lo = (w - w_hi.astype(F32)).astype(BF16)
                vj = vb[rows_j, :]
                acc = acc + jnp.exp(car) * (_dot(w_hi, vj, 1, 0) + _dot(w_lo, vj, 1, 0))
                return acc, car + jnp.sum(lk, axis=-1, keepdims=True)

            acc, _ = _sb_walk(i, tile, (jnp.zeros((SB_BLOCK, HEAD), F32), jnp.zeros((SB_BLOCK, 1), F32)))
            o_ref[rows_i, :] = acc.astype(BF16)
            o32_ref[rows_i, :] = acc
            return 0

        lax.fori_loop(0, nq, qblock, 0)

    col = lambda c0: pl.BlockSpec((T, HEAD), lambda h: (0, c0 + h))
    vec = pl.BlockSpec((1, HEAD), lambda h: (0, 0))
    out = pl.BlockSpec((T, HEAD), lambda h: (0, h))
    return pl.pallas_call(kern, name=name, grid=(SB_HEADS,),
                          in_specs=[col(0), col(SB_HEADS), col(2 * SB_HEADS), vec, vec], out_specs=[out, out],
                          out_shape=[_sds((T, D_MODEL), BF16), _sds((T, D_MODEL), F32)],
                          scratch_shapes=[pltpu.VMEM((T, HEAD), BF16)] * 3,
                          compiler_params=_cparams("parallel"))(qkv, qkv, qkv, gq, gk)


def _sb_bwd(name, qkv, gq, gk, o32, dcat):
    T = qkv.shape[0]
    nq = T // SB_BLOCK

    def kern(q_ref, k_ref, v_ref, gq_ref, gk_ref, o_ref, do_ref, dq_ref, dk_ref, dv_ref, dgq_ref, dgk_ref,
             qn, kn, vb, dqn, dkn, dvv):
        @pl.when(pl.program_id(0) == 0)
        def _():
            dgq_ref[...] = jnp.zeros_like(dgq_ref)
            dgk_ref[...] = jnp.zeros_like(dgk_ref)

        _head_norm_rows(q_ref, gq_ref[...], qn, T)
        _head_norm_rows(k_ref, gk_ref[...], kn, T)
        for r0, rc in _row_chunks(T):
            vb[r0:r0 + rc, :] = v_ref[r0:r0 + rc, :].astype(BF16)
            dkn[r0:r0 + rc, :] = jnp.zeros((rc, HEAD), F32)
            dvv[r0:r0 + rc, :] = jnp.zeros((rc, HEAD), F32)
        tri, diag_mask = _strict_upper()

        def qblock(i, _):
            rows_i = pl.ds(pl.multiple_of(i * SB_BLOCK, SB_BLOCK), SB_BLOCK)
            qi = qn[rows_i, :]
            doi = do_ref[rows_i, :]
            dob = doi.astype(BF16)
            etot = jnp.sum(dob.astype(F32) * o_ref[rows_i, :], axis=-1, keepdims=True)

            def tile(j, on_diag, st):
                dq_acc, car, ecar = st
                rows_j = pl.ds(pl.multiple_of(j * SB_BLOCK, SB_BLOCK), SB_BLOCK)
                kj = kn[rows_j, :]
                vj = vb[rows_j, :]
                lb, lk, valid, w_loc = _sb_scores(qi, kj, tri, diag_mask, on_diag)
                w = w_loc * jnp.exp(car)
                e = w * _dot(dob, vj, 1, 1)
                suff = _suffix_sum(e, tri) + e + ecar
                sig = jnp.exp(lb)
                dz = (e * (1.0 - sig) - sig * (etot - suff)) * QK_SCALE
                if on_diag:
                    dz = jnp.where(valid, dz, 0.0)
                dzb = dz.astype(BF16)
                dkn[rows_j, :] += _dot(dzb, qi, 0, 0)
                dvv[rows_j, :] += _dot(w, dob, 0, 0)
                return (dq_acc + _dot(dzb, kj, 1, 0), car + jnp.sum(lk, axis=-1, keepdims=True),
                        ecar + jnp.sum(e, axis=-1, keepdims=True))

            zcol = jnp.zeros((SB_BLOCK, 1), F32)
            dq_acc, _, _ = _sb_walk(i, tile, (jnp.zeros((SB_BLOCK, HEAD), F32), zcol, zcol))
            dqn[rows_i, :] = dq_acc
            return 0

        lax.fori_loop(0, nq, qblock, 0)

        def norm_bwd(src, gain, dnorm, dst, dgain):
            tot = jnp.zeros((1, HEAD), F32)
            for r0, rc in _row_chunks(T):
                a = src[r0:r0 + rc, :]
                r = lax.rsqrt(jnp.mean(a * a, axis=-1, keepdims=True) + EPS)
                xh = a * r
                dn = dnorm[r0:r0 + rc, :]
                tot = tot + jnp.sum(dn * xh, axis=0, keepdims=True)
                dxh = dn * gain
                dst[r0:r0 + rc, :] = (r * (dxh - xh * jnp.mean(dxh * xh, axis=-1, keepdims=True))).astype(BF16)
            dgain[0:1, :] += tot

        norm_bwd(q_ref, gq_ref[...], dqn, dq_ref, dgq_ref)
        norm_bwd(k_ref, gk_ref[...], dkn, dk_ref, dgk_ref)
        for r0, rc in _row_chunks(T):
            dv_ref[r0:r0 + rc, :] = dvv[r0:r0 + rc, :].astype(BF16)

    col = lambda c0: pl.BlockSpec((T, HEAD), lambda h: (0, c0 + h))
    vec = pl.BlockSpec((1, HEAD), lambda h: (0, 0))
    out = pl.BlockSpec((T, HEAD), lambda h: (0, h))
    st = pl.BlockSpec((8, HEAD), lambda h: (0, 0))
    return pl.pallas_call(kern, name=name, grid=(SB_HEADS,),
                          in_specs=[col(0), col(SB_HEADS), col(2 * SB_HEADS), vec, vec, out, out],
                          out_specs=[out, out, out, st, st],
                          out_shape=[_sds((T, D_MODEL), BF16)] * 3 + [_sds((8, HEAD), F32)] * 2,
                          scratch_shapes=[pltpu.VMEM((T, HEAD), BF16)] * 3 + [pltpu.VMEM((T, HEAD), F32)] * 3,
                          compiler_params=_cparams("arbitrary"))(qkv, qkv, qkv, gq, gk, o32, dcat)


def _ada_fwd(c_all, ada_w, ada_b_cols):
    L, K, n = ada_w.shape

    def kern(c_ref, w_ref, b_ref, o_ref):
        cv = c_ref[...]
        o_ref[...] = _dot(cv * _sig(cv), w_ref[...], 1, 0) + b_ref[...]

    return pl.pallas_call(kern, name="ada_fwd", grid=(L,),
                          in_specs=[pl.BlockSpec((N_DEV, K), lambda l: (0, 0)), pl.BlockSpec((None, K, n), lambda l: (l, 0, 0)),
                                    pl.BlockSpec((None, 1, n), lambda l: (l, 0, 0))],
                          out_specs=pl.BlockSpec((None, N_DEV, n), lambda l: (l, 0, 0)),
                          out_shape=_sds((L, N_DEV, n), F32), compiler_params=_cparams("parallel"))(c_all, ada_w, ada_b_cols)


def _ada_wgrad(c_all_t, dmod_cols):
    K = c_all_t.shape[0]
    L, _, n = dmod_cols.shape
    tk = 128

    def kern(c_ref, d_ref, o_ref):
        cv = c_ref[...]
        ca = cv * _sig(cv)
        acc = ca[:, 0:1] * d_ref[0:1, :]
        for b in range(1, N_DEV):
            acc = acc + ca[:, b:b + 1] * d_ref[b:b + 1, :]
        o_ref[...] = acc

    return pl.pallas_call(kern, name="ada_wgrad", grid=(L, K // tk),
                          in_specs=[pl.BlockSpec((tk, N_DEV), lambda l, m: (m, 0)), pl.BlockSpec((None, N_DEV, n), lambda l, m: (l, 0, 0))],
                          out_specs=pl.BlockSpec((None, tk, n), lambda l, m: (l, m, 0)),
                          out_shape=_sds((L, K, n), F32), compiler_params=_cparams("parallel", "parallel"))(c_all_t, dmod_cols)


def _sum_devices(g):
    _, R, n = g.shape

    def kern(g_ref, o_ref):
        acc = g_ref[0]
        for d in range(1, N_DEV):
            acc = acc + g_ref[d]
        o_ref[...] = acc

    return pl.pallas_call(kern, name="sum_devices", out_shape=_sds((R, n), F32))(g)


def _pair_sum(name, g, recv, c_idx):
    n4, L, r, cdim = g.shape
    rows = (L // 2) * r
    tr = _tile(rows, 512)
    gv = g.reshape(n4, 2, rows, cdim)
    rv = recv.reshape(n4, rows, cdim)

    def kern(c_ref, g_ref, r_ref, o_ref):
        o_ref[...] = (g_ref[...] + r_ref[...]).astype(BF16)

    gs = pltpu.PrefetchScalarGridSpec(
        num_scalar_prefetch=1, grid=(n4, rows // tr),
        in_specs=[pl.BlockSpec((None, None, tr, cdim), lambda j, m, cr: (j, cr[0], m, 0)),
                  pl.BlockSpec((None, tr, cdim), lambda j, m, cr: (j, m, 0))],
        out_specs=pl.BlockSpec((None, tr, cdim), lambda j, m, cr: (j, m, 0)))
    out = pl.pallas_call(kern, name=name, grid_spec=gs, out_shape=_sds((n4, rows, cdim), BF16),
                         compiler_params=_cparams("parallel", "parallel"))(c_idx, gv, rv)
    return out.reshape(n4, L // 2, r, cdim)


def _chip_sum(name, recv, psum, pos_idx):
    n4, hl, r, cdim = recv.shape
    rows = hl * r
    tr = _tile(rows, 512)

    def kern(pos_ref, r0, r1, r2, r3, own_ref, o_ref):
        me = pos_ref[0]
        own = own_ref[...].astype(F32)
        terms = [jnp.where(me == s, own, rr[...].astype(F32)) for s, rr in enumerate((r0, r1, r2, r3))]
        o_ref[...] = ((terms[0] + terms[1]) + terms[2]) + terms[3]

    def slot(s):
        return pl.BlockSpec((None, tr, cdim), lambda m, pos: (jnp.where(pos[0] == s, (s + 1) % n4, s), m, 0))

    gs = pltpu.PrefetchScalarGridSpec(
        num_scalar_prefetch=1, grid=(rows // tr,),
        in_specs=[slot(s) for s in range(n4)] + [pl.BlockSpec((None, tr, cdim), lambda m, pos: (pos[0], m, 0))],
        out_specs=pl.BlockSpec((None, tr, cdim), lambda m, pos: (pos[1], m, 0)))
    rv = recv.reshape(n4, rows, cdim)
    out = pl.pallas_call(kern, name=name, grid_spec=gs, out_shape=_sds((2, rows, cdim), F32),
                         compiler_params=_cparams("parallel"))(pos_idx, rv, rv, rv, rv, psum.reshape(n4, rows, cdim))
    return out.reshape(2 * hl, r, cdim)


def _adamw(name, w, g, m, v):
    shape = w.shape
    cols = shape[-1]
    rows = int(np.prod(shape[:-1]))
    tr = _tile(rows, 512) if rows % 8 == 0 else rows
    c1 = 1.0 - ADAM_B1 ** ADAM_STEP
    c2 = 1.0 - ADAM_B2 ** ADAM_STEP

    def kern(w_ref, g_ref, m_ref, v_ref, d_ref, mo_ref, vo_ref):
        gv = g_ref[...]
        mn = ADAM_B1 * m_ref[...] + (1.0 - ADAM_B1) * gv
        vn = ADAM_B2 * v_ref[...] + (1.0 - ADAM_B2) * (gv * gv)
        mo_ref[...] = mn
        vo_ref[...] = vn
        d_ref[...] = -ADAM_LR * ((mn / c1) / (jnp.sqrt(vn / c2) + ADAM_EPS) + ADAM_WD * w_ref[...])

    blk = pl.BlockSpec((tr, cols), lambda i: (i, 0))
    outs = pl.pallas_call(kern, name=name, grid=(rows // tr,), in_specs=[blk] * 4, out_specs=[blk] * 3,
                          out_shape=[_sds((rows, cols), F32)] * 3, compiler_params=_cparams("parallel"))(
        *[a.reshape(rows, cols) for a in (w, g, m, v)])
    return tuple(o.reshape(shape) for o in outs)


def _rcopy(src, dst, ssem, rsem, dev):
    return pltpu.make_async_remote_copy(src_ref=src, dst_ref=dst, send_sem=ssem, recv_sem=rsem, device_id=dev,
                                        device_id_type=MESH)


def _gather8(name, blk):
    m_per, n = blk.shape

    def body(x_ref, out_ref, send_sems, recv_sems, local_sem):
        x, y, c = _position()
        me, sibling = (x, y, c), (x, y, 1 - c)
        chips = _other_chips(x, y)

        def rows(px, py, pc):
            return out_ref.at[pl.ds((4 * px + 2 * py + pc) * m_per, m_per), :]

        def copy(k, block, to, src=None):
            return _rcopy(rows(*block) if src is None else src, rows(*block), send_sems.at[k], recv_sems.at[k], to)

        mine = pltpu.make_async_copy(x_ref, rows(*me), local_sem)
        mine.start()
        first = [copy(0, me, sibling, src=x_ref)]
        first += [copy(1 + j, me, (*chip, c), src=x_ref) for j, chip in enumerate(chips)]
        for cp in first:
            cp.start()
        passed = [copy(4 + j, (*chip, c), sibling) for j, chip in enumerate(chips)]
        for j, chip in enumerate(chips):
            copy(1 + j, (*chip, c), me).wait_recv()
            passed[j].start()
        copy(0, sibling, me).wait_recv()
        for j, chip in enumerate(chips):
            copy(4 + j, (*chip, 1 - c), me).wait_recv()
        for cp in first + passed:
            cp.wait_send()
        mine.wait()

    return pl.pallas_call(body, name=name, out_shape=_sds((N_DEV * m_per, n), blk.dtype),
                          in_specs=[pl.BlockSpec(memory_space=pltpu.VMEM)], out_specs=pl.BlockSpec(memory_space=pltpu.VMEM),
                          scratch_shapes=[pltpu.SemaphoreType.DMA((7,)), pltpu.SemaphoreType.DMA((7,)), pltpu.SemaphoreType.DMA],
                          compiler_params=pltpu.CompilerParams(vmem_limit_bytes=VMEM_LIMIT_V7X))(blk)


_ANY = pl.BlockSpec(memory_space=pl.ANY)


def _gather_weights(shards):
    n = len(shards)

    def body(*refs):
        src, out = refs[:n], refs[n:2 * n]
        send_sems, recv_sems, fsend_sems, frecv_sems = refs[2 * n:]
        x, y, c = _position()
        sibling = (x, y, 1 - c)
        chips = _other_chips(x, y)
        me_chip = 2 * x + y
        started = []
        for t in range(n):
            hl = src[t].shape[0] // 2
            mine_half = pl.ds(c * hl, hl)
            for j, (px, py) in enumerate(chips):
                cp = _rcopy(src[t].at[mine_half], out[t].at[me_chip, mine_half], send_sems.at[3 * t + j],
                            recv_sems.at[3 * t + j], (px, py, c))
                cp.start()
                started.append(cp)
        for t in range(n):
            hl = src[t].shape[0] // 2
            mine_half = pl.ds(c * hl, hl)
            for j, (px, py) in enumerate(chips):
                landed = out[t].at[2 * px + py, mine_half]
                _rcopy(landed, landed, send_sems.at[3 * t + j], recv_sems.at[3 * t + j], (px, py, c)).wait_recv()
                fw = _rcopy(landed, landed, fsend_sems.at[3 * t + j], frecv_sems.at[3 * t + j], sibling)
                fw.start()
                started.append(fw)
        for t in range(n):
            hl = src[t].shape[0] // 2
            other_half = pl.ds((1 - c) * hl, hl)
            for j, (px, py) in enumerate(chips):
                landed = out[t].at[2 * px + py, other_half]
                _rcopy(landed, landed, fsend_sems.at[3 * t + j], frecv_sems.at[3 * t + j], sibling).wait_recv()
        for cp in started:
            cp.wait_send()

    sems = [pltpu.SemaphoreType.DMA((3 * n,))] * 4
    return pl.pallas_call(body, name="gather_weights", out_shape=[_sds((N_CHIP,) + s.shape, s.dtype) for s in shards],
                          in_specs=[_ANY] * n, out_specs=[_ANY] * n, scratch_shapes=sems)(*shards)


def _place_own(name, w4, shard, pos_idx):
    n4, L, r, cdim = w4.shape
    rows = L * r
    tr = _tile(rows, 1024)

    def kern(pos_ref, s_ref, w_ref, o_ref):
        o_ref[...] = s_ref[...]

    gs = pltpu.PrefetchScalarGridSpec(
        num_scalar_prefetch=1, grid=(rows // tr,),
        in_specs=[pl.BlockSpec((tr, cdim), lambda m, pos: (m, 0)), _ANY],
        out_specs=pl.BlockSpec((None, tr, cdim), lambda m, pos: (pos[0], m, 0)))
    out = pl.pallas_call(kern, name=name, grid_spec=gs, out_shape=_sds((n4, rows, cdim), w4.dtype),
                         input_output_aliases={2: 0}, compiler_params=_cparams("parallel"))(
        pos_idx, shard.reshape(rows, cdim), w4.reshape(n4, rows, cdim))
    return out.reshape(w4.shape)


def _exchange_pair(grads):
    n = len(grads)

    def body(*refs):
        src, out = refs[:n], refs[n:2 * n]
        send_sems, recv_sems = refs[2 * n:]
        x, y, c = _position()
        sibling = (x, y, 1 - c)
        cps = []
        for t in range(n):
            hl = src[t].shape[1] // 2
            cp = _rcopy(src[t].at[:, pl.ds((1 - c) * hl, hl)], out[t], send_sems.at[t], recv_sems.at[t], sibling)
            cp.start()
            cps.append(cp)
        for cp in cps:
            cp.wait_recv()
        for cp in cps:
            cp.wait_send()

    sems = [pltpu.SemaphoreType.DMA((n,))] * 2
    return pl.pallas_call(body, name="exchange_pair",
                          out_shape=[_sds((g.shape[0], g.shape[1] // 2) + g.shape[2:], g.dtype) for g in grads],
                          in_specs=[_ANY] * n, out_specs=[_ANY] * n, scratch_shapes=sems)(*grads)


def _scatter_chips(psums):
    n = len(psums)

    def body(*refs):
        src, out = refs[:n], refs[n:2 * n]
        send_sems, recv_sems = refs[2 * n:]
        x, y, c = _position()
        chips = _other_chips(x, y)
        me_chip = 2 * x + y
        cps = []
        for t in range(n):
            for j, (px, py) in enumerate(chips):
                cp = _rcopy(src[t].at[2 * px + py], out[t].at[me_chip], send_sems.at[3 * t + j], recv_sems.at[3 * t + j],
                            (px, py, c))
                cp.start()
                cps.append(cp)
        for t in range(n):
            for j, (px, py) in enumerate(chips):
                slot = out[t].at[2 * px + py]
                _rcopy(slot, slot, send_sems.at[3 * t + j], recv_sems.at[3 * t + j], (px, py, c)).wait_recv()
        for cp in cps:
            cp.wait_send()

    sems = [pltpu.SemaphoreType.DMA((3 * n,))] * 2
    return pl.pallas_call(body, name="scatter_chips", out_shape=[_sds(p.shape, p.dtype) for p in psums],
                          in_specs=[_ANY] * n, out_specs=[_ANY] * n, scratch_shapes=sems)(*psums)


def _share_halves(full):
    n = len(full)

    def body(*refs):
        out = refs[n:2 * n]
        send_sems, recv_sems = refs[2 * n:]
        x, y, c = _position()
        sibling = (x, y, 1 - c)
        cps = []
        for t in range(n):
            hl = out[t].shape[0] // 2
            mine = out[t].at[pl.ds(c * hl, hl)]
            cp = _rcopy(mine, mine, send_sems.at[t], recv_sems.at[t], sibling)
            cp.start()
            cps.append(cp)
        for t in range(n):
            hl = out[t].shape[0] // 2
            theirs = out[t].at[pl.ds((1 - c) * hl, hl)]
            _rcopy(theirs, theirs, send_sems.at[t], recv_sems.at[t], sibling).wait_recv()
        for cp in cps:
            cp.wait_send()

    sems = [pltpu.SemaphoreType.DMA((n,))] * 2
    return pl.pallas_call(body, name="share_halves", out_shape=[_sds(h.shape, h.dtype) for h in full],
                          in_specs=[_ANY] * n, out_specs=[_ANY] * n, scratch_shapes=sems,
                          input_output_aliases={t: t for t in range(n)})(*full)


def _rope_tables(T):
    inv_freq = 1.0 / (ROPE_THETA ** (jnp.arange(0, HEAD, 2, dtype=F32) / HEAD))
    ang = jnp.arange(T, dtype=F32)[:, None] * inv_freq[None, :]
    cos, sin = jnp.cos(ang), jnp.sin(ang)
    return jnp.concatenate([cos, cos], axis=-1), jnp.concatenate([-sin, sin], axis=-1)


def _decay_table():
    lg = np.log1p(-np.exp2(-5.0 - np.arange(RET_HEADS, dtype=np.float32))).astype(np.float32)
    return jnp.asarray(np.broadcast_to(lg[:, None, None], (RET_HEADS, 1, LANE)).copy())


def _local_step(x0, target, mod, W, G, norm_mix_g, norm_ffn_g, conv_full, ev_ret_norm_g, od_q_norm_g, od_k_norm_g):
    T = x0.shape[0]
    KSH1, KSC1, KG1, KSH2, KSC2, KG2 = range(6)
    gain_mix = norm_mix_g.reshape(DEPTH, 1, D_MODEL)
    gain_ffn = norm_ffn_g.reshape(DEPTH, 1, D_MODEL)
    cosf, sinf = _rope_tables(T)
    lgt = _decay_table()

    saved = []
    xcur = x0
    for l in range(DEPTH):
        j = l // 2
        s = dict(x_in=xcur)
        h = _normmod(f"norm_mix_{l}", xcur, gain_mix, l, mod, KSC1, KSH1)
        s["h"] = h
        if l % 2 == 0:
            proj = _proj_cols(f"ev_in_{l}", h, W["ev_w_in"], j)
            a = _conv_fwd(f"conv_{l}", proj, conv_full[j])
            r, oraw, states = _retention_fwd(f"ret_{l}", proj, cosf, sinf, ev_ret_norm_g[j].reshape(1, RET_DIM), lgt)
            cat = jnp.concatenate([a, r], axis=1)
            s.update(proj=proj, oraw=oraw, states=states, cat=cat)
            y, xmid = _out_proj(f"ev_out_{l}", cat, False, W["ev_w_out"], j, xcur, mod, l, KG1)
        else:
            qkv = _proj_cols(f"od_in_{l}", h, W["od_w_qkv"], j)
            o, o32 = _sb_fwd(f"sb_{l}", qkv, od_q_norm_g[j].reshape(1, HEAD), od_k_norm_g[j].reshape(1, HEAD))
            s.update(qkv=qkv, cat=o, o32=o32)
            y, xmid = _out_proj(f"od_out_{l}", o, False, W["od_w_out"], j, xcur, mod, l, KG1)
        s.update(y1=y, x_mid=xmid)
        h2 = _normmod(f"norm_ffn_{l}", xmid, gain_ffn, l, mod, KSC2, KSH2)
        gate, up, act = _ffn_up(f"ffn_up_{l}", h2, W["ffn_w_gate"], W["ffn_w_up"], l)
        y2, xcur = _out_proj(f"ffn_down_{l}", act, True, W["ffn_w_down"], l, xmid, mod, l, KG2)
        s.update(h2=h2, gate=gate, up=up, act=act, y2=y2)
        saved.append(s)

    dy, lacc = _loss_head(xcur, target)

    dmod_rows = [None] * DEPTH
    d_mix = [None] * DEPTH
    d_ffn = [None] * DEPTH
    d_conv = [None] * 2
    d_ret = [None] * 2
    d_gq = [None] * 2
    d_gk = [None] * 2
    dx = dy
    for l in reversed(range(DEPTH)):
        j = l // 2
        s = saved[l]
        dyg, st_g2 = _gate_bwd(f"gate2_bwd_{l}", dx, s["y2"], mod, l, KG2)
        G["ffn_w_down"] = _wgrad(f"wg_down_{l}", s["act"], "stack", dyg, "full", G["ffn_w_down"], l)
        dgate, dup = _ffn_down_bwd(f"ffn_down_bwd_{l}", dyg, W["ffn_w_down"], l, s["gate"], s["up"])
        G["ffn_w_gate"] = _wgrad(f"wg_gate_{l}", s["h2"], "full", dgate, "stack", G["ffn_w_gate"], l)
        G["ffn_w_up"] = _wgrad(f"wg_up_{l}", s["h2"], "full", dup, "stack", G["ffn_w_up"], l)
        dh2 = _ffn_up_bwd(f"ffn_up_bwd_{l}", dgate, dup, W["ffn_w_gate"], W["ffn_w_up"], l)
        dxm, st_n2 = _normmod_bwd(f"norm_ffn_bwd_{l}", s["x_mid"], dh2, dx, gain_ffn, l, mod, KSC2)
        dyg1, st_g1 = _gate_bwd(f"gate1_bwd_{l}", dxm, s["y1"], mod, l, KG1)
        if l % 2 == 0:
            G["ev_w_out"] = _wgrad(f"wg_evout_{l}", s["cat"], "cols", dyg1, "full", G["ev_w_out"], j)
            dcat = _bwd_rows(f"ev_out_bwd_{l}", dyg1, W["ev_w_out"], j)
            db, dcg, du, dwc = _conv_bwd(f"conv_bwd_{l}", s["proj"], conv_full[j], dcat)
            dq, dk, dv, dg, dgr = _retention_bwd(f"ret_bwd_{l}", s["proj"], s["oraw"], s["states"], dcat, cosf, sinf,
                                                 ev_ret_norm_g[j].reshape(1, RET_DIM), lgt)
            dproj = jnp.concatenate([db, dcg, du, dq, dk, dv, dg], axis=1)
            d_conv[j], d_ret[j] = dwc[:CONV_WIDTH], dgr[0]
            G["ev_w_in"] = _wgrad(f"wg_evin_{l}", s["h"], "full", dproj, "cols", G["ev_w_in"], j)
            dh = _bwd_cols(f"ev_in_bwd_{l}", dproj, W["ev_w_in"], j)
        else:
            G["od_w_out"] = _wgrad(f"wg_odout_{l}", s["cat"], "cols", dyg1, "full", G["od_w_out"], j)
            dcat = _bwd_rows(f"od_out_bwd_{l}", dyg1, W["od_w_out"], j)
            dq, dk, dv, dgq, dgk = _sb_bwd(f"sb_bwd_{l}", s["qkv"], od_q_norm_g[j].reshape(1, HEAD),
                                           od_k_norm_g[j].reshape(1, HEAD), s["o32"], dcat)
            dproj = jnp.concatenate([dq, dk, dv], axis=1)
            d_gq[j], d_gk[j] = dgq[0], dgk[0]
            G["od_w_qkv"] = _wgrad(f"wg_odin_{l}", s["h"], "full", dproj, "cols", G["od_w_qkv"], j)
            dh = _bwd_cols(f"od_in_bwd_{l}", dproj, W["od_w_qkv"], j)
        dx, st_n1 = _normmod_bwd(f"norm_mix_bwd_{l}", s["x_in"], dh, dxm, gain_mix, l, mod, KSC1)
        dmod_rows[l] = jnp.stack([st_n1[0], st_n1[1], st_g1[0], st_n2[0], st_n2[1], st_g2[0]]).reshape(6 * D_MODEL)
        d_mix[l], d_ffn[l] = st_n1[2], st_n2[2]
    return lacc, dx, G, (dmod_rows, d_mix, d_ffn, d_ret, d_gq, d_gk, d_conv)


def kernel(x, c, ada_w, ada_b, norm_mix_g, norm_ffn_g, ev_w_in, ev_conv_w, ev_ret_norm_g, ev_w_out, od_w_qkv, od_q_norm_g, od_k_norm_g, od_w_out, ffn_w_gate, ffn_w_up, ffn_w_down, loss_target, m_ada_w, m_ada_b, m_norm_mix_g, m_norm_ffn_g, m_ev_w_in, m_ev_conv_w, m_ev_ret_norm_g, m_ev_w_out, m_od_w_qkv, m_od_q_norm_g, m_od_k_norm_g, m_od_w_out, m_ffn_w_gate, m_ffn_w_up, m_ffn_w_down, v_ada_w, v_ada_b, v_norm_mix_g, v_norm_ffn_g, v_ev_w_in, v_ev_conv_w, v_ev_ret_norm_g, v_ev_w_out, v_od_w_qkv, v_od_q_norm_g, v_od_k_norm_g, v_od_w_out, v_ffn_w_gate, v_ffn_w_up, v_ffn_w_down):
    xi, yi, ci = _position()
    chip = 2 * xi + yi
    dev = 4 * xi + 2 * yi + ci
    x0 = x[0]
    target = loss_target[0]

    n_small = D_MODEL + 2 * CONV_WIDTH * LANE
    small = jnp.concatenate([c.reshape(1, D_MODEL), ev_conv_w.reshape(1, 2 * CONV_WIDTH * LANE)], axis=1)
    small = jnp.broadcast_to(small, (8, n_small))
    g1 = _gather8("gather_cond", small).reshape(N_DEV, 8, n_small)[:, 0, :]
    c_all = g1[:, :D_MODEL]
    conv_all = g1[0::2, D_MODEL:].reshape(N_CHIP, 2, CONV_WIDTH, LANE)
    conv_full = conv_all.transpose(1, 2, 0, 3).reshape(2, CONV_WIDTH, CONV_DIM)

    n_ada = ada_w.shape[-1]
    ada_b_cols = lax.dynamic_slice_in_dim(ada_b, chip * n_ada, n_ada, axis=1).reshape(DEPTH, 1, n_ada)
    mod_cols = _ada_fwd(c_all, ada_w, ada_b_cols)
    g2 = _gather8("gather_mod", mod_cols.reshape(DEPTH * N_DEV, n_ada)).reshape(N_DEV, DEPTH, N_DEV, n_ada)
    mod_mine = lax.dynamic_index_in_dim(g2[0::2], dev, axis=2, keepdims=False)
    mod = mod_mine.transpose(1, 0, 2).reshape(DEPTH, 6, 1, D_MODEL)

    big_names = ["ev_w_in", "ev_w_out", "od_w_qkv", "od_w_out", "ffn_w_gate", "ffn_w_up", "ffn_w_down"]
    big = dict(ev_w_in=ev_w_in, ev_w_out=ev_w_out, od_w_qkv=od_w_qkv, od_w_out=od_w_out, ffn_w_gate=ffn_w_gate,
               ffn_w_up=ffn_w_up, ffn_w_down=ffn_w_down)
    pos_idx = jnp.stack([chip, ci]).astype(jnp.int32)
    shards = [big[k].astype(BF16) for k in big_names]
    W = {k: _place_own(f"place_{k}", w4, s, pos_idx) for k, w4, s in zip(big_names, _gather_weights(shards), shards)}
    G = {k: lax.empty((N_CHIP,) + big[k].shape, F32) for k in big_names}

    lacc, dx, G, small_grads = _local_step(x0, target, mod, W, G, norm_mix_g, norm_ffn_g, conv_full, ev_ret_norm_g,
                                           od_q_norm_g, od_k_norm_g)
    loss = lax.psum(lacc[0, 0], ("x", "y", "c"))
    grad_x = dx[None]
    dmod_rows, d_mix, d_ffn, d_ret, d_gq, d_gk, d_conv = small_grads

    c_idx = jnp.reshape(ci, (1,)).astype(jnp.int32)
    glist = [G[k] for k in big_names]
    recv_a = _exchange_pair(glist)
    psums = [_pair_sum(f"pair_sum_{k}", g, r, c_idx) for k, g, r in zip(big_names, glist, recv_a)]
    recv_b = _scatter_chips(psums)
    halves = [_chip_sum(f"chip_sum_{k}", r, p, pos_idx) for k, r, p in zip(big_names, recv_b, psums)]
    grads = dict(zip(big_names, _share_halves(halves)))

    pieces = [jnp.stack(dmod_rows).reshape(-1), jnp.stack(d_mix).reshape(-1), jnp.stack(d_ffn).reshape(-1),
              jnp.stack(d_ret).reshape(-1), jnp.stack(d_gq).reshape(-1), jnp.stack(d_gk).reshape(-1),
              jnp.stack(d_conv).reshape(-1)]
    sizes = [int(p.shape[0]) for p in pieces]
    n_pack = sum(sizes)
    n_cols = -(-n_pack // (8 * LANE)) * LANE
    packed = jnp.concatenate(pieces + [jnp.zeros((8 * n_cols - n_pack,), F32)]).reshape(8, n_cols)
    g3 = _gather8("gather_small", packed).reshape(N_DEV, 8, n_cols)
    tot = _sum_devices(g3).reshape(-1)
    offs = np.cumsum([0] + sizes)
    part = [tot[offs[i]:offs[i + 1]] for i in range(len(sizes))]
    grads["ada_b"] = part[0].reshape(DEPTH, 6 * D_MODEL)
    grads["norm_mix_g"] = part[1].reshape(DEPTH, D_MODEL)
    grads["norm_ffn_g"] = part[2].reshape(DEPTH, D_MODEL)
    grads["ev_ret_norm_g"] = part[3].reshape(2, RET_DIM)
    grads["od_q_norm_g"] = part[4].reshape(2, HEAD)
    grads["od_k_norm_g"] = part[5].reshape(2, HEAD)
    conv_g = part[6].reshape(2, CONV_WIDTH, CONV_DIM)
    grads["ev_conv_w"] = lax.dynamic_slice_in_dim(conv_g, chip * LANE, LANE, axis=2)
    dmod_all = g3.reshape(N_DEV, -1)[:, :DEPTH * 6 * D_MODEL].reshape(N_DEV, DEPTH, 6 * D_MODEL)
    dmod_cols = lax.dynamic_slice_in_dim(dmod_all, chip * n_ada, n_ada, axis=2).transpose(1, 0, 2)
    grads["ada_w"] = _ada_wgrad(c_all.T, dmod_cols)

    weights = dict(ada_w=ada_w, ada_b=ada_b, norm_mix_g=norm_mix_g, norm_ffn_g=norm_ffn_g, ev_w_in=ev_w_in,
                   ev_conv_w=ev_conv_w, ev_ret_norm_g=ev_ret_norm_g, ev_w_out=ev_w_out, od_w_qkv=od_w_qkv,
                   od_q_norm_g=od_q_norm_g, od_k_norm_g=od_k_norm_g, od_w_out=od_w_out, ffn_w_gate=ffn_w_gate,
                   ffn_w_up=ffn_w_up, ffn_w_down=ffn_w_down)
    m_in = dict(ada_w=m_ada_w, ada_b=m_ada_b, norm_mix_g=m_norm_mix_g, norm_ffn_g=m_norm_ffn_g, ev_w_in=m_ev_w_in,
                ev_conv_w=m_ev_conv_w, ev_ret_norm_g=m_ev_ret_norm_g, ev_w_out=m_ev_w_out, od_w_qkv=m_od_w_qkv,
                od_q_norm_g=m_od_q_norm_g, od_k_norm_g=m_od_k_norm_g, od_w_out=m_od_w_out, ffn_w_gate=m_ffn_w_gate,
                ffn_w_up=m_ffn_w_up, ffn_w_down=m_ffn_w_down)
    v_in = dict(ada_w=v_ada_w, ada_b=v_ada_b, norm_mix_g=v_norm_mix_g, norm_ffn_g=v_norm_ffn_g, ev_w_in=v_ev_w_in,
                ev_conv_w=v_ev_conv_w, ev_ret_norm_g=v_ev_ret_norm_g, ev_w_out=v_ev_w_out, od_w_qkv=v_od_w_qkv,
                od_q_norm_g=v_od_q_norm_g, od_k_norm_g=v_od_k_norm_g, od_w_out=v_od_w_out, ffn_w_gate=v_ffn_w_gate,
                ffn_w_up=v_ffn_w_up, ffn_w_down=v_ffn_w_down)
    order = list(weights)
    deltas, new_m, new_v = {}, {}, {}
    for k in order:
        deltas[k], new_m[k], new_v[k] = _adamw(f"adamw_{k}", weights[k], grads[k], m_in[k], v_in[k])
    return (loss, grad_x, *[grads[k] for k in order], *[deltas[k] for k in order], *[new_m[k] for k in order],
            *[new_v[k] for k in order])
```

```python
import functools

import numpy as np
import jax
import jax.numpy as jnp
from jax import lax
from jax.experimental import pallas as pl
from jax.experimental.pallas import tpu as pltpu

F32 = jnp.float32
BF16 = jnp.bfloat16
MESH = pl.DeviceIdType.MESH

D_MODEL = 1024
DEPTH = 4
N_CHIP = 4
N_DEV = 8
HEAD = 128
RET_HEADS = 4
SB_HEADS = 8
CONV_DIM = 512
RET_DIM = 512
CONV_WIDTH = 3
RET_CHUNK = 64
RET_BLOCK = 256
SB_BLOCK = 256
SB_QROWS = 512
EPS = 1e-6
ROPE_THETA = 10000.0
QK_SCALE = HEAD ** -0.5
LANE = 128
ROW_CHUNK = 512
ROWS_STREAMED = 1024
VMEM_LIMIT_V7X = 56 * 1024 * 1024

ADAM_LR, ADAM_B1, ADAM_B2, ADAM_EPS, ADAM_WD, ADAM_STEP = 0.001, 0.9, 0.999, 1e-08, 0.01, 10


def _cparams(*sem):
    return pltpu.CompilerParams(dimension_semantics=sem or None, vmem_limit_bytes=VMEM_LIMIT_V7X)


def _tile(n, pref):
    if n <= pref:
        return n
    for t in range(pref - pref % 8, 7, -8):
        if n % t == 0:
            return t
    return n


def _sig(v):
    return 1.0 / (1.0 + jnp.exp(-v))


def _dot(a, b, ca, cb):
    return lax.dot_general(a.astype(BF16), b.astype(BF16), (((ca,), (cb,)), ((), ())),
                           preferred_element_type=F32)


def _position():
    x, y, c = lax.axis_index("x"), lax.axis_index("y"), lax.axis_index("c")
    return x, y, c


def _other_chips(x, y):
    return [(1 - x, y), (x, 1 - y), (1 - x, 1 - y)]


def _mm(name, pairs, out_sds, out_specs, grid, contract, red_axis=None, post=None,
        extras=(), extra_specs=(), sum_pairs=True, aliases=None, inner=None):
    n_p, n_ex, n_out = len(pairs), len(extras), len(out_sds)
    n_acc = 1 if sum_pairs else n_p
    n_red = grid[red_axis] if red_axis is not None else 1

    def default_post(accs, ex, outs):
        outs[0][...] = accs[0].astype(outs[0].dtype)

    post_fn = post or default_post

    def kern(*refs):
        ab = refs[:2 * n_p]
        ex = refs[2 * n_p:2 * n_p + n_ex]
        outs = refs[2 * n_p + n_ex:2 * n_p + n_ex + n_out]
        accs = refs[2 * n_p + n_ex + n_out:]
        if inner is None:
            prods = [_dot(ab[2 * p][...], ab[2 * p + 1][...], contract[0], contract[1]) for p in range(n_p)]
        else:
            n_in, a_get, b_get = inner
            prods = []
            for p in range(n_p):
                tot = _dot(a_get(ab[2 * p], 0), b_get(ab[2 * p + 1], 0), contract[0], contract[1])
                for i in range(1, n_in):
                    tot = tot + _dot(a_get(ab[2 * p], i), b_get(ab[2 * p + 1], i), contract[0], contract[1])
                prods.append(tot)
        if sum_pairs:
            tot = prods[0]
            for p_ in prods[1:]:
                tot = tot + p_
            prods = [tot]
        if red_axis is None:
            post_fn(prods, ex, outs)
        else:
            k = pl.program_id(red_axis)

            @pl.when(k == 0)
            def _():
                for a_, p_ in zip(accs, prods):
                    a_[...] = p_

            @pl.when(k > 0)
            def _():
                for a_, p_ in zip(accs, prods):
                    a_[...] += p_

            @pl.when(k == n_red - 1)
            def _():
                post_fn([a_[...] for a_ in accs], ex, outs)

    ins, in_specs = [], []
    for a, b, sa, sb in pairs:
        ins += [a, b]
        in_specs += [sa, sb]
    ins += list(extras)
    in_specs += list(extra_specs)
    scratch = []
    if red_axis is not None:
        scratch = [pltpu.VMEM(tuple(acc_shape), F32) for acc_shape in [_acc_shape(pairs[0], contract)] * n_acc]
    sem = tuple("arbitrary" if ax == red_axis else "parallel" for ax in range(len(grid)))
    res = pl.pallas_call(kern, name=name, grid=grid, in_specs=in_specs, out_specs=list(out_specs),
                         out_shape=list(out_sds), scratch_shapes=scratch,
                         input_output_aliases=aliases or {}, compiler_params=_cparams(*sem))(*ins)
    return res


def _acc_shape(pair, contract):
    sa, sb = pair[2], pair[3]
    da = [d for d in sa.block_shape if d is not None]
    db = [d for d in sb.block_shape if d is not None]
    return (da[1 - contract[0]], db[1 - contract[1]])


def _sds(shape, dtype):
    return jax.ShapeDtypeStruct(tuple(shape), dtype)


def _proj_cols(name, h, w4, l):
    T, K = h.shape
    n = w4.shape[-1]
    tm = _tile(T, ROWS_STREAMED)
    return _mm(name, [(h, w4, pl.BlockSpec((tm, K), lambda i, m: (m, 0)),
                       pl.BlockSpec((None, None, K, n), lambda i, m: (i, l, 0, 0)))],
               [_sds((T, N_CHIP * n), F32)], [pl.BlockSpec((tm, n), lambda i, m: (m, i))],
               (N_CHIP, T // tm), (1, 0))[0]


def _out_proj(name, a, a_stacked, w4, l, xres, mod, lm, kg):
    T = xres.shape[0]
    k = w4.shape[-2]
    tm = _tile(T, 512)
    if a_stacked:
        sa = pl.BlockSpec((N_CHIP, tm, k), lambda m: (0, m, 0))
        a_get = lambda ref, i: ref[i]
    else:
        sa = pl.BlockSpec((tm, N_CHIP * k), lambda m: (m, 0))
        a_get = lambda ref, i: ref[:, i * k:(i + 1) * k]

    def post(accs, ex, outs):
        y = accs[0]
        outs[0][...] = y
        outs[1][...] = ex[0][...] + ex[1][...] * y

    row = pl.BlockSpec((tm, D_MODEL), lambda m: (m, 0))
    return _mm(name, [(a, w4, sa, pl.BlockSpec((N_CHIP, None, k, D_MODEL), lambda m: (0, l, 0, 0)))],
               [_sds((T, D_MODEL), F32)] * 2, [row, row], (T // tm,), (1, 0), post=post,
               extras=[xres, mod], extra_specs=[row, pl.BlockSpec((None, None, 1, D_MODEL), lambda m: (lm, kg, 0, 0))],
               inner=(N_CHIP, a_get, lambda ref, i: ref[i]))


def _ffn_up(name, h2, wg4, wu4, l):
    T, K = h2.shape
    n = wg4.shape[-1]
    tm = _tile(T, ROWS_STREAMED)

    def post(accs, ex, outs):
        g, u = accs
        outs[0][...] = g
        outs[1][...] = u
        outs[2][...] = (g * _sig(g) * u).astype(BF16)

    sa = pl.BlockSpec((tm, K), lambda i, m: (m, 0))
    sw = pl.BlockSpec((None, None, K, n), lambda i, m: (i, l, 0, 0))
    so = pl.BlockSpec((None, tm, n), lambda i, m: (i, m, 0))
    return _mm(name, [(h2, wg4, sa, sw), (h2, wu4, sa, sw)],
               [_sds((N_CHIP, T, n), F32), _sds((N_CHIP, T, n), F32), _sds((N_CHIP, T, n), BF16)], [so, so, so],
               (N_CHIP, T // tm), (1, 0), post=post, sum_pairs=False)


def _bwd_cols(name, dproj, w4, l):
    T = dproj.shape[0]
    K, n = w4.shape[-2:]
    tm = _tile(T, 512)
    return _mm(name, [(dproj, w4, pl.BlockSpec((tm, N_CHIP * n), lambda m: (m, 0)),
                       pl.BlockSpec((N_CHIP, None, K, n), lambda m: (0, l, 0, 0)))],
               [_sds((T, K), F32)], [pl.BlockSpec((tm, K), lambda m: (m, 0))], (T // tm,), (1, 1),
               inner=(N_CHIP, lambda ref, i: ref[:, i * n:(i + 1) * n], lambda ref, i: ref[i]))[0]


def _bwd_rows(name, dy, w4, l):
    T, N = dy.shape
    k = w4.shape[-2]
    tm = _tile(T, ROWS_STREAMED)
    return _mm(name, [(dy, w4, pl.BlockSpec((tm, N), lambda i, m: (m, 0)),
                       pl.BlockSpec((None, None, k, N), lambda i, m: (i, l, 0, 0)))],
               [_sds((T, N_CHIP * k), F32)], [pl.BlockSpec((tm, k), lambda i, m: (m, i))],
               (N_CHIP, T // tm), (1, 1))[0]


def _ffn_down_bwd(name, dy, wd4, l, gate, up):
    T, N = dy.shape
    k = wd4.shape[-2]
    tm = _tile(T, ROWS_STREAMED)

    def post(accs, ex, outs):
        da = accs[0]
        g = ex[0][...]
        u = ex[1][...]
        sg = _sig(g)
        outs[0][...] = (da * u * (sg * (1.0 + g * (1.0 - sg)))).astype(BF16)
        outs[1][...] = (da * (g * sg)).astype(BF16)

    so = pl.BlockSpec((None, tm, k), lambda i, m: (i, m, 0))
    return _mm(name, [(dy, wd4, pl.BlockSpec((tm, N), lambda i, m: (m, 0)),
                       pl.BlockSpec((None, None, k, N), lambda i, m: (i, l, 0, 0)))],
               [_sds((N_CHIP, T, k), BF16)] * 2, [so, so], (N_CHIP, T // tm), (1, 1), post=post,
               extras=[gate, up], extra_specs=[so, so])


def _ffn_up_bwd(name, dgate, dup, wg4, wu4, l):
    _, T, n = dgate.shape
    K = wg4.shape[-2]
    tm = _tile(T, 512)
    sa = pl.BlockSpec((N_CHIP, tm, n), lambda m: (0, m, 0))
    sw = pl.BlockSpec((N_CHIP, None, K, n), lambda m: (0, l, 0, 0))
    pick = lambda ref, i: ref[i]
    return _mm(name, [(dgate, wg4, sa, sw), (dup, wu4, sa, sw)],
               [_sds((T, K), F32)], [pl.BlockSpec((tm, K), lambda m: (m, 0))], (T // tm,), (1, 1),
               inner=(N_CHIP, pick, pick))[0]


def _wgrad(name, a, a_kind, b, b_kind, gbuf, l):
    r, cdim = gbuf.shape[-2:]
    T = a.shape[-2]
    tt = _tile(T, ROWS_STREAMED)

    def spec(kind, w):
        if kind == "full":
            return pl.BlockSpec((tt, w), lambda i, t: (t, 0))
        if kind == "cols":
            return pl.BlockSpec((tt, w), lambda i, t: (t, i))
        return pl.BlockSpec((None, tt, w), lambda i, t: (i, t, 0))

    def post(accs, ex, outs):
        outs[0][...] = accs[0]

    return _mm(name, [(a, b, spec(a_kind, r), spec(b_kind, cdim))], [_sds(gbuf.shape, F32)],
               [pl.BlockSpec((None, None, r, cdim), lambda i, t: (i, l, 0, 0))], (N_CHIP, T // tt), (0, 0),
               red_axis=1, post=post, extras=[gbuf], extra_specs=[pl.BlockSpec(memory_space=pl.ANY)],
               aliases={2: 0})[0]


def _vec_spec(*idx):
    return pl.BlockSpec((None,) * len(idx) + (1, D_MODEL), lambda m: tuple(idx) + (0, 0))


def _normmod(name, x, gain3, l, mod, ksc, ksh):
    T = x.shape[0]
    tm = _tile(T, 512)

    def kern(x_ref, g_ref, sc_ref, sh_ref, h_ref):
        xv = x_ref[...]
        r = lax.rsqrt(jnp.mean(xv * xv, axis=-1, keepdims=True) + EPS)
        h = (xv * r) * g_ref[...]
        h_ref[...] = (h * (1.0 + sc_ref[...]) + sh_ref[...]).astype(BF16)

    row = pl.BlockSpec((tm, D_MODEL), lambda m: (m, 0))
    return pl.pallas_call(kern, name=name, grid=(T // tm,),
                          in_specs=[row, _vec_spec(l), _vec_spec(l, ksc), _vec_spec(l, ksh)], out_specs=row,
                          out_shape=_sds((T, D_MODEL), BF16), compiler_params=_cparams("parallel"))(x, gain3, mod, mod)


def _normmod_bwd(name, x, dh, dres, gain3, l, mod, ksc):
    T = x.shape[0]
    tm = _tile(T, 512)
    nt = T // tm

    def kern(x_ref, dh_ref, dres_ref, g_ref, sc_ref, dx_ref, st_ref):
        m = pl.program_id(0)

        @pl.when(m == 0)
        def _():
            st_ref[...] = jnp.zeros_like(st_ref)

        xv = x_ref[...]
        dhv = dh_ref[...]
        r = lax.rsqrt(jnp.mean(xv * xv, axis=-1, keepdims=True) + EPS)
        xh = xv * r
        wv = g_ref[...] * (1.0 + sc_ref[...])
        dxh = dhv * wv
        dx_ref[...] = dres_ref[...] + r * (dxh - xh * jnp.mean(dxh * xh, axis=-1, keepdims=True))
        st_ref[0:1, :] += jnp.sum(dhv, axis=0, keepdims=True)
        st_ref[1:2, :] += jnp.sum(dhv * xh, axis=0, keepdims=True)

        @pl.when(m == nt - 1)
        def _():
            dw = st_ref[1:2, :]
            st_ref[2:3, :] = dw * (1.0 + sc_ref[...])
            st_ref[1:2, :] = dw * g_ref[...]

    row = pl.BlockSpec((tm, D_MODEL), lambda m: (m, 0))
    return pl.pallas_call(kern, name=name, grid=(nt,),
                          in_specs=[row, row, row, _vec_spec(l), _vec_spec(l, ksc)],
                          out_specs=[row, pl.BlockSpec((8, D_MODEL), lambda m: (0, 0))],
                          out_shape=[_sds((T, D_MODEL), F32), _sds((8, D_MODEL), F32)],
                          compiler_params=_cparams("arbitrary"))(x, dh, dres, gain3, mod)


def _gate_bwd(name, dxn, y, mod, l, kg):
    T = dxn.shape[0]
    tm = _tile(T, 512)

    def kern(d_ref, y_ref, g_ref, dy_ref, st_ref):
        @pl.when(pl.program_id(0) == 0)
        def _():
            st_ref[...] = jnp.zeros_like(st_ref)

        dv = d_ref[...]
        dy_ref[...] = (dv * g_ref[...]).astype(BF16)
        st_ref[0:1, :] += jnp.sum(dv * y_ref[...], axis=0, keepdims=True)

    row = pl.BlockSpec((tm, D_MODEL), lambda m: (m, 0))
    return pl.pallas_call(kern, name=name, grid=(T // tm,), in_specs=[row, row, _vec_spec(l, kg)],
                          out_specs=[row, pl.BlockSpec((8, D_MODEL), lambda m: (0, 0))],
                          out_shape=[_sds((T, D_MODEL), BF16), _sds((8, D_MODEL), F32)],
                          compiler_params=_cparams("arbitrary"))(dxn, y, mod)


def _loss_head(y, target):
    T = y.shape[0]
    tm = _tile(T, 512)

    def kern(y_ref, t_ref, dy_ref, acc_ref):
        @pl.when(pl.program_id(0) == 0)
        def _():
            acc_ref[...] = jnp.zeros_like(acc_ref)

        e = y_ref[...] - t_ref[...]
        dy_ref[...] = e * (1.0 / D_MODEL)
        s = jnp.sum(jnp.sum(e * e, axis=-1, keepdims=True), axis=0, keepdims=True)
        acc_ref[...] += s * (0.5 / D_MODEL)

    row = pl.BlockSpec((tm, D_MODEL), lambda m: (m, 0))
    return pl.pallas_call(kern, name="loss_head", grid=(T // tm,), in_specs=[row, row],
                          out_specs=[row, pl.BlockSpec((8, LANE), lambda m: (0, 0))],
                          out_shape=[_sds((T, D_MODEL), F32), _sds((8, LANE), F32)],
                          compiler_params=_cparams("arbitrary"))(y, target)


def _row_chunks(T):
    rc = min(ROW_CHUNK, T)
    return [(r * rc, rc) for r in range(T // rc)]


def _conv_fwd(name, proj, conv_w):
    T = proj.shape[0]
    nblk = CONV_DIM // LANE

    def kern(b_ref, c_ref, u_ref, w_ref, a_ref, zs):
        zs[0:8, :] = jnp.zeros((8, LANE), F32)
        for r0, rc in _row_chunks(T):
            zs[8 + r0:8 + r0 + rc, :] = c_ref[r0:r0 + rc, :] * u_ref[r0:r0 + rc, :]
        w0, w1, w2 = w_ref[0:1, :], w_ref[1:2, :], w_ref[2:3, :]
        for r0, rc in _row_chunks(T):
            yc = w2 * zs[8 + r0:8 + r0 + rc, :] + w1 * zs[7 + r0:7 + r0 + rc, :] + w0 * zs[6 + r0:6 + r0 + rc, :]
            a_ref[r0:r0 + rc, :] = (b_ref[r0:r0 + rc, :] * yc).astype(BF16)

    col = lambda p: pl.BlockSpec((T, LANE), lambda cb: (0, p * nblk + cb))
    return pl.pallas_call(kern, name=name, grid=(nblk,),
                          in_specs=[col(0), col(1), col(2), pl.BlockSpec((CONV_WIDTH, LANE), lambda cb: (0, cb))],
                          out_specs=pl.BlockSpec((T, LANE), lambda cb: (0, cb)),
                          out_shape=_sds((T, CONV_DIM), BF16), scratch_shapes=[pltpu.VMEM((T + 8, LANE), F32)],
                          compiler_params=_cparams("parallel"))(proj, proj, proj, conv_w)


def _conv_bwd(name, proj, conv_w, dcat):
    T = proj.shape[0]
    nblk = CONV_DIM // LANE

    def kern(b_ref, c_ref, u_ref, w_ref, da_ref, db_ref, dc_ref, du_ref, dw_ref, zs, ds):
        zs[0:8, :] = jnp.zeros((8, LANE), F32)
        ds[T:T + 8, :] = jnp.zeros((8, LANE), F32)
        for r0, rc in _row_chunks(T):
            zs[8 + r0:8 + r0 + rc, :] = c_ref[r0:r0 + rc, :] * u_ref[r0:r0 + rc, :]
        w0, w1, w2 = w_ref[0:1, :], w_ref[1:2, :], w_ref[2:3, :]
        acc = [jnp.zeros((1, LANE), F32) for _ in range(3)]
        for r0, rc in _row_chunks(T):
            z0 = zs[8 + r0:8 + r0 + rc, :]
            z1 = zs[7 + r0:7 + r0 + rc, :]
            z2 = zs[6 + r0:6 + r0 + rc, :]
            da = da_ref[r0:r0 + rc, :]
            db_ref[r0:r0 + rc, :] = (da * (w2 * z0 + w1 * z1 + w0 * z2)).astype(BF16)
            dyc = da * b_ref[r0:r0 + rc, :]
            ds[r0:r0 + rc, :] = dyc
            acc[2] = acc[2] + jnp.sum(dyc * z0, axis=0, keepdims=True)
            acc[1] = acc[1] + jnp.sum(dyc * z1, axis=0, keepdims=True)
            acc[0] = acc[0] + jnp.sum(dyc * z2, axis=0, keepdims=True)
        dw_ref[...] = jnp.zeros_like(dw_ref)
        for k in range(3):
            dw_ref[k:k + 1, :] = acc[k]
        for r0, rc in _row_chunks(T):
            dz = w2 * ds[r0:r0 + rc, :] + w1 * ds[r0 + 1:r0 + 1 + rc, :] + w0 * ds[r0 + 2:r0 + 2 + rc, :]
            dc_ref[r0:r0 + rc, :] = (dz * u_ref[r0:r0 + rc, :]).astype(BF16)
            du_ref[r0:r0 + rc, :] = (dz * c_ref[r0:r0 + rc, :]).astype(BF16)

    col = lambda p: pl.BlockSpec((T, LANE), lambda cb: (0, p * nblk + cb))
    out = pl.BlockSpec((T, LANE), lambda cb: (0, cb))
    return pl.pallas_call(kern, name=name, grid=(nblk,),
                          in_specs=[col(0), col(1), col(2), pl.BlockSpec((CONV_WIDTH, LANE), lambda cb: (0, cb)), out],
                          out_specs=[out, out, out, pl.BlockSpec((8, LANE), lambda cb: (0, cb))],
                          out_shape=[_sds((T, CONV_DIM), BF16)] * 3 + [_sds((8, CONV_DIM), F32)],
                          scratch_shapes=[pltpu.VMEM((T + 8, LANE), F32), pltpu.VMEM((T + 8, LANE), F32)],
                          compiler_params=_cparams("parallel"))(proj, proj, proj, conv_w, dcat)


_Q0, _K0, _V0, _G0 = 3 * CONV_DIM // LANE, (3 * CONV_DIM + RET_DIM) // LANE, (3 * CONV_DIM + 2 * RET_DIM) // LANE, \
    (3 * CONV_DIM + 3 * RET_DIM) // LANE


def _ret_tables(B, lg1):
    ti = lax.broadcasted_iota(jnp.int32, (B, B), 0)
    si = lax.broadcasted_iota(jnp.int32, (B, B), 1)
    dist = jnp.abs(ti - si).astype(F32)
    shift = RET_CHUNK.bit_length() - 1
    dmat = jnp.where((si >> shift) <= (ti >> shift), jnp.exp(dist * lg1), 0.0)
    tcol = lax.broadcasted_iota(jnp.int32, (B, 1), 0).astype(F32)
    qdec = jnp.exp((tcol + 1.0) * lg1)
    kdec = jnp.exp((B - 1.0 - tcol) * lg1)
    bdec = jnp.exp(float(B) * lg1)
    return dmat, qdec, kdec, bdec


def _retention_fwd(name, proj, cosf, sinf, gr, lgt):
    T = proj.shape[0]
    B = min(RET_BLOCK, T)
    nb = T // B

    def kern(q_ref, k_ref, v_ref, g_ref, cos_ref, sin_ref, gr_ref, lg_ref, r_ref, o_ref, st_ref, S):
        @pl.when(pl.program_id(1) == 0)
        def _():
            S[...] = jnp.zeros_like(S)

        cosv, sinv = cos_ref[...], sin_ref[...]
        rot = lambda a: a * cosv + pltpu.roll(a, HEAD // 2, 1) * sinv
        qr = rot(q_ref[...])
        kr = rot(k_ref[...]) * QK_SCALE
        v = v_ref[...]
        dmat, qdec, kdec, bdec = _ret_tables(B, lg_ref[0:1, 0:1])
        sv = S[...]
        st_ref[...] = sv
        pd = _dot(qr, kr, 1, 1) * dmat
        o = _dot(pd, v, 1, 0) + _dot(qr * qdec, sv, 1, 0)
        S[...] = bdec * sv + _dot(kr * kdec, v, 0, 0)
        o_ref[...] = o
        rs = lax.rsqrt(jnp.mean(o * o, axis=-1, keepdims=True) + EPS)
        g = g_ref[...]
        r_ref[...] = (g * _sig(g) * (o * rs * gr_ref[...])).astype(BF16)

    col = lambda c0: pl.BlockSpec((B, HEAD), lambda h, i: (i, c0 + h))
    tab = pl.BlockSpec((B, HEAD), lambda h, i: (i, 0))
    outc = pl.BlockSpec((B, HEAD), lambda h, i: (i, h))
    return pl.pallas_call(
        kern, name=name, grid=(RET_HEADS, nb),
        in_specs=[col(_Q0), col(_K0), col(_V0), col(_G0), tab, tab, pl.BlockSpec((1, HEAD), lambda h, i: (0, h)),
                  pl.BlockSpec((None, 1, LANE), lambda h, i: (h, 0, 0))],
        out_specs=[outc, outc, pl.BlockSpec((None, None, HEAD, HEAD), lambda h, i: (h, i, 0, 0))],
        out_shape=[_sds((T, RET_DIM), BF16), _sds((T, RET_DIM), F32), _sds((RET_HEADS, nb, HEAD, HEAD), F32)],
        scratch_shapes=[pltpu.VMEM((HEAD, HEAD), F32)],
        compiler_params=_cparams("parallel", "arbitrary"))(proj, proj, proj, proj, cosf, sinf, gr, lgt)


def _retention_bwd(name, proj, oraw, states, dcat, cosf, sinf, gr, lgt):
    T = proj.shape[0]
    B = min(RET_BLOCK, T)
    nb = T // B

    def kern(q_ref, k_ref, v_ref, g_ref, o_ref, dr_ref, st_ref, cos_ref, sin_ref, gr_ref, lg_ref,
             dq_ref, dk_ref, dv_ref, dg_ref, dgr_ref, dS):
        @pl.when(pl.program_id(1) == 0)
        def _():
            dS[...] = jnp.zeros_like(dS)
            dgr_ref[...] = jnp.zeros_like(dgr_ref)

        cosv, sinv = cos_ref[...], sin_ref[...]
        rot = lambda a: a * cosv + pltpu.roll(a, HEAD // 2, 1) * sinv
        rot_t = lambda a: a * cosv + pltpu.roll(a * sinv, HEAD // 2, 1)
        qr = rot(q_ref[...])
        kr = rot(k_ref[...]) * QK_SCALE
        v = v_ref[...]
        dmat, qdec, kdec, bdec = _ret_tables(B, lg_ref[0:1, 0:1])
        o = o_ref[...]
        rs = lax.rsqrt(jnp.mean(o * o, axis=-1, keepdims=True) + EPS)
        xh = o * rs
        g = g_ref[...]
        sg = _sig(g)
        grv = gr_ref[...]
        dr = dr_ref[...]
        dn = dr * (g * sg)
        dg_ref[...] = (dr * (xh * grv) * (sg * (1.0 + g * (1.0 - sg)))).astype(BF16)
        dgr_ref[0:1, :] += jnp.sum(dn * xh, axis=0, keepdims=True)
        dxh = dn * grv
        do = rs * (dxh - xh * jnp.mean(dxh * xh, axis=-1, keepdims=True))
        sp = st_ref[...]
        dsv = dS[...]
        pd = _dot(qr, kr, 1, 1) * dmat
        dp = _dot(do, v, 1, 1) * dmat
        dv_ref[...] = (_dot(pd, do, 0, 0) + _dot(kr * kdec, dsv, 1, 0)).astype(BF16)
        dqr = _dot(dp, kr, 1, 0) + _dot(do, sp, 1, 1) * qdec
        dkr = _dot(dp, qr, 0, 0) + _dot(v, dsv, 1, 1) * kdec
        dS[...] = bdec * dsv + _dot(qr * qdec, do, 0, 0)
        dq_ref[...] = rot_t(dqr).astype(BF16)
        dk_ref[...] = (rot_t(dkr) * QK_SCALE).astype(BF16)

    rev = lambda i: nb - 1 - i
    col = lambda c0: pl.BlockSpec((B, HEAD), lambda h, i: (rev(i), c0 + h))
    tab = pl.BlockSpec((B, HEAD), lambda h, i: (rev(i), 0))
    outc = pl.BlockSpec((B, HEAD), lambda h, i: (rev(i), h))
    return pl.pallas_call(
        kern, name=name, grid=(RET_HEADS, nb),
        in_specs=[col(_Q0), col(_K0), col(_V0), col(_G0), outc,
                  pl.BlockSpec((B, HEAD), lambda h, i: (rev(i), CONV_DIM // LANE + h)),
                  pl.BlockSpec((None, None, HEAD, HEAD), lambda h, i: (h, rev(i), 0, 0)), tab, tab,
                  pl.BlockSpec((1, HEAD), lambda h, i: (0, h)), pl.BlockSpec((None, 1, LANE), lambda h, i: (h, 0, 0))],
        out_specs=[outc, outc, outc, outc, pl.BlockSpec((8, HEAD), lambda h, i: (0, h))],
        out_shape=[_sds((T, RET_DIM), BF16)] * 4 + [_sds((8, RET_DIM), F32)],
        scratch_shapes=[pltpu.VMEM((HEAD, HEAD), F32)],
        compiler_params=_cparams("parallel", "arbitrary"))(proj, proj, proj, proj, oraw, dcat, states, cosf, sinf, gr, lgt)


def _strict_upper(q_rows):
    r = lax.broadcasted_iota(jnp.int32, (SB_BLOCK, SB_BLOCK), 0)
    c = lax.broadcasted_iota(jnp.int32, (SB_BLOCK, SB_BLOCK), 1)
    rq = lax.broadcasted_iota(jnp.int32, (q_rows, SB_BLOCK), 0)
    cq = lax.broadcasted_iota(jnp.int32, (q_rows, SB_BLOCK), 1)
    return (r > c).astype(BF16), cq - rq


def _suffix_sum(vals, tri):
    hi = vals.astype(BF16)
    lo = (vals - hi.astype(F32)).astype(BF16)
    dn = (((1,), (0,)), ((), ()))
    return lax.dot_general(hi, tri, dn, preferred_element_type=F32) + lax.dot_general(lo, tri, dn, preferred_element_type=F32)


def _sb_scores(qi, kj, tri, col_minus_row, mask_off):
    z = _dot(qi, kj, 1, 1) * QK_SCALE
    lb = jnp.minimum(z, 0.0) - jnp.log(1.0 + jnp.exp(-jnp.abs(z)))
    if mask_off is not None:
        valid = col_minus_row < mask_off
        lk = jnp.where(valid, lb - z, 0.0)
        w_loc = jnp.where(valid, jnp.exp(lb + _suffix_sum(lk, tri)), 0.0)
        return lb, lk, valid, w_loc
    lk = lb - z
    return lb, lk, None, jnp.exp(lb + _suffix_sum(lk, tri))


def _sb_walk(i, ratio, tile, st):
    for d in reversed(range(ratio)):
        st = tile(ratio * i + d, -SB_BLOCK * d, st)
    n_free = ratio * i

    def pair(p, s):
        j = n_free - 1 - 2 * p
        return tile(j - 1, None, tile(j, None, s))

    st = lax.fori_loop(0, n_free // 2, pair, st)
    return lax.fori_loop(0, n_free % 2, lambda _, s: tile(0, None, s), st)


def _head_norm_rows(src, gain, dst, T):
    for r0, rc in _row_chunks(T):
        a = src[r0:r0 + rc, :]
        r = lax.rsqrt(jnp.mean(a * a, axis=-1, keepdims=True) + EPS)
        dst[r0:r0 + rc, :] = (a * r * gain).astype(BF16)


def _sb_fwd(name, qkv, gq, gk):
    T = qkv.shape[0]
    qr = min(SB_QROWS, T)
    nq, ratio = T // qr, qr // SB_BLOCK

    def kern(q_ref, k_ref, v_ref, gq_ref, gk_ref, o_ref, o32_ref, qn, kn, vb):
        _head_norm_rows(q_ref, gq_ref[...], qn, T)
        _head_norm_rows(k_ref, gk_ref[...], kn, T)
        for r0, rc in _row_chunks(T):
            vb[r0:r0 + rc, :] = v_ref[r0:r0 + rc, :].astype(BF16)
        tri, diag_mask = _strict_upper(qr)

        def qblock(i, _):
            rows_i = pl.ds(pl.multiple_of(i * qr, qr), qr)
            qi = qn[rows_i, :]

            def tile(j, mask_off, st):
                acc, car = st
                rows_j = pl.ds(pl.multiple_of(j * SB_BLOCK, SB_BLOCK), SB_BLOCK)
                _, lk, _, w = _sb_scores(qi, kn[rows_j, :], tri, diag_mask, mask_off)
                w_hi = w.astype(BF16)
                w_lo = (w - w_hi.astype(F32)).astype(BF16)
                vj = vb[rows_j, :]
                acc = acc + jnp.exp(car) * (_dot(w_hi, vj, 1, 0) + _dot(w_lo, vj, 1, 0))
                return acc, car + jnp.sum(lk, axis=-1, keepdims=True)

            acc, _ = _sb_walk(i, ratio, tile, (jnp.zeros((qr, HEAD), F32), jnp.zeros((qr, 1), F32)))
            o_ref[rows_i, :] = acc.astype(BF16)
            o32_ref[rows_i, :] = acc
            return 0

        lax.fori_loop(0, nq, qblock, 0)

    col = lambda c0: pl.BlockSpec((T, HEAD), lambda h: (0, c0 + h))
    vec = pl.BlockSpec((1, HEAD), lambda h: (0, 0))
    out = pl.BlockSpec((T, HEAD), lambda h: (0, h))
    return pl.pallas_call(kern, name=name, grid=(SB_HEADS,),
                          in_specs=[col(0), col(SB_HEADS), col(2 * SB_HEADS), vec, vec], out_specs=[out, out],
                          out_shape=[_sds((T, D_MODEL), BF16), _sds((T, D_MODEL), F32)],
                          scratch_shapes=[pltpu.VMEM((T, HEAD), BF16)] * 3,
                          compiler_params=_cparams("parallel"))(qkv, qkv, qkv, gq, gk)


def _sb_bwd(name, qkv, gq, gk, o32, dcat):
    T = qkv.shape[0]
    qr = min(SB_QROWS, T)
    nq, ratio = T // qr, qr // SB_BLOCK

    def kern(q_ref, k_ref, v_ref, gq_ref, gk_ref, o_ref, do_ref, dq_ref, dk_ref, dv_ref, dgq_ref, dgk_ref,
             qn, kn, vb, dqn, dkn, dvv):
        @pl.when(pl.program_id(0) == 0)
        def _():
            dgq_ref[...] = jnp.zeros_like(dgq_ref)
            dgk_ref[...] = jnp.zeros_like(dgk_ref)

        _head_norm_rows(q_ref, gq_ref[...], qn, T)
        _head_norm_rows(k_ref, gk_ref[...], kn, T)
        for r0, rc in _row_chunks(T):
            vb[r0:r0 + rc, :] = v_ref[r0:r0 + rc, :].astype(BF16)
            dkn[r0:r0 + rc, :] = jnp.zeros((rc, HEAD), F32)
            dvv[r0:r0 + rc, :] = jnp.zeros((rc, HEAD), F32)
        tri, diag_mask = _strict_upper(qr)

        def qblock(i, _):
            rows_i = pl.ds(pl.multiple_of(i * qr, qr), qr)
            qi = qn[rows_i, :]
            doi = do_ref[rows_i, :]
            dob = doi.astype(BF16)
            etot = jnp.sum(dob.astype(F32) * o_ref[rows_i, :], axis=-1, keepdims=True)

            def tile(j, mask_off, st):
                dq_acc, car, ecar = st
                rows_j = pl.ds(pl.multiple_of(j * SB_BLOCK, SB_BLOCK), SB_BLOCK)
                kj = kn[rows_j, :]
                vj = vb[rows_j, :]
                lb, lk, valid, w_loc = _sb_scores(qi, kj, tri, diag_mask, mask_off)
                w = w_loc * jnp.exp(car)
                e = w * _dot(dob, vj, 1, 1)
                suff = _suffix_sum(e, tri) + e + ecar
                sig = jnp.exp(lb)
                dz = (e * (1.0 - sig) - sig * (etot - suff)) * QK_SCALE
                if mask_off is not None:
                    dz = jnp.where(valid, dz, 0.0)
                dzb = dz.astype(BF16)
                dkn[rows_j, :] += _dot(dzb, qi, 0, 0)
                dvv[rows_j, :] += _dot(w, dob, 0, 0)
                return (dq_acc + _dot(dzb, kj, 1, 0), car + jnp.sum(lk, axis=-1, keepdims=True),
                        ecar + jnp.sum(e, axis=-1, keepdims=True))

            zcol = jnp.zeros((qr, 1), F32)
            dq_acc, _, _ = _sb_walk(i, ratio, tile, (jnp.zeros((qr, HEAD), F32), zcol, zcol))
            dqn[rows_i, :] = dq_acc
            return 0

        lax.fori_loop(0, nq, qblock, 0)

        def norm_bwd(src, gain, dnorm, dst, dgain):
            tot = jnp.zeros((1, HEAD), F32)
            for r0, rc in _row_chunks(T):
                a = src[r0:r0 + rc, :]
                r = lax.rsqrt(jnp.mean(a * a, axis=-1, keepdims=True) + EPS)
                xh = a * r
                dn = dnorm[r0:r0 + rc, :]
                tot = tot + jnp.sum(dn * xh, axis=0, keepdims=True)
                dxh = dn * gain
                dst[r0:r0 + rc, :] = (r * (dxh - xh * jnp.mean(dxh * xh, axis=-1, keepdims=True))).astype(BF16)
            dgain[0:1, :] += tot

        norm_bwd(q_ref, gq_ref[...], dqn, dq_ref, dgq_ref)
        norm_bwd(k_ref, gk_ref[...], dkn, dk_ref, dgk_ref)
        for r0, rc in _row_chunks(T):
            dv_ref[r0:r0 + rc, :] = dvv[r0:r0 + rc, :].astype(BF16)

    col = lambda c0: pl.BlockSpec((T, HEAD), lambda h: (0, c0 + h))
    vec = pl.BlockSpec((1, HEAD), lambda h: (0, 0))
    out = pl.BlockSpec((T, HEAD), lambda h: (0, h))
    st = pl.BlockSpec((8, HEAD), lambda h: (0, 0))
    return pl.pallas_call(kern, name=name, grid=(SB_HEADS,),
                          in_specs=[col(0), col(SB_HEADS), col(2 * SB_HEADS), vec, vec, out, out],
                          out_specs=[out, out, out, st, st],
                          out_shape=[_sds((T, D_MODEL), BF16)] * 3 + [_sds((8, HEAD), F32)] * 2,
                          scratch_shapes=[pltpu.VMEM((T, HEAD), BF16)] * 3 + [pltpu.VMEM((T, HEAD), F32)] * 3,
                          compiler_params=_cparams("arbitrary"))(qkv, qkv, qkv, gq, gk, o32, dcat)


def _ada_fwd(c_all, ada_w, ada_b_cols):
    L, K, n = ada_w.shape

    def kern(c_ref, w_ref, b_ref, o_ref):
        cv = c_ref[...]
        o_ref[...] = _dot(cv * _sig(cv), w_ref[...], 1, 0) + b_ref[...]

    return pl.pallas_call(kern, name="ada_fwd", grid=(L,),
                          in_specs=[pl.BlockSpec((N_DEV, K), lambda l: (0, 0)), pl.BlockSpec((None, K, n), lambda l: (l, 0, 0)),
                                    pl.BlockSpec((None, 1, n), lambda l: (l, 0, 0))],
                          out_specs=pl.BlockSpec((None, N_DEV, n), lambda l: (l, 0, 0)),
                          out_shape=_sds((L, N_DEV, n), F32), compiler_params=_cparams("parallel"))(c_all, ada_w, ada_b_cols)


def _ada_wgrad(c_all_t, dmod_cols):
    K = c_all_t.shape[0]
    L, _, n = dmod_cols.shape
    tk = 128

    def kern(c_ref, d_ref, o_ref):
        cv = c_ref[...]
        ca = cv * _sig(cv)
        acc = ca[:, 0:1] * d_ref[0:1, :]
        for b in range(1, N_DEV):
            acc = acc + ca[:, b:b + 1] * d_ref[b:b + 1, :]
        o_ref[...] = acc

    return pl.pallas_call(kern, name="ada_wgrad", grid=(L, K // tk),
                          in_specs=[pl.BlockSpec((tk, N_DEV), lambda l, m: (m, 0)), pl.BlockSpec((None, N_DEV, n), lambda l, m: (l, 0, 0))],
                          out_specs=pl.BlockSpec((None, tk, n), lambda l, m: (l, m, 0)),
                          out_shape=_sds((L, K, n), F32), compiler_params=_cparams("parallel", "parallel"))(c_all_t, dmod_cols)


def _sum_devices(g):
    _, R, n = g.shape

    def kern(g_ref, o_ref):
        acc = g_ref[0]
        for d in range(1, N_DEV):
            acc = acc + g_ref[d]
        o_ref[...] = acc

    return pl.pallas_call(kern, name="sum_devices", out_shape=_sds((R, n), F32))(g)


def _pair_sum(name, g, recv, c_idx):
    n4, L, r, cdim = g.shape
    rows = (L // 2) * r
    tr = _tile(rows, 512)
    gv = g.reshape(n4, 2, rows, cdim)
    rv = recv.reshape(n4, rows, cdim)

    def kern(c_ref, g_ref, r_ref, o_ref):
        o_ref[...] = (g_ref[...] + r_ref[...]).astype(BF16)

    gs = pltpu.PrefetchScalarGridSpec(
        num_scalar_prefetch=1, grid=(n4, rows // tr),
        in_specs=[pl.BlockSpec((None, None, tr, cdim), lambda j, m, cr: (j, cr[0], m, 0)),
                  pl.BlockSpec((None, tr, cdim), lambda j, m, cr: (j, m, 0))],
        out_specs=pl.BlockSpec((None, tr, cdim), lambda j, m, cr: (j, m, 0)))
    out = pl.pallas_call(kern, name=name, grid_spec=gs, out_shape=_sds((n4, rows, cdim), BF16),
                         compiler_params=_cparams("parallel", "parallel"))(c_idx, gv, rv)
    return out.reshape(n4, L // 2, r, cdim)


def _chip_sum(name, recv, psum, pos_idx):
    n4, hl, r, cdim = recv.shape
    rows = hl * r
    tr = _tile(rows, 512)

    def kern(pos_ref, r0, r1, r2, r3, own_ref, o_ref):
        me = pos_ref[0]
        own = own_ref[...].astype(F32)
        terms = [jnp.where(me == s, own, rr[...].astype(F32)) for s, rr in enumerate((r0, r1, r2, r3))]
        o_ref[...] = ((terms[0] + terms[1]) + terms[2]) + terms[3]

    def slot(s):
        return pl.BlockSpec((None, tr, cdim), lambda m, pos: (jnp.where(pos[0] == s, (s + 1) % n4, s), m, 0))

    gs = pltpu.PrefetchScalarGridSpec(
        num_scalar_prefetch=1, grid=(rows // tr,),
        in_specs=[slot(s) for s in range(n4)] + [pl.BlockSpec((None, tr, cdim), lambda m, pos: (pos[0], m, 0))],
        out_specs=pl.BlockSpec((None, tr, cdim), lambda m, pos: (pos[1], m, 0)))
    rv = recv.reshape(n4, rows, cdim)
    out = pl.pallas_call(kern, name=name, grid_spec=gs, out_shape=_sds((2, rows, cdim), F32),
                         compiler_params=_cparams("parallel"))(pos_idx, rv, rv, rv, rv, psum.reshape(n4, rows, cdim))
    return out.reshape(2 * hl, r, cdim)


def _adamw(name, w, g, m, v):
    shape = w.shape
    cols = shape[-1]
    rows = int(np.prod(shape[:-1]))
    tr = _tile(rows, 512) if rows % 8 == 0 else rows
    c1 = 1.0 - ADAM_B1 ** ADAM_STEP
    c2 = 1.0 - ADAM_B2 ** ADAM_STEP

    def kern(w_ref, g_ref, m_ref, v_ref, d_ref, mo_ref, vo_ref):
        gv = g_ref[...]
        mn = ADAM_B1 * m_ref[...] + (1.0 - ADAM_B1) * gv
        vn = ADAM_B2 * v_ref[...] + (1.0 - ADAM_B2) * (gv * gv)
        mo_ref[...] = mn
        vo_ref[...] = vn
        d_ref[...] = -ADAM_LR * ((mn / c1) / (jnp.sqrt(vn / c2) + ADAM_EPS) + ADAM_WD * w_ref[...])

    blk = pl.BlockSpec((tr, cols), lambda i: (i, 0))
    outs = pl.pallas_call(kern, name=name, grid=(rows // tr,), in_specs=[blk] * 4, out_specs=[blk] * 3,
                          out_shape=[_sds((rows, cols), F32)] * 3, compiler_params=_cparams("parallel"))(
        *[a.reshape(rows, cols) for a in (w, g, m, v)])
    return tuple(o.reshape(shape) for o in outs)


def _rcopy(src, dst, ssem, rsem, dev):
    return pltpu.make_async_remote_copy(src_ref=src, dst_ref=dst, send_sem=ssem, recv_sem=rsem, device_id=dev,
                                        device_id_type=MESH)


def _gather8(name, blk):
    m_per, n = blk.shape

    def body(x_ref, out_ref, send_sems, recv_sems, local_sem):
        x, y, c = _position()
        me, sibling = (x, y, c), (x, y, 1 - c)
        chips = _other_chips(x, y)

        def rows(px, py, pc):
            return out_ref.at[pl.ds((4 * px + 2 * py + pc) * m_per, m_per), :]

        def copy(k, block, to, src=None):
            return _rcopy(rows(*block) if src is None else src, rows(*block), send_sems.at[k], recv_sems.at[k], to)

        mine = pltpu.make_async_copy(x_ref, rows(*me), local_sem)
        mine.start()
        first = [copy(0, me, sibling, src=x_ref)]
        first += [copy(1 + j, me, (*chip, c), src=x_ref) for j, chip in enumerate(chips)]
        for cp in first:
            cp.start()
        passed = [copy(4 + j, (*chip, c), sibling) for j, chip in enumerate(chips)]
        for j, chip in enumerate(chips):
            copy(1 + j, (*chip, c), me).wait_recv()
            passed[j].start()
        copy(0, sibling, me).wait_recv()
        for j, chip in enumerate(chips):
            copy(4 + j, (*chip, 1 - c), me).wait_recv()
        for cp in first + passed:
            cp.wait_send()
        mine.wait()

    return pl.pallas_call(body, name=name, out_shape=_sds((N_DEV * m_per, n), blk.dtype),
                          in_specs=[pl.BlockSpec(memory_space=pltpu.VMEM)], out_specs=pl.BlockSpec(memory_space=pltpu.VMEM),
                          scratch_shapes=[pltpu.SemaphoreType.DMA((7,)), pltpu.SemaphoreType.DMA((7,)), pltpu.SemaphoreType.DMA],
                          compiler_params=pltpu.CompilerParams(vmem_limit_bytes=VMEM_LIMIT_V7X))(blk)


_ANY = pl.BlockSpec(memory_space=pl.ANY)


def _gather_weights(shards):
    n = len(shards)

    def body(*refs):
        src, out = refs[:n], refs[n:2 * n]
        send_sems, recv_sems, fsend_sems, frecv_sems = refs[2 * n:]
        x, y, c = _position()
        sibling = (x, y, 1 - c)
        chips = _other_chips(x, y)
        me_chip = 2 * x + y
        started = []
        for t in range(n):
            hl = src[t].shape[0] // 2
            mine_half = pl.ds(c * hl, hl)
            for j, (px, py) in enumerate(chips):
                cp = _rcopy(src[t].at[mine_half], out[t].at[me_chip, mine_half], send_sems.at[3 * t + j],
                            recv_sems.at[3 * t + j], (px, py, c))
                cp.start()
                started.append(cp)
        for t in range(n):
            hl = src[t].shape[0] // 2
            mine_half = pl.ds(c * hl, hl)
            for j, (px, py) in enumerate(chips):
                landed = out[t].at[2 * px + py, mine_half]
                _rcopy(landed, landed, send_sems.at[3 * t + j], recv_sems.at[3 * t + j], (px, py, c)).wait_recv()
                fw = _rcopy(landed, landed, fsend_sems.at[3 * t + j], frecv_sems.at[3 * t + j], sibling)
                fw.start()
                started.append(fw)
        for t in range(n):
            hl = src[t].shape[0] // 2
            other_half = pl.ds((1 - c) * hl, hl)
            for j, (px, py) in enumerate(chips):
                landed = out[t].at[2 * px + py, other_half]
                _rcopy(landed, landed, fsend_sems.at[3 * t + j], frecv_sems.at[3 * t + j], sibling).wait_recv()
        for cp in started:
            cp.wait_send()

    sems = [pltpu.SemaphoreType.DMA((3 * n,))] * 4
    return pl.pallas_call(body, name="gather_weights", out_shape=[_sds((N_CHIP,) + s.shape, s.dtype) for s in shards],
                          in_specs=[_ANY] * n, out_specs=[_ANY] * n, scratch_shapes=sems)(*shards)


def _place_own(name, w4, shard, pos_idx):
    n4, L, r, cdim = w4.shape
    rows = L * r
    tr = _tile(rows, 1024)

    def kern(pos_ref, s_ref, w_ref, o_ref):
        o_ref[...] = s_ref[...]

    gs = pltpu.PrefetchScalarGridSpec(
        num_scalar_prefetch=1, grid=(rows // tr,),
        in_specs=[pl.BlockSpec((tr, cdim), lambda m, pos: (m, 0)), _ANY],
        out_specs=pl.BlockSpec((None, tr, cdim), lambda m, pos: (pos[0], m, 0)))
    out = pl.pallas_call(kern, name=name, grid_spec=gs, out_shape=_sds((n4, rows, cdim), w4.dtype),
                         input_output_aliases={2: 0}, compiler_params=_cparams("parallel"))(
        pos_idx, shard.reshape(rows, cdim), w4.reshape(n4, rows, cdim))
    return out.reshape(w4.shape)


def _exchange_pair(grads):
    n = len(grads)

    def body(*refs):
        src, out = refs[:n], refs[n:2 * n]
        send_sems, recv_sems = refs[2 * n:]
        x, y, c = _position()
        sibling = (x, y, 1 - c)
        cps = []
        for t in range(n):
            hl = src[t].shape[1] // 2
            cp = _rcopy(src[t].at[:, pl.ds((1 - c) * hl, hl)], out[t], send_sems.at[t], recv_sems.at[t], sibling)
            cp.start()
            cps.append(cp)
        for cp in cps:
            cp.wait_recv()
        for cp in cps:
            cp.wait_send()

    sems = [pltpu.SemaphoreType.DMA((n,))] * 2
    return pl.pallas_call(body, name="exchange_pair",
                          out_shape=[_sds((g.shape[0], g.shape[1] // 2) + g.shape[2:], g.dtype) for g in grads],
                          in_specs=[_ANY] * n, out_specs=[_ANY] * n, scratch_shapes=sems)(*grads)


def _scatter_chips(psums):
    n = len(psums)

    def body(*refs):
        src, out = refs[:n], refs[n:2 * n]
        send_sems, recv_sems = refs[2 * n:]
        x, y, c = _position()
        chips = _other_chips(x, y)
        me_chip = 2 * x + y
        cps = []
        for t in range(n):
            for j, (px, py) in enumerate(chips):
                cp = _rcopy(src[t].at[2 * px + py], out[t].at[me_chip], send_sems.at[3 * t + j], recv_sems.at[3 * t + j],
                            (px, py, c))
                cp.start()
                cps.append(cp)
        for t in range(n):
            for j, (px, py) in enumerate(chips):
                slot = out[t].at[2 * px + py]
                _rcopy(slot, slot, send_sems.at[3 * t + j], recv_sems.at[3 * t + j], (px, py, c)).wait_recv()
        for cp in cps:
            cp.wait_send()

    sems = [pltpu.SemaphoreType.DMA((3 * n,))] * 2
    return pl.pallas_call(body, name="scatter_chips", out_shape=[_sds(p.shape, p.dtype) for p in psums],
                          in_specs=[_ANY] * n, out_specs=[_ANY] * n, scratch_shapes=sems)(*psums)


def _share_halves(full):
    n = len(full)

    def body(*refs):
        out = refs[n:2 * n]
        send_sems, recv_sems = refs[2 * n:]
        x, y, c = _position()
        sibling = (x, y, 1 - c)
        cps = []
        for t in range(n):
            hl = out[t].shape[0] // 2
            mine = out[t].at[pl.ds(c * hl, hl)]
            cp = _rcopy(mine, mine, send_sems.at[t], recv_sems.at[t], sibling)
            cp.start()
            cps.append(cp)
        for t in range(n):
            hl = out[t].shape[0] // 2
            theirs = out[t].at[pl.ds((1 - c) * hl, hl)]
            _rcopy(theirs, theirs, send_sems.at[t], recv_sems.at[t], sibling).wait_recv()
        for cp in cps:
            cp.wait_send()

    sems = [pltpu.SemaphoreType.DMA((n,))] * 2
    return pl.pallas_call(body, name="share_halves", out_shape=[_sds(h.shape, h.dtype) for h in full],
                          in_specs=[_ANY] * n, out_specs=[_ANY] * n, scratch_shapes=sems,
                          input_output_aliases={t: t for t in range(n)})(*full)


def _rope_tables(T):
    inv_freq = 1.0 / (ROPE_THETA ** (jnp.arange(0, HEAD, 2, dtype=F32) / HEAD))
    ang = jnp.arange(T, dtype=F32)[:, None] * inv_freq[None, :]
    cos, sin = jnp.cos(ang), jnp.sin(ang)
    return jnp.concatenate([cos, cos], axis=-1), jnp.concatenate([-sin, sin], axis=-1)


def _decay_table():
    lg = np.log1p(-np.exp2(-5.0 - np.arange(RET_HEADS, dtype=np.float32))).astype(np.float32)
    return jnp.asarray(np.broadcast_to(lg[:, None, None], (RET_HEADS, 1, LANE)).copy())


def _local_step(x0, target, mod, W, G, norm_mix_g, norm_ffn_g, conv_full, ev_ret_norm_g, od_q_norm_g, od_k_norm_g):
    T = x0.shape[0]
    KSH1, KSC1, KG1, KSH2, KSC2, KG2 = range(6)
    gain_mix = norm_mix_g.reshape(DEPTH, 1, D_MODEL)
    gain_ffn = norm_ffn_g.reshape(DEPTH, 1, D_MODEL)
    cosf, sinf = _rope_tables(T)
    lgt = _decay_table()

    saved = []
    xcur = x0
    for l in range(DEPTH):
        j = l // 2
        s = dict(x_in=xcur)
        h = _normmod(f"norm_mix_{l}", xcur, gain_mix, l, mod, KSC1, KSH1)
        s["h"] = h
        if l % 2 == 0:
            proj = _proj_cols(f"ev_in_{l}", h, W["ev_w_in"], j)
            a = _conv_fwd(f"conv_{l}", proj, conv_full[j])
            r, oraw, states = _retention_fwd(f"ret_{l}", proj, cosf, sinf, ev_ret_norm_g[j].reshape(1, RET_DIM), lgt)
            cat = jnp.concatenate([a, r], axis=1)
            s.update(proj=proj, oraw=oraw, states=states, cat=cat)
            y, xmid = _out_proj(f"ev_out_{l}", cat, False, W["ev_w_out"], j, xcur, mod, l, KG1)
        else:
            qkv = _proj_cols(f"od_in_{l}", h, W["od_w_qkv"], j)
            o, o32 = _sb_fwd(f"sb_{l}", qkv, od_q_norm_g[j].reshape(1, HEAD), od_k_norm_g[j].reshape(1, HEAD))
            s.update(qkv=qkv, cat=o, o32=o32)
            y, xmid = _out_proj(f"od_out_{l}", o, False, W["od_w_out"], j, xcur, mod, l, KG1)
        s.update(y1=y, x_mid=xmid)
        h2 = _normmod(f"norm_ffn_{l}", xmid, gain_ffn, l, mod, KSC2, KSH2)
        gate, up, act = _ffn_up(f"ffn_up_{l}", h2, W["ffn_w_gate"], W["ffn_w_up"], l)
        y2, xcur = _out_proj(f"ffn_down_{l}", act, True, W["ffn_w_down"], l, xmid, mod, l, KG2)
        s.update(h2=h2, gate=gate, up=up, act=act, y2=y2)
        saved.append(s)

    dy, lacc = _loss_head(xcur, target)

    dmod_rows = [None] * DEPTH
    d_mix = [None] * DEPTH
    d_ffn = [None] * DEPTH
    d_conv = [None] * 2
    d_ret = [None] * 2
    d_gq = [None] * 2
    d_gk = [None] * 2
    dx = dy
    for l in reversed(range(DEPTH)):
        j = l // 2
        s = saved[l]
        dyg, st_g2 = _gate_bwd(f"gate2_bwd_{l}", dx, s["y2"], mod, l, KG2)
        G["ffn_w_down"] = _wgrad(f"wg_down_{l}", s["act"], "stack", dyg, "full", G["ffn_w_down"], l)
        dgate, dup = _ffn_down_bwd(f"ffn_down_bwd_{l}", dyg, W["ffn_w_down"], l, s["gate"], s["up"])
        G["ffn_w_gate"] = _wgrad(f"wg_gate_{l}", s["h2"], "full", dgate, "stack", G["ffn_w_gate"], l)
        G["ffn_w_up"] = _wgrad(f"wg_up_{l}", s["h2"], "full", dup, "stack", G["ffn_w_up"], l)
        dh2 = _ffn_up_bwd(f"ffn_up_bwd_{l}", dgate, dup, W["ffn_w_gate"], W["ffn_w_up"], l)
        dxm, st_n2 = _normmod_bwd(f"norm_ffn_bwd_{l}", s["x_mid"], dh2, dx, gain_ffn, l, mod, KSC2)
        dyg1, st_g1 = _gate_bwd(f"gate1_bwd_{l}", dxm, s["y1"], mod, l, KG1)
        if l % 2 == 0:
            G["ev_w_out"] = _wgrad(f"wg_evout_{l}", s["cat"], "cols", dyg1, "full", G["ev_w_out"], j)
            dcat = _bwd_rows(f"ev_out_bwd_{l}", dyg1, W["ev_w_out"], j)
            db, dcg, du, dwc = _conv_bwd(f"conv_bwd_{l}", s["proj"], conv_full[j], dcat)
            dq, dk, dv, dg, dgr = _retention_bwd(f"ret_bwd_{l}", s["proj"], s["oraw"], s["states"], dcat, cosf, sinf,
                                                 ev_ret_norm_g[j].reshape(1, RET_DIM), lgt)
            dproj = jnp.concatenate([db, dcg, du, dq, dk, dv, dg], axis=1)
            d_conv[j], d_ret[j] = dwc[:CONV_WIDTH], dgr[0]
            G["ev_w_in"] = _wgrad(f"wg_evin_{l}", s["h"], "full", dproj, "cols", G["ev_w_in"], j)
            dh = _bwd_cols(f"ev_in_bwd_{l}", dproj, W["ev_w_in"], j)
        else:
            G["od_w_out"] = _wgrad(f"wg_odout_{l}", s["cat"], "cols", dyg1, "full", G["od_w_out"], j)
            dcat = _bwd_rows(f"od_out_bwd_{l}", dyg1, W["od_w_out"], j)
            dq, dk, dv, dgq, dgk = _sb_bwd(f"sb_bwd_{l}", s["qkv"], od_q_norm_g[j].reshape(1, HEAD),
                                           od_k_norm_g[j].reshape(1, HEAD), s["o32"], dcat)
            dproj = jnp.concatenate([dq, dk, dv], axis=1)
            d_gq[j], d_gk[j] = dgq[0], dgk[0]
            G["od_w_qkv"] = _wgrad(f"wg_odin_{l}", s["h"], "full", dproj, "cols", G["od_w_qkv"], j)
            dh = _bwd_cols(f"od_in_bwd_{l}", dproj, W["od_w_qkv"], j)
        dx, st_n1 = _normmod_bwd(f"norm_mix_bwd_{l}", s["x_in"], dh, dxm, gain_mix, l, mod, KSC1)
        dmod_rows[l] = jnp.stack([st_n1[0], st_n1[1], st_g1[0], st_n2[0], st_n2[1], st_g2[0]]).reshape(6 * D_MODEL)
        d_mix[l], d_ffn[l] = st_n1[2], st_n2[2]
    return lacc, dx, G, (dmod_rows, d_mix, d_ffn, d_ret, d_gq, d_gk, d_conv)


def kernel(x, c, ada_w, ada_b, norm_mix_g, norm_ffn_g, ev_w_in, ev_conv_w, ev_ret_norm_g, ev_w_out, od_w_qkv, od_q_norm_g, od_k_norm_g, od_w_out, ffn_w_gate, ffn_w_up, ffn_w_down, loss_target, m_ada_w, m_ada_b, m_norm_mix_g, m_norm_ffn_g, m_ev_w_in, m_ev_conv_w, m_ev_ret_norm_g, m_ev_w_out, m_od_w_qkv, m_od_q_norm_g, m_od_k_norm_g, m_od_w_out, m_ffn_w_gate, m_ffn_w_up, m_ffn_w_down, v_ada_w, v_ada_b, v_norm_mix_g, v_norm_ffn_g, v_ev_w_in, v_ev_conv_w, v_ev_ret_norm_g, v_ev_w_out, v_od_w_qkv, v_od_q_norm_g, v_od_k_norm_g, v_od_w_out, v_ffn_w_gate, v_ffn_w_up, v_ffn_w_down):
    xi, yi, ci = _position()
    chip = 2 * xi + yi
    dev = 4 * xi + 2 * yi + ci
    x0 = x[0]
    target = loss_target[0]

    n_small = D_MODEL + 2 * CONV_WIDTH * LANE
    small = jnp.concatenate([c.reshape(1, D_MODEL), ev_conv_w.reshape(1, 2 * CONV_WIDTH * LANE)], axis=1)
    small = jnp.broadcast_to(small, (8, n_small))
    g1 = _gather8("gather_cond", small).reshape(N_DEV, 8, n_small)[:, 0, :]
    c_all = g1[:, :D_MODEL]
    conv_all = g1[0::2, D_MODEL:].reshape(N_CHIP, 2, CONV_WIDTH, LANE)
    conv_full = conv_all.transpose(1, 2, 0, 3).reshape(2, CONV_WIDTH, CONV_DIM)

    n_ada = ada_w.shape[-1]
    ada_b_cols = lax.dynamic_slice_in_dim(ada_b, chip * n_ada, n_ada, axis=1).reshape(DEPTH, 1, n_ada)
    mod_cols = _ada_fwd(c_all, ada_w, ada_b_cols)
    g2 = _gather8("gather_mod", mod_cols.reshape(DEPTH * N_DEV, n_ada)).reshape(N_DEV, DEPTH, N_DEV, n_ada)
    mod_mine = lax.dynamic_index_in_dim(g2[0::2], dev, axis=2, keepdims=False)
    mod = mod_mine.transpose(1, 0, 2).reshape(DEPTH, 6, 1, D_MODEL)

    big_names = ["ev_w_in", "ev_w_out", "od_w_qkv", "od_w_out", "ffn_w_gate", "ffn_w_up", "ffn_w_down"]
    big = dict(ev_w_in=ev_w_in, ev_w_out=ev_w_out, od_w_qkv=od_w_qkv, od_w_out=od_w_out, ffn_w_gate=ffn_w_gate,
               ffn_w_up=ffn_w_up, ffn_w_down=ffn_w_down)
    pos_idx = jnp.stack([chip, ci]).astype(jnp.int32)
    shards = [big[k].astype(BF16) for k in big_names]
    W = {k: _place_own(f"place_{k}", w4, s, pos_idx) for k, w4, s in zip(big_names, _gather_weights(shards), shards)}
    G = {k: lax.empty((N_CHIP,) + big[k].shape, F32) for k in big_names}

    lacc, dx, G, small_grads = _local_step(x0, target, mod, W, G, norm_mix_g, norm_ffn_g, conv_full, ev_ret_norm_g,
                                           od_q_norm_g, od_k_norm_g)
    loss = lax.psum(lacc[0, 0], ("x", "y", "c"))
    grad_x = dx[None]
    dmod_rows, d_mix, d_ffn, d_ret, d_gq, d_gk, d_conv = small_grads

    c_idx = jnp.reshape(ci, (1,)).astype(jnp.int32)
    glist = [G[k] for k in big_names]
    recv_a = _exchange_pair(glist)
    psums = [_pair_sum(f"pair_sum_{k}", g, r, c_idx) for k, g, r in zip(big_names, glist, recv_a)]
    recv_b = _scatter_chips(psums)
    halves = [_chip_sum(f"chip_sum_{k}", r, p, pos_idx) for k, r, p in zip(big_names, recv_b, psums)]
    grads = dict(zip(big_names, _share_halves(halves)))

    pieces = [jnp.stack(dmod_rows).reshape(-1), jnp.stack(d_mix).reshape(-1), jnp.stack(d_ffn).reshape(-1),
              jnp.stack(d_ret).reshape(-1), jnp.stack(d_gq).reshape(-1), jnp.stack(d_gk).reshape(-1),
              jnp.stack(d_conv).reshape(-1)]
    sizes = [int(p.shape[0]) for p in pieces]
    n_pack = sum(sizes)
    n_cols = -(-n_pack // (8 * LANE)) * LANE
    packed = jnp.concatenate(pieces + [jnp.zeros((8 * n_cols - n_pack,), F32)]).reshape(8, n_cols)
    g3 = _gather8("gather_small", packed).reshape(N_DEV, 8, n_cols)
    tot = _sum_devices(g3).reshape(-1)
    offs = np.cumsum([0] + sizes)
    part = [tot[offs[i]:offs[i + 1]] for i in range(len(sizes))]
    grads["ada_b"] = part[0].reshape(DEPTH, 6 * D_MODEL)
    grads["norm_mix_g"] = part[1].reshape(DEPTH, D_MODEL)
    grads["norm_ffn_g"] = part[2].reshape(DEPTH, D_MODEL)
    grads["ev_ret_norm_g"] = part[3].reshape(2, RET_DIM)
    grads["od_q_norm_g"] = part[4].reshape(2, HEAD)
    grads["od_k_norm_g"] = part[5].reshape(2, HEAD)
    conv_g = part[6].reshape(2, CONV_WIDTH, CONV_DIM)
    grads["ev_conv_w"] = lax.dynamic_slice_in_dim(conv_g, chip * LANE, LANE, axis=2)
    dmod_all = g3.reshape(N_DEV, -1)[:, :DEPTH * 6 * D_MODEL].reshape(N_DEV, DEPTH, 6 * D_MODEL)
    dmod_cols = lax.dynamic_slice_in_dim(dmod_all, chip * n_ada, n_ada, axis=2).transpose(1, 0, 2)
    grads["ada_w"] = _ada_wgrad(c_all.T, dmod_cols)

    weights = dict(ada_w=ada_w, ada_b=ada_b, norm_mix_g=norm_mix_g, norm_ffn_g=norm_ffn_g, ev_w_in=ev_w_in,
                   ev_conv_w=ev_conv_w, ev_ret_norm_g=ev_ret_norm_g, ev_w_out=ev_w_out, od_w_qkv=od_w_qkv,
                   od_q_norm_g=od_q_norm_g, od_k_norm_g=od_k_norm_g, od_w_out=od_w_out, ffn_w_gate=ffn_w_gate,
                   ffn_w_up=ffn_w_up, ffn_w_down=ffn_w_down)
    m_in = dict(ada_w=m_ada_w, ada_b=m_ada_b, norm_mix_g=m_norm_mix_g, norm_ffn_g=m_norm_ffn_g, ev_w_in=m_ev_w_in,
                ev_conv_w=m_ev_conv_w, ev_ret_norm_g=m_ev_ret_norm_g, ev_w_out=m_ev_w_out, od_w_qkv=m_od_w_qkv,
                od_q_norm_g=m_od_q_norm_g, od_k_norm_g=m_od_k_norm_g, od_w_out=m_od_w_out, ffn_w_gate=m_ffn_w_gate,
                ffn_w_up=m_ffn_w_up, ffn_w_down=m_ffn_w_down)
    v_in = dict(ada_w=v_ada_w, ada_b=v_ada_b, norm_mix_g=v_norm_mix_g, norm_ffn_g=v_norm_ffn_g, ev_w_in=v_ev_w_in,
                ev_conv_w=v_ev_conv_w, ev_ret_norm_g=v_ev_ret_norm_g, ev_w_out=v_ev_w_out, od_w_qkv=v_od_w_qkv,
                od_q_norm_g=v_od_q_norm_g, od_k_norm_g=v_od_k_norm_g, od_w_out=v_od_w_out, ffn_w_gate=v_ffn_w_gate,
                ffn_w_up=v_ffn_w_up, ffn_w_down=v_ffn_w_down)
    order = list(weights)
    deltas, new_m, new_v = {}, {}, {}
    for k in order:
        deltas[k], new_m[k], new_v[k] = _adamw(f"adamw_{k}", weights[k], grads[k], m_in[k], v_in[k])
    return (loss, grad_x, *[grads[k] for k in order], *[deltas[k] for k in order], *[new_m[k] for k in order],
            *[new_v[k] for k in order])
```

```python
import functools

import numpy as np
import jax
import jax.numpy as jnp
from jax import lax
from jax.experimental import pallas as pl
from jax.experimental.pallas import tpu as pltpu

F32 = jnp.float32
BF16 = jnp.bfloat16
MESH = pl.DeviceIdType.MESH

D_MODEL = 1024
DEPTH = 4
N_CHIP = 4
N_DEV = 8
HEAD = 128
RET_HEADS = 4
SB_HEADS = 8
CONV_DIM = 512
RET_DIM = 512
CONV_WIDTH = 3
RET_CHUNK = 64
RET_BLOCK = 256
SB_BLOCK = 256
SB_QROWS = 512
EPS = 1e-6
ROPE_THETA = 10000.0
QK_SCALE = HEAD ** -0.5
LANE = 128
ROW_CHUNK = 512
ROWS_STREAMED = 1024
VMEM_LIMIT_V7X = 56 * 1024 * 1024

ADAM_LR, ADAM_B1, ADAM_B2, ADAM_EPS, ADAM_WD, ADAM_STEP = 0.001, 0.9, 0.999, 1e-08, 0.01, 10


def _cparams(*sem):
    return pltpu.CompilerParams(dimension_semantics=sem or None, vmem_limit_bytes=VMEM_LIMIT_V7X)


def _tile(n, pref):
    if n <= pref:
        return n
    for t in range(pref - pref % 8, 7, -8):
        if n % t == 0:
            return t
    return n


def _sig(v):
    return 1.0 / (1.0 + jnp.exp(-v))


def _dot(a, b, ca, cb):
    return lax.dot_general(a.astype(BF16), b.astype(BF16), (((ca,), (cb,)), ((), ())),
                           preferred_element_type=F32)


def _position():
    x, y, c = lax.axis_index("x"), lax.axis_index("y"), lax.axis_index("c")
    return x, y, c


def _other_chips(x, y):
    return [(1 - x, y), (x, 1 - y), (1 - x, 1 - y)]


def _mm(name, pairs, out_sds, out_specs, grid, contract, red_axis=None, post=None,
        extras=(), extra_specs=(), sum_pairs=True, aliases=None, inner=None):
    n_p, n_ex, n_out = len(pairs), len(extras), len(out_sds)
    n_acc = 1 if sum_pairs else n_p
    n_red = grid[red_axis] if red_axis is not None else 1

    def default_post(accs, ex, outs):
        outs[0][...] = accs[0].astype(outs[0].dtype)

    post_fn = post or default_post

    def kern(*refs):
        ab = refs[:2 * n_p]
        ex = refs[2 * n_p:2 * n_p + n_ex]
        outs = refs[2 * n_p + n_ex:2 * n_p + n_ex + n_out]
        accs = refs[2 * n_p + n_ex + n_out:]
        if inner is None:
            prods = [_dot(ab[2 * p][...], ab[2 * p + 1][...], contract[0], contract[1]) for p in range(n_p)]
        else:
            n_in, a_get, b_get = inner
            prods = []
            for p in range(n_p):
                tot = _dot(a_get(ab[2 * p], 0), b_get(ab[2 * p + 1], 0), contract[0], contract[1])
                for i in range(1, n_in):
                    tot = tot + _dot(a_get(ab[2 * p], i), b_get(ab[2 * p + 1], i), contract[0], contract[1])
                prods.append(tot)
        if sum_pairs:
            tot = prods[0]
            for p_ in prods[1:]:
                tot = tot + p_
            prods = [tot]
        if red_axis is None:
            post_fn(prods, ex, outs)
        else:
            k = pl.program_id(red_axis)

            @pl.when(k == 0)
            def _():
                for a_, p_ in zip(accs, prods):
                    a_[...] = p_

            @pl.when(k > 0)
            def _():
                for a_, p_ in zip(accs, prods):
                    a_[...] += p_

            @pl.when(k == n_red - 1)
            def _():
                post_fn([a_[...] for a_ in accs], ex, outs)

    ins, in_specs = [], []
    for a, b, sa, sb in pairs:
        ins += [a, b]
        in_specs += [sa, sb]
    ins += list(extras)
    in_specs += list(extra_specs)
    scratch = []
    if red_axis is not None:
        scratch = [pltpu.VMEM(tuple(acc_shape), F32) for acc_shape in [_acc_shape(pairs[0], contract)] * n_acc]
    sem = tuple("arbitrary" if ax == red_axis else "parallel" for ax in range(len(grid)))
    res = pl.pallas_call(kern, name=name, grid=grid, in_specs=in_specs, out_specs=list(out_specs),
                         out_shape=list(out_sds), scratch_shapes=scratch,
                         input_output_aliases=aliases or {}, compiler_params=_cparams(*sem))(*ins)
    return res


def _acc_shape(pair, contract):
    sa, sb = pair[2], pair[3]
    da = [d for d in sa.block_shape if d is not None]
    db = [d for d in sb.block_shape if d is not None]
    return (da[1 - contract[0]], db[1 - contract[1]])


def _sds(shape, dtype):
    return jax.ShapeDtypeStruct(tuple(shape), dtype)


def _proj_cols(name, h, w4, l):
    T, K = h.shape
    n = w4.shape[-1]
    tm = _tile(T, ROWS_STREAMED)
    return _mm(name, [(h, w4, pl.BlockSpec((tm, K), lambda i, m: (m, 0)),
                       pl.BlockSpec((None, None, K, n), lambda i, m: (i, l, 0, 0)))],
               [_sds((T, N_CHIP * n), F32)], [pl.BlockSpec((tm, n), lambda i, m: (m, i))],
               (N_CHIP, T // tm), (1, 0))[0]


def _out_proj(name, a, a_stacked, w4, l, xres, mod, lm, kg):
    T = xres.shape[0]
    k = w4.shape[-2]
    tm = _tile(T, 512)
    if a_stacked:
        sa = pl.BlockSpec((N_CHIP, tm, k), lambda m: (0, m, 0))
        a_get = lambda ref, i: ref[i]
    else:
        sa = pl.BlockSpec((tm, N_CHIP * k), lambda m: (m, 0))
        a_get = lambda ref, i: ref[:, i * k:(i + 1) * k]

    def post(accs, ex, outs):
        y = accs[0]
        outs[0][...] = y
        outs[1][...] = ex[0][...] + ex[1][...] * y

    row = pl.BlockSpec((tm, D_MODEL), lambda m: (m, 0))
    return _mm(name, [(a, w4, sa, pl.BlockSpec((N_CHIP, None, k, D_MODEL), lambda m: (0, l, 0, 0)))],
               [_sds((T, D_MODEL), F32)] * 2, [row, row], (T // tm,), (1, 0), post=post,
               extras=[xres, mod], extra_specs=[row, pl.BlockSpec((None, None, 1, D_MODEL), lambda m: (lm, kg, 0, 0))],
               inner=(N_CHIP, a_get, lambda ref, i: ref[i]))


def _ffn_up(name, h2, wg4, wu4, l):
    T, K = h2.shape
    n = wg4.shape[-1]
    tm = _tile(T, ROWS_STREAMED)

    def post(accs, ex, outs):
        g, u = accs
        outs[0][...] = g
        outs[1][...] = u
        outs[2][...] = (g * _sig(g) * u).astype(BF16)

    sa = pl.BlockSpec((tm, K), lambda i, m: (m, 0))
    sw = pl.BlockSpec((None, None, K, n), lambda i, m: (i, l, 0, 0))
    so = pl.BlockSpec((None, tm, n), lambda i, m: (i, m, 0))
    return _mm(name, [(h2, wg4, sa, sw), (h2, wu4, sa, sw)],
               [_sds((N_CHIP, T, n), F32), _sds((N_CHIP, T, n), F32), _sds((N_CHIP, T, n), BF16)], [so, so, so],
               (N_CHIP, T // tm), (1, 0), post=post, sum_pairs=False)


def _bwd_cols(name, dproj, w4, l):
    T = dproj.shape[0]
    K, n = w4.shape[-2:]
    tm = _tile(T, 512)
    return _mm(name, [(dproj, w4, pl.BlockSpec((tm, N_CHIP * n), lambda m: (m, 0)),
                       pl.BlockSpec((N_CHIP, None, K, n), lambda m: (0, l, 0, 0)))],
               [_sds((T, K), F32)], [pl.BlockSpec((tm, K), lambda m: (m, 0))], (T // tm,), (1, 1),
               inner=(N_CHIP, lambda ref, i: ref[:, i * n:(i + 1) * n], lambda ref, i: ref[i]))[0]


def _bwd_rows(name, dy, w4, l):
    T, N = dy.shape
    k = w4.shape[-2]
    tm = _tile(T, ROWS_STREAMED)
    return _mm(name, [(dy, w4, pl.BlockSpec((tm, N), lambda i, m: (m, 0)),
                       pl.BlockSpec((None, None, k, N), lambda i, m: (i, l, 0, 0)))],
               [_sds((T, N_CHIP * k), F32)], [pl.BlockSpec((tm, k), lambda i, m: (m, i))],
               (N_CHIP, T // tm), (1, 1))[0]


def _ffn_down_bwd(name, dy, wd4, l, gate, up):
    T, N = dy.shape
    k = wd4.shape[-2]
    tm = _tile(T, ROWS_STREAMED)

    def post(accs, ex, outs):
        da = accs[0]
        g = ex[0][...]
        u = ex[1][...]
        sg = _sig(g)
        outs[0][...] = (da * u * (sg * (1.0 + g * (1.0 - sg)))).astype(BF16)
        outs[1][...] = (da * (g * sg)).astype(BF16)

    so = pl.BlockSpec((None, tm, k), lambda i, m: (i, m, 0))
    return _mm(name, [(dy, wd4, pl.BlockSpec((tm, N), lambda i, m: (m, 0)),
                       pl.BlockSpec((None, None, k, N), lambda i, m: (i, l, 0, 0)))],
               [_sds((N_CHIP, T, k), BF16)] * 2, [so, so], (N_CHIP, T // tm), (1, 1), post=post,
               extras=[gate, up], extra_specs=[so, so])


def _ffn_up_bwd(name, dgate, dup, wg4, wu4, l):
    _, T, n = dgate.shape
    K = wg4.shape[-2]
    tm = _tile(T, 512)
    sa = pl.BlockSpec((N_CHIP, tm, n), lambda m: (0, m, 0))
    sw = pl.BlockSpec((N_CHIP, None, K, n), lambda m: (0, l, 0, 0))
    pick = lambda ref, i: ref[i]
    return _mm(name, [(dgate, wg4, sa, sw), (dup, wu4, sa, sw)],
               [_sds((T, K), F32)], [pl.BlockSpec((tm, K), lambda m: (m, 0))], (T // tm,), (1, 1),
               inner=(N_CHIP, pick, pick))[0]


def _wgrad(name, a, a_kind, b, b_kind, gbuf, l):
    r, cdim = gbuf.shape[-2:]
    T = a.shape[-2]
    tt = _tile(T, ROWS_STREAMED)

    def spec(kind, w):
        if kind == "full":
            return pl.BlockSpec((tt, w), lambda i, t: (t, 0))
        if kind == "cols":
            return pl.BlockSpec((tt, w), lambda i, t: (t, i))
        return pl.BlockSpec((None, tt, w), lambda i, t: (i, t, 0))

    def post(accs, ex, outs):
        outs[0][...] = accs[0]

    return _mm(name, [(a, b, spec(a_kind, r), spec(b_kind, cdim))], [_sds(gbuf.shape, F32)],
               [pl.BlockSpec((None, None, r, cdim), lambda i, t: (i, l, 0, 0))], (N_CHIP, T // tt), (0, 0),
               red_axis=1, post=post, extras=[gbuf], extra_specs=[pl.BlockSpec(memory_space=pl.ANY)],
               aliases={2: 0})[0]


def _vec_spec(*idx):
    return pl.BlockSpec((None,) * len(idx) + (1, D_MODEL), lambda m: tuple(idx) + (0, 0))


def _normmod(name, x, gain3, l, mod, ksc, ksh):
    T = x.shape[0]
    tm = _tile(T, 512)

    def kern(x_ref, g_ref, sc_ref, sh_ref, h_ref):
        xv = x_ref[...]
        r = lax.rsqrt(jnp.mean(xv * xv, axis=-1, keepdims=True) + EPS)
        h = (xv * r) * g_ref[...]
        h_ref[...] = (h * (1.0 + sc_ref[...]) + sh_ref[...]).astype(BF16)

    row = pl.BlockSpec((tm, D_MODEL), lambda m: (m, 0))
    return pl.pallas_call(kern, name=name, grid=(T // tm,),
                          in_specs=[row, _vec_spec(l), _vec_spec(l, ksc), _vec_spec(l, ksh)], out_specs=row,
                          out_shape=_sds((T, D_MODEL), BF16), compiler_params=_cparams("parallel"))(x, gain3, mod, mod)


def _normmod_bwd(name, x, dh, dres, gain3, l, mod, ksc):
    T = x.shape[0]
    tm = _tile(T, 512)
    nt = T // tm

    def kern(x_ref, dh_ref, dres_ref, g_ref, sc_ref, dx_ref, st_ref):
        m = pl.program_id(0)

        @pl.when(m == 0)
        def _():
            st_ref[...] = jnp.zeros_like(st_ref)

        xv = x_ref[...]
        dhv = dh_ref[...]
        r = lax.rsqrt(jnp.mean(xv * xv, axis=-1, keepdims=True) + EPS)
        xh = xv * r
        wv = g_ref[...] * (1.0 + sc_ref[...])
        dxh = dhv * wv
        dx_ref[...] = dres_ref[...] + r * (dxh - xh * jnp.mean(dxh * xh, axis=-1, keepdims=True))
        st_ref[0:1, :] += jnp.sum(dhv, axis=0, keepdims=True)
        st_ref[1:2, :] += jnp.sum(dhv * xh, axis=0, keepdims=True)

        @pl.when(m == nt - 1)
        def _():
            dw = st_ref[1:2, :]
            st_ref[2:3, :] = dw * (1.0 + sc_ref[...])
            st_ref[1:2, :] = dw * g_ref[...]

    row = pl.BlockSpec((tm, D_MODEL), lambda m: (m, 0))
    return pl.pallas_call(kern, name=name, grid=(nt,),
                          in_specs=[row, row, row, _vec_spec(l), _vec_spec(l, ksc)],
                          out_specs=[row, pl.BlockSpec((8, D_MODEL), lambda m: (0, 0))],
                          out_shape=[_sds((T, D_MODEL), F32), _sds((8, D_MODEL), F32)],
                          compiler_params=_cparams("arbitrary"))(x, dh, dres, gain3, mod)


def _gate_bwd(name, dxn, y, mod, l, kg):
    T = dxn.shape[0]
    tm = _tile(T, 512)

    def kern(d_ref, y_ref, g_ref, dy_ref, st_ref):
        @pl.when(pl.program_id(0) == 0)
        def _():
            st_ref[...] = jnp.zeros_like(st_ref)

        dv = d_ref[...]
        dy_ref[...] = (dv * g_ref[...]).astype(BF16)
        st_ref[0:1, :] += jnp.sum(dv * y_ref[...], axis=0, keepdims=True)

    row = pl.BlockSpec((tm, D_MODEL), lambda m: (m, 0))
    return pl.pallas_call(kern, name=name, grid=(T // tm,), in_specs=[row, row, _vec_spec(l, kg)],
                          out_specs=[row, pl.BlockSpec((8, D_MODEL), lambda m: (0, 0))],
                          out_shape=[_sds((T, D_MODEL), BF16), _sds((8, D_MODEL), F32)],
                          compiler_params=_cparams("arbitrary"))(dxn, y, mod)


def _loss_head(y, target):
    T = y.shape[0]
    tm = _tile(T, 512)

    def kern(y_ref, t_ref, dy_ref, acc_ref):
        @pl.when(pl.program_id(0) == 0)
        def _():
            acc_ref[...] = jnp.zeros_like(acc_ref)

        e = y_ref[...] - t_ref[...]
        dy_ref[...] = e * (1.0 / D_MODEL)
        s = jnp.sum(jnp.sum(e * e, axis=-1, keepdims=True), axis=0, keepdims=True)
        acc_ref[...] += s * (0.5 / D_MODEL)

    row = pl.BlockSpec((tm, D_MODEL), lambda m: (m, 0))
    return pl.pallas_call(kern, name="loss_head", grid=(T // tm,), in_specs=[row, row],
                          out_specs=[row, pl.BlockSpec((8, LANE), lambda m: (0, 0))],
                          out_shape=[_sds((T, D_MODEL), F32), _sds((8, LANE), F32)],
                          compiler_params=_cparams("arbitrary"))(y, target)


def _row_chunks(T):
    rc = min(ROW_CHUNK, T)
    return [(r * rc, rc) for r in range(T // rc)]


def _conv_fwd(name, proj, conv_w):
    T = proj.shape[0]
    nblk = CONV_DIM // LANE

    def kern(b_ref, c_ref, u_ref, w_ref, a_ref, zs):
        zs[0:8, :] = jnp.zeros((8, LANE), F32)
        for r0, rc in _row_chunks(T):
            zs[8 + r0:8 + r0 + rc, :] = c_ref[r0:r0 + rc, :] * u_ref[r0:r0 + rc, :]
        w0, w1, w2 = w_ref[0:1, :], w_ref[1:2, :], w_ref[2:3, :]
        for r0, rc in _row_chunks(T):
            yc = w2 * zs[8 + r0:8 + r0 + rc, :] + w1 * zs[7 + r0:7 + r0 + rc, :] + w0 * zs[6 + r0:6 + r0 + rc, :]
            a_ref[r0:r0 + rc, :] = (b_ref[r0:r0 + rc, :] * yc).astype(BF16)

    col = lambda p: pl.BlockSpec((T, LANE), lambda cb: (0, p * nblk + cb))
    return pl.pallas_call(kern, name=name, grid=(nblk,),
                          in_specs=[col(0), col(1), col(2), pl.BlockSpec((CONV_WIDTH, LANE), lambda cb: (0, cb))],
                          out_specs=pl.BlockSpec((T, LANE), lambda cb: (0, cb)),
                          out_shape=_sds((T, CONV_DIM), BF16), scratch_shapes=[pltpu.VMEM((T + 8, LANE), F32)],
                          compiler_params=_cparams("parallel"))(proj, proj, proj, conv_w)


def _conv_bwd(name, proj, conv_w, dcat):
    T = proj.shape[0]
    nblk = CONV_DIM // LANE

    def kern(b_ref, c_ref, u_ref, w_ref, da_ref, db_ref, dc_ref, du_ref, dw_ref, zs, ds):
        zs[0:8, :] = jnp.zeros((8, LANE), F32)
        ds[T:T + 8, :] = jnp.zeros((8, LANE), F32)
        for r0, rc in _row_chunks(T):
            zs[8 + r0:8 + r0 + rc, :] = c_ref[r0:r0 + rc, :] * u_ref[r0:r0 + rc, :]
        w0, w1, w2 = w_ref[0:1, :], w_ref[1:2, :], w_ref[2:3, :]
        acc = [jnp.zeros((1, LANE), F32) for _ in range(3)]
        for r0, rc in _row_chunks(T):
            z0 = zs[8 + r0:8 + r0 + rc, :]
            z1 = zs[7 + r0:7 + r0 + rc, :]
            z2 = zs[6 + r0:6 + r0 + rc, :]
            da = da_ref[r0:r0 + rc, :]
            db_ref[r0:r0 + rc, :] = (da * (w2 * z0 + w1 * z1 + w0 * z2)).astype(BF16)
            dyc = da * b_ref[r0:r0 + rc, :]
            ds[r0:r0 + rc, :] = dyc
            acc[2] = acc[2] + jnp.sum(dyc * z0, axis=0, keepdims=True)
            acc[1] = acc[1] + jnp.sum(dyc * z1, axis=0, keepdims=True)
            acc[0] = acc[0] + jnp.sum(dyc * z2, axis=0, keepdims=True)
        dw_ref[...] = jnp.zeros_like(dw_ref)
        for k in range(3):
            dw_ref[k:k + 1, :] = acc[k]
        for r0, rc in _row_chunks(T):
            dz = w2 * ds[r0:r0 + rc, :] + w1 * ds[r0 + 1:r0 + 1 + rc, :] + w0 * ds[r0 + 2:r0 + 2 + rc, :]
            dc_ref[r0:r0 + rc, :] = (dz * u_ref[r0:r0 + rc, :]).astype(BF16)
            du_ref[r0:r0 + rc, :] = (dz * c_ref[r0:r0 + rc, :]).astype(BF16)

    col = lambda p: pl.BlockSpec((T, LANE), lambda cb: (0, p * nblk + cb))
    out = pl.BlockSpec((T, LANE), lambda cb: (0, cb))
    return pl.pallas_call(kern, name=name, grid=(nblk,),
                          in_specs=[col(0), col(1), col(2), pl.BlockSpec((CONV_WIDTH, LANE), lambda cb: (0, cb)), out],
                          out_specs=[out, out, out, pl.BlockSpec((8, LANE), lambda cb: (0, cb))],
                          out_shape=[_sds((T, CONV_DIM), BF16)] * 3 + [_sds((8, CONV_DIM), F32)],
                          scratch_shapes=[pltpu.VMEM((T + 8, LANE), F32), pltpu.VMEM((T + 8, LANE), F32)],
                          compiler_params=_cparams("parallel"))(proj, proj, proj, conv_w, dcat)


_Q0, _K0, _V0, _G0 = 3 * CONV_DIM // LANE, (3 * CONV_DIM + RET_DIM) // LANE, (3 * CONV_DIM + 2 * RET_DIM) // LANE, \
    (3 * CONV_DIM + 3 * RET_DIM) // LANE


def _ret_tables(B, lg1):
    ti = lax.broadcasted_iota(jnp.int32, (B, B), 0)
    si = lax.broadcasted_iota(jnp.int32, (B, B), 1)
    dist = jnp.abs(ti - si).astype(F32)
    shift = RET_CHUNK.bit_length() - 1
    dmat = jnp.where((si >> shift) <= (ti >> shift), jnp.exp(dist * lg1), 0.0)
    tcol = lax.broadcasted_iota(jnp.int32, (B, 1), 0).astype(F32)
    qdec = jnp.exp((tcol + 1.0) * lg1)
    kdec = jnp.exp((B - 1.0 - tcol) * lg1)
    bdec = jnp.exp(float(B) * lg1)
    return dmat, qdec, kdec, bdec


def _retention_fwd(name, proj, cosf, sinf, gr, lgt):
    T = proj.shape[0]
    B = min(RET_BLOCK, T)
    nb = T // B

    def kern(q_ref, k_ref, v_ref, g_ref, cos_ref, sin_ref, gr_ref, lg_ref, r_ref, o_ref, st_ref, S):
        @pl.when(pl.program_id(1) == 0)
        def _():
            S[...] = jnp.zeros_like(S)

        cosv, sinv = cos_ref[...], sin_ref[...]
        rot = lambda a: a * cosv + pltpu.roll(a, HEAD // 2, 1) * sinv
        qr = rot(q_ref[...])
        kr = rot(k_ref[...]) * QK_SCALE
        v = v_ref[...]
        dmat, qdec, kdec, bdec = _ret_tables(B, lg_ref[0:1, 0:1])
        sv = S[...]
        st_ref[...] = sv
        pd = _dot(qr, kr, 1, 1) * dmat
        o = _dot(pd, v, 1, 0) + _dot(qr * qdec, sv, 1, 0)
        S[...] = bdec * sv + _dot(kr * kdec, v, 0, 0)
        o_ref[...] = o
        rs = lax.rsqrt(jnp.mean(o * o, axis=-1, keepdims=True) + EPS)
        g = g_ref[...]
        r_ref[...] = (g * _sig(g) * (o * rs * gr_ref[...])).astype(BF16)

    col = lambda c0: pl.BlockSpec((B, HEAD), lambda h, i: (i, c0 + h))
    tab = pl.BlockSpec((B, HEAD), lambda h, i: (i, 0))
    outc = pl.BlockSpec((B, HEAD), lambda h, i: (i, h))
    return pl.pallas_call(
        kern, name=name, grid=(RET_HEADS, nb),
        in_specs=[col(_Q0), col(_K0), col(_V0), col(_G0), tab, tab, pl.BlockSpec((1, HEAD), lambda h, i: (0, h)),
                  pl.BlockSpec((None, 1, LANE), lambda h, i: (h, 0, 0))],
        out_specs=[outc, outc, pl.BlockSpec((None, None, HEAD, HEAD), lambda h, i: (h, i, 0, 0))],
        out_shape=[_sds((T, RET_DIM), BF16), _sds((T, RET_DIM), F32), _sds((RET_HEADS, nb, HEAD, HEAD), F32)],
        scratch_shapes=[pltpu.VMEM((HEAD, HEAD), F32)],
        compiler_params=_cparams("parallel", "arbitrary"))(proj, proj, proj, proj, cosf, sinf, gr, lgt)


def _retention_bwd(name, proj, oraw, states, dcat, cosf, sinf, gr, lgt):
    T = proj.shape[0]
    B = min(RET_BLOCK, T)
    nb = T // B

    def kern(q_ref, k_ref, v_ref, g_ref, o_ref, dr_ref, st_ref, cos_ref, sin_ref, gr_ref, lg_ref,
             dq_ref, dk_ref, dv_ref, dg_ref, dgr_ref, dS):
        @pl.when(pl.program_id(1) == 0)
        def _():
            dS[...] = jnp.zeros_like(dS)
            dgr_ref[...] = jnp.zeros_like(dgr_ref)

        cosv, sinv = cos_ref[...], sin_ref[...]
        rot = lambda a: a * cosv + pltpu.roll(a, HEAD // 2, 1) * sinv
        rot_t = lambda a: a * cosv + pltpu.roll(a * sinv, HEAD // 2, 1)
        qr = rot(q_ref[...])
        kr = rot(k_ref[...]) * QK_SCALE
        v = v_ref[...]
        dmat, qdec, kdec, bdec = _ret_tables(B, lg_ref[0:1, 0:1])
        o = o_ref[...]
        rs = lax.rsqrt(jnp.mean(o * o, axis=-1, keepdims=True) + EPS)
        xh = o * rs
        g = g_ref[...]
        sg = _sig(g)
        grv = gr_ref[...]
        dr = dr_ref[...]
        dn = dr * (g * sg)
        dg_ref[...] = (dr * (xh * grv) * (sg * (1.0 + g * (1.0 - sg)))).astype(BF16)
        dgr_ref[0:1, :] += jnp.sum(dn * xh, axis=0, keepdims=True)
        dxh = dn * grv
        do = rs * (dxh - xh * jnp.mean(dxh * xh, axis=-1, keepdims=True))
        sp = st_ref[...]
        dsv = dS[...]
        pd = _dot(qr, kr, 1, 1) * dmat
        dp = _dot(do, v, 1, 1) * dmat
        dv_ref[...] = (_dot(pd, do, 0, 0) + _dot(kr * kdec, dsv, 1, 0)).astype(BF16)
        dqr = _dot(dp, kr, 1, 0) + _dot(do, sp, 1, 1) * qdec
        dkr = _dot(dp, qr, 0, 0) + _dot(v, dsv, 1, 1) * kdec
        dS[...] = bdec * dsv + _dot(qr * qdec, do, 0, 0)
        dq_ref[...] = rot_t(dqr).astype(BF16)
        dk_ref[...] = (rot_t(dkr) * QK_SCALE).astype(BF16)

    rev = lambda i: nb - 1 - i
    col = lambda c0: pl.BlockSpec((B, HEAD), lambda h, i: (rev(i), c0 + h))
    tab = pl.BlockSpec((B, HEAD), lambda h, i: (rev(i), 0))
    outc = pl.BlockSpec((B, HEAD), lambda h, i: (rev(i), h))
    return pl.pallas_call(
        kern, name=name, grid=(RET_HEADS, nb),
        in_specs=[col(_Q0), col(_K0), col(_V0), col(_G0), outc,
                  pl.BlockSpec((B, HEAD), lambda h, i: (rev(i), CONV_DIM // LANE + h)),
                  pl.BlockSpec((None, None, HEAD, HEAD), lambda h, i: (h, rev(i), 0, 0)), tab, tab,
                  pl.BlockSpec((1, HEAD), lambda h, i: (0, h)), pl.BlockSpec((None, 1, LANE), lambda h, i: (h, 0, 0))],
        out_specs=[outc, outc, outc, outc, pl.BlockSpec((8, HEAD), lambda h, i: (0, h))],
        out_shape=[_sds((T, RET_DIM), BF16)] * 4 + [_sds((8, RET_DIM), F32)],
        scratch_shapes=[pltpu.VMEM((HEAD, HEAD), F32)],
        compiler_params=_cparams("parallel", "arbitrary"))(proj, proj, proj, proj, oraw, dcat, states, cosf, sinf, gr, lgt)


def _strict_upper(q_rows):
    r = lax.broadcasted_iota(jnp.int32, (SB_BLOCK, SB_BLOCK), 0)
    c = lax.broadcasted_iota(jnp.int32, (SB_BLOCK, SB_BLOCK), 1)
    rq = lax.broadcasted_iota(jnp.int32, (q_rows, SB_BLOCK), 0)
    cq = lax.broadcasted_iota(jnp.int32, (q_rows, SB_BLOCK), 1)
    return (r > c).astype(BF16), cq - rq


def _suffix_sum(vals, tri):
    hi = vals.astype(BF16)
    lo = (vals - hi.astype(F32)).astype(BF16)
    dn = (((1,), (0,)), ((), ()))
    return lax.dot_general(hi, tri, dn, preferred_element_type=F32) + lax.dot_general(lo, tri, dn, preferred_element_type=F32)


def _sb_scores(qi, kj, tri, col_minus_row, mask_off):
    z = _dot(qi, kj, 1, 1) * QK_SCALE
    lb = jnp.minimum(z, 0.0) - jnp.log(1.0 + jnp.exp(-jnp.abs(z)))
    if mask_off is not None:
        valid = col_minus_row < mask_off
        lk = jnp.where(valid, lb - z, 0.0)
        w_loc = jnp.where(valid, jnp.exp(lb + _suffix_sum(lk, tri)), 0.0)
        return lb, lk, valid, w_loc
    lk = lb - z
    return lb, lk, None, jnp.exp(lb + _suffix_sum(lk, tri))


def _sb_walk(i, ratio, tile, st):
    for d in reversed(range(ratio)):
        st = tile(ratio * i + d, -SB_BLOCK * d, st)
    n_free = ratio * i

    def pair(p, s):
        j = n_free - 1 - 2 * p
        return tile(j - 1, None, tile(j, None, s))

    st = lax.fori_loop(0, n_free // 2, pair, st)
    return lax.fori_loop(0, n_free % 2, lambda _, s: tile(0, None, s), st)


def _head_norm_rows(src, gain, dst, T):
    for r0, rc in _row_chunks(T):
        a = src[r0:r0 + rc, :]
        r = lax.rsqrt(jnp.mean(a * a, axis=-1, keepdims=True) + EPS)
        dst[r0:r0 + rc, :] = (a * r * gain).astype(BF16)


def _sb_fwd(name, qkv, gq, gk, gather=None):
    T = qkv.shape[0]
    qr = min(SB_QROWS, T)
    nq, ratio = T // qr, qr // SB_BLOCK
    n_g = len(gather[0]) if gather else 0

    def kern(q_ref, k_ref, v_ref, gq_ref, gk_ref, *rest):
        shard_refs, rest = rest[:n_g], rest[2 * n_g:]
        o_ref, o32_ref, rest = rest[0], rest[1], rest[2:]
        w4_refs, rest = rest[:n_g], rest[n_g:]
        qn, kn, vb, sems = rest[0], rest[1], rest[2], rest[3:]
        head = pl.program_id(0)
        if gather:
            @pl.when(head == 0)
            def _():
                _gather_start(shard_refs, w4_refs, sems, gather[2])

        _head_norm_rows(q_ref, gq_ref[...], qn, T)
        _head_norm_rows(k_ref, gk_ref[...], kn, T)
        for r0, rc in _row_chunks(T):
            vb[r0:r0 + rc, :] = v_ref[r0:r0 + rc, :].astype(BF16)
        tri, diag_mask = _strict_upper(qr)

        def qblock(i, _):
            rows_i = pl.ds(pl.multiple_of(i * qr, qr), qr)
            qi = qn[rows_i, :]

            def tile(j, mask_off, st):
                acc, car = st
                rows_j = pl.ds(pl.multiple_of(j * SB_BLOCK, SB_BLOCK), SB_BLOCK)
                _, lk, _, w = _sb_scores(qi, kn[rows_j, :], tri, diag_mask, mask_off)
                w_hi = w.astype(BF16)
                w_lo = (w - w_hi.astype(F32)).astype(BF16)
                vj = vb[rows_j, :]
                acc = acc + jnp.exp(car) * (_dot(w_hi, vj, 1, 0) + _dot(w_lo, vj, 1, 0))
                return acc, car + jnp.sum(lk, axis=-1, keepdims=True)

            acc, _ = _sb_walk(i, ratio, tile, (jnp.zeros((qr, HEAD), F32), jnp.zeros((qr, 1), F32)))
            o_ref[rows_i, :] = acc.astype(BF16)
            o32_ref[rows_i, :] = acc
            return 0

        lax.fori_loop(0, nq, qblock, 0)
        if gather:
            @pl.when(head == SB_HEADS - 2)
            def _():
                _gather_forward(shard_refs, w4_refs, sems, gather[2])

            @pl.when(head == SB_HEADS - 1)
            def _():
                _gather_finish(shard_refs, w4_refs, sems, gather[2])

    col = lambda c0: pl.BlockSpec((T, HEAD), lambda h: (0, c0 + h))
    vec = pl.BlockSpec((1, HEAD), lambda h: (0, 0))
    out = pl.BlockSpec((T, HEAD), lambda h: (0, h))
    extra_in = list(gather[0]) + list(gather[1]) if gather else []
    res = pl.pallas_call(kern, name=name, grid=(SB_HEADS,),
                         in_specs=[col(0), col(SB_HEADS), col(2 * SB_HEADS), vec, vec] + [_ANY] * (2 * n_g),
                         out_specs=[out, out] + [_ANY] * n_g,
                         out_shape=[_sds((T, D_MODEL), BF16), _sds((T, D_MODEL), F32)]
                         + [_sds(w.shape, w.dtype) for w in (gather[1] if gather else [])],
                         scratch_shapes=[pltpu.VMEM((T, HEAD), BF16)] * 3 + (_gather_sems(n_g) if gather else []),
                         input_output_aliases={5 + n_g + t: 2 + t for t in range(n_g)},
                         compiler_params=_cparams("arbitrary" if gather else "parallel"))(qkv, qkv, qkv, gq, gk, *extra_in)
    return res[0], res[1], list(res[2:])


def _sb_bwd(name, qkv, gq, gk, o32, dcat, scatter=None):
    T = qkv.shape[0]
    qr = min(SB_QROWS, T)
    nq, ratio = T // qr, qr // SB_BLOCK
    n_s = len(scatter[0]) if scatter else 0

    def kern(q_ref, k_ref, v_ref, gq_ref, gk_ref, o_ref, do_ref, *rest):
        ps_refs, rest = rest[:n_s], rest[n_s:]
        dq_ref, dk_ref, dv_ref, dgq_ref, dgk_ref = rest[:5]
        slot_refs, rest = rest[5:5 + n_s], rest[5 + n_s:]
        qn, kn, vb, dqn, dkn, dvv = rest[:6]
        sems = rest[6:]

        @pl.when(pl.program_id(0) == 0)
        def _():
            dgq_ref[...] = jnp.zeros_like(dgq_ref)
            dgk_ref[...] = jnp.zeros_like(dgk_ref)
            if scatter:
                _scatter_start(ps_refs, slot_refs, sems, scatter[1])

        _head_norm_rows(q_ref, gq_ref[...], qn, T)
        _head_norm_rows(k_ref, gk_ref[...], kn, T)
        for r0, rc in _row_chunks(T):
            vb[r0:r0 + rc, :] = v_ref[r0:r0 + rc, :].astype(BF16)
            dkn[r0:r0 + rc, :] = jnp.zeros((rc, HEAD), F32)
            dvv[r0:r0 + rc, :] = jnp.zeros((rc, HEAD), F32)
        tri, diag_mask = _strict_upper(qr)

        def qblock(i, _):
            rows_i = pl.ds(pl.multiple_of(i * qr, qr), qr)
            qi = qn[rows_i, :]
            doi = do_ref[rows_i, :]
            dob = doi.astype(BF16)
            etot = jnp.sum(dob.astype(F32) * o_ref[rows_i, :], axis=-1, keepdims=True)

            def tile(j, mask_off, st):
                dq_acc, car, ecar = st
                rows_j = pl.ds(pl.multiple_of(j * SB_BLOCK, SB_BLOCK), SB_BLOCK)
                kj = kn[rows_j, :]
                vj = vb[rows_j, :]
                lb, lk, valid, w_loc = _sb_scores(qi, kj, tri, diag_mask, mask_off)
                w = w_loc * jnp.exp(car)
                e = w * _dot(dob, vj, 1, 1)
                suff = _suffix_sum(e, tri) + e + ecar
                sig = jnp.exp(lb)
                dz = (e * (1.0 - sig) - sig * (etot - suff)) * QK_SCALE
                if mask_off is not None:
                    dz = jnp.where(valid, dz, 0.0)
                dzb = dz.astype(BF16)
                dkn[rows_j, :] += _dot(dzb, qi, 0, 0)
                dvv[rows_j, :] += _dot(w, dob, 0, 0)
                return (dq_acc + _dot(dzb, kj, 1, 0), car + jnp.sum(lk, axis=-1, keepdims=True),
                        ecar + jnp.sum(e, axis=-1, keepdims=True))

            zcol = jnp.zeros((qr, 1), F32)
            dq_acc, _, _ = _sb_walk(i, ratio, tile, (jnp.zeros((qr, HEAD), F32), zcol, zcol))
            dqn[rows_i, :] = dq_acc
            return 0

        lax.fori_loop(0, nq, qblock, 0)

        def norm_bwd(src, gain, dnorm, dst, dgain):
            tot = jnp.zeros((1, HEAD), F32)
            for r0, rc in _row_chunks(T):
                a = src[r0:r0 + rc, :]
                r = lax.rsqrt(jnp.mean(a * a, axis=-1, keepdims=True) + EPS)
                xh = a * r
                dn = dnorm[r0:r0 + rc, :]
                tot = tot + jnp.sum(dn * xh, axis=0, keepdims=True)
                dxh = dn * gain
                dst[r0:r0 + rc, :] = (r * (dxh - xh * jnp.mean(dxh * xh, axis=-1, keepdims=True))).astype(BF16)
            dgain[0:1, :] += tot

        norm_bwd(q_ref, gq_ref[...], dqn, dq_ref, dgq_ref)
        norm_bwd(k_ref, gk_ref[...], dkn, dk_ref, dgk_ref)
        for r0, rc in _row_chunks(T):
            dv_ref[r0:r0 + rc, :] = dvv[r0:r0 + rc, :].astype(BF16)
        if scatter:
            @pl.when(pl.program_id(0) == SB_HEADS - 1)
            def _():
                _scatter_finish(ps_refs, slot_refs, sems, scatter[1])

    col = lambda c0: pl.BlockSpec((T, HEAD), lambda h: (0, c0 + h))
    vec = pl.BlockSpec((1, HEAD), lambda h: (0, 0))
    out = pl.BlockSpec((T, HEAD), lambda h: (0, h))
    st = pl.BlockSpec((8, HEAD), lambda h: (0, 0))
    ps = list(scatter[0]) if scatter else []
    res = pl.pallas_call(kern, name=name, grid=(SB_HEADS,),
                         in_specs=[col(0), col(SB_HEADS), col(2 * SB_HEADS), vec, vec, out, out] + [_ANY] * n_s,
                         out_specs=[out, out, out, st, st] + [_ANY] * n_s,
                         out_shape=[_sds((T, D_MODEL), BF16)] * 3 + [_sds((8, HEAD), F32)] * 2
                         + [_sds(p.shape, p.dtype) for p in ps],
                         scratch_shapes=[pltpu.VMEM((T, HEAD), BF16)] * 3 + [pltpu.VMEM((T, HEAD), F32)] * 3
                         + (_scatter_sems(n_s) if scatter else []),
                         compiler_params=_cparams("arbitrary"))(qkv, qkv, qkv, gq, gk, o32, dcat, *ps)
    return tuple(res[:5]) + (list(res[5:]),)


def _ada_fwd(c_all, ada_w, ada_b_cols):
    L, K, n = ada_w.shape

    def kern(c_ref, w_ref, b_ref, o_ref):
        cv = c_ref[...]
        o_ref[...] = _dot(cv * _sig(cv), w_ref[...], 1, 0) + b_ref[...]

    return pl.pallas_call(kern, name="ada_fwd", grid=(L,),
                          in_specs=[pl.BlockSpec((N_DEV, K), lambda l: (0, 0)), pl.BlockSpec((None, K, n), lambda l: (l, 0, 0)),
                                    pl.BlockSpec((None, 1, n), lambda l: (l, 0, 0))],
                          out_specs=pl.BlockSpec((None, N_DEV, n), lambda l: (l, 0, 0)),
                          out_shape=_sds((L, N_DEV, n), F32), compiler_params=_cparams("parallel"))(c_all, ada_w, ada_b_cols)


def _ada_wgrad(c_all_t, dmod_cols):
    K = c_all_t.shape[0]
    L, _, n = dmod_cols.shape
    tk = 128

    def kern(c_ref, d_ref, o_ref):
        cv = c_ref[...]
        ca = cv * _sig(cv)
        acc = ca[:, 0:1] * d_ref[0:1, :]
        for b in range(1, N_DEV):
            acc = acc + ca[:, b:b + 1] * d_ref[b:b + 1, :]
        o_ref[...] = acc

    return pl.pallas_call(kern, name="ada_wgrad", grid=(L, K // tk),
                          in_specs=[pl.BlockSpec((tk, N_DEV), lambda l, m: (m, 0)), pl.BlockSpec((None, N_DEV, n), lambda l, m: (l, 0, 0))],
                          out_specs=pl.BlockSpec((None, tk, n), lambda l, m: (l, m, 0)),
                          out_shape=_sds((L, K, n), F32), compiler_params=_cparams("parallel", "parallel"))(c_all_t, dmod_cols)


def _sum_devices(g):
    _, R, n = g.shape

    def kern(g_ref, o_ref):
        acc = g_ref[0]
        for d in range(1, N_DEV):
            acc = acc + g_ref[d]
        o_ref[...] = acc

    return pl.pallas_call(kern, name="sum_devices", out_shape=_sds((R, n), F32))(g)


def _pair_sum(name, g, recv, half):
    n4, L, r, cdim = g.shape
    rows = (L // 2) * r
    tr = _tile(rows, 512)
    gv = g.reshape(n4, 2, rows, cdim)
    rv = recv.reshape(n4, rows, cdim)

    def kern(g_ref, r_ref, o_ref):
        o_ref[...] = (g_ref[...] + r_ref[...]).astype(BF16)

    out = pl.pallas_call(kern, name=name, grid=(n4, rows // tr),
                         in_specs=[pl.BlockSpec((None, None, tr, cdim), lambda j, m: (j, half, m, 0)),
                                   pl.BlockSpec((None, tr, cdim), lambda j, m: (j, m, 0))],
                         out_specs=pl.BlockSpec((None, tr, cdim), lambda j, m: (j, m, 0)),
                         out_shape=_sds((n4, rows, cdim), BF16),
                         compiler_params=_cparams("parallel", "parallel"))(gv, rv)
    return out.reshape(n4, L // 2, r, cdim)


def _chip_sum(name, recv, psum, pos_idx, half, total):
    n4, hl, r, cdim = recv.shape
    rows = hl * r
    tr = _tile(rows, 512)

    def kern(pos_ref, r0, r1, r2, r3, own_ref, t_ref, o_ref):
        me = pos_ref[0]
        own = own_ref[...].astype(F32)
        terms = [jnp.where(me == s, own, rr[...].astype(F32)) for s, rr in enumerate((r0, r1, r2, r3))]
        o_ref[...] = ((terms[0] + terms[1]) + terms[2]) + terms[3]

    def slot(s):
        return pl.BlockSpec((None, tr, cdim), lambda m, pos: (jnp.where(pos[0] == s, (s + 1) % n4, s), m, 0))

    gs = pltpu.PrefetchScalarGridSpec(
        num_scalar_prefetch=1, grid=(rows // tr,),
        in_specs=[slot(s) for s in range(n4)] + [pl.BlockSpec((None, tr, cdim), lambda m, pos: (pos[0], m, 0)), _ANY],
        out_specs=pl.BlockSpec((None, tr, cdim), lambda m, pos: (half, m, 0)))
    rv = recv.reshape(n4, rows, cdim)
    out = pl.pallas_call(kern, name=name, grid_spec=gs, out_shape=_sds((2, rows, cdim), F32),
                         input_output_aliases={6: 0}, compiler_params=_cparams("parallel"))(
        pos_idx, rv, rv, rv, rv, psum.reshape(n4, rows, cdim), total.reshape(2, rows, cdim))
    return out.reshape(2 * hl, r, cdim)


def _adamw(name, w, g, m, v):
    shape = w.shape
    cols = shape[-1]
    rows = int(np.prod(shape[:-1]))
    tr = _tile(rows, 512) if rows % 8 == 0 else rows
    c1 = 1.0 - ADAM_B1 ** ADAM_STEP
    c2 = 1.0 - ADAM_B2 ** ADAM_STEP

    def kern(w_ref, g_ref, m_ref, v_ref, d_ref, mo_ref, vo_ref):
        gv = g_ref[...]
        mn = ADAM_B1 * m_ref[...] + (1.0 - ADAM_B1) * gv
        vn = ADAM_B2 * v_ref[...] + (1.0 - ADAM_B2) * (gv * gv)
        mo_ref[...] = mn
        vo_ref[...] = vn
        d_ref[...] = -ADAM_LR * ((mn / c1) / (jnp.sqrt(vn / c2) + ADAM_EPS) + ADAM_WD * w_ref[...])

    blk = pl.BlockSpec((tr, cols), lambda i: (i, 0))
    outs = pl.pallas_call(kern, name=name, grid=(rows // tr,), in_specs=[blk] * 4, out_specs=[blk] * 3,
                          out_shape=[_sds((rows, cols), F32)] * 3, compiler_params=_cparams("parallel"))(
        *[a.reshape(rows, cols) for a in (w, g, m, v)])
    return tuple(o.reshape(shape) for o in outs)


def _rcopy(src, dst, ssem, rsem, dev):
    return pltpu.make_async_remote_copy(src_ref=src, dst_ref=dst, send_sem=ssem, recv_sem=rsem, device_id=dev,
                                        device_id_type=MESH)


def _gather8(name, blk):
    m_per, n = blk.shape

    def body(x_ref, out_ref, send_sems, recv_sems, local_sem):
        x, y, c = _position()
        me, sibling = (x, y, c), (x, y, 1 - c)
        chips = _other_chips(x, y)

        def rows(px, py, pc):
            return out_ref.at[pl.ds((4 * px + 2 * py + pc) * m_per, m_per), :]

        def copy(k, block, to, src=None):
            return _rcopy(rows(*block) if src is None else src, rows(*block), send_sems.at[k], recv_sems.at[k], to)

        mine = pltpu.make_async_copy(x_ref, rows(*me), local_sem)
        mine.start()
        first = [copy(0, me, sibling, src=x_ref)]
        first += [copy(1 + j, me, (*chip, c), src=x_ref) for j, chip in enumerate(chips)]
        for cp in first:
            cp.start()
        passed = [copy(4 + j, (*chip, c), sibling) for j, chip in enumerate(chips)]
        for j, chip in enumerate(chips):
            copy(1 + j, (*chip, c), me).wait_recv()
            passed[j].start()
        copy(0, sibling, me).wait_recv()
        for j, chip in enumerate(chips):
            copy(4 + j, (*chip, 1 - c), me).wait_recv()
        for cp in first + passed:
            cp.wait_send()
        mine.wait()

    return pl.pallas_call(body, name=name, out_shape=_sds((N_DEV * m_per, n), blk.dtype),
                          in_specs=[pl.BlockSpec(memory_space=pltpu.VMEM)], out_specs=pl.BlockSpec(memory_space=pltpu.VMEM),
                          scratch_shapes=[pltpu.SemaphoreType.DMA((7,)), pltpu.SemaphoreType.DMA((7,)), pltpu.SemaphoreType.DMA],
                          compiler_params=pltpu.CompilerParams(vmem_limit_bytes=VMEM_LIMIT_V7X))(blk)


_ANY = pl.BlockSpec(memory_space=pl.ANY)


def _gather_sems(n):
    return [pltpu.SemaphoreType.DMA((3 * n,))] * 4


def _gather_start(src, out, sems, half):
    send_sems, recv_sems = sems[0], sems[1]
    x, y, c = _position()
    chips = _other_chips(x, y)
    me_chip = 2 * x + y

    @pl.when(c == half)
    def _():
        for t in range(len(src)):
            hl = src[t].shape[0] // 2
            rows = pl.ds(half * hl, hl)
            for j, (px, py) in enumerate(chips):
                _rcopy(src[t].at[rows], out[t].at[me_chip, rows], send_sems.at[3 * t + j], recv_sems.at[3 * t + j],
                       (px, py, half)).start()


def _gather_forward(src, out, sems, half):
    send_sems, recv_sems, fsend_sems, frecv_sems = sems
    x, y, c = _position()
    chips = _other_chips(x, y)

    @pl.when(c == half)
    def _():
        for t in range(len(src)):
            hl = src[t].shape[0] // 2
            rows = pl.ds(half * hl, hl)
            for j, (px, py) in enumerate(chips):
                landed = out[t].at[2 * px + py, rows]
                _rcopy(landed, landed, send_sems.at[3 * t + j], recv_sems.at[3 * t + j], (px, py, half)).wait_recv()
                _rcopy(landed, landed, fsend_sems.at[3 * t + j], frecv_sems.at[3 * t + j], (x, y, 1 - half)).start()


def _gather_finish(src, out, sems, half):
    send_sems, recv_sems, fsend_sems, frecv_sems = sems
    x, y, c = _position()
    chips = _other_chips(x, y)
    me_chip = 2 * x + y

    @pl.when(c == half)
    def _():
        for t in range(len(src)):
            hl = src[t].shape[0] // 2
            rows = pl.ds(half * hl, hl)
            for j, (px, py) in enumerate(chips):
                landed = out[t].at[2 * px + py, rows]
                _rcopy(src[t].at[rows], out[t].at[me_chip, rows], send_sems.at[3 * t + j], recv_sems.at[3 * t + j],
                       (px, py, half)).wait_send()
                _rcopy(landed, landed, fsend_sems.at[3 * t + j], frecv_sems.at[3 * t + j], (x, y, 1 - half)).wait_send()

    @pl.when(c != half)
    def _():
        for t in range(len(src)):
            hl = src[t].shape[0] // 2
            rows = pl.ds(half * hl, hl)
            for j, (px, py) in enumerate(chips):
                landed = out[t].at[2 * px + py, rows]
                _rcopy(landed, landed, fsend_sems.at[3 * t + j], frecv_sems.at[3 * t + j], (x, y, half)).wait_recv()


def _gather_weights(shards, half):
    n = len(shards)

    def body(*refs):
        src, out, sems = refs[:n], refs[n:2 * n], refs[2 * n:]
        _gather_start(src, out, sems, half)
        _gather_forward(src, out, sems, half)
        _gather_finish(src, out, sems, half)

    return pl.pallas_call(body, name="gather_weights", out_shape=[_sds((N_CHIP,) + s.shape, s.dtype) for s in shards],
                          in_specs=[_ANY] * n, out_specs=[_ANY] * n, scratch_shapes=_gather_sems(n))(*shards)


def _place_own(name, w4, shard, pos_idx):
    n4, L, r, cdim = w4.shape
    rows = L * r
    tr = _tile(rows, 1024)

    def kern(pos_ref, s_ref, w_ref, o_ref):
        o_ref[...] = s_ref[...]

    gs = pltpu.PrefetchScalarGridSpec(
        num_scalar_prefetch=1, grid=(rows // tr,),
        in_specs=[pl.BlockSpec((tr, cdim), lambda m, pos: (m, 0)), _ANY],
        out_specs=pl.BlockSpec((None, tr, cdim), lambda m, pos: (pos[0], m, 0)))
    out = pl.pallas_call(kern, name=name, grid_spec=gs, out_shape=_sds((n4, rows, cdim), w4.dtype),
                         input_output_aliases={2: 0}, compiler_params=_cparams("parallel"))(
        pos_idx, shard.reshape(rows, cdim), w4.reshape(n4, rows, cdim))
    return out.reshape(w4.shape)


def _exchange_pair(grads, to_c):
    n = len(grads)

    def body(*refs):
        src, out = refs[:n], refs[n:2 * n]
        send_sems, recv_sems = refs[2 * n:]
        x, y, c = _position()
        sibling = (x, y, 1 - c)

        @pl.when(c != to_c)
        def _():
            cps = []
            for t in range(n):
                hl = src[t].shape[1] // 2
                cp = _rcopy(src[t].at[:, pl.ds(to_c * hl, hl)], out[t], send_sems.at[t], recv_sems.at[t], sibling)
                cp.start()
                cps.append(cp)
            for cp in cps:
                cp.wait_send()

        @pl.when(c == to_c)
        def _():
            for t in range(n):
                _rcopy(out[t], out[t], send_sems.at[t], recv_sems.at[t], sibling).wait_recv()

    sems = [pltpu.SemaphoreType.DMA((n,))] * 2
    return pl.pallas_call(body, name=f"exchange_pair_{to_c}",
                          out_shape=[_sds((g.shape[0], g.shape[1] // 2) + g.shape[2:], g.dtype) for g in grads],
                          in_specs=[_ANY] * n, out_specs=[_ANY] * n, scratch_shapes=sems)(*grads)


def _scatter_sems(n):
    return [pltpu.SemaphoreType.DMA((3 * n,))] * 2


def _scatter_start(src, out, sems, half):
    send_sems, recv_sems = sems
    x, y, c = _position()
    chips = _other_chips(x, y)
    me_chip = 2 * x + y

    @pl.when(c == half)
    def _():
        for t in range(len(src)):
            for j, (px, py) in enumerate(chips):
                _rcopy(src[t].at[2 * px + py], out[t].at[me_chip], send_sems.at[3 * t + j], recv_sems.at[3 * t + j],
                       (px, py, half)).start()


def _scatter_finish(src, out, sems, half):
    send_sems, recv_sems = sems
    x, y, c = _position()
    chips = _other_chips(x, y)
    me_chip = 2 * x + y

    @pl.when(c == half)
    def _():
        for t in range(len(src)):
            for j, (px, py) in enumerate(chips):
                slot = out[t].at[2 * px + py]
                _rcopy(slot, slot, send_sems.at[3 * t + j], recv_sems.at[3 * t + j], (px, py, half)).wait_recv()
        for t in range(len(src)):
            for j, (px, py) in enumerate(chips):
                _rcopy(src[t].at[2 * px + py], out[t].at[me_chip], send_sems.at[3 * t + j], recv_sems.at[3 * t + j],
                       (px, py, half)).wait_send()


def _scatter_chips(psums, half):
    n = len(psums)

    def body(*refs):
        src, out, sems = refs[:n], refs[n:2 * n], refs[2 * n:]
        _scatter_start(src, out, sems, half)
        _scatter_finish(src, out, sems, half)

    return pl.pallas_call(body, name="scatter_chips", out_shape=[_sds(p.shape, p.dtype) for p in psums],
                          in_specs=[_ANY] * n, out_specs=[_ANY] * n, scratch_shapes=_scatter_sems(n))(*psums)


def _share_halves(full):
    n = len(full)

    def body(*refs):
        out = refs[n:2 * n]
        send_sems, recv_sems = refs[2 * n:]
        x, y, c = _position()
        sibling = (x, y, 1 - c)
        cps = []
        for t in range(n):
            hl = out[t].shape[0] // 2
            mine = out[t].at[pl.ds(c * hl, hl)]
            cp = _rcopy(mine, mine, send_sems.at[t], recv_sems.at[t], sibling)
            cp.start()
            cps.append(cp)
        for t in range(n):
            hl = out[t].shape[0] // 2
            theirs = out[t].at[pl.ds((1 - c) * hl, hl)]
            _rcopy(theirs, theirs, send_sems.at[t], recv_sems.at[t], sibling).wait_recv()
        for cp in cps:
            cp.wait_send()

    sems = [pltpu.SemaphoreType.DMA((n,))] * 2
    return pl.pallas_call(body, name="share_halves", out_shape=[_sds(h.shape, h.dtype) for h in full],
                          in_specs=[_ANY] * n, out_specs=[_ANY] * n, scratch_shapes=sems,
                          input_output_aliases={t: t for t in range(n)})(*full)


def _rope_tables(T):
    inv_freq = 1.0 / (ROPE_THETA ** (jnp.arange(0, HEAD, 2, dtype=F32) / HEAD))
    ang = jnp.arange(T, dtype=F32)[:, None] * inv_freq[None, :]
    cos, sin = jnp.cos(ang), jnp.sin(ang)
    return jnp.concatenate([cos, cos], axis=-1), jnp.concatenate([-sin, sin], axis=-1)


def _decay_table():
    lg = np.log1p(-np.exp2(-5.0 - np.arange(RET_HEADS, dtype=np.float32))).astype(np.float32)
    return jnp.asarray(np.broadcast_to(lg[:, None, None], (RET_HEADS, 1, LANE)).copy())


def _local_step(x0, target, mod, W, G, norm_mix_g, norm_ffn_g, conv_full, ev_ret_norm_g, od_q_norm_g, od_k_norm_g,
                fused=None):
    T = x0.shape[0]
    KSH1, KSC1, KG1, KSH2, KSC2, KG2 = range(6)
    gain_mix = norm_mix_g.reshape(DEPTH, 1, D_MODEL)
    gain_ffn = norm_ffn_g.reshape(DEPTH, 1, D_MODEL)
    cosf, sinf = _rope_tables(T)
    lgt = _decay_table()

    saved = []
    xcur = x0
    for l in range(DEPTH):
        j = l // 2
        s = dict(x_in=xcur)
        h = _normmod(f"norm_mix_{l}", xcur, gain_mix, l, mod, KSC1, KSH1)
        s["h"] = h
        if l % 2 == 0:
            proj = _proj_cols(f"ev_in_{l}", h, W["ev_w_in"], j)
            a = _conv_fwd(f"conv_{l}", proj, conv_full[j])
            r, oraw, states = _retention_fwd(f"ret_{l}", proj, cosf, sinf, ev_ret_norm_g[j].reshape(1, RET_DIM), lgt)
            cat = jnp.concatenate([a, r], axis=1)
            s.update(proj=proj, oraw=oraw, states=states, cat=cat)
            y, xmid = _out_proj(f"ev_out_{l}", cat, False, W["ev_w_out"], j, xcur, mod, l, KG1)
        else:
            qkv = _proj_cols(f"od_in_{l}", h, W["od_w_qkv"], j)
            gather = (fused["shards"], [W[k] for k in fused["names"]], 1) if fused is not None and l == 1 else None
            o, o32, w4s = _sb_fwd(f"sb_{l}", qkv, od_q_norm_g[j].reshape(1, HEAD), od_k_norm_g[j].reshape(1, HEAD), gather)
            if gather:
                W = dict(zip(fused["names"], w4s))
            s.update(qkv=qkv, cat=o, o32=o32)
            y, xmid = _out_proj(f"od_out_{l}", o, False, W["od_w_out"], j, xcur, mod, l, KG1)
        s.update(y1=y, x_mid=xmid)
        h2 = _normmod(f"norm_ffn_{l}", xmid, gain_ffn, l, mod, KSC2, KSH2)
        gate, up, act = _ffn_up(f"ffn_up_{l}", h2, W["ffn_w_gate"], W["ffn_w_up"], l)
        y2, xcur = _out_proj(f"ffn_down_{l}", act, True, W["ffn_w_down"], l, xmid, mod, l, KG2)
        s.update(h2=h2, gate=gate, up=up, act=act, y2=y2)
        saved.append(s)

    dy, lacc = _loss_head(xcur, target)

    dmod_rows = [None] * DEPTH
    d_mix = [None] * DEPTH
    d_ffn = [None] * DEPTH
    d_conv = [None] * 2
    d_ret = [None] * 2
    d_gq = [None] * 2
    d_gk = [None] * 2
    dx = dy
    reduce_up = None
    for l in reversed(range(DEPTH)):
        j = l // 2
        s = saved[l]
        dyg, st_g2 = _gate_bwd(f"gate2_bwd_{l}", dx, s["y2"], mod, l, KG2)
        G["ffn_w_down"] = _wgrad(f"wg_down_{l}", s["act"], "stack", dyg, "full", G["ffn_w_down"], l)
        dgate, dup = _ffn_down_bwd(f"ffn_down_bwd_{l}", dyg, W["ffn_w_down"], l, s["gate"], s["up"])
        G["ffn_w_gate"] = _wgrad(f"wg_gate_{l}", s["h2"], "full", dgate, "stack", G["ffn_w_gate"], l)
        G["ffn_w_up"] = _wgrad(f"wg_up_{l}", s["h2"], "full", dup, "stack", G["ffn_w_up"], l)
        dh2 = _ffn_up_bwd(f"ffn_up_bwd_{l}", dgate, dup, W["ffn_w_gate"], W["ffn_w_up"], l)
        dxm, st_n2 = _normmod_bwd(f"norm_ffn_bwd_{l}", s["x_mid"], dh2, dx, gain_ffn, l, mod, KSC2)
        dyg1, st_g1 = _gate_bwd(f"gate1_bwd_{l}", dxm, s["y1"], mod, l, KG1)
        if l % 2 == 0:
            G["ev_w_out"] = _wgrad(f"wg_evout_{l}", s["cat"], "cols", dyg1, "full", G["ev_w_out"], j)
            dcat = _bwd_rows(f"ev_out_bwd_{l}", dyg1, W["ev_w_out"], j)
            db, dcg, du, dwc = _conv_bwd(f"conv_bwd_{l}", s["proj"], conv_full[j], dcat)
            dq, dk, dv, dg, dgr = _retention_bwd(f"ret_bwd_{l}", s["proj"], s["oraw"], s["states"], dcat, cosf, sinf,
                                                 ev_ret_norm_g[j].reshape(1, RET_DIM), lgt)
            dproj = jnp.concatenate([db, dcg, du, dq, dk, dv, dg], axis=1)
            d_conv[j], d_ret[j] = dwc[:CONV_WIDTH], dgr[0]
            G["ev_w_in"] = _wgrad(f"wg_evin_{l}", s["h"], "full", dproj, "cols", G["ev_w_in"], j)
            dh = _bwd_cols(f"ev_in_bwd_{l}", dproj, W["ev_w_in"], j)
        else:
            G["od_w_out"] = _wgrad(f"wg_odout_{l}", s["cat"], "cols", dyg1, "full", G["od_w_out"], j)
            dcat = _bwd_rows(f"od_out_bwd_{l}", dyg1, W["od_w_out"], j)
            scatter = None
            if fused is not None and l == 1:
                g_now = [G[k] for k in fused["names"]]
                recv_up = _exchange_pair(g_now, 1)
                ps_up = [_pair_sum(f"pair_sum_up_{k}", g, r, 1) for k, g, r in zip(fused["names"], g_now, recv_up)]
                scatter = (ps_up, 1)
            dq, dk, dv, dgq, dgk, slots_up = _sb_bwd(f"sb_bwd_{l}", s["qkv"], od_q_norm_g[j].reshape(1, HEAD),
                                                     od_k_norm_g[j].reshape(1, HEAD), s["o32"], dcat, scatter)
            if scatter:
                reduce_up = (ps_up, slots_up)
            dproj = jnp.concatenate([dq, dk, dv], axis=1)
            d_gq[j], d_gk[j] = dgq[0], dgk[0]
            G["od_w_qkv"] = _wgrad(f"wg_odin_{l}", s["h"], "full", dproj, "cols", G["od_w_qkv"], j)
            dh = _bwd_cols(f"od_in_bwd_{l}", dproj, W["od_w_qkv"], j)
        dx, st_n1 = _normmod_bwd(f"norm_mix_bwd_{l}", s["x_in"], dh, dxm, gain_mix, l, mod, KSC1)
        dmod_rows[l] = jnp.stack([st_n1[0], st_n1[1], st_g1[0], st_n2[0], st_n2[1], st_g2[0]]).reshape(6 * D_MODEL)
        d_mix[l], d_ffn[l] = st_n1[2], st_n2[2]
    return lacc, dx, G, (dmod_rows, d_mix, d_ffn, d_ret, d_gq, d_gk, d_conv), reduce_up


def kernel(x, c, ada_w, ada_b, norm_mix_g, norm_ffn_g, ev_w_in, ev_conv_w, ev_ret_norm_g, ev_w_out, od_w_qkv, od_q_norm_g, od_k_norm_g, od_w_out, ffn_w_gate, ffn_w_up, ffn_w_down, loss_target, m_ada_w, m_ada_b, m_norm_mix_g, m_norm_ffn_g, m_ev_w_in, m_ev_conv_w, m_ev_ret_norm_g, m_ev_w_out, m_od_w_qkv, m_od_q_norm_g, m_od_k_norm_g, m_od_w_out, m_ffn_w_gate, m_ffn_w_up, m_ffn_w_down, v_ada_w, v_ada_b, v_norm_mix_g, v_norm_ffn_g, v_ev_w_in, v_ev_conv_w, v_ev_ret_norm_g, v_ev_w_out, v_od_w_qkv, v_od_q_norm_g, v_od_k_norm_g, v_od_w_out, v_ffn_w_gate, v_ffn_w_up, v_ffn_w_down):
    xi, yi, ci = _position()
    chip = 2 * xi + yi
    dev = 4 * xi + 2 * yi + ci
    x0 = x[0]
    target = loss_target[0]

    n_small = D_MODEL + 2 * CONV_WIDTH * LANE
    small = jnp.concatenate([c.reshape(1, D_MODEL), ev_conv_w.reshape(1, 2 * CONV_WIDTH * LANE)], axis=1)
    small = jnp.broadcast_to(small, (8, n_small))
    g1 = _gather8("gather_cond", small).reshape(N_DEV, 8, n_small)[:, 0, :]
    c_all = g1[:, :D_MODEL]
    conv_all = g1[0::2, D_MODEL:].reshape(N_CHIP, 2, CONV_WIDTH, LANE)
    conv_full = conv_all.transpose(1, 2, 0, 3).reshape(2, CONV_WIDTH, CONV_DIM)

    n_ada = ada_w.shape[-1]
    ada_b_cols = lax.dynamic_slice_in_dim(ada_b, chip * n_ada, n_ada, axis=1).reshape(DEPTH, 1, n_ada)
    mod_cols = _ada_fwd(c_all, ada_w, ada_b_cols)
    g2 = _gather8("gather_mod", mod_cols.reshape(DEPTH * N_DEV, n_ada)).reshape(N_DEV, DEPTH, N_DEV, n_ada)
    mod_mine = lax.dynamic_index_in_dim(g2[0::2], dev, axis=2, keepdims=False)
    mod = mod_mine.transpose(1, 0, 2).reshape(DEPTH, 6, 1, D_MODEL)

    big_names = ["ev_w_in", "ev_w_out", "od_w_qkv", "od_w_out", "ffn_w_gate", "ffn_w_up", "ffn_w_down"]
    big = dict(ev_w_in=ev_w_in, ev_w_out=ev_w_out, od_w_qkv=od_w_qkv, od_w_out=od_w_out, ffn_w_gate=ffn_w_gate,
               ffn_w_up=ffn_w_up, ffn_w_down=ffn_w_down)
    pos_idx = jnp.stack([chip, ci]).astype(jnp.int32)
    shards = [big[k].astype(BF16) for k in big_names]
    W = {k: _place_own(f"place_{k}", w4, s, pos_idx) for k, w4, s in zip(big_names, _gather_weights(shards, 0), shards)}
    G = {k: lax.empty((N_CHIP,) + big[k].shape, F32) for k in big_names}

    lacc, dx, G, small_grads, (ps_up, slots_up) = _local_step(
        x0, target, mod, W, G, norm_mix_g, norm_ffn_g, conv_full, ev_ret_norm_g, od_q_norm_g, od_k_norm_g,
        fused=dict(names=big_names, shards=shards))
    loss = lax.psum(lacc[0, 0], ("x", "y", "c"))
    grad_x = dx[None]
    dmod_rows, d_mix, d_ffn, d_ret, d_gq, d_gk, d_conv = small_grads

    totals = [_chip_sum(f"chip_sum_up_{k}", r, p, pos_idx, 1, lax.empty(big[k].shape, F32))
              for k, r, p in zip(big_names, slots_up, ps_up)]
    glist = [G[k] for k in big_names]
    recv_lo = _exchange_pair(glist, 0)
    ps_lo = [_pair_sum(f"pair_sum_lo_{k}", g, r, 0) for k, g, r in zip(big_names, glist, recv_lo)]
    slots_lo = _scatter_chips(ps_lo, 0)
    totals = [_chip_sum(f"chip_sum_lo_{k}", r, p, pos_idx, 0, t) for k, r, p, t in zip(big_names, slots_lo, ps_lo, totals)]
    grads = dict(zip(big_names, _share_halves(totals)))

    pieces = [jnp.stack(dmod_rows).reshape(-1), jnp.stack(d_mix).reshape(-1), jnp.stack(d_ffn).reshape(-1),
              jnp.stack(d_ret).reshape(-1), jnp.stack(d_gq).reshape(-1), jnp.stack(d_gk).reshape(-1),
              jnp.stack(d_conv).reshape(-1)]
    sizes = [int(p.shape[0]) for p in pieces]
    n_pack = sum(sizes)
    n_cols = -(-n_pack // (8 * LANE)) * LANE
    packed = jnp.concatenate(pieces + [jnp.zeros((8 * n_cols - n_pack,), F32)]).reshape(8, n_cols)
    g3 = _gather8("gather_small", packed).reshape(N_DEV, 8, n_cols)
    tot = _sum_devices(g3).reshape(-1)
    offs = np.cumsum([0] + sizes)
    part = [tot[offs[i]:offs[i + 1]] for i in range(len(sizes))]
    grads["ada_b"] = part[0].reshape(DEPTH, 6 * D_MODEL)
    grads["norm_mix_g"] = part[1].reshape(DEPTH, D_MODEL)
    grads["norm_ffn_g"] = part[2].reshape(DEPTH, D_MODEL)
    grads["ev_ret_norm_g"] = part[3].reshape(2, RET_DIM)
    grads["od_q_norm_g"] = part[4].reshape(2, HEAD)
    grads["od_k_norm_g"] = part[5].reshape(2, HEAD)
    conv_g = part[6].reshape(2, CONV_WIDTH, CONV_DIM)
    grads["ev_conv_w"] = lax.dynamic_slice_in_dim(conv_g, chip * LANE, LANE, axis=2)
    dmod_all = g3.reshape(N_DEV, -1)[:, :DEPTH * 6 * D_MODEL].reshape(N_DEV, DEPTH, 6 * D_MODEL)
    dmod_cols = lax.dynamic_slice_in_dim(dmod_all, chip * n_ada, n_ada, axis=2).transpose(1, 0, 2)
    grads["ada_w"] = _ada_wgrad(c_all.T, dmod_cols)

    weights = dict(ada_w=ada_w, ada_b=ada_b, norm_mix_g=norm_mix_g, norm_ffn_g=norm_ffn_g, ev_w_in=ev_w_in,
                   ev_conv_w=ev_conv_w, ev_ret_norm_g=ev_ret_norm_g, ev_w_out=ev_w_out, od_w_qkv=od_w_qkv,
                   od_q_norm_g=od_q_norm_g, od_k_norm_g=od_k_norm_g, od_w_out=od_w_out, ffn_w_gate=ffn_w_gate,
                   ffn_w_up=ffn_w_up, ffn_w_down=ffn_w_down)
    m_in = dict(ada_w=m_ada_w, ada_b=m_ada_b, norm_mix_g=m_norm_mix_g, norm_ffn_g=m_norm_ffn_g, ev_w_in=m_ev_w_in,
                ev_conv_w=m_ev_conv_w, ev_ret_norm_g=m_ev_ret_norm_g, ev_w_out=m_ev_w_out, od_w_qkv=m_od_w_qkv,
                od_q_norm_g=m_od_q_norm_g, od_k_norm_g=m_od_k_norm_g, od_w_out=m_od_w_out, ffn_w_gate=m_ffn_w_gate,
                ffn_w_up=m_ffn_w_up, ffn_w_down=m_ffn_w_down)
    v_in = dict(ada_w=v_ada_w, ada_b=v_ada_b, norm_mix_g=v_norm_mix_g, norm_ffn_g=v_norm_ffn_g, ev_w_in=v_ev_w_in,
                ev_conv_w=v_ev_conv_w, ev_ret_norm_g=v_ev_ret_norm_g, ev_w_out=v_ev_w_out, od_w_qkv=v_od_w_qkv,
                od_q_norm_g=v_od_q_norm_g, od_k_norm_g=v_od_k_norm_g, od_w_out=v_od_w_out, ffn_w_gate=v_ffn_w_gate,
                ffn_w_up=v_ffn_w_up, ffn_w_down=v_ffn_w_down)
    order = list(weights)
    deltas, new_m, new_v = {}, {}, {}
    for k in order:
        deltas[k], new_m[k], new_v[k] = _adamw(f"adamw_{k}", weights[k], grads[k], m_in[k], v_in[k])
    return (loss, grad_x, *[grads[k] for k in order], *[deltas[k] for k in order], *[new_m[k] for k in order],
            *[new_v[k] for k in order])
```

```python
import functools

import numpy as np
import jax
import jax.numpy as jnp
from jax import lax
from jax.experimental import pallas as pl
from jax.experimental.pallas import tpu as pltpu

F32 = jnp.float32
BF16 = jnp.bfloat16
MESH = pl.DeviceIdType.MESH

D_MODEL = 1024
DEPTH = 4
N_CHIP = 4
N_DEV = 8
HEAD = 128
RET_HEADS = 4
SB_HEADS = 8
CONV_DIM = 512
RET_DIM = 512
CONV_WIDTH = 3
RET_CHUNK = 64
RET_BLOCK = 256
SB_BLOCK = 256
SB_QROWS = 512
EPS = 1e-6
ROPE_THETA = 10000.0
QK_SCALE = HEAD ** -0.5
LANE = 128
ROW_CHUNK = 512
ROWS_STREAMED = 1024
VMEM_LIMIT_V7X = 56 * 1024 * 1024

ADAM_LR, ADAM_B1, ADAM_B2, ADAM_EPS, ADAM_WD, ADAM_STEP = 0.001, 0.9, 0.999, 1e-08, 0.01, 10


def _cparams(*sem):
    return pltpu.CompilerParams(dimension_semantics=sem or None, vmem_limit_bytes=VMEM_LIMIT_V7X)


def _tile(n, pref):
    if n <= pref:
        return n
    for t in range(pref - pref % 8, 7, -8):
        if n % t == 0:
            return t
    return n


def _sig(v):
    return 1.0 / (1.0 + jnp.exp(-v))


def _dot(a, b, ca, cb):
    return lax.dot_general(a.astype(BF16), b.astype(BF16), (((ca,), (cb,)), ((), ())),
                           preferred_element_type=F32)


def _position():
    x, y, c = lax.axis_index("x"), lax.axis_index("y"), lax.axis_index("c")
    return x, y, c


def _other_chips(x, y):
    return [(1 - x, y), (x, 1 - y), (1 - x, 1 - y)]


def _mm(name, pairs, out_sds, out_specs, grid, contract, red_axis=None, post=None,
        extras=(), extra_specs=(), sum_pairs=True, aliases=None, inner=None):
    n_p, n_ex, n_out = len(pairs), len(extras), len(out_sds)
    n_acc = 1 if sum_pairs else n_p
    n_red = grid[red_axis] if red_axis is not None else 1

    def default_post(accs, ex, outs):
        outs[0][...] = accs[0].astype(outs[0].dtype)

    post_fn = post or default_post

    def kern(*refs):
        ab = refs[:2 * n_p]
        ex = refs[2 * n_p:2 * n_p + n_ex]
        outs = refs[2 * n_p + n_ex:2 * n_p + n_ex + n_out]
        accs = refs[2 * n_p + n_ex + n_out:]
        if inner is None:
            prods = [_dot(ab[2 * p][...], ab[2 * p + 1][...], contract[0], contract[1]) for p in range(n_p)]
        else:
            n_in, a_get, b_get = inner
            prods = []
            for p in range(n_p):
                tot = _dot(a_get(ab[2 * p], 0), b_get(ab[2 * p + 1], 0), contract[0], contract[1])
                for i in range(1, n_in):
                    tot = tot + _dot(a_get(ab[2 * p], i), b_get(ab[2 * p + 1], i), contract[0], contract[1])
                prods.append(tot)
        if sum_pairs:
            tot = prods[0]
            for p_ in prods[1:]:
                tot = tot + p_
            prods = [tot]
        if red_axis is None:
            post_fn(prods, ex, outs)
        else:
            k = pl.program_id(red_axis)

            @pl.when(k == 0)
            def _():
                for a_, p_ in zip(accs, prods):
                    a_[...] = p_

            @pl.when(k > 0)
            def _():
                for a_, p_ in zip(accs, prods):
                    a_[...] += p_

            @pl.when(k == n_red - 1)
            def _():
                post_fn([a_[...] for a_ in accs], ex, outs)

    ins, in_specs = [], []
    for a, b, sa, sb in pairs:
        ins += [a, b]
        in_specs += [sa, sb]
    ins += list(extras)
    in_specs += list(extra_specs)
    scratch = []
    if red_axis is not None:
        scratch = [pltpu.VMEM(tuple(acc_shape), F32) for acc_shape in [_acc_shape(pairs[0], contract)] * n_acc]
    sem = tuple("arbitrary" if ax == red_axis else "parallel" for ax in range(len(grid)))
    res = pl.pallas_call(kern, name=name, grid=grid, in_specs=in_specs, out_specs=list(out_specs),
                         out_shape=list(out_sds), scratch_shapes=scratch,
                         input_output_aliases=aliases or {}, compiler_params=_cparams(*sem))(*ins)
    return res


def _acc_shape(pair, contract):
    sa, sb = pair[2], pair[3]
    da = [d for d in sa.block_shape if d is not None]
    db = [d for d in sb.block_shape if d is not None]
    return (da[1 - contract[0]], db[1 - contract[1]])


def _sds(shape, dtype):
    return jax.ShapeDtypeStruct(tuple(shape), dtype)


def _proj_cols(name, h, w4, l):
    T, K = h.shape
    n = w4.shape[-1]
    tm = _tile(T, ROWS_STREAMED)
    return _mm(name, [(h, w4, pl.BlockSpec((tm, K), lambda i, m: (m, 0)),
                       pl.BlockSpec((None, None, K, n), lambda i, m: (i, l, 0, 0)))],
               [_sds((T, N_CHIP * n), F32)], [pl.BlockSpec((tm, n), lambda i, m: (m, i))],
               (N_CHIP, T // tm), (1, 0))[0]


def _out_proj(name, a, a_stacked, w4, l, xres, mod, lm, kg, next_norm=None):
    T = xres.shape[0]
    k = w4.shape[-2]
    tm = _tile(T, 512)
    if a_stacked:
        sa = pl.BlockSpec((N_CHIP, tm, k), lambda m: (0, m, 0))
        a_get = lambda ref, i: ref[i]
    else:
        sa = pl.BlockSpec((tm, N_CHIP * k), lambda m: (m, 0))
        a_get = lambda ref, i: ref[:, i * k:(i + 1) * k]

    def post(accs, ex, outs):
        y = accs[0]
        outs[0][...] = y
        xn = ex[0][...] + ex[1][...] * y
        outs[1][...] = xn
        if next_norm:
            r = lax.rsqrt(jnp.mean(xn * xn, axis=-1, keepdims=True) + EPS)
            outs[2][...] = ((xn * r) * ex[2][...] * (1.0 + ex[3][...]) + ex[4][...]).astype(BF16)

    row = pl.BlockSpec((tm, D_MODEL), lambda m: (m, 0))
    extras, extra_specs = [xres, mod], [row, _vec_spec(lm, kg)]
    out_sds, out_specs = [_sds((T, D_MODEL), F32)] * 2, [row, row]
    if next_norm:
        gain3, lg, ln, ksc, ksh = next_norm
        extras += [gain3, mod, mod]
        extra_specs += [_vec_spec(lg), _vec_spec(ln, ksc), _vec_spec(ln, ksh)]
        out_sds.append(_sds((T, D_MODEL), BF16))
        out_specs.append(row)
    return _mm(name, [(a, w4, sa, pl.BlockSpec((N_CHIP, None, k, D_MODEL), lambda m: (0, l, 0, 0)))],
               out_sds, out_specs, (T // tm,), (1, 0), post=post, extras=extras, extra_specs=extra_specs,
               inner=(N_CHIP, a_get, lambda ref, i: ref[i]))


def _ffn_up(name, h2, wg4, wu4, l):
    T, K = h2.shape
    n = wg4.shape[-1]
    tm = _tile(T, ROWS_STREAMED)

    def post(accs, ex, outs):
        g, u = accs
        outs[0][...] = g
        outs[1][...] = u
        outs[2][...] = (g * _sig(g) * u).astype(BF16)

    sa = pl.BlockSpec((tm, K), lambda i, m: (m, 0))
    sw = pl.BlockSpec((None, None, K, n), lambda i, m: (i, l, 0, 0))
    so = pl.BlockSpec((None, tm, n), lambda i, m: (i, m, 0))
    return _mm(name, [(h2, wg4, sa, sw), (h2, wu4, sa, sw)],
               [_sds((N_CHIP, T, n), F32), _sds((N_CHIP, T, n), F32), _sds((N_CHIP, T, n), BF16)], [so, so, so],
               (N_CHIP, T // tm), (1, 0), post=post, sum_pairs=False)


def _bwd_cols(name, dproj, w4, l):
    T = dproj.shape[0]
    K, n = w4.shape[-2:]
    tm = _tile(T, 512)
    return _mm(name, [(dproj, w4, pl.BlockSpec((tm, N_CHIP * n), lambda m: (m, 0)),
                       pl.BlockSpec((N_CHIP, None, K, n), lambda m: (0, l, 0, 0)))],
               [_sds((T, K), F32)], [pl.BlockSpec((tm, K), lambda m: (m, 0))], (T // tm,), (1, 1),
               inner=(N_CHIP, lambda ref, i: ref[:, i * n:(i + 1) * n], lambda ref, i: ref[i]))[0]


def _bwd_rows(name, dy, w4, l):
    T, N = dy.shape
    k = w4.shape[-2]
    tm = _tile(T, ROWS_STREAMED)
    return _mm(name, [(dy, w4, pl.BlockSpec((tm, N), lambda i, m: (m, 0)),
                       pl.BlockSpec((None, None, k, N), lambda i, m: (i, l, 0, 0)))],
               [_sds((T, N_CHIP * k), F32)], [pl.BlockSpec((tm, k), lambda i, m: (m, i))],
               (N_CHIP, T // tm), (1, 1))[0]


def _ffn_down_bwd(name, dy, wd4, l, gate, up):
    T, N = dy.shape
    k = wd4.shape[-2]
    tm = _tile(T, ROWS_STREAMED)

    def post(accs, ex, outs):
        da = accs[0]
        g = ex[0][...]
        u = ex[1][...]
        sg = _sig(g)
        outs[0][...] = (da * u * (sg * (1.0 + g * (1.0 - sg)))).astype(BF16)
        outs[1][...] = (da * (g * sg)).astype(BF16)

    so = pl.BlockSpec((None, tm, k), lambda i, m: (i, m, 0))
    return _mm(name, [(dy, wd4, pl.BlockSpec((tm, N), lambda i, m: (m, 0)),
                       pl.BlockSpec((None, None, k, N), lambda i, m: (i, l, 0, 0)))],
               [_sds((N_CHIP, T, k), BF16)] * 2, [so, so], (N_CHIP, T // tm), (1, 1), post=post,
               extras=[gate, up], extra_specs=[so, so])


def _ffn_up_bwd(name, dgate, dup, wg4, wu4, l):
    _, T, n = dgate.shape
    K = wg4.shape[-2]
    tm = _tile(T, 512)
    sa = pl.BlockSpec((N_CHIP, tm, n), lambda m: (0, m, 0))
    sw = pl.BlockSpec((N_CHIP, None, K, n), lambda m: (0, l, 0, 0))
    pick = lambda ref, i: ref[i]
    return _mm(name, [(dgate, wg4, sa, sw), (dup, wu4, sa, sw)],
               [_sds((T, K), F32)], [pl.BlockSpec((tm, K), lambda m: (m, 0))], (T // tm,), (1, 1),
               inner=(N_CHIP, pick, pick))[0]


def _wgrad(name, a, a_kind, b, b_kind, gbuf, l):
    r, cdim = gbuf.shape[-2:]
    T = a.shape[-2]
    tt = _tile(T, ROWS_STREAMED)

    def spec(kind, w):
        if kind == "full":
            return pl.BlockSpec((tt, w), lambda i, t: (t, 0))
        if kind == "cols":
            return pl.BlockSpec((tt, w), lambda i, t: (t, i))
        return pl.BlockSpec((None, tt, w), lambda i, t: (i, t, 0))

    def post(accs, ex, outs):
        outs[0][...] = accs[0]

    return _mm(name, [(a, b, spec(a_kind, r), spec(b_kind, cdim))], [_sds(gbuf.shape, F32)],
               [pl.BlockSpec((None, None, r, cdim), lambda i, t: (i, l, 0, 0))], (N_CHIP, T // tt), (0, 0),
               red_axis=1, post=post, extras=[gbuf], extra_specs=[pl.BlockSpec(memory_space=pl.ANY)],
               aliases={2: 0})[0]


def _vec_spec(*idx):
    return pl.BlockSpec((None,) * len(idx) + (1, D_MODEL), lambda m: tuple(idx) + (0, 0))


def _normmod(name, x, gain3, l, mod, ksc, ksh):
    T = x.shape[0]
    tm = _tile(T, 512)

    def kern(x_ref, g_ref, sc_ref, sh_ref, h_ref):
        xv = x_ref[...]
        r = lax.rsqrt(jnp.mean(xv * xv, axis=-1, keepdims=True) + EPS)
        h = (xv * r) * g_ref[...]
        h_ref[...] = (h * (1.0 + sc_ref[...]) + sh_ref[...]).astype(BF16)

    row = pl.BlockSpec((tm, D_MODEL), lambda m: (m, 0))
    return pl.pallas_call(kern, name=name, grid=(T // tm,),
                          in_specs=[row, _vec_spec(l), _vec_spec(l, ksc), _vec_spec(l, ksh)], out_specs=row,
                          out_shape=_sds((T, D_MODEL), BF16), compiler_params=_cparams("parallel"))(x, gain3, mod, mod)


def _normmod_bwd(name, x, dh, dres, gain3, l, mod, ksc, gate=None):
    T = x.shape[0]
    tm = _tile(T, 512)
    nt = T // tm

    def kern(x_ref, dh_ref, dres_ref, g_ref, sc_ref, *rest):
        if gate:
            y_ref, gv_ref, dx_ref, st_ref, dyg_ref = rest
        else:
            dx_ref, st_ref = rest
        m = pl.program_id(0)

        @pl.when(m == 0)
        def _():
            st_ref[...] = jnp.zeros_like(st_ref)

        xv = x_ref[...]
        dhv = dh_ref[...]
        r = lax.rsqrt(jnp.mean(xv * xv, axis=-1, keepdims=True) + EPS)
        xh = xv * r
        wv = g_ref[...] * (1.0 + sc_ref[...])
        dxh = dhv * wv
        dxv = dres_ref[...] + r * (dxh - xh * jnp.mean(dxh * xh, axis=-1, keepdims=True))
        dx_ref[...] = dxv
        st_ref[0:1, :] += jnp.sum(dhv, axis=0, keepdims=True)
        st_ref[1:2, :] += jnp.sum(dhv * xh, axis=0, keepdims=True)
        if gate:
            dyg_ref[...] = (dxv * gv_ref[...]).astype(BF16)
            st_ref[3:4, :] += jnp.sum(dxv * y_ref[...], axis=0, keepdims=True)

        @pl.when(m == nt - 1)
        def _():
            dw = st_ref[1:2, :]
            st_ref[2:3, :] = dw * (1.0 + sc_ref[...])
            st_ref[1:2, :] = dw * g_ref[...]

    row = pl.BlockSpec((tm, D_MODEL), lambda m: (m, 0))
    stat = pl.BlockSpec((8, D_MODEL), lambda m: (0, 0))
    ins, in_specs = [x, dh, dres, gain3, mod], [row, row, row, _vec_spec(l), _vec_spec(l, ksc)]
    out_specs, out_shape = [row, stat], [_sds((T, D_MODEL), F32), _sds((8, D_MODEL), F32)]
    if gate:
        ins += [gate[0], mod]
        in_specs += [row, _vec_spec(gate[1], gate[2])]
        out_specs.append(row)
        out_shape.append(_sds((T, D_MODEL), BF16))
    return pl.pallas_call(kern, name=name, grid=(nt,), in_specs=in_specs, out_specs=out_specs, out_shape=out_shape,
                          compiler_params=_cparams("arbitrary"))(*ins)


def _gate_bwd(name, dxn, y, mod, l, kg):
    T = dxn.shape[0]
    tm = _tile(T, 512)

    def kern(d_ref, y_ref, g_ref, dy_ref, st_ref):
        @pl.when(pl.program_id(0) == 0)
        def _():
            st_ref[...] = jnp.zeros_like(st_ref)

        dv = d_ref[...]
        dy_ref[...] = (dv * g_ref[...]).astype(BF16)
        st_ref[0:1, :] += jnp.sum(dv * y_ref[...], axis=0, keepdims=True)

    row = pl.BlockSpec((tm, D_MODEL), lambda m: (m, 0))
    return pl.pallas_call(kern, name=name, grid=(T // tm,), in_specs=[row, row, _vec_spec(l, kg)],
                          out_specs=[row, pl.BlockSpec((8, D_MODEL), lambda m: (0, 0))],
                          out_shape=[_sds((T, D_MODEL), BF16), _sds((8, D_MODEL), F32)],
                          compiler_params=_cparams("arbitrary"))(dxn, y, mod)


def _loss_head(y, target):
    T = y.shape[0]
    tm = _tile(T, 512)

    def kern(y_ref, t_ref, dy_ref, acc_ref):
        @pl.when(pl.program_id(0) == 0)
        def _():
            acc_ref[...] = jnp.zeros_like(acc_ref)

        e = y_ref[...] - t_ref[...]
        dy_ref[...] = e * (1.0 / D_MODEL)
        s = jnp.sum(jnp.sum(e * e, axis=-1, keepdims=True), axis=0, keepdims=True)
        acc_ref[...] += s * (0.5 / D_MODEL)

    row = pl.BlockSpec((tm, D_MODEL), lambda m: (m, 0))
    return pl.pallas_call(kern, name="loss_head", grid=(T // tm,), in_specs=[row, row],
                          out_specs=[row, pl.BlockSpec((8, LANE), lambda m: (0, 0))],
                          out_shape=[_sds((T, D_MODEL), F32), _sds((8, LANE), F32)],
                          compiler_params=_cparams("arbitrary"))(y, target)


def _row_chunks(T):
    rc = min(ROW_CHUNK, T)
    return [(r * rc, rc) for r in range(T // rc)]


def _conv_fwd(name, proj, conv_w):
    T = proj.shape[0]
    nblk = CONV_DIM // LANE

    def kern(b_ref, c_ref, u_ref, w_ref, a_ref, zs):
        zs[0:8, :] = jnp.zeros((8, LANE), F32)
        for r0, rc in _row_chunks(T):
            zs[8 + r0:8 + r0 + rc, :] = c_ref[r0:r0 + rc, :] * u_ref[r0:r0 + rc, :]
        w0, w1, w2 = w_ref[0:1, :], w_ref[1:2, :], w_ref[2:3, :]
        for r0, rc in _row_chunks(T):
            yc = w2 * zs[8 + r0:8 + r0 + rc, :] + w1 * zs[7 + r0:7 + r0 + rc, :] + w0 * zs[6 + r0:6 + r0 + rc, :]
            a_ref[r0:r0 + rc, :] = (b_ref[r0:r0 + rc, :] * yc).astype(BF16)

    col = lambda p: pl.BlockSpec((T, LANE), lambda cb: (0, p * nblk + cb))
    return pl.pallas_call(kern, name=name, grid=(nblk,),
                          in_specs=[col(0), col(1), col(2), pl.BlockSpec((CONV_WIDTH, LANE), lambda cb: (0, cb))],
                          out_specs=pl.BlockSpec((T, LANE), lambda cb: (0, cb)),
                          out_shape=_sds((T, CONV_DIM), BF16), scratch_shapes=[pltpu.VMEM((T + 8, LANE), F32)],
                          compiler_params=_cparams("parallel"))(proj, proj, proj, conv_w)


def _conv_bwd(name, proj, conv_w, dcat):
    T = proj.shape[0]
    nblk = CONV_DIM // LANE

    def kern(b_ref, c_ref, u_ref, w_ref, da_ref, db_ref, dc_ref, du_ref, dw_ref, zs, ds):
        zs[0:8, :] = jnp.zeros((8, LANE), F32)
        ds[T:T + 8, :] = jnp.zeros((8, LANE), F32)
        for r0, rc in _row_chunks(T):
            zs[8 + r0:8 + r0 + rc, :] = c_ref[r0:r0 + rc, :] * u_ref[r0:r0 + rc, :]
        w0, w1, w2 = w_ref[0:1, :], w_ref[1:2, :], w_ref[2:3, :]
        acc = [jnp.zeros((1, LANE), F32) for _ in range(3)]
        for r0, rc in _row_chunks(T):
            z0 = zs[8 + r0:8 + r0 + rc, :]
            z1 = zs[7 + r0:7 + r0 + rc, :]
            z2 = zs[6 + r0:6 + r0 + rc, :]
            da = da_ref[r0:r0 + rc, :]
            db_ref[r0:r0 + rc, :] = (da * (w2 * z0 + w1 * z1 + w0 * z2)).astype(BF16)
            dyc = da * b_ref[r0:r0 + rc, :]
            ds[r0:r0 + rc, :] = dyc
            acc[2] = acc[2] + jnp.sum(dyc * z0, axis=0, keepdims=True)
            acc[1] = acc[1] + jnp.sum(dyc * z1, axis=0, keepdims=True)
            acc[0] = acc[0] + jnp.sum(dyc * z2, axis=0, keepdims=True)
        dw_ref[...] = jnp.zeros_like(dw_ref)
        for k in range(3):
            dw_ref[k:k + 1, :] = acc[k]
        for r0, rc in _row_chunks(T):
            dz = w2 * ds[r0:r0 + rc, :] + w1 * ds[r0 + 1:r0 + 1 + rc, :] + w0 * ds[r0 + 2:r0 + 2 + rc, :]
            dc_ref[r0:r0 + rc, :] = (dz * u_ref[r0:r0 + rc, :]).astype(BF16)
            du_ref[r0:r0 + rc, :] = (dz * c_ref[r0:r0 + rc, :]).astype(BF16)

    col = lambda p: pl.BlockSpec((T, LANE), lambda cb: (0, p * nblk + cb))
    out = pl.BlockSpec((T, LANE), lambda cb: (0, cb))
    return pl.pallas_call(kern, name=name, grid=(nblk,),
                          in_specs=[col(0), col(1), col(2), pl.BlockSpec((CONV_WIDTH, LANE), lambda cb: (0, cb)), out],
                          out_specs=[out, out, out, pl.BlockSpec((8, LANE), lambda cb: (0, cb))],
                          out_shape=[_sds((T, CONV_DIM), BF16)] * 3 + [_sds((8, CONV_DIM), F32)],
                          scratch_shapes=[pltpu.VMEM((T + 8, LANE), F32), pltpu.VMEM((T + 8, LANE), F32)],
                          compiler_params=_cparams("parallel"))(proj, proj, proj, conv_w, dcat)


_Q0, _K0, _V0, _G0 = 3 * CONV_DIM // LANE, (3 * CONV_DIM + RET_DIM) // LANE, (3 * CONV_DIM + 2 * RET_DIM) // LANE, \
    (3 * CONV_DIM + 3 * RET_DIM) // LANE


def _ret_tables(B, lg1):
    ti = lax.broadcasted_iota(jnp.int32, (B, B), 0)
    si = lax.broadcasted_iota(jnp.int32, (B, B), 1)
    dist = jnp.abs(ti - si).astype(F32)
    shift = RET_CHUNK.bit_length() - 1
    dmat = jnp.where((si >> shift) <= (ti >> shift), jnp.exp(dist * lg1), 0.0)
    tcol = lax.broadcasted_iota(jnp.int32, (B, 1), 0).astype(F32)
    qdec = jnp.exp((tcol + 1.0) * lg1)
    kdec = jnp.exp((B - 1.0 - tcol) * lg1)
    bdec = jnp.exp(float(B) * lg1)
    return dmat, qdec, kdec, bdec


def _retention_fwd(name, proj, cosf, sinf, gr, lgt):
    T = proj.shape[0]
    B = min(RET_BLOCK, T)
    nb = T // B

    def kern(q_ref, k_ref, v_ref, g_ref, cos_ref, sin_ref, gr_ref, lg_ref, r_ref, o_ref, st_ref, S):
        @pl.when(pl.program_id(1) == 0)
        def _():
            S[...] = jnp.zeros_like(S)

        cosv, sinv = cos_ref[...], sin_ref[...]
        rot = lambda a: a * cosv + pltpu.roll(a, HEAD // 2, 1) * sinv
        qr = rot(q_ref[...])
        kr = rot(k_ref[...]) * QK_SCALE
        v = v_ref[...]
        dmat, qdec, kdec, bdec = _ret_tables(B, lg_ref[0:1, 0:1])
        sv = S[...]
        st_ref[...] = sv
        pd = _dot(qr, kr, 1, 1) * dmat
        o = _dot(pd, v, 1, 0) + _dot(qr * qdec, sv, 1, 0)
        S[...] = bdec * sv + _dot(kr * kdec, v, 0, 0)
        o_ref[...] = o
        rs = lax.rsqrt(jnp.mean(o * o, axis=-1, keepdims=True) + EPS)
        g = g_ref[...]
        r_ref[...] = (g * _sig(g) * (o * rs * gr_ref[...])).astype(BF16)

    col = lambda c0: pl.BlockSpec((B, HEAD), lambda h, i: (i, c0 + h))
    tab = pl.BlockSpec((B, HEAD), lambda h, i: (i, 0))
    outc = pl.BlockSpec((B, HEAD), lambda h, i: (i, h))
    return pl.pallas_call(
        kern, name=name, grid=(RET_HEADS, nb),
        in_specs=[col(_Q0), col(_K0), col(_V0), col(_G0), tab, tab, pl.BlockSpec((1, HEAD), lambda h, i: (0, h)),
                  pl.BlockSpec((None, 1, LANE), lambda h, i: (h, 0, 0))],
        out_specs=[outc, outc, pl.BlockSpec((None, None, HEAD, HEAD), lambda h, i: (h, i, 0, 0))],
        out_shape=[_sds((T, RET_DIM), BF16), _sds((T, RET_DIM), F32), _sds((RET_HEADS, nb, HEAD, HEAD), F32)],
        scratch_shapes=[pltpu.VMEM((HEAD, HEAD), F32)],
        compiler_params=_cparams("parallel", "arbitrary"))(proj, proj, proj, proj, cosf, sinf, gr, lgt)


def _retention_bwd(name, proj, oraw, states, dcat, cosf, sinf, gr, lgt):
    T = proj.shape[0]
    B = min(RET_BLOCK, T)
    nb = T // B

    def kern(q_ref, k_ref, v_ref, g_ref, o_ref, dr_ref, st_ref, cos_ref, sin_ref, gr_ref, lg_ref,
             dq_ref, dk_ref, dv_ref, dg_ref, dgr_ref, dS):
        @pl.when(pl.program_id(1) == 0)
        def _():
            dS[...] = jnp.zeros_like(dS)
            dgr_ref[...] = jnp.zeros_like(dgr_ref)

        cosv, sinv = cos_ref[...], sin_ref[...]
        rot = lambda a: a * cosv + pltpu.roll(a, HEAD // 2, 1) * sinv
        rot_t = lambda a: a * cosv + pltpu.roll(a * sinv, HEAD // 2, 1)
        qr = rot(q_ref[...])
        kr = rot(k_ref[...]) * QK_SCALE
        v = v_ref[...]
        dmat, qdec, kdec, bdec = _ret_tables(B, lg_ref[0:1, 0:1])
        o = o_ref[...]
        rs = lax.rsqrt(jnp.mean(o * o, axis=-1, keepdims=True) + EPS)
        xh = o * rs
        g = g_ref[...]
        sg = _sig(g)
        grv = gr_ref[...]
        dr = dr_ref[...]
        dn = dr * (g * sg)
        dg_ref[...] = (dr * (xh * grv) * (sg * (1.0 + g * (1.0 - sg)))).astype(BF16)
        dgr_ref[0:1, :] += jnp.sum(dn * xh, axis=0, keepdims=True)
        dxh = dn * grv
        do = rs * (dxh - xh * jnp.mean(dxh * xh, axis=-1, keepdims=True))
        sp = st_ref[...]
        dsv = dS[...]
        pd = _dot(qr, kr, 1, 1) * dmat
        dp = _dot(do, v, 1, 1) * dmat
        dv_ref[...] = (_dot(pd, do, 0, 0) + _dot(kr * kdec, dsv, 1, 0)).astype(BF16)
        dqr = _dot(dp, kr, 1, 0) + _dot(do, sp, 1, 1) * qdec
        dkr = _dot(dp, qr, 0, 0) + _dot(v, dsv, 1, 1) * kdec
        dS[...] = bdec * dsv + _dot(qr * qdec, do, 0, 0)
        dq_ref[...] = rot_t(dqr).astype(BF16)
        dk_ref[...] = (rot_t(dkr) * QK_SCALE).astype(BF16)

    rev = lambda i: nb - 1 - i
    col = lambda c0: pl.BlockSpec((B, HEAD), lambda h, i: (rev(i), c0 + h))
    tab = pl.BlockSpec((B, HEAD), lambda h, i: (rev(i), 0))
    outc = pl.BlockSpec((B, HEAD), lambda h, i: (rev(i), h))
    return pl.pallas_call(
        kern, name=name, grid=(RET_HEADS, nb),
        in_specs=[col(_Q0), col(_K0), col(_V0), col(_G0), outc,
                  pl.BlockSpec((B, HEAD), lambda h, i: (rev(i), CONV_DIM // LANE + h)),
                  pl.BlockSpec((None, None, HEAD, HEAD), lambda h, i: (h, rev(i), 0, 0)), tab, tab,
                  pl.BlockSpec((1, HEAD), lambda h, i: (0, h)), pl.BlockSpec((None, 1, LANE), lambda h, i: (h, 0, 0))],
        out_specs=[outc, outc, outc, outc, pl.BlockSpec((8, HEAD), lambda h, i: (0, h))],
        out_shape=[_sds((T, RET_DIM), BF16)] * 4 + [_sds((8, RET_DIM), F32)],
        scratch_shapes=[pltpu.VMEM((HEAD, HEAD), F32)],
        compiler_params=_cparams("parallel", "arbitrary"))(proj, proj, proj, proj, oraw, dcat, states, cosf, sinf, gr, lgt)


def _strict_upper(q_rows):
    r = lax.broadcasted_iota(jnp.int32, (SB_BLOCK, SB_BLOCK), 0)
    c = lax.broadcasted_iota(jnp.int32, (SB_BLOCK, SB_BLOCK), 1)
    rq = lax.broadcasted_iota(jnp.int32, (q_rows, SB_BLOCK), 0)
    cq = lax.broadcasted_iota(jnp.int32, (q_rows, SB_BLOCK), 1)
    return (r > c).astype(BF16), cq - rq


def _suffix_sum(vals, tri):
    hi = vals.astype(BF16)
    lo = (vals - hi.astype(F32)).astype(BF16)
    dn = (((1,), (0,)), ((), ()))
    return lax.dot_general(hi, tri, dn, preferred_element_type=F32) + lax.dot_general(lo, tri, dn, preferred_element_type=F32)


def _sb_scores(qi, kj, tri, col_minus_row, mask_off):
    z = _dot(qi, kj, 1, 1) * QK_SCALE
    lb = jnp.minimum(z, 0.0) - jnp.log(1.0 + jnp.exp(-jnp.abs(z)))
    if mask_off is not None:
        valid = col_minus_row < mask_off
        lk = jnp.where(valid, lb - z, 0.0)
        w_loc = jnp.where(valid, jnp.exp(lb + _suffix_sum(lk, tri)), 0.0)
        return lb, lk, valid, w_loc
    lk = lb - z
    return lb, lk, None, jnp.exp(lb + _suffix_sum(lk, tri))


def _sb_walk(i, ratio, tile, st):
    for d in reversed(range(ratio)):
        st = tile(ratio * i + d, -SB_BLOCK * d, st)
    n_free = ratio * i

    def pair(p, s):
        j = n_free - 1 - 2 * p
        return tile(j - 1, None, tile(j, None, s))

    st = lax.fori_loop(0, n_free // 2, pair, st)
    return lax.fori_loop(0, n_free % 2, lambda _, s: tile(0, None, s), st)


def _head_norm_rows(src, gain, dst, T):
    for r0, rc in _row_chunks(T):
        a = src[r0:r0 + rc, :]
        r = lax.rsqrt(jnp.mean(a * a, axis=-1, keepdims=True) + EPS)
        dst[r0:r0 + rc, :] = (a * r * gain).astype(BF16)


def _sb_fwd(name, qkv, gq, gk, gather=None):
    T = qkv.shape[0]
    qr = min(SB_QROWS, T)
    nq, ratio = T // qr, qr // SB_BLOCK
    n_g = len(gather[0]) if gather else 0

    def kern(q_ref, k_ref, v_ref, gq_ref, gk_ref, *rest):
        shard_refs, rest = rest[:n_g], rest[2 * n_g:]
        o_ref, o32_ref, rest = rest[0], rest[1], rest[2:]
        w4_refs, rest = rest[:n_g], rest[n_g:]
        qn, kn, vb, sems = rest[0], rest[1], rest[2], rest[3:]
        head = pl.program_id(0)
        if gather:
            @pl.when(head == 0)
            def _():
                _gather_start(shard_refs, w4_refs, sems, gather[2])

        _head_norm_rows(q_ref, gq_ref[...], qn, T)
        _head_norm_rows(k_ref, gk_ref[...], kn, T)
        for r0, rc in _row_chunks(T):
            vb[r0:r0 + rc, :] = v_ref[r0:r0 + rc, :].astype(BF16)
        tri, diag_mask = _strict_upper(qr)

        def qblock(i, _):
            rows_i = pl.ds(pl.multiple_of(i * qr, qr), qr)
            qi = qn[rows_i, :]

            def tile(j, mask_off, st):
                acc, car = st
                rows_j = pl.ds(pl.multiple_of(j * SB_BLOCK, SB_BLOCK), SB_BLOCK)
                _, lk, _, w = _sb_scores(qi, kn[rows_j, :], tri, diag_mask, mask_off)
                w_hi = w.astype(BF16)
                w_lo = (w - w_hi.astype(F32)).astype(BF16)
                vj = vb[rows_j, :]
                acc = acc + jnp.exp(car) * (_dot(w_hi, vj, 1, 0) + _dot(w_lo, vj, 1, 0))
                return acc, car + jnp.sum(lk, axis=-1, keepdims=True)

            acc, _ = _sb_walk(i, ratio, tile, (jnp.zeros((qr, HEAD), F32), jnp.zeros((qr, 1), F32)))
            o_ref[rows_i, :] = acc.astype(BF16)
            o32_ref[rows_i, :] = acc
            return 0

        lax.fori_loop(0, nq, qblock, 0)
        if gather:
            @pl.when(head == SB_HEADS - 2)
            def _():
                _gather_forward(shard_refs, w4_refs, sems, gather[2])

            @pl.when(head == SB_HEADS - 1)
            def _():
                _gather_finish(shard_refs, w4_refs, sems, gather[2])

    col = lambda c0: pl.BlockSpec((T, HEAD), lambda h: (0, c0 + h))
    vec = pl.BlockSpec((1, HEAD), lambda h: (0, 0))
    out = pl.BlockSpec((T, HEAD), lambda h: (0, h))
    extra_in = list(gather[0]) + list(gather[1]) if gather else []
    res = pl.pallas_call(kern, name=name, grid=(SB_HEADS,),
                         in_specs=[col(0), col(SB_HEADS), col(2 * SB_HEADS), vec, vec] + [_ANY] * (2 * n_g),
                         out_specs=[out, out] + [_ANY] * n_g,
                         out_shape=[_sds((T, D_MODEL), BF16), _sds((T, D_MODEL), F32)]
                         + [_sds(w.shape, w.dtype) for w in (gather[1] if gather else [])],
                         scratch_shapes=[pltpu.VMEM((T, HEAD), BF16)] * 3 + (_gather_sems(n_g) if gather else []),
                         input_output_aliases={5 + n_g + t: 2 + t for t in range(n_g)},
                         compiler_params=_cparams("arbitrary" if gather else "parallel"))(qkv, qkv, qkv, gq, gk, *extra_in)
    return res[0], res[1], list(res[2:])


def _sb_bwd(name, qkv, gq, gk, o32, dcat, scatter=None):
    T = qkv.shape[0]
    qr = min(SB_QROWS, T)
    nq, ratio = T // qr, qr // SB_BLOCK
    n_s = len(scatter[0]) if scatter else 0

    def kern(q_ref, k_ref, v_ref, gq_ref, gk_ref, o_ref, do_ref, *rest):
        ps_refs, rest = rest[:n_s], rest[n_s:]
        dq_ref, dk_ref, dv_ref, dgq_ref, dgk_ref = rest[:5]
        slot_refs, rest = rest[5:5 + n_s], rest[5 + n_s:]
        qn, kn, vb, dqn, dkn, dvv = rest[:6]
        sems = rest[6:]

        @pl.when(pl.program_id(0) == 0)
        def _():
            dgq_ref[...] = jnp.zeros_like(dgq_ref)
            dgk_ref[...] = jnp.zeros_like(dgk_ref)
            if scatter:
                _scatter_start(ps_refs, slot_refs, sems, scatter[1])

        _head_norm_rows(q_ref, gq_ref[...], qn, T)
        _head_norm_rows(k_ref, gk_ref[...], kn, T)
        for r0, rc in _row_chunks(T):
            vb[r0:r0 + rc, :] = v_ref[r0:r0 + rc, :].astype(BF16)
            dkn[r0:r0 + rc, :] = jnp.zeros((rc, HEAD), F32)
            dvv[r0:r0 + rc, :] = jnp.zeros((rc, HEAD), F32)
        tri, diag_mask = _strict_upper(qr)

        def qblock(i, _):
            rows_i = pl.ds(pl.multiple_of(i * qr, qr), qr)
            qi = qn[rows_i, :]
            doi = do_ref[rows_i, :]
            dob = doi.astype(BF16)
            etot = jnp.sum(dob.astype(F32) * o_ref[rows_i, :], axis=-1, keepdims=True)

            def tile(j, mask_off, st):
                dq_acc, car, ecar = st
                rows_j = pl.ds(pl.multiple_of(j * SB_BLOCK, SB_BLOCK), SB_BLOCK)
                kj = kn[rows_j, :]
                vj = vb[rows_j, :]
                lb, lk, valid, w_loc = _sb_scores(qi, kj, tri, diag_mask, mask_off)
                w = w_loc * jnp.exp(car)
                e = w * _dot(dob, vj, 1, 1)
                suff = _suffix_sum(e, tri) + e + ecar
                sig = jnp.exp(lb)
                dz = (e * (1.0 - sig) - sig * (etot - suff)) * QK_SCALE
                if mask_off is not None:
                    dz = jnp.where(valid, dz, 0.0)
                dzb = dz.astype(BF16)
                dkn[rows_j, :] += _dot(dzb, qi, 0, 0)
                dvv[rows_j, :] += _dot(w, dob, 0, 0)
                return (dq_acc + _dot(dzb, kj, 1, 0), car + jnp.sum(lk, axis=-1, keepdims=True),
                        ecar + jnp.sum(e, axis=-1, keepdims=True))

            zcol = jnp.zeros((qr, 1), F32)
            dq_acc, _, _ = _sb_walk(i, ratio, tile, (jnp.zeros((qr, HEAD), F32), zcol, zcol))
            dqn[rows_i, :] = dq_acc
            return 0

        lax.fori_loop(0, nq, qblock, 0)

        def norm_bwd(src, gain, dnorm, dst, dgain):
            tot = jnp.zeros((1, HEAD), F32)
            for r0, rc in _row_chunks(T):
                a = src[r0:r0 + rc, :]
                r = lax.rsqrt(jnp.mean(a * a, axis=-1, keepdims=True) + EPS)
                xh = a * r
                dn = dnorm[r0:r0 + rc, :]
                tot = tot + jnp.sum(dn * xh, axis=0, keepdims=True)
                dxh = dn * gain
                dst[r0:r0 + rc, :] = (r * (dxh - xh * jnp.mean(dxh * xh, axis=-1, keepdims=True))).astype(BF16)
            dgain[0:1, :] += tot

        norm_bwd(q_ref, gq_ref[...], dqn, dq_ref, dgq_ref)
        norm_bwd(k_ref, gk_ref[...], dkn, dk_ref, dgk_ref)
        for r0, rc in _row_chunks(T):
            dv_ref[r0:r0 + rc, :] = dvv[r0:r0 + rc, :].astype(BF16)
        if scatter:
            @pl.when(pl.program_id(0) == SB_HEADS - 1)
            def _():
                _scatter_finish(ps_refs, slot_refs, sems, scatter[1])

    col = lambda c0: pl.BlockSpec((T, HEAD), lambda h: (0, c0 + h))
    vec = pl.BlockSpec((1, HEAD), lambda h: (0, 0))
    out = pl.BlockSpec((T, HEAD), lambda h: (0, h))
    st = pl.BlockSpec((8, HEAD), lambda h: (0, 0))
    ps = list(scatter[0]) if scatter else []
    res = pl.pallas_call(kern, name=name, grid=(SB_HEADS,),
                         in_specs=[col(0), col(SB_HEADS), col(2 * SB_HEADS), vec, vec, out, out] + [_ANY] * n_s,
                         out_specs=[out, out, out, st, st] + [_ANY] * n_s,
                         out_shape=[_sds((T, D_MODEL), BF16)] * 3 + [_sds((8, HEAD), F32)] * 2
                         + [_sds(p.shape, p.dtype) for p in ps],
                         scratch_shapes=[pltpu.VMEM((T, HEAD), BF16)] * 3 + [pltpu.VMEM((T, HEAD), F32)] * 3
                         + (_scatter_sems(n_s) if scatter else []),
                         compiler_params=_cparams("arbitrary"))(qkv, qkv, qkv, gq, gk, o32, dcat, *ps)
    return tuple(res[:5]) + (list(res[5:]),)


def _ada_fwd(c_all, ada_w, ada_b_cols):
    L, K, n = ada_w.shape

    def kern(c_ref, w_ref, b_ref, o_ref):
        cv = c_ref[...]
        o_ref[...] = _dot(cv * _sig(cv), w_ref[...], 1, 0) + b_ref[...]

    return pl.pallas_call(kern, name="ada_fwd", grid=(L,),
                          in_specs=[pl.BlockSpec((N_DEV, K), lambda l: (0, 0)), pl.BlockSpec((None, K, n), lambda l: (l, 0, 0)),
                                    pl.BlockSpec((None, 1, n), lambda l: (l, 0, 0))],
                          out_specs=pl.BlockSpec((None, N_DEV, n), lambda l: (l, 0, 0)),
                          out_shape=_sds((L, N_DEV, n), F32), compiler_params=_cparams("parallel"))(c_all, ada_w, ada_b_cols)


def _ada_wgrad(c_all_t, dmod_cols):
    K = c_all_t.shape[0]
    L, _, n = dmod_cols.shape
    tk = 128

    def kern(c_ref, d_ref, o_ref):
        cv = c_ref[...]
        ca = cv * _sig(cv)
        acc = ca[:, 0:1] * d_ref[0:1, :]
        for b in range(1, N_DEV):
            acc = acc + ca[:, b:b + 1] * d_ref[b:b + 1, :]
        o_ref[...] = acc

    return pl.pallas_call(kern, name="ada_wgrad", grid=(L, K // tk),
                          in_specs=[pl.BlockSpec((tk, N_DEV), lambda l, m: (m, 0)), pl.BlockSpec((None, N_DEV, n), lambda l, m: (l, 0, 0))],
                          out_specs=pl.BlockSpec((None, tk, n), lambda l, m: (l, m, 0)),
                          out_shape=_sds((L, K, n), F32), compiler_params=_cparams("parallel", "parallel"))(c_all_t, dmod_cols)


def _sum_devices(g):
    _, R, n = g.shape

    def kern(g_ref, o_ref):
        acc = g_ref[0]
        for d in range(1, N_DEV):
            acc = acc + g_ref[d]
        o_ref[...] = acc

    return pl.pallas_call(kern, name="sum_devices", out_shape=_sds((R, n), F32))(g)


def _pair_sum(name, g, recv, half):
    n4, L, r, cdim = g.shape
    rows = (L // 2) * r
    tr = _tile(rows, 512)
    gv = g.reshape(n4, 2, rows, cdim)
    rv = recv.reshape(n4, rows, cdim)

    def kern(g_ref, r_ref, o_ref):
        o_ref[...] = (g_ref[...] + r_ref[...]).astype(BF16)

    out = pl.pallas_call(kern, name=name, grid=(n4, rows // tr),
                         in_specs=[pl.BlockSpec((None, None, tr, cdim), lambda j, m: (j, half, m, 0)),
                                   pl.BlockSpec((None, tr, cdim), lambda j, m: (j, m, 0))],
                         out_specs=pl.BlockSpec((None, tr, cdim), lambda j, m: (j, m, 0)),
                         out_shape=_sds((n4, rows, cdim), BF16),
                         compiler_params=_cparams("parallel", "parallel"))(gv, rv)
    return out.reshape(n4, L // 2, r, cdim)


def _chip_sum(name, recv, psum, pos_idx, half, total):
    n4, hl, r, cdim = recv.shape
    rows = hl * r
    tr = _tile(rows, 512)

    def kern(pos_ref, r0, r1, r2, r3, own_ref, t_ref, o_ref):
        me = pos_ref[0]
        own = own_ref[...].astype(F32)
        terms = [jnp.where(me == s, own, rr[...].astype(F32)) for s, rr in enumerate((r0, r1, r2, r3))]
        o_ref[...] = ((terms[0] + terms[1]) + terms[2]) + terms[3]

    def slot(s):
        return pl.BlockSpec((None, tr, cdim), lambda m, pos: (jnp.where(pos[0] == s, (s + 1) % n4, s), m, 0))

    gs = pltpu.PrefetchScalarGridSpec(
        num_scalar_prefetch=1, grid=(rows // tr,),
        in_specs=[slot(s) for s in range(n4)] + [pl.BlockSpec((None, tr, cdim), lambda m, pos: (pos[0], m, 0)), _ANY],
        out_specs=pl.BlockSpec((None, tr, cdim), lambda m, pos: (half, m, 0)))
    rv = recv.reshape(n4, rows, cdim)
    out = pl.pallas_call(kern, name=name, grid_spec=gs, out_shape=_sds((2, rows, cdim), F32),
                         input_output_aliases={6: 0}, compiler_params=_cparams("parallel"))(
        pos_idx, rv, rv, rv, rv, psum.reshape(n4, rows, cdim), total.reshape(2, rows, cdim))
    return out.reshape(2 * hl, r, cdim)


def _adamw(name, w, g, m, v):
    shape = w.shape
    cols = shape[-1]
    rows = int(np.prod(shape[:-1]))
    tr = _tile(rows, 512) if rows % 8 == 0 else rows
    c1 = 1.0 - ADAM_B1 ** ADAM_STEP
    c2 = 1.0 - ADAM_B2 ** ADAM_STEP

    def kern(w_ref, g_ref, m_ref, v_ref, d_ref, mo_ref, vo_ref):
        gv = g_ref[...]
        mn = ADAM_B1 * m_ref[...] + (1.0 - ADAM_B1) * gv
        vn = ADAM_B2 * v_ref[...] + (1.0 - ADAM_B2) * (gv * gv)
        mo_ref[...] = mn
        vo_ref[...] = vn
        d_ref[...] = -ADAM_LR * ((mn / c1) / (jnp.sqrt(vn / c2) + ADAM_EPS) + ADAM_WD * w_ref[...])

    blk = pl.BlockSpec((tr, cols), lambda i: (i, 0))
    outs = pl.pallas_call(kern, name=name, grid=(rows // tr,), in_specs=[blk] * 4, out_specs=[blk] * 3,
                          out_shape=[_sds((rows, cols), F32)] * 3, compiler_params=_cparams("parallel"))(
        *[a.reshape(rows, cols) for a in (w, g, m, v)])
    return tuple(o.reshape(shape) for o in outs)


def _rcopy(src, dst, ssem, rsem, dev):
    return pltpu.make_async_remote_copy(src_ref=src, dst_ref=dst, send_sem=ssem, recv_sem=rsem, device_id=dev,
                                        device_id_type=MESH)


def _gather8(name, blk):
    m_per, n = blk.shape

    def body(x_ref, out_ref, send_sems, recv_sems, local_sem):
        x, y, c = _position()
        me, sibling = (x, y, c), (x, y, 1 - c)
        chips = _other_chips(x, y)

        def rows(px, py, pc):
            return out_ref.at[pl.ds((4 * px + 2 * py + pc) * m_per, m_per), :]

        def copy(k, block, to, src=None):
            return _rcopy(rows(*block) if src is None else src, rows(*block), send_sems.at[k], recv_sems.at[k], to)

        mine = pltpu.make_async_copy(x_ref, rows(*me), local_sem)
        mine.start()
        first = [copy(0, me, sibling, src=x_ref)]
        first += [copy(1 + j, me, (*chip, c), src=x_ref) for j, chip in enumerate(chips)]
        for cp in first:
            cp.start()
        passed = [copy(4 + j, (*chip, c), sibling) for j, chip in enumerate(chips)]
        for j, chip in enumerate(chips):
            copy(1 + j, (*chip, c), me).wait_recv()
            passed[j].start()
        copy(0, sibling, me).wait_recv()
        for j, chip in enumerate(chips):
            copy(4 + j, (*chip, 1 - c), me).wait_recv()
        for cp in first + passed:
            cp.wait_send()
        mine.wait()

    return pl.pallas_call(body, name=name, out_shape=_sds((N_DEV * m_per, n), blk.dtype),
                          in_specs=[pl.BlockSpec(memory_space=pltpu.VMEM)], out_specs=pl.BlockSpec(memory_space=pltpu.VMEM),
                          scratch_shapes=[pltpu.SemaphoreType.DMA((7,)), pltpu.SemaphoreType.DMA((7,)), pltpu.SemaphoreType.DMA],
                          compiler_params=pltpu.CompilerParams(vmem_limit_bytes=VMEM_LIMIT_V7X))(blk)


_ANY = pl.BlockSpec(memory_space=pl.ANY)


def _gather_sems(n):
    return [pltpu.SemaphoreType.DMA((3 * n,))] * 4


def _gather_start(src, out, sems, half):
    send_sems, recv_sems = sems[0], sems[1]
    x, y, c = _position()
    chips = _other_chips(x, y)
    me_chip = 2 * x + y

    @pl.when(c == half)
    def _():
        for t in range(len(src)):
            hl = src[t].shape[0] // 2
            rows = pl.ds(half * hl, hl)
            for j, (px, py) in enumerate(chips):
                _rcopy(src[t].at[rows], out[t].at[me_chip, rows], send_sems.at[3 * t + j], recv_sems.at[3 * t + j],
                       (px, py, half)).start()


def _gather_forward(src, out, sems, half):
    send_sems, recv_sems, fsend_sems, frecv_sems = sems
    x, y, c = _position()
    chips = _other_chips(x, y)

    @pl.when(c == half)
    def _():
        for t in range(len(src)):
            hl = src[t].shape[0] // 2
            rows = pl.ds(half * hl, hl)
            for j, (px, py) in enumerate(chips):
                landed = out[t].at[2 * px + py, rows]
                _rcopy(landed, landed, send_sems.at[3 * t + j], recv_sems.at[3 * t + j], (px, py, half)).wait_recv()
                _rcopy(landed, landed, fsend_sems.at[3 * t + j], frecv_sems.at[3 * t + j], (x, y, 1 - half)).start()


def _gather_finish(src, out, sems, half):
    send_sems, recv_sems, fsend_sems, frecv_sems = sems
    x, y, c = _position()
    chips = _other_chips(x, y)
    me_chip = 2 * x + y

    @pl.when(c == half)
    def _():
        for t in range(len(src)):
            hl = src[t].shape[0] // 2
            rows = pl.ds(half * hl, hl)
            for j, (px, py) in enumerate(chips):
                landed = out[t].at[2 * px + py, rows]
                _rcopy(src[t].at[rows], out[t].at[me_chip, rows], send_sems.at[3 * t + j], recv_sems.at[3 * t + j],
                       (px, py, half)).wait_send()
                _rcopy(landed, landed, fsend_sems.at[3 * t + j], frecv_sems.at[3 * t + j], (x, y, 1 - half)).wait_send()

    @pl.when(c != half)
    def _():
        for t in range(len(src)):
            hl = src[t].shape[0] // 2
            rows = pl.ds(half * hl, hl)
            for j, (px, py) in enumerate(chips):
                landed = out[t].at[2 * px + py, rows]
                _rcopy(landed, landed, fsend_sems.at[3 * t + j], frecv_sems.at[3 * t + j], (x, y, half)).wait_recv()


def _gather_weights(shards, half):
    n = len(shards)

    def body(*refs):
        src, out, sems = refs[:n], refs[n:2 * n], refs[2 * n:]
        _gather_start(src, out, sems, half)
        _gather_forward(src, out, sems, half)
        _gather_finish(src, out, sems, half)

    return pl.pallas_call(body, name="gather_weights", out_shape=[_sds((N_CHIP,) + s.shape, s.dtype) for s in shards],
                          in_specs=[_ANY] * n, out_specs=[_ANY] * n, scratch_shapes=_gather_sems(n))(*shards)


def _place_own(name, w4, shard, pos_idx):
    n4, L, r, cdim = w4.shape
    rows = L * r
    tr = _tile(rows, 1024)

    def kern(pos_ref, s_ref, w_ref, o_ref):
        o_ref[...] = s_ref[...]

    gs = pltpu.PrefetchScalarGridSpec(
        num_scalar_prefetch=1, grid=(rows // tr,),
        in_specs=[pl.BlockSpec((tr, cdim), lambda m, pos: (m, 0)), _ANY],
        out_specs=pl.BlockSpec((None, tr, cdim), lambda m, pos: (pos[0], m, 0)))
    out = pl.pallas_call(kern, name=name, grid_spec=gs, out_shape=_sds((n4, rows, cdim), w4.dtype),
                         input_output_aliases={2: 0}, compiler_params=_cparams("parallel"))(
        pos_idx, shard.reshape(rows, cdim), w4.reshape(n4, rows, cdim))
    return out.reshape(w4.shape)


def _exchange_pair(grads, to_c):
    n = len(grads)

    def body(*refs):
        src, out = refs[:n], refs[n:2 * n]
        send_sems, recv_sems = refs[2 * n:]
        x, y, c = _position()
        sibling = (x, y, 1 - c)

        @pl.when(c != to_c)
        def _():
            cps = []
            for t in range(n):
                hl = src[t].shape[1] // 2
                cp = _rcopy(src[t].at[:, pl.ds(to_c * hl, hl)], out[t], send_sems.at[t], recv_sems.at[t], sibling)
                cp.start()
                cps.append(cp)
            for cp in cps:
                cp.wait_send()

        @pl.when(c == to_c)
        def _():
            for t in range(n):
                _rcopy(out[t], out[t], send_sems.at[t], recv_sems.at[t], sibling).wait_recv()

    sems = [pltpu.SemaphoreType.DMA((n,))] * 2
    return pl.pallas_call(body, name=f"exchange_pair_{to_c}",
                          out_shape=[_sds((g.shape[0], g.shape[1] // 2) + g.shape[2:], g.dtype) for g in grads],
                          in_specs=[_ANY] * n, out_specs=[_ANY] * n, scratch_shapes=sems)(*grads)


def _scatter_sems(n):
    return [pltpu.SemaphoreType.DMA((3 * n,))] * 2


def _scatter_start(src, out, sems, half):
    send_sems, recv_sems = sems
    x, y, c = _position()
    chips = _other_chips(x, y)
    me_chip = 2 * x + y

    @pl.when(c == half)
    def _():
        for t in range(len(src)):
            for j, (px, py) in enumerate(chips):
                _rcopy(src[t].at[2 * px + py], out[t].at[me_chip], send_sems.at[3 * t + j], recv_sems.at[3 * t + j],
                       (px, py, half)).start()


def _scatter_finish(src, out, sems, half):
    send_sems, recv_sems = sems
    x, y, c = _position()
    chips = _other_chips(x, y)
    me_chip = 2 * x + y

    @pl.when(c == half)
    def _():
        for t in range(len(src)):
            for j, (px, py) in enumerate(chips):
                slot = out[t].at[2 * px + py]
                _rcopy(slot, slot, send_sems.at[3 * t + j], recv_sems.at[3 * t + j], (px, py, half)).wait_recv()
        for t in range(len(src)):
            for j, (px, py) in enumerate(chips):
                _rcopy(src[t].at[2 * px + py], out[t].at[me_chip], send_sems.at[3 * t + j], recv_sems.at[3 * t + j],
                       (px, py, half)).wait_send()


def _scatter_chips(psums, half):
    n = len(psums)

    def body(*refs):
        src, out, sems = refs[:n], refs[n:2 * n], refs[2 * n:]
        _scatter_start(src, out, sems, half)
        _scatter_finish(src, out, sems, half)

    return pl.pallas_call(body, name="scatter_chips", out_shape=[_sds(p.shape, p.dtype) for p in psums],
                          in_specs=[_ANY] * n, out_specs=[_ANY] * n, scratch_shapes=_scatter_sems(n))(*psums)


def _share_halves(full):
    n = len(full)

    def body(*refs):
        out = refs[n:2 * n]
        send_sems, recv_sems = refs[2 * n:]
        x, y, c = _position()
        sibling = (x, y, 1 - c)
        cps = []
        for t in range(n):
            hl = out[t].shape[0] // 2
            mine = out[t].at[pl.ds(c * hl, hl)]
            cp = _rcopy(mine, mine, send_sems.at[t], recv_sems.at[t], sibling)
            cp.start()
            cps.append(cp)
        for t in range(n):
            hl = out[t].shape[0] // 2
            theirs = out[t].at[pl.ds((1 - c) * hl, hl)]
            _rcopy(theirs, theirs, send_sems.at[t], recv_sems.at[t], sibling).wait_recv()
        for cp in cps:
            cp.wait_send()

    sems = [pltpu.SemaphoreType.DMA((n,))] * 2
    return pl.pallas_call(body, name="share_halves", out_shape=[_sds(h.shape, h.dtype) for h in full],
                          in_specs=[_ANY] * n, out_specs=[_ANY] * n, scratch_shapes=sems,
                          input_output_aliases={t: t for t in range(n)})(*full)


def _rope_tables(T):
    inv_freq = 1.0 / (ROPE_THETA ** (jnp.arange(0, HEAD, 2, dtype=F32) / HEAD))
    ang = jnp.arange(T, dtype=F32)[:, None] * inv_freq[None, :]
    cos, sin = jnp.cos(ang), jnp.sin(ang)
    return jnp.concatenate([cos, cos], axis=-1), jnp.concatenate([-sin, sin], axis=-1)


def _decay_table():
    lg = np.log1p(-np.exp2(-5.0 - np.arange(RET_HEADS, dtype=np.float32))).astype(np.float32)
    return jnp.asarray(np.broadcast_to(lg[:, None, None], (RET_HEADS, 1, LANE)).copy())


def _local_step(x0, target, mod, W, G, norm_mix_g, norm_ffn_g, conv_full, ev_ret_norm_g, od_q_norm_g, od_k_norm_g,
                fused=None):
    T = x0.shape[0]
    KSH1, KSC1, KG1, KSH2, KSC2, KG2 = range(6)
    gain_mix = norm_mix_g.reshape(DEPTH, 1, D_MODEL)
    gain_ffn = norm_ffn_g.reshape(DEPTH, 1, D_MODEL)
    cosf, sinf = _rope_tables(T)
    lgt = _decay_table()

    saved = []
    xcur = x0
    h = _normmod("norm_mix_0", x0, gain_mix, 0, mod, KSC1, KSH1)
    for l in range(DEPTH):
        j = l // 2
        s = dict(x_in=xcur, h=h)
        to_ffn = (gain_ffn, l, l, KSC2, KSH2)
        if l % 2 == 0:
            proj = _proj_cols(f"ev_in_{l}", h, W["ev_w_in"], j)
            a = _conv_fwd(f"conv_{l}", proj, conv_full[j])
            r, oraw, states = _retention_fwd(f"ret_{l}", proj, cosf, sinf, ev_ret_norm_g[j].reshape(1, RET_DIM), lgt)
            cat = jnp.concatenate([a, r], axis=1)
            s.update(proj=proj, oraw=oraw, states=states, cat=cat)
            y, xmid, h2 = _out_proj(f"ev_out_{l}", cat, False, W["ev_w_out"], j, xcur, mod, l, KG1, to_ffn)
        else:
            qkv = _proj_cols(f"od_in_{l}", h, W["od_w_qkv"], j)
            gather = (fused["shards"], [W[k] for k in fused["names"]], 1) if fused is not None and l == 1 else None
            o, o32, w4s = _sb_fwd(f"sb_{l}", qkv, od_q_norm_g[j].reshape(1, HEAD), od_k_norm_g[j].reshape(1, HEAD), gather)
            if gather:
                W = dict(zip(fused["names"], w4s))
            s.update(qkv=qkv, cat=o, o32=o32)
            y, xmid, h2 = _out_proj(f"od_out_{l}", o, False, W["od_w_out"], j, xcur, mod, l, KG1, to_ffn)
        s.update(y1=y, x_mid=xmid)
        gate, up, act = _ffn_up(f"ffn_up_{l}", h2, W["ffn_w_gate"], W["ffn_w_up"], l)
        to_mix = (gain_mix, l + 1, l + 1, KSC1, KSH1) if l + 1 < DEPTH else None
        y2, xcur, *h_next = _out_proj(f"ffn_down_{l}", act, True, W["ffn_w_down"], l, xmid, mod, l, KG2, to_mix)
        h = h_next[0] if h_next else None
        s.update(h2=h2, gate=gate, up=up, act=act, y2=y2)
        saved.append(s)

    dy, lacc = _loss_head(xcur, target)

    dmod_rows = [None] * DEPTH
    d_mix = [None] * DEPTH
    d_ffn = [None] * DEPTH
    d_conv = [None] * 2
    d_ret = [None] * 2
    d_gq = [None] * 2
    d_gk = [None] * 2
    dx = dy
    reduce_up = None
    dyg, st_top = _gate_bwd("gate2_bwd_top", dx, saved[DEPTH - 1]["y2"], mod, DEPTH - 1, KG2)
    d_gate2 = st_top[0]
    for l in reversed(range(DEPTH)):
        j = l // 2
        s = saved[l]
        G["ffn_w_down"] = _wgrad(f"wg_down_{l}", s["act"], "stack", dyg, "full", G["ffn_w_down"], l)
        dgate, dup = _ffn_down_bwd(f"ffn_down_bwd_{l}", dyg, W["ffn_w_down"], l, s["gate"], s["up"])
        G["ffn_w_gate"] = _wgrad(f"wg_gate_{l}", s["h2"], "full", dgate, "stack", G["ffn_w_gate"], l)
        G["ffn_w_up"] = _wgrad(f"wg_up_{l}", s["h2"], "full", dup, "stack", G["ffn_w_up"], l)
        dh2 = _ffn_up_bwd(f"ffn_up_bwd_{l}", dgate, dup, W["ffn_w_gate"], W["ffn_w_up"], l)
        dxm, st_n2, dyg1 = _normmod_bwd(f"norm_ffn_bwd_{l}", s["x_mid"], dh2, dx, gain_ffn, l, mod, KSC2,
                                        gate=(s["y1"], l, KG1))
        if l % 2 == 0:
            G["ev_w_out"] = _wgrad(f"wg_evout_{l}", s["cat"], "cols", dyg1, "full", G["ev_w_out"], j)
            dcat = _bwd_rows(f"ev_out_bwd_{l}", dyg1, W["ev_w_out"], j)
            db, dcg, du, dwc = _conv_bwd(f"conv_bwd_{l}", s["proj"], conv_full[j], dcat)
            dq, dk, dv, dg, dgr = _retention_bwd(f"ret_bwd_{l}", s["proj"], s["oraw"], s["states"], dcat, cosf, sinf,
                                                 ev_ret_norm_g[j].reshape(1, RET_DIM), lgt)
            dproj = jnp.concatenate([db, dcg, du, dq, dk, dv, dg], axis=1)
            d_conv[j], d_ret[j] = dwc[:CONV_WIDTH], dgr[0]
            G["ev_w_in"] = _wgrad(f"wg_evin_{l}", s["h"], "full", dproj, "cols", G["ev_w_in"], j)
            dh = _bwd_cols(f"ev_in_bwd_{l}", dproj, W["ev_w_in"], j)
        else:
            G["od_w_out"] = _wgrad(f"wg_odout_{l}", s["cat"], "cols", dyg1, "full", G["od_w_out"], j)
            dcat = _bwd_rows(f"od_out_bwd_{l}", dyg1, W["od_w_out"], j)
            scatter = None
            if fused is not None and l == 1:
                g_now = [G[k] for k in fused["names"]]
                recv_up = _exchange_pair(g_now, 1)
                ps_up = [_pair_sum(f"pair_sum_up_{k}", g, r, 1) for k, g, r in zip(fused["names"], g_now, recv_up)]
                scatter = (ps_up, 1)
            dq, dk, dv, dgq, dgk, slots_up = _sb_bwd(f"sb_bwd_{l}", s["qkv"], od_q_norm_g[j].reshape(1, HEAD),
                                                     od_k_norm_g[j].reshape(1, HEAD), s["o32"], dcat, scatter)
            if scatter:
                reduce_up = (ps_up, slots_up)
            dproj = jnp.concatenate([dq, dk, dv], axis=1)
            d_gq[j], d_gk[j] = dgq[0], dgk[0]
            G["od_w_qkv"] = _wgrad(f"wg_odin_{l}", s["h"], "full", dproj, "cols", G["od_w_qkv"], j)
            dh = _bwd_cols(f"od_in_bwd_{l}", dproj, W["od_w_qkv"], j)
        below = (saved[l - 1]["y2"], l - 1, KG2) if l > 0 else None
        dx, st_n1, *dyg_below = _normmod_bwd(f"norm_mix_bwd_{l}", s["x_in"], dh, dxm, gain_mix, l, mod, KSC1, gate=below)
        dmod_rows[l] = jnp.stack([st_n1[0], st_n1[1], st_n2[3], st_n2[0], st_n2[1], d_gate2]).reshape(6 * D_MODEL)
        d_mix[l], d_ffn[l] = st_n1[2], st_n2[2]
        if below:
            dyg, d_gate2 = dyg_below[0], st_n1[3]
    return lacc, dx, G, (dmod_rows, d_mix, d_ffn, d_ret, d_gq, d_gk, d_conv), reduce_up


def kernel(x, c, ada_w, ada_b, norm_mix_g, norm_ffn_g, ev_w_in, ev_conv_w, ev_ret_norm_g, ev_w_out, od_w_qkv, od_q_norm_g, od_k_norm_g, od_w_out, ffn_w_gate, ffn_w_up, ffn_w_down, loss_target, m_ada_w, m_ada_b, m_norm_mix_g, m_norm_ffn_g, m_ev_w_in, m_ev_conv_w, m_ev_ret_norm_g, m_ev_w_out, m_od_w_qkv, m_od_q_norm_g, m_od_k_norm_g, m_od_w_out, m_ffn_w_gate, m_ffn_w_up, m_ffn_w_down, v_ada_w, v_ada_b, v_norm_mix_g, v_norm_ffn_g, v_ev_w_in, v_ev_conv_w, v_ev_ret_norm_g, v_ev_w_out, v_od_w_qkv, v_od_q_norm_g, v_od_k_norm_g, v_od_w_out, v_ffn_w_gate, v_ffn_w_up, v_ffn_w_down):
    xi, yi, ci = _position()
    chip = 2 * xi + yi
    dev = 4 * xi + 2 * yi + ci
    x0 = x[0]
    target = loss_target[0]

    n_small = D_MODEL + 2 * CONV_WIDTH * LANE
    small = jnp.concatenate([c.reshape(1, D_MODEL), ev_conv_w.reshape(1, 2 * CONV_WIDTH * LANE)], axis=1)
    small = jnp.broadcast_to(small, (8, n_small))
    g1 = _gather8("gather_cond", small).reshape(N_DEV, 8, n_small)[:, 0, :]
    c_all = g1[:, :D_MODEL]
    conv_all = g1[0::2, D_MODEL:].reshape(N_CHIP, 2, CONV_WIDTH, LANE)
    conv_full = conv_all.transpose(1, 2, 0, 3).reshape(2, CONV_WIDTH, CONV_DIM)

    n_ada = ada_w.shape[-1]
    ada_b_cols = lax.dynamic_slice_in_dim(ada_b, chip * n_ada, n_ada, axis=1).reshape(DEPTH, 1, n_ada)
    mod_cols = _ada_fwd(c_all, ada_w, ada_b_cols)
    g2 = _gather8("gather_mod", mod_cols.reshape(DEPTH * N_DEV, n_ada)).reshape(N_DEV, DEPTH, N_DEV, n_ada)
    mod_mine = lax.dynamic_index_in_dim(g2[0::2], dev, axis=2, keepdims=False)
    mod = mod_mine.transpose(1, 0, 2).reshape(DEPTH, 6, 1, D_MODEL)

    big_names = ["ev_w_in", "ev_w_out", "od_w_qkv", "od_w_out", "ffn_w_gate", "ffn_w_up", "ffn_w_down"]
    big = dict(ev_w_in=ev_w_in, ev_w_out=ev_w_out, od_w_qkv=od_w_qkv, od_w_out=od_w_out, ffn_w_gate=ffn_w_gate,
               ffn_w_up=ffn_w_up, ffn_w_down=ffn_w_down)
    pos_idx = jnp.stack([chip, ci]).astype(jnp.int32)
    shards = [big[k].astype(BF16) for k in big_names]
    W = {k: _place_own(f"place_{k}", w4, s, pos_idx) for k, w4, s in zip(big_names, _gather_weights(shards, 0), shards)}
    G = {k: lax.empty((N_CHIP,) + big[k].shape, F32) for k in big_names}

    lacc, dx, G, small_grads, (ps_up, slots_up) = _local_step(
        x0, target, mod, W, G, norm_mix_g, norm_ffn_g, conv_full, ev_ret_norm_g, od_q_norm_g, od_k_norm_g,
        fused=dict(names=big_names, shards=shards))
    loss = lax.psum(lacc[0, 0], ("x", "y", "c"))
    grad_x = dx[None]
    dmod_rows, d_mix, d_ffn, d_ret, d_gq, d_gk, d_conv = small_grads

    totals = [_chip_sum(f"chip_sum_up_{k}", r, p, pos_idx, 1, lax.empty(big[k].shape, F32))
              for k, r, p in zip(big_names, slots_up, ps_up)]
    glist = [G[k] for k in big_names]
    recv_lo = _exchange_pair(glist, 0)
    ps_lo = [_pair_sum(f"pair_sum_lo_{k}", g, r, 0) for k, g, r in zip(big_names, glist, recv_lo)]
    slots_lo = _scatter_chips(ps_lo, 0)
    totals = [_chip_sum(f"chip_sum_lo_{k}", r, p, pos_idx, 0, t) for k, r, p, t in zip(big_names, slots_lo, ps_lo, totals)]
    grads = dict(zip(big_names, _share_halves(totals)))

    pieces = [jnp.stack(dmod_rows).reshape(-1), jnp.stack(d_mix).reshape(-1), jnp.stack(d_ffn).reshape(-1),
              jnp.stack(d_ret).reshape(-1), jnp.stack(d_gq).reshape(-1), jnp.stack(d_gk).reshape(-1),
              jnp.stack(d_conv).reshape(-1)]
    sizes = [int(p.shape[0]) for p in pieces]
    n_pack = sum(sizes)
    n_cols = -(-n_pack // (8 * LANE)) * LANE
    packed = jnp.concatenate(pieces + [jnp.zeros((8 * n_cols - n_pack,), F32)]).reshape(8, n_cols)
    g3 = _gather8("gather_small", packed).reshape(N_DEV, 8, n_cols)
    tot = _sum_devices(g3).reshape(-1)
    offs = np.cumsum([0] + sizes)
    part = [tot[offs[i]:offs[i + 1]] for i in range(len(sizes))]
    grads["ada_b"] = part[0].reshape(DEPTH, 6 * D_MODEL)
    grads["norm_mix_g"] = part[1].reshape(DEPTH, D_MODEL)
    grads["norm_ffn_g"] = part[2].reshape(DEPTH, D_MODEL)
    grads["ev_ret_norm_g"] = part[3].reshape(2, RET_DIM)
    grads["od_q_norm_g"] = part[4].reshape(2, HEAD)
    grads["od_k_norm_g"] = part[5].reshape(2, HEAD)
    conv_g = part[6].reshape(2, CONV_WIDTH, CONV_DIM)
    grads["ev_conv_w"] = lax.dynamic_slice_in_dim(conv_g, chip * LANE, LANE, axis=2)
    dmod_all = g3.reshape(N_DEV, -1)[:, :DEPTH * 6 * D_MODEL].reshape(N_DEV, DEPTH, 6 * D_MODEL)
    dmod_cols = lax.dynamic_slice_in_dim(dmod_all, chip * n_ada, n_ada, axis=2).transpose(1, 0, 2)
    grads["ada_w"] = _ada_wgrad(c_all.T, dmod_cols)

    weights = dict(ada_w=ada_w, ada_b=ada_b, norm_mix_g=norm_mix_g, norm_ffn_g=norm_ffn_g, ev_w_in=ev_w_in,
                   ev_conv_w=ev_conv_w, ev_ret_norm_g=ev_ret_norm_g, ev_w_out=ev_w_out, od_w_qkv=od_w_qkv,
                   od_q_norm_g=od_q_norm_g, od_k_norm_g=od_k_norm_g, od_w_out=od_w_out, ffn_w_gate=ffn_w_gate,
                   ffn_w_up=ffn_w_up, ffn_w_down=ffn_w_down)
    m_in = dict(ada_w=m_ada_w, ada_b=m_ada_b, norm_mix_g=m_norm_mix_g, norm_ffn_g=m_norm_ffn_g, ev_w_in=m_ev_w_in,
                ev_conv_w=m_ev_conv_w, ev_ret_norm_g=m_ev_ret_norm_g, ev_w_out=m_ev_w_out, od_w_qkv=m_od_w_qkv,
                od_q_norm_g=m_od_q_norm_g, od_k_norm_g=m_od_k_norm_g, od_w_out=m_od_w_out, ffn_w_gate=m_ffn_w_gate,
                ffn_w_up=m_ffn_w_up, ffn_w_down=m_ffn_w_down)
    v_in = dict(ada_w=v_ada_w, ada_b=v_ada_b, norm_mix_g=v_norm_mix_g, norm_ffn_g=v_norm_ffn_g, ev_w_in=v_ev_w_in,
                ev_conv_w=v_ev_conv_w, ev_ret_norm_g=v_ev_ret_norm_g, ev_w_out=v_ev_w_out, od_w_qkv=v_od_w_qkv,
                od_q_norm_g=v_od_q_norm_g, od_k_norm_g=v_od_k_norm_g, od_w_out=v_od_w_out, ffn_w_gate=v_ffn_w_gate,
                ffn_w_up=v_ffn_w_up, ffn_w_down=v_ffn_w_down)
    order = list(weights)
    deltas, new_m, new_v = {}, {}, {}
    for k in order:
        deltas[k], new_m[k], new_v[k] = _adamw(f"adamw_{k}", weights[k], grads[k], m_in[k], v_in[k])
    return (loss, grad_x, *[grads[k] for k in order], *[deltas[k] for k in order], *[new_m[k] for k in order],
            *[new_v[k] for k in order])
```

```python
import functools

import numpy as np
import jax
import jax.numpy as jnp
from jax import lax
from jax.experimental import pallas as pl
from jax.experimental.pallas import tpu as pltpu

F32 = jnp.float32
BF16 = jnp.bfloat16
MESH = pl.DeviceIdType.MESH

D_MODEL = 1024
DEPTH = 4
N_CHIP = 4
N_DEV = 8
HEAD = 128
RET_HEADS = 4
SB_HEADS = 8
CONV_DIM = 512
RET_DIM = 512
CONV_WIDTH = 3
RET_CHUNK = 64
RET_BLOCK = 256
SB_BLOCK = 256
SB_QROWS = 512
EPS = 1e-6
ROPE_THETA = 10000.0
QK_SCALE = HEAD ** -0.5
LANE = 128
ROW_CHUNK = 512
ROWS_STREAMED = 1024
VMEM_LIMIT_V7X = 56 * 1024 * 1024

ADAM_LR, ADAM_B1, ADAM_B2, ADAM_EPS, ADAM_WD, ADAM_STEP = 0.001, 0.9, 0.999, 1e-08, 0.01, 10


def _cparams(*sem):
    return pltpu.CompilerParams(dimension_semantics=sem or None, vmem_limit_bytes=VMEM_LIMIT_V7X)


def _tile(n, pref):
    if n <= pref:
        return n
    for t in range(pref - pref % 8, 7, -8):
        if n % t == 0:
            return t
    return n


def _sig(v):
    return 1.0 / (1.0 + jnp.exp(-v))


def _dot(a, b, ca, cb):
    return lax.dot_general(a.astype(BF16), b.astype(BF16), (((ca,), (cb,)), ((), ())),
                           preferred_element_type=F32)


def _position():
    x, y, c = lax.axis_index("x"), lax.axis_index("y"), lax.axis_index("c")
    return x, y, c


def _other_chips(x, y):
    return [(1 - x, y), (x, 1 - y), (1 - x, 1 - y)]


def _mm(name, pairs, out_sds, out_specs, grid, contract, red_axis=None, post=None,
        extras=(), extra_specs=(), sum_pairs=True, aliases=None, inner=None):
    n_p, n_ex, n_out = len(pairs), len(extras), len(out_sds)
    n_acc = 1 if sum_pairs else n_p
    n_red = grid[red_axis] if red_axis is not None else 1

    def default_post(accs, ex, outs):
        outs[0][...] = accs[0].astype(outs[0].dtype)

    post_fn = post or default_post

    def kern(*refs):
        ab = refs[:2 * n_p]
        ex = refs[2 * n_p:2 * n_p + n_ex]
        outs = refs[2 * n_p + n_ex:2 * n_p + n_ex + n_out]
        accs = refs[2 * n_p + n_ex + n_out:]
        if inner is None:
            prods = [_dot(ab[2 * p][...], ab[2 * p + 1][...], contract[0], contract[1]) for p in range(n_p)]
        else:
            n_in, a_get, b_get = inner
            prods = []
            for p in range(n_p):
                tot = _dot(a_get(ab[2 * p], 0), b_get(ab[2 * p + 1], 0), contract[0], contract[1])
                for i in range(1, n_in):
                    tot = tot + _dot(a_get(ab[2 * p], i), b_get(ab[2 * p + 1], i), contract[0], contract[1])
                prods.append(tot)
        if sum_pairs:
            tot = prods[0]
            for p_ in prods[1:]:
                tot = tot + p_
            prods = [tot]
        if red_axis is None:
            post_fn(prods, ex, outs)
        else:
            k = pl.program_id(red_axis)

            @pl.when(k == 0)
            def _():
                for a_, p_ in zip(accs, prods):
                    a_[...] = p_

            @pl.when(k > 0)
            def _():
                for a_, p_ in zip(accs, prods):
                    a_[...] += p_

            @pl.when(k == n_red - 1)
            def _():
                post_fn([a_[...] for a_ in accs], ex, outs)

    ins, in_specs = [], []
    for a, b, sa, sb in pairs:
        ins += [a, b]
        in_specs += [sa, sb]
    ins += list(extras)
    in_specs += list(extra_specs)
    scratch = []
    if red_axis is not None:
        scratch = [pltpu.VMEM(tuple(acc_shape), F32) for acc_shape in [_acc_shape(pairs[0], contract)] * n_acc]
    sem = tuple("arbitrary" if ax == red_axis else "parallel" for ax in range(len(grid)))
    res = pl.pallas_call(kern, name=name, grid=grid, in_specs=in_specs, out_specs=list(out_specs),
                         out_shape=list(out_sds), scratch_shapes=scratch,
                         input_output_aliases=aliases or {}, compiler_params=_cparams(*sem))(*ins)
    return res


def _acc_shape(pair, contract):
    sa, sb = pair[2], pair[3]
    da = [d for d in sa.block_shape if d is not None]
    db = [d for d in sb.block_shape if d is not None]
    return (da[1 - contract[0]], db[1 - contract[1]])


def _sds(shape, dtype):
    return jax.ShapeDtypeStruct(tuple(shape), dtype)


def _proj_cols(name, h, w4, l):
    T, K = h.shape
    n = w4.shape[-1]
    tm = _tile(T, ROWS_STREAMED)
    return _mm(name, [(h, w4, pl.BlockSpec((tm, K), lambda i, m: (m, 0)),
                       pl.BlockSpec((None, None, K, n), lambda i, m: (i, l, 0, 0)))],
               [_sds((T, N_CHIP * n), F32)], [pl.BlockSpec((tm, n), lambda i, m: (m, i))],
               (N_CHIP, T // tm), (1, 0))[0]


def _out_proj(name, a, a_stacked, w4, l, xres, mod, lm, kg, next_norm=None):
    T = xres.shape[0]
    k = w4.shape[-2]
    tm = _tile(T, 512)
    if a_stacked:
        sa = pl.BlockSpec((N_CHIP, tm, k), lambda m: (0, m, 0))
        a_get = lambda ref, i: ref[i]
    else:
        sa = pl.BlockSpec((tm, N_CHIP * k), lambda m: (m, 0))
        a_get = lambda ref, i: ref[:, i * k:(i + 1) * k]

    def post(accs, ex, outs):
        y = accs[0]
        outs[0][...] = y
        xn = ex[0][...] + ex[1][...] * y
        outs[1][...] = xn
        if next_norm:
            r = lax.rsqrt(jnp.mean(xn * xn, axis=-1, keepdims=True) + EPS)
            outs[2][...] = ((xn * r) * ex[2][...] * (1.0 + ex[3][...]) + ex[4][...]).astype(BF16)

    row = pl.BlockSpec((tm, D_MODEL), lambda m: (m, 0))
    extras, extra_specs = [xres, mod], [row, _vec_spec(lm, kg)]
    out_sds, out_specs = [_sds((T, D_MODEL), F32)] * 2, [row, row]
    if next_norm:
        gain3, lg, ln, ksc, ksh = next_norm
        extras += [gain3, mod, mod]
        extra_specs += [_vec_spec(lg), _vec_spec(ln, ksc), _vec_spec(ln, ksh)]
        out_sds.append(_sds((T, D_MODEL), BF16))
        out_specs.append(row)
    return _mm(name, [(a, w4, sa, pl.BlockSpec((N_CHIP, None, k, D_MODEL), lambda m: (0, l, 0, 0)))],
               out_sds, out_specs, (T // tm,), (1, 0), post=post, extras=extras, extra_specs=extra_specs,
               inner=(N_CHIP, a_get, lambda ref, i: ref[i]))


def _ffn_up(name, h2, wg4, wu4, l):
    T, K = h2.shape
    n = wg4.shape[-1]
    tm = _tile(T, ROWS_STREAMED)

    def post(accs, ex, outs):
        g, u = accs
        outs[0][...] = g
        outs[1][...] = u
        outs[2][...] = (g * _sig(g) * u).astype(BF16)

    sa = pl.BlockSpec((tm, K), lambda i, m: (m, 0))
    sw = pl.BlockSpec((None, None, K, n), lambda i, m: (i, l, 0, 0))
    so = pl.BlockSpec((None, tm, n), lambda i, m: (i, m, 0))
    return _mm(name, [(h2, wg4, sa, sw), (h2, wu4, sa, sw)],
               [_sds((N_CHIP, T, n), F32), _sds((N_CHIP, T, n), F32), _sds((N_CHIP, T, n), BF16)], [so, so, so],
               (N_CHIP, T // tm), (1, 0), post=post, sum_pairs=False)


def _bwd_cols(name, dproj, w4, l):
    T = dproj.shape[0]
    K, n = w4.shape[-2:]
    tm = _tile(T, 512)
    return _mm(name, [(dproj, w4, pl.BlockSpec((tm, N_CHIP * n), lambda m: (m, 0)),
                       pl.BlockSpec((N_CHIP, None, K, n), lambda m: (0, l, 0, 0)))],
               [_sds((T, K), F32)], [pl.BlockSpec((tm, K), lambda m: (m, 0))], (T // tm,), (1, 1),
               inner=(N_CHIP, lambda ref, i: ref[:, i * n:(i + 1) * n], lambda ref, i: ref[i]))[0]


def _bwd_rows(name, dy, w4, l):
    T, N = dy.shape
    k = w4.shape[-2]
    tm = _tile(T, ROWS_STREAMED)
    return _mm(name, [(dy, w4, pl.BlockSpec((tm, N), lambda i, m: (m, 0)),
                       pl.BlockSpec((None, None, k, N), lambda i, m: (i, l, 0, 0)))],
               [_sds((T, N_CHIP * k), F32)], [pl.BlockSpec((tm, k), lambda i, m: (m, i))],
               (N_CHIP, T // tm), (1, 1))[0]


def _ffn_down_bwd(name, dy, wd4, l, gate, up):
    T, N = dy.shape
    k = wd4.shape[-2]
    tm = _tile(T, ROWS_STREAMED)

    def post(accs, ex, outs):
        da = accs[0]
        g = ex[0][...]
        u = ex[1][...]
        sg = _sig(g)
        outs[0][...] = (da * u * (sg * (1.0 + g * (1.0 - sg)))).astype(BF16)
        outs[1][...] = (da * (g * sg)).astype(BF16)

    so = pl.BlockSpec((None, tm, k), lambda i, m: (i, m, 0))
    return _mm(name, [(dy, wd4, pl.BlockSpec((tm, N), lambda i, m: (m, 0)),
                       pl.BlockSpec((None, None, k, N), lambda i, m: (i, l, 0, 0)))],
               [_sds((N_CHIP, T, k), BF16)] * 2, [so, so], (N_CHIP, T // tm), (1, 1), post=post,
               extras=[gate, up], extra_specs=[so, so])


def _ffn_up_bwd(name, dgate, dup, wg4, wu4, l):
    _, T, n = dgate.shape
    K = wg4.shape[-2]
    tm = _tile(T, 512)
    sa = pl.BlockSpec((N_CHIP, tm, n), lambda m: (0, m, 0))
    sw = pl.BlockSpec((N_CHIP, None, K, n), lambda m: (0, l, 0, 0))
    pick = lambda ref, i: ref[i]
    return _mm(name, [(dgate, wg4, sa, sw), (dup, wu4, sa, sw)],
               [_sds((T, K), F32)], [pl.BlockSpec((tm, K), lambda m: (m, 0))], (T // tm,), (1, 1),
               inner=(N_CHIP, pick, pick))[0]


def _wgrad(name, a, a_kind, b, b_kind, gbuf, l):
    r, cdim = gbuf.shape[-2:]
    T = a.shape[-2]
    tt = _tile(T, ROWS_STREAMED)

    def spec(kind, w):
        if kind == "full":
            return pl.BlockSpec((tt, w), lambda i, t: (t, 0))
        if kind == "cols":
            return pl.BlockSpec((tt, w), lambda i, t: (t, i))
        return pl.BlockSpec((None, tt, w), lambda i, t: (i, t, 0))

    def post(accs, ex, outs):
        outs[0][...] = accs[0]

    return _mm(name, [(a, b, spec(a_kind, r), spec(b_kind, cdim))], [_sds(gbuf.shape, F32)],
               [pl.BlockSpec((None, None, r, cdim), lambda i, t: (i, l, 0, 0))], (N_CHIP, T // tt), (0, 0),
               red_axis=1, post=post, extras=[gbuf], extra_specs=[pl.BlockSpec(memory_space=pl.ANY)],
               aliases={2: 0})[0]


def _vec_spec(*idx):
    return pl.BlockSpec((None,) * len(idx) + (1, D_MODEL), lambda m: tuple(idx) + (0, 0))


def _normmod(name, x, gain3, l, mod, ksc, ksh):
    T = x.shape[0]
    tm = _tile(T, 512)

    def kern(x_ref, g_ref, sc_ref, sh_ref, h_ref):
        xv = x_ref[...]
        r = lax.rsqrt(jnp.mean(xv * xv, axis=-1, keepdims=True) + EPS)
        h = (xv * r) * g_ref[...]
        h_ref[...] = (h * (1.0 + sc_ref[...]) + sh_ref[...]).astype(BF16)

    row = pl.BlockSpec((tm, D_MODEL), lambda m: (m, 0))
    return pl.pallas_call(kern, name=name, grid=(T // tm,),
                          in_specs=[row, _vec_spec(l), _vec_spec(l, ksc), _vec_spec(l, ksh)], out_specs=row,
                          out_shape=_sds((T, D_MODEL), BF16), compiler_params=_cparams("parallel"))(x, gain3, mod, mod)


def _normmod_bwd(name, x, dh, dres, gain3, l, mod, ksc, gate=None):
    T = x.shape[0]
    tm = _tile(T, 512)
    nt = T // tm

    def kern(x_ref, dh_ref, dres_ref, g_ref, sc_ref, *rest):
        if gate:
            y_ref, gv_ref, dx_ref, st_ref, dyg_ref = rest
        else:
            dx_ref, st_ref = rest
        m = pl.program_id(0)

        @pl.when(m == 0)
        def _():
            st_ref[...] = jnp.zeros_like(st_ref)

        xv = x_ref[...]
        dhv = dh_ref[...]
        r = lax.rsqrt(jnp.mean(xv * xv, axis=-1, keepdims=True) + EPS)
        xh = xv * r
        wv = g_ref[...] * (1.0 + sc_ref[...])
        dxh = dhv * wv
        dxv = dres_ref[...] + r * (dxh - xh * jnp.mean(dxh * xh, axis=-1, keepdims=True))
        dx_ref[...] = dxv
        st_ref[0:1, :] += jnp.sum(dhv, axis=0, keepdims=True)
        st_ref[1:2, :] += jnp.sum(dhv * xh, axis=0, keepdims=True)
        if gate:
            dyg_ref[...] = (dxv * gv_ref[...]).astype(BF16)
            st_ref[3:4, :] += jnp.sum(dxv * y_ref[...], axis=0, keepdims=True)

        @pl.when(m == nt - 1)
        def _():
            dw = st_ref[1:2, :]
            st_ref[2:3, :] = dw * (1.0 + sc_ref[...])
            st_ref[1:2, :] = dw * g_ref[...]

    row = pl.BlockSpec((tm, D_MODEL), lambda m: (m, 0))
    stat = pl.BlockSpec((8, D_MODEL), lambda m: (0, 0))
    ins, in_specs = [x, dh, dres, gain3, mod], [row, row, row, _vec_spec(l), _vec_spec(l, ksc)]
    out_specs, out_shape = [row, stat], [_sds((T, D_MODEL), F32), _sds((8, D_MODEL), F32)]
    if gate:
        ins += [gate[0], mod]
        in_specs += [row, _vec_spec(gate[1], gate[2])]
        out_specs.append(row)
        out_shape.append(_sds((T, D_MODEL), BF16))
    return pl.pallas_call(kern, name=name, grid=(nt,), in_specs=in_specs, out_specs=out_specs, out_shape=out_shape,
                          compiler_params=_cparams("arbitrary"))(*ins)


def _gate_bwd(name, dxn, y, mod, l, kg):
    T = dxn.shape[0]
    tm = _tile(T, 512)

    def kern(d_ref, y_ref, g_ref, dy_ref, st_ref):
        @pl.when(pl.program_id(0) == 0)
        def _():
            st_ref[...] = jnp.zeros_like(st_ref)

        dv = d_ref[...]
        dy_ref[...] = (dv * g_ref[...]).astype(BF16)
        st_ref[0:1, :] += jnp.sum(dv * y_ref[...], axis=0, keepdims=True)

    row = pl.BlockSpec((tm, D_MODEL), lambda m: (m, 0))
    return pl.pallas_call(kern, name=name, grid=(T // tm,), in_specs=[row, row, _vec_spec(l, kg)],
                          out_specs=[row, pl.BlockSpec((8, D_MODEL), lambda m: (0, 0))],
                          out_shape=[_sds((T, D_MODEL), BF16), _sds((8, D_MODEL), F32)],
                          compiler_params=_cparams("arbitrary"))(dxn, y, mod)


def _loss_head(y, target):
    T = y.shape[0]
    tm = _tile(T, 512)

    def kern(y_ref, t_ref, dy_ref, acc_ref):
        @pl.when(pl.program_id(0) == 0)
        def _():
            acc_ref[...] = jnp.zeros_like(acc_ref)

        e = y_ref[...] - t_ref[...]
        dy_ref[...] = e * (1.0 / D_MODEL)
        s = jnp.sum(jnp.sum(e * e, axis=-1, keepdims=True), axis=0, keepdims=True)
        acc_ref[...] += s * (0.5 / D_MODEL)

    row = pl.BlockSpec((tm, D_MODEL), lambda m: (m, 0))
    return pl.pallas_call(kern, name="loss_head", grid=(T // tm,), in_specs=[row, row],
                          out_specs=[row, pl.BlockSpec((8, LANE), lambda m: (0, 0))],
                          out_shape=[_sds((T, D_MODEL), F32), _sds((8, LANE), F32)],
                          compiler_params=_cparams("arbitrary"))(y, target)


def _row_chunks(T):
    rc = min(ROW_CHUNK, T)
    return [(r * rc, rc) for r in range(T // rc)]


def _conv_fwd(name, proj, conv_w):
    T = proj.shape[0]
    nblk = CONV_DIM // LANE

    def kern(b_ref, c_ref, u_ref, w_ref, a_ref, zs):
        zs[0:8, :] = jnp.zeros((8, LANE), F32)
        for r0, rc in _row_chunks(T):
            zs[8 + r0:8 + r0 + rc, :] = c_ref[r0:r0 + rc, :] * u_ref[r0:r0 + rc, :]
        w0, w1, w2 = w_ref[0:1, :], w_ref[1:2, :], w_ref[2:3, :]
        for r0, rc in _row_chunks(T):
            yc = w2 * zs[8 + r0:8 + r0 + rc, :] + w1 * zs[7 + r0:7 + r0 + rc, :] + w0 * zs[6 + r0:6 + r0 + rc, :]
            a_ref[r0:r0 + rc, :] = (b_ref[r0:r0 + rc, :] * yc).astype(BF16)

    col = lambda p: pl.BlockSpec((T, LANE), lambda cb: (0, p * nblk + cb))
    return pl.pallas_call(kern, name=name, grid=(nblk,),
                          in_specs=[col(0), col(1), col(2), pl.BlockSpec((CONV_WIDTH, LANE), lambda cb: (0, cb))],
                          out_specs=pl.BlockSpec((T, LANE), lambda cb: (0, cb)),
                          out_shape=_sds((T, CONV_DIM), BF16), scratch_shapes=[pltpu.VMEM((T + 8, LANE), F32)],
                          compiler_params=_cparams("parallel"))(proj, proj, proj, conv_w)


def _conv_bwd(name, proj, conv_w, dcat):
    T = proj.shape[0]
    nblk = CONV_DIM // LANE

    def kern(b_ref, c_ref, u_ref, w_ref, da_ref, db_ref, dc_ref, du_ref, dw_ref, zs, ds):
        zs[0:8, :] = jnp.zeros((8, LANE), F32)
        ds[T:T + 8, :] = jnp.zeros((8, LANE), F32)
        for r0, rc in _row_chunks(T):
            zs[8 + r0:8 + r0 + rc, :] = c_ref[r0:r0 + rc, :] * u_ref[r0:r0 + rc, :]
        w0, w1, w2 = w_ref[0:1, :], w_ref[1:2, :], w_ref[2:3, :]
        acc = [jnp.zeros((1, LANE), F32) for _ in range(3)]
        for r0, rc in _row_chunks(T):
            z0 = zs[8 + r0:8 + r0 + rc, :]
            z1 = zs[7 + r0:7 + r0 + rc, :]
            z2 = zs[6 + r0:6 + r0 + rc, :]
            da = da_ref[r0:r0 + rc, :]
            db_ref[r0:r0 + rc, :] = (da * (w2 * z0 + w1 * z1 + w0 * z2)).astype(BF16)
            dyc = da * b_ref[r0:r0 + rc, :]
            ds[r0:r0 + rc, :] = dyc
            acc[2] = acc[2] + jnp.sum(dyc * z0, axis=0, keepdims=True)
            acc[1] = acc[1] + jnp.sum(dyc * z1, axis=0, keepdims=True)
            acc[0] = acc[0] + jnp.sum(dyc * z2, axis=0, keepdims=True)
        dw_ref[...] = jnp.zeros_like(dw_ref)
        for k in range(3):
            dw_ref[k:k + 1, :] = acc[k]
        for r0, rc in _row_chunks(T):
            dz = w2 * ds[r0:r0 + rc, :] + w1 * ds[r0 + 1:r0 + 1 + rc, :] + w0 * ds[r0 + 2:r0 + 2 + rc, :]
            dc_ref[r0:r0 + rc, :] = (dz * u_ref[r0:r0 + rc, :]).astype(BF16)
            du_ref[r0:r0 + rc, :] = (dz * c_ref[r0:r0 + rc, :]).astype(BF16)

    col = lambda p: pl.BlockSpec((T, LANE), lambda cb: (0, p * nblk + cb))
    out = pl.BlockSpec((T, LANE), lambda cb: (0, cb))
    return pl.pallas_call(kern, name=name, grid=(nblk,),
                          in_specs=[col(0), col(1), col(2), pl.BlockSpec((CONV_WIDTH, LANE), lambda cb: (0, cb)), out],
                          out_specs=[out, out, out, pl.BlockSpec((8, LANE), lambda cb: (0, cb))],
                          out_shape=[_sds((T, CONV_DIM), BF16)] * 3 + [_sds((8, CONV_DIM), F32)],
                          scratch_shapes=[pltpu.VMEM((T + 8, LANE), F32), pltpu.VMEM((T + 8, LANE), F32)],
                          compiler_params=_cparams("parallel"))(proj, proj, proj, conv_w, dcat)


_Q0, _K0, _V0, _G0 = 3 * CONV_DIM // LANE, (3 * CONV_DIM + RET_DIM) // LANE, (3 * CONV_DIM + 2 * RET_DIM) // LANE, \
    (3 * CONV_DIM + 3 * RET_DIM) // LANE


def _ret_tables(B, lg1):
    ti = lax.broadcasted_iota(jnp.int32, (B, B), 0)
    si = lax.broadcasted_iota(jnp.int32, (B, B), 1)
    dist = jnp.abs(ti - si).astype(F32)
    shift = RET_CHUNK.bit_length() - 1
    dmat = jnp.where((si >> shift) <= (ti >> shift), jnp.exp(dist * lg1), 0.0)
    tcol = lax.broadcasted_iota(jnp.int32, (B, 1), 0).astype(F32)
    qdec = jnp.exp((tcol + 1.0) * lg1)
    kdec = jnp.exp((B - 1.0 - tcol) * lg1)
    bdec = jnp.exp(float(B) * lg1)
    return dmat, qdec, kdec, bdec


def _retention_fwd(name, proj, cosf, sinf, gr, lgt):
    T = proj.shape[0]
    B = min(RET_BLOCK, T)
    nb = T // B

    def kern(q_ref, k_ref, v_ref, g_ref, cos_ref, sin_ref, gr_ref, lg_ref, r_ref, o_ref, st_ref, S):
        @pl.when(pl.program_id(1) == 0)
        def _():
            S[...] = jnp.zeros_like(S)

        cosv, sinv = cos_ref[...], sin_ref[...]
        rot = lambda a: a * cosv + pltpu.roll(a, HEAD // 2, 1) * sinv
        qr = rot(q_ref[...])
        kr = rot(k_ref[...]) * QK_SCALE
        v = v_ref[...]
        dmat, qdec, kdec, bdec = _ret_tables(B, lg_ref[0:1, 0:1])
        sv = S[...]
        st_ref[...] = sv
        pd = _dot(qr, kr, 1, 1) * dmat
        o = _dot(pd, v, 1, 0) + _dot(qr * qdec, sv, 1, 0)
        S[...] = bdec * sv + _dot(kr * kdec, v, 0, 0)
        o_ref[...] = o
        rs = lax.rsqrt(jnp.mean(o * o, axis=-1, keepdims=True) + EPS)
        g = g_ref[...]
        r_ref[...] = (g * _sig(g) * (o * rs * gr_ref[...])).astype(BF16)

    col = lambda c0: pl.BlockSpec((B, HEAD), lambda h, i: (i, c0 + h))
    tab = pl.BlockSpec((B, HEAD), lambda h, i: (i, 0))
    outc = pl.BlockSpec((B, HEAD), lambda h, i: (i, h))
    return pl.pallas_call(
        kern, name=name, grid=(RET_HEADS, nb),
        in_specs=[col(_Q0), col(_K0), col(_V0), col(_G0), tab, tab, pl.BlockSpec((1, HEAD), lambda h, i: (0, h)),
                  pl.BlockSpec((None, 1, LANE), lambda h, i: (h, 0, 0))],
        out_specs=[outc, outc, pl.BlockSpec((None, None, HEAD, HEAD), lambda h, i: (h, i, 0, 0))],
        out_shape=[_sds((T, RET_DIM), BF16), _sds((T, RET_DIM), F32), _sds((RET_HEADS, nb, HEAD, HEAD), F32)],
        scratch_shapes=[pltpu.VMEM((HEAD, HEAD), F32)],
        compiler_params=_cparams("parallel", "arbitrary"))(proj, proj, proj, proj, cosf, sinf, gr, lgt)


def _retention_bwd(name, proj, oraw, states, dcat, cosf, sinf, gr, lgt):
    T = proj.shape[0]
    B = min(RET_BLOCK, T)
    nb = T // B

    def kern(q_ref, k_ref, v_ref, g_ref, o_ref, dr_ref, st_ref, cos_ref, sin_ref, gr_ref, lg_ref,
             dq_ref, dk_ref, dv_ref, dg_ref, dgr_ref, dS):
        @pl.when(pl.program_id(1) == 0)
        def _():
            dS[...] = jnp.zeros_like(dS)
            dgr_ref[...] = jnp.zeros_like(dgr_ref)

        cosv, sinv = cos_ref[...], sin_ref[...]
        rot = lambda a: a * cosv + pltpu.roll(a, HEAD // 2, 1) * sinv
        rot_t = lambda a: a * cosv + pltpu.roll(a * sinv, HEAD // 2, 1)
        qr = rot(q_ref[...])
        kr = rot(k_ref[...]) * QK_SCALE
        v = v_ref[...]
        dmat, qdec, kdec, bdec = _ret_tables(B, lg_ref[0:1, 0:1])
        o = o_ref[...]
        rs = lax.rsqrt(jnp.mean(o * o, axis=-1, keepdims=True) + EPS)
        xh = o * rs
        g = g_ref[...]
        sg = _sig(g)
        grv = gr_ref[...]
        dr = dr_ref[...]
        dn = dr * (g * sg)
        dg_ref[...] = (dr * (xh * grv) * (sg * (1.0 + g * (1.0 - sg)))).astype(BF16)
        dgr_ref[0:1, :] += jnp.sum(dn * xh, axis=0, keepdims=True)
        dxh = dn * grv
        do = rs * (dxh - xh * jnp.mean(dxh * xh, axis=-1, keepdims=True))
        sp = st_ref[...]
        dsv = dS[...]
        pd = _dot(qr, kr, 1, 1) * dmat
        dp = _dot(do, v, 1, 1) * dmat
        dv_ref[...] = (_dot(pd, do, 0, 0) + _dot(kr * kdec, dsv, 1, 0)).astype(BF16)
        dqr = _dot(dp, kr, 1, 0) + _dot(do, sp, 1, 1) * qdec
        dkr = _dot(dp, qr, 0, 0) + _dot(v, dsv, 1, 1) * kdec
        dS[...] = bdec * dsv + _dot(qr * qdec, do, 0, 0)
        dq_ref[...] = rot_t(dqr).astype(BF16)
        dk_ref[...] = (rot_t(dkr) * QK_SCALE).astype(BF16)

    rev = lambda i: nb - 1 - i
    col = lambda c0: pl.BlockSpec((B, HEAD), lambda h, i: (rev(i), c0 + h))
    tab = pl.BlockSpec((B, HEAD), lambda h, i: (rev(i), 0))
    outc = pl.BlockSpec((B, HEAD), lambda h, i: (rev(i), h))
    return pl.pallas_call(
        kern, name=name, grid=(RET_HEADS, nb),
        in_specs=[col(_Q0), col(_K0), col(_V0), col(_G0), outc,
                  pl.BlockSpec((B, HEAD), lambda h, i: (rev(i), CONV_DIM // LANE + h)),
                  pl.BlockSpec((None, None, HEAD, HEAD), lambda h, i: (h, rev(i), 0, 0)), tab, tab,
                  pl.BlockSpec((1, HEAD), lambda h, i: (0, h)), pl.BlockSpec((None, 1, LANE), lambda h, i: (h, 0, 0))],
        out_specs=[outc, outc, outc, outc, pl.BlockSpec((8, HEAD), lambda h, i: (0, h))],
        out_shape=[_sds((T, RET_DIM), BF16)] * 4 + [_sds((8, RET_DIM), F32)],
        scratch_shapes=[pltpu.VMEM((HEAD, HEAD), F32)],
        compiler_params=_cparams("parallel", "arbitrary"))(proj, proj, proj, proj, oraw, dcat, states, cosf, sinf, gr, lgt)


def _strict_upper(q_rows):
    r = lax.broadcasted_iota(jnp.int32, (SB_BLOCK, SB_BLOCK), 0)
    c = lax.broadcasted_iota(jnp.int32, (SB_BLOCK, SB_BLOCK), 1)
    rq = lax.broadcasted_iota(jnp.int32, (q_rows, SB_BLOCK), 0)
    cq = lax.broadcasted_iota(jnp.int32, (q_rows, SB_BLOCK), 1)
    return (r > c).astype(BF16), cq - rq


def _suffix_sum(vals, tri):
    hi = vals.astype(BF16)
    lo = (vals - hi.astype(F32)).astype(BF16)
    dn = (((1,), (0,)), ((), ()))
    return lax.dot_general(hi, tri, dn, preferred_element_type=F32) + lax.dot_general(lo, tri, dn, preferred_element_type=F32)


def _sb_scores(qi, kj, tri, col_minus_row, mask_off):
    z = _dot(qi, kj, 1, 1) * QK_SCALE
    lb = jnp.minimum(z, 0.0) - jnp.log(1.0 + jnp.exp(-jnp.abs(z)))
    if mask_off is not None:
        valid = col_minus_row < mask_off
        lk = jnp.where(valid, lb - z, 0.0)
        w_loc = jnp.where(valid, jnp.exp(lb + _suffix_sum(lk, tri)), 0.0)
        return lb, lk, valid, w_loc
    lk = lb - z
    return lb, lk, None, jnp.exp(lb + _suffix_sum(lk, tri))


def _sb_walk(i, ratio, tile, st):
    for d in reversed(range(ratio)):
        st = tile(ratio * i + d, -SB_BLOCK * d, st, SB_BLOCK * d)
    n_free = ratio * i

    def pair(p, s):
        j = n_free - 1 - 2 * p
        return tile(j - 1, None, tile(j, None, s))

    st = lax.fori_loop(0, n_free // 2, pair, st)
    return lax.fori_loop(0, n_free % 2, lambda _, s: tile(0, None, s), st)


def _head_norm_rows(src, gain, dst, T):
    for r0, rc in _row_chunks(T):
        a = src[r0:r0 + rc, :]
        r = lax.rsqrt(jnp.mean(a * a, axis=-1, keepdims=True) + EPS)
        dst[r0:r0 + rc, :] = (a * r * gain).astype(BF16)


def _sb_fwd(name, qkv, gq, gk, gather=None):
    T = qkv.shape[0]
    qr = min(SB_QROWS, T)
    nq, ratio = T // qr, qr // SB_BLOCK
    n_g = len(gather[0]) if gather else 0

    def kern(q_ref, k_ref, v_ref, gq_ref, gk_ref, *rest):
        shard_refs, rest = rest[:n_g], rest[2 * n_g:]
        o_ref, o32_ref, rest = rest[0], rest[1], rest[2:]
        w4_refs, rest = rest[:n_g], rest[n_g:]
        qn, kn, vb, sems = rest[0], rest[1], rest[2], rest[3:]
        head = pl.program_id(0)
        if gather:
            @pl.when(head == 0)
            def _():
                _gather_start(shard_refs, w4_refs, sems, gather[2])

        _head_norm_rows(q_ref, gq_ref[...], qn, T)
        _head_norm_rows(k_ref, gk_ref[...], kn, T)
        for r0, rc in _row_chunks(T):
            vb[r0:r0 + rc, :] = v_ref[r0:r0 + rc, :].astype(BF16)
        tri, diag_mask = _strict_upper(qr)

        def qblock(i, _):
            rows_i = pl.ds(pl.multiple_of(i * qr, qr), qr)
            qi = qn[rows_i, :]

            def tile(j, mask_off, st, row0=0):
                acc, car = st
                rows_j = pl.ds(pl.multiple_of(j * SB_BLOCK, SB_BLOCK), SB_BLOCK)
                _, lk, _, w = _sb_scores(qi[row0:], kn[rows_j, :], tri, diag_mask[row0:], mask_off)
                w_hi = w.astype(BF16)
                w_lo = (w - w_hi.astype(F32)).astype(BF16)
                vj = vb[rows_j, :]
                acc_new = acc[row0:] + jnp.exp(car[row0:]) * (_dot(w_hi, vj, 1, 0) + _dot(w_lo, vj, 1, 0))
                car_new = car[row0:] + jnp.sum(lk, axis=-1, keepdims=True)
                if row0:
                    return jnp.concatenate([acc[:row0], acc_new]), jnp.concatenate([car[:row0], car_new])
                return acc_new, car_new

            acc, _ = _sb_walk(i, ratio, tile, (jnp.zeros((qr, HEAD), F32), jnp.zeros((qr, 1), F32)))
            o_ref[rows_i, :] = acc.astype(BF16)
            o32_ref[rows_i, :] = acc
            return 0

        lax.fori_loop(0, nq, qblock, 0)
        if gather:
            @pl.when(head == SB_HEADS - 2)
            def _():
                _gather_forward(shard_refs, w4_refs, sems, gather[2])

            @pl.when(head == SB_HEADS - 1)
            def _():
                _gather_finish(shard_refs, w4_refs, sems, gather[2])

    col = lambda c0: pl.BlockSpec((T, HEAD), lambda h: (0, c0 + h))
    vec = pl.BlockSpec((1, HEAD), lambda h: (0, 0))
    out = pl.BlockSpec((T, HEAD), lambda h: (0, h))
    extra_in = list(gather[0]) + list(gather[1]) if gather else []
    res = pl.pallas_call(kern, name=name, grid=(SB_HEADS,),
                         in_specs=[col(0), col(SB_HEADS), col(2 * SB_HEADS), vec, vec] + [_ANY] * (2 * n_g),
                         out_specs=[out, out] + [_ANY] * n_g,
                         out_shape=[_sds((T, D_MODEL), BF16), _sds((T, D_MODEL), F32)]
                         + [_sds(w.shape, w.dtype) for w in (gather[1] if gather else [])],
                         scratch_shapes=[pltpu.VMEM((T, HEAD), BF16)] * 3 + (_gather_sems(n_g) if gather else []),
                         input_output_aliases={5 + n_g + t: 2 + t for t in range(n_g)},
                         compiler_params=_cparams("arbitrary" if gather else "parallel"))(qkv, qkv, qkv, gq, gk, *extra_in)
    return res[0], res[1], list(res[2:])


def _sb_bwd(name, qkv, gq, gk, o32, dcat, scatter=None):
    T = qkv.shape[0]
    qr = min(SB_QROWS, T)
    nq, ratio = T // qr, qr // SB_BLOCK
    n_s = len(scatter[0]) if scatter else 0

    def kern(q_ref, k_ref, v_ref, gq_ref, gk_ref, o_ref, do_ref, *rest):
        ps_refs, rest = rest[:n_s], rest[n_s:]
        dq_ref, dk_ref, dv_ref, dgq_ref, dgk_ref = rest[:5]
        slot_refs, rest = rest[5:5 + n_s], rest[5 + n_s:]
        qn, kn, vb, dqn, dkn, dvv = rest[:6]
        sems = rest[6:]

        @pl.when(pl.program_id(0) == 0)
        def _():
            dgq_ref[...] = jnp.zeros_like(dgq_ref)
            dgk_ref[...] = jnp.zeros_like(dgk_ref)
            if scatter:
                _scatter_start(ps_refs, slot_refs, sems, scatter[1])

        _head_norm_rows(q_ref, gq_ref[...], qn, T)
        _head_norm_rows(k_ref, gk_ref[...], kn, T)
        for r0, rc in _row_chunks(T):
            vb[r0:r0 + rc, :] = v_ref[r0:r0 + rc, :].astype(BF16)
            dkn[r0:r0 + rc, :] = jnp.zeros((rc, HEAD), F32)
            dvv[r0:r0 + rc, :] = jnp.zeros((rc, HEAD), F32)
        tri, diag_mask = _strict_upper(qr)

        def qblock(i, _):
            rows_i = pl.ds(pl.multiple_of(i * qr, qr), qr)
            qi = qn[rows_i, :]
            doi = do_ref[rows_i, :]
            dob = doi.astype(BF16)
            etot = jnp.sum(dob.astype(F32) * o_ref[rows_i, :], axis=-1, keepdims=True)

            def tile(j, mask_off, st, row0=0):
                dq_acc, car, ecar = st
                rows_j = pl.ds(pl.multiple_of(j * SB_BLOCK, SB_BLOCK), SB_BLOCK)
                kj = kn[rows_j, :]
                vj = vb[rows_j, :]
                q_live, do_live = qi[row0:], dob[row0:]
                lb, lk, valid, w_loc = _sb_scores(q_live, kj, tri, diag_mask[row0:], mask_off)
                w = w_loc * jnp.exp(car[row0:])
                e = w * _dot(do_live, vj, 1, 1)
                suff = _suffix_sum(e, tri) + e + ecar[row0:]
                sig = jnp.exp(lb)
                dz = (e * (1.0 - sig) - sig * (etot[row0:] - suff)) * QK_SCALE
                if mask_off is not None:
                    dz = jnp.where(valid, dz, 0.0)
                dzb = dz.astype(BF16)
                dkn[rows_j, :] += _dot(dzb, q_live, 0, 0)
                dvv[rows_j, :] += _dot(w, do_live, 0, 0)
                new = (dq_acc[row0:] + _dot(dzb, kj, 1, 0), car[row0:] + jnp.sum(lk, axis=-1, keepdims=True),
                       ecar[row0:] + jnp.sum(e, axis=-1, keepdims=True))
                if row0:
                    return tuple(jnp.concatenate([old[:row0], upd]) for old, upd in zip(st, new))
                return new

            zcol = jnp.zeros((qr, 1), F32)
            dq_acc, _, _ = _sb_walk(i, ratio, tile, (jnp.zeros((qr, HEAD), F32), zcol, zcol))
            dqn[rows_i, :] = dq_acc
            return 0

        lax.fori_loop(0, nq, qblock, 0)

        def norm_bwd(src, gain, dnorm, dst, dgain):
            tot = jnp.zeros((1, HEAD), F32)
            for r0, rc in _row_chunks(T):
                a = src[r0:r0 + rc, :]
                r = lax.rsqrt(jnp.mean(a * a, axis=-1, keepdims=True) + EPS)
                xh = a * r
                dn = dnorm[r0:r0 + rc, :]
                tot = tot + jnp.sum(dn * xh, axis=0, keepdims=True)
                dxh = dn * gain
                dst[r0:r0 + rc, :] = (r * (dxh - xh * jnp.mean(dxh * xh, axis=-1, keepdims=True))).astype(BF16)
            dgain[0:1, :] += tot

        norm_bwd(q_ref, gq_ref[...], dqn, dq_ref, dgq_ref)
        norm_bwd(k_ref, gk_ref[...], dkn, dk_ref, dgk_ref)
        for r0, rc in _row_chunks(T):
            dv_ref[r0:r0 + rc, :] = dvv[r0:r0 + rc, :].astype(BF16)
        if scatter:
            @pl.when(pl.program_id(0) == SB_HEADS - 1)
            def _():
                _scatter_finish(ps_refs, slot_refs, sems, scatter[1])

    col = lambda c0: pl.BlockSpec((T, HEAD), lambda h: (0, c0 + h))
    vec = pl.BlockSpec((1, HEAD), lambda h: (0, 0))
    out = pl.BlockSpec((T, HEAD), lambda h: (0, h))
    st = pl.BlockSpec((8, HEAD), lambda h: (0, 0))
    ps = list(scatter[0]) if scatter else []
    res = pl.pallas_call(kern, name=name, grid=(SB_HEADS,),
                         in_specs=[col(0), col(SB_HEADS), col(2 * SB_HEADS), vec, vec, out, out] + [_ANY] * n_s,
                         out_specs=[out, out, out, st, st] + [_ANY] * n_s,
                         out_shape=[_sds((T, D_MODEL), BF16)] * 3 + [_sds((8, HEAD), F32)] * 2
                         + [_sds(p.shape, p.dtype) for p in ps],
                         scratch_shapes=[pltpu.VMEM((T, HEAD), BF16)] * 3 + [pltpu.VMEM((T, HEAD), F32)] * 3
                         + (_scatter_sems(n_s) if scatter else []),
                         compiler_params=_cparams("arbitrary"))(qkv, qkv, qkv, gq, gk, o32, dcat, *ps)
    return tuple(res[:5]) + (list(res[5:]),)


def _ada_fwd(c_all, ada_w, ada_b_cols):
    L, K, n = ada_w.shape

    def kern(c_ref, w_ref, b_ref, o_ref):
        cv = c_ref[...]
        o_ref[...] = _dot(cv * _sig(cv), w_ref[...], 1, 0) + b_ref[...]

    return pl.pallas_call(kern, name="ada_fwd", grid=(L,),
                          in_specs=[pl.BlockSpec((N_DEV, K), lambda l: (0, 0)), pl.BlockSpec((None, K, n), lambda l: (l, 0, 0)),
                                    pl.BlockSpec((None, 1, n), lambda l: (l, 0, 0))],
                          out_specs=pl.BlockSpec((None, N_DEV, n), lambda l: (l, 0, 0)),
                          out_shape=_sds((L, N_DEV, n), F32), compiler_params=_cparams("parallel"))(c_all, ada_w, ada_b_cols)


def _ada_wgrad(c_all_t, dmod_cols):
    K = c_all_t.shape[0]
    L, _, n = dmod_cols.shape
    tk = 128

    def kern(c_ref, d_ref, o_ref):
        cv = c_ref[...]
        ca = cv * _sig(cv)
        acc = ca[:, 0:1] * d_ref[0:1, :]
        for b in range(1, N_DEV):
            acc = acc + ca[:, b:b + 1] * d_ref[b:b + 1, :]
        o_ref[...] = acc

    return pl.pallas_call(kern, name="ada_wgrad", grid=(L, K // tk),
                          in_specs=[pl.BlockSpec((tk, N_DEV), lambda l, m: (m, 0)), pl.BlockSpec((None, N_DEV, n), lambda l, m: (l, 0, 0))],
                          out_specs=pl.BlockSpec((None, tk, n), lambda l, m: (l, m, 0)),
                          out_shape=_sds((L, K, n), F32), compiler_params=_cparams("parallel", "parallel"))(c_all_t, dmod_cols)


def _sum_devices(g):
    _, R, n = g.shape

    def kern(g_ref, o_ref):
        acc = g_ref[0]
        for d in range(1, N_DEV):
            acc = acc + g_ref[d]
        o_ref[...] = acc

    return pl.pallas_call(kern, name="sum_devices", out_shape=_sds((R, n), F32))(g)


def _pair_sum(name, g, recv, half):
    n4, L, r, cdim = g.shape
    rows = (L // 2) * r
    tr = _tile(rows, 512)
    gv = g.reshape(n4, 2, rows, cdim)
    rv = recv.reshape(n4, rows, cdim)

    def kern(g_ref, r_ref, o_ref):
        o_ref[...] = (g_ref[...] + r_ref[...]).astype(BF16)

    out = pl.pallas_call(kern, name=name, grid=(n4, rows // tr),
                         in_specs=[pl.BlockSpec((None, None, tr, cdim), lambda j, m: (j, half, m, 0)),
                                   pl.BlockSpec((None, tr, cdim), lambda j, m: (j, m, 0))],
                         out_specs=pl.BlockSpec((None, tr, cdim), lambda j, m: (j, m, 0)),
                         out_shape=_sds((n4, rows, cdim), BF16),
                         compiler_params=_cparams("parallel", "parallel"))(gv, rv)
    return out.reshape(n4, L // 2, r, cdim)


def _chip_sum(name, recv, psum, pos_idx, half, total):
    n4, hl, r, cdim = recv.shape
    rows = hl * r
    tr = _tile(rows, 512)

    def kern(pos_ref, r0, r1, r2, r3, own_ref, t_ref, o_ref):
        me = pos_ref[0]
        own = own_ref[...].astype(F32)
        terms = [jnp.where(me == s, own, rr[...].astype(F32)) for s, rr in enumerate((r0, r1, r2, r3))]
        o_ref[...] = ((terms[0] + terms[1]) + terms[2]) + terms[3]

    def slot(s):
        return pl.BlockSpec((None, tr, cdim), lambda m, pos: (jnp.where(pos[0] == s, (s + 1) % n4, s), m, 0))

    gs = pltpu.PrefetchScalarGridSpec(
        num_scalar_prefetch=1, grid=(rows // tr,),
        in_specs=[slot(s) for s in range(n4)] + [pl.BlockSpec((None, tr, cdim), lambda m, pos: (pos[0], m, 0)), _ANY],
        out_specs=pl.BlockSpec((None, tr, cdim), lambda m, pos: (half, m, 0)))
    rv = recv.reshape(n4, rows, cdim)
    out = pl.pallas_call(kern, name=name, grid_spec=gs, out_shape=_sds((2, rows, cdim), F32),
                         input_output_aliases={6: 0}, compiler_params=_cparams("parallel"))(
        pos_idx, rv, rv, rv, rv, psum.reshape(n4, rows, cdim), total.reshape(2, rows, cdim))
    return out.reshape(2 * hl, r, cdim)


def _adamw(name, w, g, m, v):
    shape = w.shape
    cols = shape[-1]
    rows = int(np.prod(shape[:-1]))
    tr = _tile(rows, 512) if rows % 8 == 0 else rows
    c1 = 1.0 - ADAM_B1 ** ADAM_STEP
    c2 = 1.0 - ADAM_B2 ** ADAM_STEP

    def kern(w_ref, g_ref, m_ref, v_ref, d_ref, mo_ref, vo_ref):
        gv = g_ref[...]
        mn = ADAM_B1 * m_ref[...] + (1.0 - ADAM_B1) * gv
        vn = ADAM_B2 * v_ref[...] + (1.0 - ADAM_B2) * (gv * gv)
        mo_ref[...] = mn
        vo_ref[...] = vn
        d_ref[...] = -ADAM_LR * ((mn / c1) / (jnp.sqrt(vn / c2) + ADAM_EPS) + ADAM_WD * w_ref[...])

    blk = pl.BlockSpec((tr, cols), lambda i: (i, 0))
    outs = pl.pallas_call(kern, name=name, grid=(rows // tr,), in_specs=[blk] * 4, out_specs=[blk] * 3,
                          out_shape=[_sds((rows, cols), F32)] * 3, compiler_params=_cparams("parallel"))(
        *[a.reshape(rows, cols) for a in (w, g, m, v)])
    return tuple(o.reshape(shape) for o in outs)


def _rcopy(src, dst, ssem, rsem, dev):
    return pltpu.make_async_remote_copy(src_ref=src, dst_ref=dst, send_sem=ssem, recv_sem=rsem, device_id=dev,
                                        device_id_type=MESH)


def _gather8(name, blk):
    m_per, n = blk.shape

    def body(x_ref, out_ref, send_sems, recv_sems, local_sem):
        x, y, c = _position()
        me, sibling = (x, y, c), (x, y, 1 - c)
        chips = _other_chips(x, y)

        def rows(px, py, pc):
            return out_ref.at[pl.ds((4 * px + 2 * py + pc) * m_per, m_per), :]

        def copy(k, block, to, src=None):
            return _rcopy(rows(*block) if src is None else src, rows(*block), send_sems.at[k], recv_sems.at[k], to)

        mine = pltpu.make_async_copy(x_ref, rows(*me), local_sem)
        mine.start()
        first = [copy(0, me, sibling, src=x_ref)]
        first += [copy(1 + j, me, (*chip, c), src=x_ref) for j, chip in enumerate(chips)]
        for cp in first:
            cp.start()
        passed = [copy(4 + j, (*chip, c), sibling) for j, chip in enumerate(chips)]
        for j, chip in enumerate(chips):
            copy(1 + j, (*chip, c), me).wait_recv()
            passed[j].start()
        copy(0, sibling, me).wait_recv()
        for j, chip in enumerate(chips):
            copy(4 + j, (*chip, 1 - c), me).wait_recv()
        for cp in first + passed:
            cp.wait_send()
        mine.wait()

    return pl.pallas_call(body, name=name, out_shape=_sds((N_DEV * m_per, n), blk.dtype),
                          in_specs=[pl.BlockSpec(memory_space=pltpu.VMEM)], out_specs=pl.BlockSpec(memory_space=pltpu.VMEM),
                          scratch_shapes=[pltpu.SemaphoreType.DMA((7,)), pltpu.SemaphoreType.DMA((7,)), pltpu.SemaphoreType.DMA],
                          compiler_params=pltpu.CompilerParams(vmem_limit_bytes=VMEM_LIMIT_V7X))(blk)


_ANY = pl.BlockSpec(memory_space=pl.ANY)


def _gather_sems(n):
    return [pltpu.SemaphoreType.DMA((3 * n,))] * 4


def _gather_start(src, out, sems, half):
    send_sems, recv_sems = sems[0], sems[1]
    x, y, c = _position()
    chips = _other_chips(x, y)
    me_chip = 2 * x + y

    @pl.when(c == half)
    def _():
        for t in range(len(src)):
            hl = src[t].shape[0] // 2
            rows = pl.ds(half * hl, hl)
            for j, (px, py) in enumerate(chips):
                _rcopy(src[t].at[rows], out[t].at[me_chip, rows], send_sems.at[3 * t + j], recv_sems.at[3 * t + j],
                       (px, py, half)).start()


def _gather_forward(src, out, sems, half):
    send_sems, recv_sems, fsend_sems, frecv_sems = sems
    x, y, c = _position()
    chips = _other_chips(x, y)

    @pl.when(c == half)
    def _():
        for t in range(len(src)):
            hl = src[t].shape[0] // 2
            rows = pl.ds(half * hl, hl)
            for j, (px, py) in enumerate(chips):
                landed = out[t].at[2 * px + py, rows]
                _rcopy(landed, landed, send_sems.at[3 * t + j], recv_sems.at[3 * t + j], (px, py, half)).wait_recv()
                _rcopy(landed, landed, fsend_sems.at[3 * t + j], frecv_sems.at[3 * t + j], (x, y, 1 - half)).start()


def _gather_finish(src, out, sems, half):
    send_sems, recv_sems, fsend_sems, frecv_sems = sems
    x, y, c = _position()
    chips = _other_chips(x, y)
    me_chip = 2 * x + y

    @pl.when(c == half)
    def _():
        for t in range(len(src)):
            hl = src[t].shape[0] // 2
            rows = pl.ds(half * hl, hl)
            for j, (px, py) in enumerate(chips):
                landed = out[t].at[2 * px + py, rows]
                _rcopy(src[t].at[rows], out[t].at[me_chip, rows], send_sems.at[3 * t + j], recv_sems.at[3 * t + j],
                       (px, py, half)).wait_send()
                _rcopy(landed, landed, fsend_sems.at[3 * t + j], frecv_sems.at[3 * t + j], (x, y, 1 - half)).wait_send()

    @pl.when(c != half)
    def _():
        for t in range(len(src)):
            hl = src[t].shape[0] // 2
            rows = pl.ds(half * hl, hl)
            for j, (px, py) in enumerate(chips):
                landed = out[t].at[2 * px + py, rows]
                _rcopy(landed, landed, fsend_sems.at[3 * t + j], frecv_sems.at[3 * t + j], (x, y, half)).wait_recv()


def _gather_weights(shards, half):
    n = len(shards)

    def body(*refs):
        src, out, sems = refs[:n], refs[n:2 * n], refs[2 * n:]
        _gather_start(src, out, sems, half)
        _gather_forward(src, out, sems, half)
        _gather_finish(src, out, sems, half)

    return pl.pallas_call(body, name="gather_weights", out_shape=[_sds((N_CHIP,) + s.shape, s.dtype) for s in shards],
                          in_specs=[_ANY] * n, out_specs=[_ANY] * n, scratch_shapes=_gather_sems(n))(*shards)


def _place_own(name, w4, shard, pos_idx):
    n4, L, r, cdim = w4.shape
    rows = L * r
    tr = _tile(rows, 1024)

    def kern(pos_ref, s_ref, w_ref, o_ref):
        o_ref[...] = s_ref[...]

    gs = pltpu.PrefetchScalarGridSpec(
        num_scalar_prefetch=1, grid=(rows // tr,),
        in_specs=[pl.BlockSpec((tr, cdim), lambda m, pos: (m, 0)), _ANY],
        out_specs=pl.BlockSpec((None, tr, cdim), lambda m, pos: (pos[0], m, 0)))
    out = pl.pallas_call(kern, name=name, grid_spec=gs, out_shape=_sds((n4, rows, cdim), w4.dtype),
                         input_output_aliases={2: 0}, compiler_params=_cparams("parallel"))(
        pos_idx, shard.reshape(rows, cdim), w4.reshape(n4, rows, cdim))
    return out.reshape(w4.shape)


def _exchange_pair(grads, to_c):
    n = len(grads)

    def body(*refs):
        src, out = refs[:n], refs[n:2 * n]
        send_sems, recv_sems = refs[2 * n:]
        x, y, c = _position()
        sibling = (x, y, 1 - c)

        @pl.when(c != to_c)
        def _():
            cps = []
            for t in range(n):
                hl = src[t].shape[1] // 2
                cp = _rcopy(src[t].at[:, pl.ds(to_c * hl, hl)], out[t], send_sems.at[t], recv_sems.at[t], sibling)
                cp.start()
                cps.append(cp)
            for cp in cps:
                cp.wait_send()

        @pl.when(c == to_c)
        def _():
            for t in range(n):
                _rcopy(out[t], out[t], send_sems.at[t], recv_sems.at[t], sibling).wait_recv()

    sems = [pltpu.SemaphoreType.DMA((n,))] * 2
    return pl.pallas_call(body, name=f"exchange_pair_{to_c}",
                          out_shape=[_sds((g.shape[0], g.shape[1] // 2) + g.shape[2:], g.dtype) for g in grads],
                          in_specs=[_ANY] * n, out_specs=[_ANY] * n, scratch_shapes=sems)(*grads)


def _scatter_sems(n):
    return [pltpu.SemaphoreType.DMA((3 * n,))] * 2


def _scatter_start(src, out, sems, half):
    send_sems, recv_sems = sems
    x, y, c = _position()
    chips = _other_chips(x, y)
    me_chip = 2 * x + y

    @pl.when(c == half)
    def _():
        for t in range(len(src)):
            for j, (px, py) in enumerate(chips):
                _rcopy(src[t].at[2 * px + py], out[t].at[me_chip], send_sems.at[3 * t + j], recv_sems.at[3 * t + j],
                       (px, py, half)).start()


def _scatter_finish(src, out, sems, half):
    send_sems, recv_sems = sems
    x, y, c = _position()
    chips = _other_chips(x, y)
    me_chip = 2 * x + y

    @pl.when(c == half)
    def _():
        for t in range(len(src)):
            for j, (px, py) in enumerate(chips):
                slot = out[t].at[2 * px + py]
                _rcopy(slot, slot, send_sems.at[3 * t + j], recv_sems.at[3 * t + j], (px, py, half)).wait_recv()
        for t in range(len(src)):
            for j, (px, py) in enumerate(chips):
                _rcopy(src[t].at[2 * px + py], out[t].at[me_chip], send_sems.at[3 * t + j], recv_sems.at[3 * t + j],
                       (px, py, half)).wait_send()


def _scatter_chips(psums, half):
    n = len(psums)

    def body(*refs):
        src, out, sems = refs[:n], refs[n:2 * n], refs[2 * n:]
        _scatter_start(src, out, sems, half)
        _scatter_finish(src, out, sems, half)

    return pl.pallas_call(body, name="scatter_chips", out_shape=[_sds(p.shape, p.dtype) for p in psums],
                          in_specs=[_ANY] * n, out_specs=[_ANY] * n, scratch_shapes=_scatter_sems(n))(*psums)


def _share_halves(full):
    n = len(full)

    def body(*refs):
        out = refs[n:2 * n]
        send_sems, recv_sems = refs[2 * n:]
        x, y, c = _position()
        sibling = (x, y, 1 - c)
        cps = []
        for t in range(n):
            hl = out[t].shape[0] // 2
            mine = out[t].at[pl.ds(c * hl, hl)]
            cp = _rcopy(mine, mine, send_sems.at[t], recv_sems.at[t], sibling)
            cp.start()
            cps.append(cp)
        for t in range(n):
            hl = out[t].shape[0] // 2
            theirs = out[t].at[pl.ds((1 - c) * hl, hl)]
            _rcopy(theirs, theirs, send_sems.at[t], recv_sems.at[t], sibling).wait_recv()
        for cp in cps:
            cp.wait_send()

    sems = [pltpu.SemaphoreType.DMA((n,))] * 2
    return pl.pallas_call(body, name="share_halves", out_shape=[_sds(h.shape, h.dtype) for h in full],
                          in_specs=[_ANY] * n, out_specs=[_ANY] * n, scratch_shapes=sems,
                          input_output_aliases={t: t for t in range(n)})(*full)


def _rope_tables(T):
    inv_freq = 1.0 / (ROPE_THETA ** (jnp.arange(0, HEAD, 2, dtype=F32) / HEAD))
    ang = jnp.arange(T, dtype=F32)[:, None] * inv_freq[None, :]
    cos, sin = jnp.cos(ang), jnp.sin(ang)
    return jnp.concatenate([cos, cos], axis=-1), jnp.concatenate([-sin, sin], axis=-1)


def _decay_table():
    lg = np.log1p(-np.exp2(-5.0 - np.arange(RET_HEADS, dtype=np.float32))).astype(np.float32)
    return jnp.asarray(np.broadcast_to(lg[:, None, None], (RET_HEADS, 1, LANE)).copy())


def _local_step(x0, target, mod, W, G, norm_mix_g, norm_ffn_g, conv_full, ev_ret_norm_g, od_q_norm_g, od_k_norm_g,
                fused=None):
    T = x0.shape[0]
    KSH1, KSC1, KG1, KSH2, KSC2, KG2 = range(6)
    gain_mix = norm_mix_g.reshape(DEPTH, 1, D_MODEL)
    gain_ffn = norm_ffn_g.reshape(DEPTH, 1, D_MODEL)
    cosf, sinf = _rope_tables(T)
    lgt = _decay_table()

    saved = []
    xcur = x0
    h = _normmod("norm_mix_0", x0, gain_mix, 0, mod, KSC1, KSH1)
    for l in range(DEPTH):
        j = l // 2
        s = dict(x_in=xcur, h=h)
        to_ffn = (gain_ffn, l, l, KSC2, KSH2)
        if l % 2 == 0:
            proj = _proj_cols(f"ev_in_{l}", h, W["ev_w_in"], j)
            a = _conv_fwd(f"conv_{l}", proj, conv_full[j])
            r, oraw, states = _retention_fwd(f"ret_{l}", proj, cosf, sinf, ev_ret_norm_g[j].reshape(1, RET_DIM), lgt)
            cat = jnp.concatenate([a, r], axis=1)
            s.update(proj=proj, oraw=oraw, states=states, cat=cat)
            y, xmid, h2 = _out_proj(f"ev_out_{l}", cat, False, W["ev_w_out"], j, xcur, mod, l, KG1, to_ffn)
        else:
            qkv = _proj_cols(f"od_in_{l}", h, W["od_w_qkv"], j)
            gather = (fused["shards"], [W[k] for k in fused["names"]], 1) if fused is not None and l == 1 else None
            o, o32, w4s = _sb_fwd(f"sb_{l}", qkv, od_q_norm_g[j].reshape(1, HEAD), od_k_norm_g[j].reshape(1, HEAD), gather)
            if gather:
                W = dict(zip(fused["names"], w4s))
            s.update(qkv=qkv, cat=o, o32=o32)
            y, xmid, h2 = _out_proj(f"od_out_{l}", o, False, W["od_w_out"], j, xcur, mod, l, KG1, to_ffn)
        s.update(y1=y, x_mid=xmid)
        gate, up, act = _ffn_up(f"ffn_up_{l}", h2, W["ffn_w_gate"], W["ffn_w_up"], l)
        to_mix = (gain_mix, l + 1, l + 1, KSC1, KSH1) if l + 1 < DEPTH else None
        y2, xcur, *h_next = _out_proj(f"ffn_down_{l}", act, True, W["ffn_w_down"], l, xmid, mod, l, KG2, to_mix)
        h = h_next[0] if h_next else None
        s.update(h2=h2, gate=gate, up=up, act=act, y2=y2)
        saved.append(s)

    dy, lacc = _loss_head(xcur, target)

    dmod_rows = [None] * DEPTH
    d_mix = [None] * DEPTH
    d_ffn = [None] * DEPTH
    d_conv = [None] * 2
    d_ret = [None] * 2
    d_gq = [None] * 2
    d_gk = [None] * 2
    dx = dy
    reduce_up = None
    dyg, st_top = _gate_bwd("gate2_bwd_top", dx, saved[DEPTH - 1]["y2"], mod, DEPTH - 1, KG2)
    d_gate2 = st_top[0]
    for l in reversed(range(DEPTH)):
        j = l // 2
        s = saved[l]
        G["ffn_w_down"] = _wgrad(f"wg_down_{l}", s["act"], "stack", dyg, "full", G["ffn_w_down"], l)
        dgate, dup = _ffn_down_bwd(f"ffn_down_bwd_{l}", dyg, W["ffn_w_down"], l, s["gate"], s["up"])
        G["ffn_w_gate"] = _wgrad(f"wg_gate_{l}", s["h2"], "full", dgate, "stack", G["ffn_w_gate"], l)
        G["ffn_w_up"] = _wgrad(f"wg_up_{l}", s["h2"], "full", dup, "stack", G["ffn_w_up"], l)
        dh2 = _ffn_up_bwd(f"ffn_up_bwd_{l}", dgate, dup, W["ffn_w_gate"], W["ffn_w_up"], l)
        dxm, st_n2, dyg1 = _normmod_bwd(f"norm_ffn_bwd_{l}", s["x_mid"], dh2, dx, gain_ffn, l, mod, KSC2,
                                        gate=(s["y1"], l, KG1))
        if l % 2 == 0:
            G["ev_w_out"] = _wgrad(f"wg_evout_{l}", s["cat"], "cols", dyg1, "full", G["ev_w_out"], j)
            dcat = _bwd_rows(f"ev_out_bwd_{l}", dyg1, W["ev_w_out"], j)
            db, dcg, du, dwc = _conv_bwd(f"conv_bwd_{l}", s["proj"], conv_full[j], dcat)
            dq, dk, dv, dg, dgr = _retention_bwd(f"ret_bwd_{l}", s["proj"], s["oraw"], s["states"], dcat, cosf, sinf,
                                                 ev_ret_norm_g[j].reshape(1, RET_DIM), lgt)
            dproj = jnp.concatenate([db, dcg, du, dq, dk, dv, dg], axis=1)
            d_conv[j], d_ret[j] = dwc[:CONV_WIDTH], dgr[0]
            G["ev_w_in"] = _wgrad(f"wg_evin_{l}", s["h"], "full", dproj, "cols", G["ev_w_in"], j)
            dh = _bwd_cols(f"ev_in_bwd_{l}", dproj, W["ev_w_in"], j)
        else:
            G["od_w_out"] = _wgrad(f"wg_odout_{l}", s["cat"], "cols", dyg1, "full", G["od_w_out"], j)
            dcat = _bwd_rows(f"od_out_bwd_{l}", dyg1, W["od_w_out"], j)
            scatter = None
            if fused is not None and l == 1:
                g_now = [G[k] for k in fused["names"]]
                recv_up = _exchange_pair(g_now, 1)
                ps_up = [_pair_sum(f"pair_sum_up_{k}", g, r, 1) for k, g, r in zip(fused["names"], g_now, recv_up)]
                scatter = (ps_up, 1)
            dq, dk, dv, dgq, dgk, slots_up = _sb_bwd(f"sb_bwd_{l}", s["qkv"], od_q_norm_g[j].reshape(1, HEAD),
                                                     od_k_norm_g[j].reshape(1, HEAD), s["o32"], dcat, scatter)
            if scatter:
                reduce_up = (ps_up, slots_up)
            dproj = jnp.concatenate([dq, dk, dv], axis=1)
            d_gq[j], d_gk[j] = dgq[0], dgk[0]
            G["od_w_qkv"] = _wgrad(f"wg_odin_{l}", s["h"], "full", dproj, "cols", G["od_w_qkv"], j)
            dh = _bwd_cols(f"od_in_bwd_{l}", dproj, W["od_w_qkv"], j)
        below = (saved[l - 1]["y2"], l - 1, KG2) if l > 0 else None
        dx, st_n1, *dyg_below = _normmod_bwd(f"norm_mix_bwd_{l}", s["x_in"], dh, dxm, gain_mix, l, mod, KSC1, gate=below)
        dmod_rows[l] = jnp.stack([st_n1[0], st_n1[1], st_n2[3], st_n2[0], st_n2[1], d_gate2]).reshape(6 * D_MODEL)
        d_mix[l], d_ffn[l] = st_n1[2], st_n2[2]
        if below:
            dyg, d_gate2 = dyg_below[0], st_n1[3]
    return lacc, dx, G, (dmod_rows, d_mix, d_ffn, d_ret, d_gq, d_gk, d_conv), reduce_up


def kernel(x, c, ada_w, ada_b, norm_mix_g, norm_ffn_g, ev_w_in, ev_conv_w, ev_ret_norm_g, ev_w_out, od_w_qkv, od_q_norm_g, od_k_norm_g, od_w_out, ffn_w_gate, ffn_w_up, ffn_w_down, loss_target, m_ada_w, m_ada_b, m_norm_mix_g, m_norm_ffn_g, m_ev_w_in, m_ev_conv_w, m_ev_ret_norm_g, m_ev_w_out, m_od_w_qkv, m_od_q_norm_g, m_od_k_norm_g, m_od_w_out, m_ffn_w_gate, m_ffn_w_up, m_ffn_w_down, v_ada_w, v_ada_b, v_norm_mix_g, v_norm_ffn_g, v_ev_w_in, v_ev_conv_w, v_ev_ret_norm_g, v_ev_w_out, v_od_w_qkv, v_od_q_norm_g, v_od_k_norm_g, v_od_w_out, v_ffn_w_gate, v_ffn_w_up, v_ffn_w_down):
    xi, yi, ci = _position()
    chip = 2 * xi + yi
    dev = 4 * xi + 2 * yi + ci
    x0 = x[0]
    target = loss_target[0]

    n_small = D_MODEL + 2 * CONV_WIDTH * LANE
    small = jnp.concatenate([c.reshape(1, D_MODEL), ev_conv_w.reshape(1, 2 * CONV_WIDTH * LANE)], axis=1)
    small = jnp.broadcast_to(small, (8, n_small))
    g1 = _gather8("gather_cond", small).reshape(N_DEV, 8, n_small)[:, 0, :]
    c_all = g1[:, :D_MODEL]
    conv_all = g1[0::2, D_MODEL:].reshape(N_CHIP, 2, CONV_WIDTH, LANE)
    conv_full = conv_all.transpose(1, 2, 0, 3).reshape(2, CONV_WIDTH, CONV_DIM)

    n_ada = ada_w.shape[-1]
    ada_b_cols = lax.dynamic_slice_in_dim(ada_b, chip * n_ada, n_ada, axis=1).reshape(DEPTH, 1, n_ada)
    mod_cols = _ada_fwd(c_all, ada_w, ada_b_cols)
    g2 = _gather8("gather_mod", mod_cols.reshape(DEPTH * N_DEV, n_ada)).reshape(N_DEV, DEPTH, N_DEV, n_ada)
    mod_mine = lax.dynamic_index_in_dim(g2[0::2], dev, axis=2, keepdims=False)
    mod = mod_mine.transpose(1, 0, 2).reshape(DEPTH, 6, 1, D_MODEL)

    big_names = ["ev_w_in", "ev_w_out", "od_w_qkv", "od_w_out", "ffn_w_gate", "ffn_w_up", "ffn_w_down"]
    big = dict(ev_w_in=ev_w_in, ev_w_out=ev_w_out, od_w_qkv=od_w_qkv, od_w_out=od_w_out, ffn_w_gate=ffn_w_gate,
               ffn_w_up=ffn_w_up, ffn_w_down=ffn_w_down)
    pos_idx = jnp.stack([chip, ci]).astype(jnp.int32)
    shards = [big[k].astype(BF16) for k in big_names]
    W = {k: _place_own(f"place_{k}", w4, s, pos_idx) for k, w4, s in zip(big_names, _gather_weights(shards, 0), shards)}
    G = {k: lax.empty((N_CHIP,) + big[k].shape, F32) for k in big_names}

    lacc, dx, G, small_grads, (ps_up, slots_up) = _local_step(
        x0, target, mod, W, G, norm_mix_g, norm_ffn_g, conv_full, ev_ret_norm_g, od_q_norm_g, od_k_norm_g,
        fused=dict(names=big_names, shards=shards))
    loss = lax.psum(lacc[0, 0], ("x", "y", "c"))
    grad_x = dx[None]
    dmod_rows, d_mix, d_ffn, d_ret, d_gq, d_gk, d_conv = small_grads

    totals = [_chip_sum(f"chip_sum_up_{k}", r, p, pos_idx, 1, lax.empty(big[k].shape, F32))
              for k, r, p in zip(big_names, slots_up, ps_up)]
    glist = [G[k] for k in big_names]
    recv_lo = _exchange_pair(glist, 0)
    ps_lo = [_pair_sum(f"pair_sum_lo_{k}", g, r, 0) for k, g, r in zip(big_names, glist, recv_lo)]
    slots_lo = _scatter_chips(ps_lo, 0)
    totals = [_chip_sum(f"chip_sum_lo_{k}", r, p, pos_idx, 0, t) for k, r, p, t in zip(big_names, slots_lo, ps_lo, totals)]
    grads = dict(zip(big_names, _share_halves(totals)))

    pieces = [jnp.stack(dmod_rows).reshape(-1), jnp.stack(d_mix).reshape(-1), jnp.stack(d_ffn).reshape(-1),
              jnp.stack(d_ret).reshape(-1), jnp.stack(d_gq).reshape(-1), jnp.stack(d_gk).reshape(-1),
              jnp.stack(d_conv).reshape(-1)]
    sizes = [int(p.shape[0]) for p in pieces]
    n_pack = sum(sizes)
    n_cols = -(-n_pack // (8 * LANE)) * LANE
    packed = jnp.concatenate(pieces + [jnp.zeros((8 * n_cols - n_pack,), F32)]).reshape(8, n_cols)
    g3 = _gather8("gather_small", packed).reshape(N_DEV, 8, n_cols)
    tot = _sum_devices(g3).reshape(-1)
    offs = np.cumsum([0] + sizes)
    part = [tot[offs[i]:offs[i + 1]] for i in range(len(sizes))]
    grads["ada_b"] = part[0].reshape(DEPTH, 6 * D_MODEL)
    grads["norm_mix_g"] = part[1].reshape(DEPTH, D_MODEL)
    grads["norm_ffn_g"] = part[2].reshape(DEPTH, D_MODEL)
    grads["ev_ret_norm_g"] = part[3].reshape(2, RET_DIM)
    grads["od_q_norm_g"] = part[4].reshape(2, HEAD)
    grads["od_k_norm_g"] = part[5].reshape(2, HEAD)
    conv_g = part[6].reshape(2, CONV_WIDTH, CONV_DIM)
    grads["ev_conv_w"] = lax.dynamic_slice_in_dim(conv_g, chip * LANE, LANE, axis=2)
    dmod_all = g3.reshape(N_DEV, -1)[:, :DEPTH * 6 * D_MODEL].reshape(N_DEV, DEPTH, 6 * D_MODEL)
    dmod_cols = lax.dynamic_slice_in_dim(dmod_all, chip * n_ada, n_ada, axis=2).transpose(1, 0, 2)
    grads["ada_w"] = _ada_wgrad(c_all.T, dmod_cols)

    weights = dict(ada_w=ada_w, ada_b=ada_b, norm_mix_g=norm_mix_g, norm_ffn_g=norm_ffn_g, ev_w_in=ev_w_in,
                   ev_conv_w=ev_conv_w, ev_ret_norm_g=ev_ret_norm_g, ev_w_out=ev_w_out, od_w_qkv=od_w_qkv,
                   od_q_norm_g=od_q_norm_g, od_k_norm_g=od_k_norm_g, od_w_out=od_w_out, ffn_w_gate=ffn_w_gate,
                   ffn_w_up=ffn_w_up, ffn_w_down=ffn_w_down)
    m_in = dict(ada_w=m_ada_w, ada_b=m_ada_b, norm_mix_g=m_norm_mix_g, norm_ffn_g=m_norm_ffn_g, ev_w_in=m_ev_w_in,
                ev_conv_w=m_ev_conv_w, ev_ret_norm_g=m_ev_ret_norm_g, ev_w_out=m_ev_w_out, od_w_qkv=m_od_w_qkv,
                od_q_norm_g=m_od_q_norm_g, od_k_norm_g=m_od_k_norm_g, od_w_out=m_od_w_out, ffn_w_gate=m_ffn_w_gate,
                ffn_w_up=m_ffn_w_up, ffn_w_down=m_ffn_w_down)
    v_in = dict(ada_w=v_ada_w, ada_b=v_ada_b, norm_mix_g=v_norm_mix_g, norm_ffn_g=v_norm_ffn_g, ev_w_in=v_ev_w_in,
                ev_conv_w=v_ev_conv_w, ev_ret_norm_g=v_ev_ret_norm_g, ev_w_out=v_ev_w_out, od_w_qkv=v_od_w_qkv,
                od_q_norm_g=v_od_q_norm_g, od_k_norm_g=v_od_k_norm_g, od_w_out=v_od_w_out, ffn_w_gate=v_ffn_w_gate,
                ffn_w_up=v_ffn_w_up, ffn_w_down=v_ffn_w_down)
    order = list(weights)
    deltas, new_m, new_v = {}, {}, {}
    for k in order:
        deltas[k], new_m[k], new_v[k] = _adamw(f"adamw_{k}", weights[k], grads[k], m_in[k], v_in[k])
    return (loss, grad_x, *[grads[k] for k in order], *[deltas[k] for k in order], *[new_m[k] for k in order],
            *[new_v[k] for k in order])
```

```python
import functools

import numpy as np
import jax
import jax.numpy as jnp
from jax import lax
from jax.experimental import pallas as pl
from jax.experimental.pallas import tpu as pltpu

F32 = jnp.float32
BF16 = jnp.bfloat16
MESH = pl.DeviceIdType.MESH

D_MODEL = 1024
DEPTH = 4
N_CHIP = 4
N_DEV = 8
HEAD = 128
RET_HEADS = 4
SB_HEADS = 8
CONV_DIM = 512
RET_DIM = 512
CONV_WIDTH = 3
RET_CHUNK = 64
RET_BLOCK = 256
SB_BLOCK = 256
SB_QROWS = 512
EPS = 1e-6
ROPE_THETA = 10000.0
QK_SCALE = HEAD ** -0.5
LANE = 128
ROW_CHUNK = 512
ROWS_STREAMED = 1024
VMEM_LIMIT_V7X = 56 * 1024 * 1024

ADAM_LR, ADAM_B1, ADAM_B2, ADAM_EPS, ADAM_WD, ADAM_STEP = 0.001, 0.9, 0.999, 1e-08, 0.01, 10


def _cparams(*sem):
    return pltpu.CompilerParams(dimension_semantics=sem or None, vmem_limit_bytes=VMEM_LIMIT_V7X)


def _tile(n, pref):
    if n <= pref:
        return n
    for t in range(pref - pref % 8, 7, -8):
        if n % t == 0:
            return t
    return n


def _sig(v):
    return 1.0 / (1.0 + jnp.exp(-v))


def _dot(a, b, ca, cb):
    return lax.dot_general(a.astype(BF16), b.astype(BF16), (((ca,), (cb,)), ((), ())),
                           preferred_element_type=F32)


def _position():
    x, y, c = lax.axis_index("x"), lax.axis_index("y"), lax.axis_index("c")
    return x, y, c


def _other_chips(x, y):
    return [(1 - x, y), (x, 1 - y), (1 - x, 1 - y)]


def _mm(name, pairs, out_sds, out_specs, grid, contract, red_axis=None, post=None,
        extras=(), extra_specs=(), sum_pairs=True, aliases=None, inner=None):
    n_p, n_ex, n_out = len(pairs), len(extras), len(out_sds)
    n_acc = 1 if sum_pairs else n_p
    n_red = grid[red_axis] if red_axis is not None else 1

    def default_post(accs, ex, outs):
        outs[0][...] = accs[0].astype(outs[0].dtype)

    post_fn = post or default_post

    def kern(*refs):
        ab = refs[:2 * n_p]
        ex = refs[2 * n_p:2 * n_p + n_ex]
        outs = refs[2 * n_p + n_ex:2 * n_p + n_ex + n_out]
        accs = refs[2 * n_p + n_ex + n_out:]
        if inner is None:
            prods = [_dot(ab[2 * p][...], ab[2 * p + 1][...], contract[0], contract[1]) for p in range(n_p)]
        else:
            n_in, a_get, b_get = inner
            prods = []
            for p in range(n_p):
                tot = _dot(a_get(ab[2 * p], 0), b_get(ab[2 * p + 1], 0), contract[0], contract[1])
                for i in range(1, n_in):
                    tot = tot + _dot(a_get(ab[2 * p], i), b_get(ab[2 * p + 1], i), contract[0], contract[1])
                prods.append(tot)
        if sum_pairs:
            tot = prods[0]
            for p_ in prods[1:]:
                tot = tot + p_
            prods = [tot]
        if red_axis is None:
            post_fn(prods, ex, outs)
        else:
            k = pl.program_id(red_axis)

            @pl.when(k == 0)
            def _():
                for a_, p_ in zip(accs, prods):
                    a_[...] = p_

            @pl.when(k > 0)
            def _():
                for a_, p_ in zip(accs, prods):
                    a_[...] += p_

            @pl.when(k == n_red - 1)
            def _():
                post_fn([a_[...] for a_ in accs], ex, outs)

    ins, in_specs = [], []
    for a, b, sa, sb in pairs:
        ins += [a, b]
        in_specs += [sa, sb]
    ins += list(extras)
    in_specs += list(extra_specs)
    scratch = []
    if red_axis is not None:
        scratch = [pltpu.VMEM(tuple(acc_shape), F32) for acc_shape in [_acc_shape(pairs[0], contract)] * n_acc]
    sem = tuple("arbitrary" if ax == red_axis else "parallel" for ax in range(len(grid)))
    res = pl.pallas_call(kern, name=name, grid=grid, in_specs=in_specs, out_specs=list(out_specs),
                         out_shape=list(out_sds), scratch_shapes=scratch,
                         input_output_aliases=aliases or {}, compiler_params=_cparams(*sem))(*ins)
    return res


def _acc_shape(pair, contract):
    sa, sb = pair[2], pair[3]
    da = [d for d in sa.block_shape if d is not None]
    db = [d for d in sb.block_shape if d is not None]
    return (da[1 - contract[0]], db[1 - contract[1]])


def _sds(shape, dtype):
    return jax.ShapeDtypeStruct(tuple(shape), dtype)


def _proj_cols(name, h, w4, l):
    T, K = h.shape
    n = w4.shape[-1]
    tm = _tile(T, ROWS_STREAMED)
    return _mm(name, [(h, w4, pl.BlockSpec((tm, K), lambda i, m: (m, 0)),
                       pl.BlockSpec((None, None, K, n), lambda i, m: (i, l, 0, 0)))],
               [_sds((T, N_CHIP * n), F32)], [pl.BlockSpec((tm, n), lambda i, m: (m, i))],
               (N_CHIP, T // tm), (1, 0))[0]


def _out_proj(name, a, a_stacked, w4, l, xres, mod, lm, kg, next_norm=None):
    T = xres.shape[0]
    k = w4.shape[-2]
    tm = _tile(T, 512)
    if a_stacked:
        sa = pl.BlockSpec((N_CHIP, tm, k), lambda m: (0, m, 0))
        a_get = lambda ref, i: ref[i]
    else:
        sa = pl.BlockSpec((tm, N_CHIP * k), lambda m: (m, 0))
        a_get = lambda ref, i: ref[:, i * k:(i + 1) * k]

    def post(accs, ex, outs):
        y = accs[0]
        outs[0][...] = y
        xn = ex[0][...] + ex[1][...] * y
        outs[1][...] = xn
        if next_norm:
            r = lax.rsqrt(jnp.mean(xn * xn, axis=-1, keepdims=True) + EPS)
            outs[2][...] = ((xn * r) * ex[2][...] * (1.0 + ex[3][...]) + ex[4][...]).astype(BF16)

    row = pl.BlockSpec((tm, D_MODEL), lambda m: (m, 0))
    extras, extra_specs = [xres, mod], [row, _vec_spec(lm, kg)]
    out_sds, out_specs = [_sds((T, D_MODEL), F32)] * 2, [row, row]
    if next_norm:
        gain3, lg, ln, ksc, ksh = next_norm
        extras += [gain3, mod, mod]
        extra_specs += [_vec_spec(lg), _vec_spec(ln, ksc), _vec_spec(ln, ksh)]
        out_sds.append(_sds((T, D_MODEL), BF16))
        out_specs.append(row)
    return _mm(name, [(a, w4, sa, pl.BlockSpec((N_CHIP, None, k, D_MODEL), lambda m: (0, l, 0, 0)))],
               out_sds, out_specs, (T // tm,), (1, 0), post=post, extras=extras, extra_specs=extra_specs,
               inner=(N_CHIP, a_get, lambda ref, i: ref[i]))


def _ffn_up(name, h2, wg4, wu4, l):
    T, K = h2.shape
    n = wg4.shape[-1]
    tm = _tile(T, ROWS_STREAMED)

    def post(accs, ex, outs):
        g, u = accs
        outs[0][...] = g
        outs[1][...] = u
        outs[2][...] = (g * _sig(g) * u).astype(BF16)

    sa = pl.BlockSpec((tm, K), lambda i, m: (m, 0))
    sw = pl.BlockSpec((None, None, K, n), lambda i, m: (i, l, 0, 0))
    so = pl.BlockSpec((None, tm, n), lambda i, m: (i, m, 0))
    return _mm(name, [(h2, wg4, sa, sw), (h2, wu4, sa, sw)],
               [_sds((N_CHIP, T, n), F32), _sds((N_CHIP, T, n), F32), _sds((N_CHIP, T, n), BF16)], [so, so, so],
               (N_CHIP, T // tm), (1, 0), post=post, sum_pairs=False)


def _bwd_cols(name, dproj, w4, l):
    T = dproj.shape[0]
    K, n = w4.shape[-2:]
    tm = _tile(T, 512)
    return _mm(name, [(dproj, w4, pl.BlockSpec((tm, N_CHIP * n), lambda m: (m, 0)),
                       pl.BlockSpec((N_CHIP, None, K, n), lambda m: (0, l, 0, 0)))],
               [_sds((T, K), F32)], [pl.BlockSpec((tm, K), lambda m: (m, 0))], (T // tm,), (1, 1),
               inner=(N_CHIP, lambda ref, i: ref[:, i * n:(i + 1) * n], lambda ref, i: ref[i]))[0]


def _bwd_rows(name, dy, w4, l):
    T, N = dy.shape
    k = w4.shape[-2]
    tm = _tile(T, ROWS_STREAMED)
    return _mm(name, [(dy, w4, pl.BlockSpec((tm, N), lambda i, m: (m, 0)),
                       pl.BlockSpec((None, None, k, N), lambda i, m: (i, l, 0, 0)))],
               [_sds((T, N_CHIP * k), F32)], [pl.BlockSpec((tm, k), lambda i, m: (m, i))],
               (N_CHIP, T // tm), (1, 1))[0]


def _ffn_down_bwd(name, dy, wd4, l, gate, up):
    T, N = dy.shape
    k = wd4.shape[-2]
    tm = _tile(T, ROWS_STREAMED)

    def post(accs, ex, outs):
        da = accs[0]
        g = ex[0][...]
        u = ex[1][...]
        sg = _sig(g)
        outs[0][...] = (da * u * (sg * (1.0 + g * (1.0 - sg)))).astype(BF16)
        outs[1][...] = (da * (g * sg)).astype(BF16)

    so = pl.BlockSpec((None, tm, k), lambda i, m: (i, m, 0))
    return _mm(name, [(dy, wd4, pl.BlockSpec((tm, N), lambda i, m: (m, 0)),
                       pl.BlockSpec((None, None, k, N), lambda i, m: (i, l, 0, 0)))],
               [_sds((N_CHIP, T, k), BF16)] * 2, [so, so], (N_CHIP, T // tm), (1, 1), post=post,
               extras=[gate, up], extra_specs=[so, so])


def _ffn_up_bwd(name, dgate, dup, wg4, wu4, l):
    _, T, n = dgate.shape
    K = wg4.shape[-2]
    tm = _tile(T, 512)
    sa = pl.BlockSpec((N_CHIP, tm, n), lambda m: (0, m, 0))
    sw = pl.BlockSpec((N_CHIP, None, K, n), lambda m: (0, l, 0, 0))
    pick = lambda ref, i: ref[i]
    return _mm(name, [(dgate, wg4, sa, sw), (dup, wu4, sa, sw)],
               [_sds((T, K), F32)], [pl.BlockSpec((tm, K), lambda m: (m, 0))], (T // tm,), (1, 1),
               inner=(N_CHIP, pick, pick))[0]


def _wgrad(name, a, a_kind, b, b_kind, gbuf, l):
    r, cdim = gbuf.shape[-2:]
    T = a.shape[-2]
    tt = _tile(T, ROWS_STREAMED)

    def spec(kind, w):
        if kind == "full":
            return pl.BlockSpec((tt, w), lambda i, t: (t, 0))
        if kind == "cols":
            return pl.BlockSpec((tt, w), lambda i, t: (t, i))
        return pl.BlockSpec((None, tt, w), lambda i, t: (i, t, 0))

    def post(accs, ex, outs):
        outs[0][...] = accs[0]

    return _mm(name, [(a, b, spec(a_kind, r), spec(b_kind, cdim))], [_sds(gbuf.shape, F32)],
               [pl.BlockSpec((None, None, r, cdim), lambda i, t: (i, l, 0, 0))], (N_CHIP, T // tt), (0, 0),
               red_axis=1, post=post, extras=[gbuf], extra_specs=[pl.BlockSpec(memory_space=pl.ANY)],
               aliases={2: 0})[0]


def _vec_spec(*idx):
    return pl.BlockSpec((None,) * len(idx) + (1, D_MODEL), lambda m: tuple(idx) + (0, 0))


def _normmod(name, x, gain3, l, mod, ksc, ksh):
    T = x.shape[0]
    tm = _tile(T, 512)

    def kern(x_ref, g_ref, sc_ref, sh_ref, h_ref):
        xv = x_ref[...]
        r = lax.rsqrt(jnp.mean(xv * xv, axis=-1, keepdims=True) + EPS)
        h = (xv * r) * g_ref[...]
        h_ref[...] = (h * (1.0 + sc_ref[...]) + sh_ref[...]).astype(BF16)

    row = pl.BlockSpec((tm, D_MODEL), lambda m: (m, 0))
    return pl.pallas_call(kern, name=name, grid=(T // tm,),
                          in_specs=[row, _vec_spec(l), _vec_spec(l, ksc), _vec_spec(l, ksh)], out_specs=row,
                          out_shape=_sds((T, D_MODEL), BF16), compiler_params=_cparams("parallel"))(x, gain3, mod, mod)


def _normmod_bwd(name, x, dh, dres, gain3, l, mod, ksc, gate=None):
    T = x.shape[0]
    tm = _tile(T, 512)
    nt = T // tm

    def kern(x_ref, dh_ref, dres_ref, g_ref, sc_ref, *rest):
        if gate:
            y_ref, gv_ref, dx_ref, st_ref, dyg_ref = rest
        else:
            dx_ref, st_ref = rest
        m = pl.program_id(0)

        @pl.when(m == 0)
        def _():
            st_ref[...] = jnp.zeros_like(st_ref)

        xv = x_ref[...]
        dhv = dh_ref[...]
        r = lax.rsqrt(jnp.mean(xv * xv, axis=-1, keepdims=True) + EPS)
        xh = xv * r
        wv = g_ref[...] * (1.0 + sc_ref[...])
        dxh = dhv * wv
        dxv = dres_ref[...] + r * (dxh - xh * jnp.mean(dxh * xh, axis=-1, keepdims=True))
        dx_ref[...] = dxv
        st_ref[0:1, :] += jnp.sum(dhv, axis=0, keepdims=True)
        st_ref[1:2, :] += jnp.sum(dhv * xh, axis=0, keepdims=True)
        if gate:
            dyg_ref[...] = (dxv * gv_ref[...]).astype(BF16)
            st_ref[3:4, :] += jnp.sum(dxv * y_ref[...], axis=0, keepdims=True)

        @pl.when(m == nt - 1)
        def _():
            dw = st_ref[1:2, :]
            st_ref[2:3, :] = dw * (1.0 + sc_ref[...])
            st_ref[1:2, :] = dw * g_ref[...]

    row = pl.BlockSpec((tm, D_MODEL), lambda m: (m, 0))
    stat = pl.BlockSpec((8, D_MODEL), lambda m: (0, 0))
    ins, in_specs = [x, dh, dres, gain3, mod], [row, row, row, _vec_spec(l), _vec_spec(l, ksc)]
    out_specs, out_shape = [row, stat], [_sds((T, D_MODEL), F32), _sds((8, D_MODEL), F32)]
    if gate:
        ins += [gate[0], mod]
        in_specs += [row, _vec_spec(gate[1], gate[2])]
        out_specs.append(row)
        out_shape.append(_sds((T, D_MODEL), BF16))
    return pl.pallas_call(kern, name=name, grid=(nt,), in_specs=in_specs, out_specs=out_specs, out_shape=out_shape,
                          compiler_params=_cparams("arbitrary"))(*ins)


def _gate_bwd(name, dxn, y, mod, l, kg):
    T = dxn.shape[0]
    tm = _tile(T, 512)

    def kern(d_ref, y_ref, g_ref, dy_ref, st_ref):
        @pl.when(pl.program_id(0) == 0)
        def _():
            st_ref[...] = jnp.zeros_like(st_ref)

        dv = d_ref[...]
        dy_ref[...] = (dv * g_ref[...]).astype(BF16)
        st_ref[0:1, :] += jnp.sum(dv * y_ref[...], axis=0, keepdims=True)

    row = pl.BlockSpec((tm, D_MODEL), lambda m: (m, 0))
    return pl.pallas_call(kern, name=name, grid=(T // tm,), in_specs=[row, row, _vec_spec(l, kg)],
                          out_specs=[row, pl.BlockSpec((8, D_MODEL), lambda m: (0, 0))],
                          out_shape=[_sds((T, D_MODEL), BF16), _sds((8, D_MODEL), F32)],
                          compiler_params=_cparams("arbitrary"))(dxn, y, mod)


def _loss_head(y, target):
    T = y.shape[0]
    tm = _tile(T, 512)

    def kern(y_ref, t_ref, dy_ref, acc_ref):
        @pl.when(pl.program_id(0) == 0)
        def _():
            acc_ref[...] = jnp.zeros_like(acc_ref)

        e = y_ref[...] - t_ref[...]
        dy_ref[...] = e * (1.0 / D_MODEL)
        s = jnp.sum(jnp.sum(e * e, axis=-1, keepdims=True), axis=0, keepdims=True)
        acc_ref[...] += s * (0.5 / D_MODEL)

    row = pl.BlockSpec((tm, D_MODEL), lambda m: (m, 0))
    return pl.pallas_call(kern, name="loss_head", grid=(T // tm,), in_specs=[row, row],
                          out_specs=[row, pl.BlockSpec((8, LANE), lambda m: (0, 0))],
                          out_shape=[_sds((T, D_MODEL), F32), _sds((8, LANE), F32)],
                          compiler_params=_cparams("arbitrary"))(y, target)


def _row_chunks(T):
    rc = min(ROW_CHUNK, T)
    return [(r * rc, rc) for r in range(T // rc)]


def _conv_fwd(name, proj, conv_w):
    T = proj.shape[0]
    nblk = CONV_DIM // LANE

    def kern(b_ref, c_ref, u_ref, w_ref, a_ref, zs):
        zs[0:8, :] = jnp.zeros((8, LANE), F32)
        for r0, rc in _row_chunks(T):
            zs[8 + r0:8 + r0 + rc, :] = c_ref[r0:r0 + rc, :] * u_ref[r0:r0 + rc, :]
        w0, w1, w2 = w_ref[0:1, :], w_ref[1:2, :], w_ref[2:3, :]
        for r0, rc in _row_chunks(T):
            yc = w2 * zs[8 + r0:8 + r0 + rc, :] + w1 * zs[7 + r0:7 + r0 + rc, :] + w0 * zs[6 + r0:6 + r0 + rc, :]
            a_ref[r0:r0 + rc, :] = (b_ref[r0:r0 + rc, :] * yc).astype(BF16)

    col = lambda p: pl.BlockSpec((T, LANE), lambda cb: (0, p * nblk + cb))
    return pl.pallas_call(kern, name=name, grid=(nblk,),
                          in_specs=[col(0), col(1), col(2), pl.BlockSpec((CONV_WIDTH, LANE), lambda cb: (0, cb))],
                          out_specs=pl.BlockSpec((T, LANE), lambda cb: (0, cb)),
                          out_shape=_sds((T, CONV_DIM), BF16), scratch_shapes=[pltpu.VMEM((T + 8, LANE), F32)],
                          compiler_params=_cparams("parallel"))(proj, proj, proj, conv_w)


def _conv_bwd(name, proj, conv_w, dcat):
    T = proj.shape[0]
    nblk = CONV_DIM // LANE

    def kern(b_ref, c_ref, u_ref, w_ref, da_ref, db_ref, dc_ref, du_ref, dw_ref, zs, ds):
        zs[0:8, :] = jnp.zeros((8, LANE), F32)
        ds[T:T + 8, :] = jnp.zeros((8, LANE), F32)
        for r0, rc in _row_chunks(T):
            zs[8 + r0:8 + r0 + rc, :] = c_ref[r0:r0 + rc, :] * u_ref[r0:r0 + rc, :]
        w0, w1, w2 = w_ref[0:1, :], w_ref[1:2, :], w_ref[2:3, :]
        acc = [jnp.zeros((1, LANE), F32) for _ in range(3)]
        for r0, rc in _row_chunks(T):
            z0 = zs[8 + r0:8 + r0 + rc, :]
            z1 = zs[7 + r0:7 + r0 + rc, :]
            z2 = zs[6 + r0:6 + r0 + rc, :]
            da = da_ref[r0:r0 + rc, :]
            db_ref[r0:r0 + rc, :] = (da * (w2 * z0 + w1 * z1 + w0 * z2)).astype(BF16)
            dyc = da * b_ref[r0:r0 + rc, :]
            ds[r0:r0 + rc, :] = dyc
            acc[2] = acc[2] + jnp.sum(dyc * z0, axis=0, keepdims=True)
            acc[1] = acc[1] + jnp.sum(dyc * z1, axis=0, keepdims=True)
            acc[0] = acc[0] + jnp.sum(dyc * z2, axis=0, keepdims=True)
        dw_ref[...] = jnp.zeros_like(dw_ref)
        for k in range(3):
            dw_ref[k:k + 1, :] = acc[k]
        for r0, rc in _row_chunks(T):
            dz = w2 * ds[r0:r0 + rc, :] + w1 * ds[r0 + 1:r0 + 1 + rc, :] + w0 * ds[r0 + 2:r0 + 2 + rc, :]
            dc_ref[r0:r0 + rc, :] = (dz * u_ref[r0:r0 + rc, :]).astype(BF16)
            du_ref[r0:r0 + rc, :] = (dz * c_ref[r0:r0 + rc, :]).astype(BF16)

    col = lambda p: pl.BlockSpec((T, LANE), lambda cb: (0, p * nblk + cb))
    out = pl.BlockSpec((T, LANE), lambda cb: (0, cb))
    return pl.pallas_call(kern, name=name, grid=(nblk,),
                          in_specs=[col(0), col(1), col(2), pl.BlockSpec((CONV_WIDTH, LANE), lambda cb: (0, cb)), out],
                          out_specs=[out, out, out, pl.BlockSpec((8, LANE), lambda cb: (0, cb))],
                          out_shape=[_sds((T, CONV_DIM), BF16)] * 3 + [_sds((8, CONV_DIM), F32)],
                          scratch_shapes=[pltpu.VMEM((T + 8, LANE), F32), pltpu.VMEM((T + 8, LANE), F32)],
                          compiler_params=_cparams("parallel"))(proj, proj, proj, conv_w, dcat)


_Q0, _K0, _V0, _G0 = 3 * CONV_DIM // LANE, (3 * CONV_DIM + RET_DIM) // LANE, (3 * CONV_DIM + 2 * RET_DIM) // LANE, \
    (3 * CONV_DIM + 3 * RET_DIM) // LANE


def _ret_tables(B, lg1):
    ti = lax.broadcasted_iota(jnp.int32, (B, B), 0)
    si = lax.broadcasted_iota(jnp.int32, (B, B), 1)
    dist = jnp.abs(ti - si).astype(F32)
    shift = RET_CHUNK.bit_length() - 1
    dmat = jnp.where((si >> shift) <= (ti >> shift), jnp.exp(dist * lg1), 0.0)
    tcol = lax.broadcasted_iota(jnp.int32, (B, 1), 0).astype(F32)
    qdec = jnp.exp((tcol + 1.0) * lg1)
    kdec = jnp.exp((B - 1.0 - tcol) * lg1)
    bdec = jnp.exp(float(B) * lg1)
    return dmat, qdec, kdec, bdec


def _retention_fwd(name, proj, cosf, sinf, gr, lgt):
    T = proj.shape[0]
    B = min(RET_BLOCK, T)
    nb = T // B

    def kern(q_ref, k_ref, v_ref, g_ref, cos_ref, sin_ref, gr_ref, lg_ref, r_ref, o_ref, st_ref, S):
        @pl.when(pl.program_id(1) == 0)
        def _():
            S[...] = jnp.zeros_like(S)

        cosv, sinv = cos_ref[...], sin_ref[...]
        rot = lambda a: a * cosv + pltpu.roll(a, HEAD // 2, 1) * sinv
        qr = rot(q_ref[...])
        kr = rot(k_ref[...]) * QK_SCALE
        v = v_ref[...]
        dmat, qdec, kdec, bdec = _ret_tables(B, lg_ref[0:1, 0:1])
        sv = S[...]
        st_ref[...] = sv
        pd = _dot(qr, kr, 1, 1) * dmat
        o = _dot(pd, v, 1, 0) + _dot(qr * qdec, sv, 1, 0)
        S[...] = bdec * sv + _dot(kr * kdec, v, 0, 0)
        o_ref[...] = o
        rs = lax.rsqrt(jnp.mean(o * o, axis=-1, keepdims=True) + EPS)
        g = g_ref[...]
        r_ref[...] = (g * _sig(g) * (o * rs * gr_ref[...])).astype(BF16)

    col = lambda c0: pl.BlockSpec((B, HEAD), lambda h, i: (i, c0 + h))
    tab = pl.BlockSpec((B, HEAD), lambda h, i: (i, 0))
    outc = pl.BlockSpec((B, HEAD), lambda h, i: (i, h))
    return pl.pallas_call(
        kern, name=name, grid=(RET_HEADS, nb),
        in_specs=[col(_Q0), col(_K0), col(_V0), col(_G0), tab, tab, pl.BlockSpec((1, HEAD), lambda h, i: (0, h)),
                  pl.BlockSpec((None, 1, LANE), lambda h, i: (h, 0, 0))],
        out_specs=[outc, outc, pl.BlockSpec((None, None, HEAD, HEAD), lambda h, i: (h, i, 0, 0))],
        out_shape=[_sds((T, RET_DIM), BF16), _sds((T, RET_DIM), F32), _sds((RET_HEADS, nb, HEAD, HEAD), F32)],
        scratch_shapes=[pltpu.VMEM((HEAD, HEAD), F32)],
        compiler_params=_cparams("parallel", "arbitrary"))(proj, proj, proj, proj, cosf, sinf, gr, lgt)


def _retention_bwd(name, proj, oraw, states, dcat, cosf, sinf, gr, lgt):
    T = proj.shape[0]
    B = min(RET_BLOCK, T)
    nb = T // B

    def kern(q_ref, k_ref, v_ref, g_ref, o_ref, dr_ref, st_ref, cos_ref, sin_ref, gr_ref, lg_ref,
             dq_ref, dk_ref, dv_ref, dg_ref, dgr_ref, dS):
        @pl.when(pl.program_id(1) == 0)
        def _():
            dS[...] = jnp.zeros_like(dS)
            dgr_ref[...] = jnp.zeros_like(dgr_ref)

        cosv, sinv = cos_ref[...], sin_ref[...]
        rot = lambda a: a * cosv + pltpu.roll(a, HEAD // 2, 1) * sinv
        rot_t = lambda a: a * cosv + pltpu.roll(a * sinv, HEAD // 2, 1)
        qr = rot(q_ref[...])
        kr = rot(k_ref[...]) * QK_SCALE
        v = v_ref[...]
        dmat, qdec, kdec, bdec = _ret_tables(B, lg_ref[0:1, 0:1])
        o = o_ref[...]
        rs = lax.rsqrt(jnp.mean(o * o, axis=-1, keepdims=True) + EPS)
        xh = o * rs
        g = g_ref[...]
        sg = _sig(g)
        grv = gr_ref[...]
        dr = dr_ref[...]
        dn = dr * (g * sg)
        dg_ref[...] = (dr * (xh * grv) * (sg * (1.0 + g * (1.0 - sg)))).astype(BF16)
        dgr_ref[0:1, :] += jnp.sum(dn * xh, axis=0, keepdims=True)
        dxh = dn * grv
        do = rs * (dxh - xh * jnp.mean(dxh * xh, axis=-1, keepdims=True))
        sp = st_ref[...]
        dsv = dS[...]
        pd = _dot(qr, kr, 1, 1) * dmat
        dp = _dot(do, v, 1, 1) * dmat
        dv_ref[...] = (_dot(pd, do, 0, 0) + _dot(kr * kdec, dsv, 1, 0)).astype(BF16)
        dqr = _dot(dp, kr, 1, 0) + _dot(do, sp, 1, 1) * qdec
        dkr = _dot(dp, qr, 0, 0) + _dot(v, dsv, 1, 1) * kdec
        dS[...] = bdec * dsv + _dot(qr * qdec, do, 0, 0)
        dq_ref[...] = rot_t(dqr).astype(BF16)
        dk_ref[...] = (rot_t(dkr) * QK_SCALE).astype(BF16)

    rev = lambda i: nb - 1 - i
    col = lambda c0: pl.BlockSpec((B, HEAD), lambda h, i: (rev(i), c0 + h))
    tab = pl.BlockSpec((B, HEAD), lambda h, i: (rev(i), 0))
    outc = pl.BlockSpec((B, HEAD), lambda h, i: (rev(i), h))
    return pl.pallas_call(
        kern, name=name, grid=(RET_HEADS, nb),
        in_specs=[col(_Q0), col(_K0), col(_V0), col(_G0), outc,
                  pl.BlockSpec((B, HEAD), lambda h, i: (rev(i), CONV_DIM // LANE + h)),
                  pl.BlockSpec((None, None, HEAD, HEAD), lambda h, i: (h, rev(i), 0, 0)), tab, tab,
                  pl.BlockSpec((1, HEAD), lambda h, i: (0, h)), pl.BlockSpec((None, 1, LANE), lambda h, i: (h, 0, 0))],
        out_specs=[outc, outc, outc, outc, pl.BlockSpec((8, HEAD), lambda h, i: (0, h))],
        out_shape=[_sds((T, RET_DIM), BF16)] * 4 + [_sds((8, RET_DIM), F32)],
        scratch_shapes=[pltpu.VMEM((HEAD, HEAD), F32)],
        compiler_params=_cparams("parallel", "arbitrary"))(proj, proj, proj, proj, oraw, dcat, states, cosf, sinf, gr, lgt)


def _strict_upper(q_rows):
    r = lax.broadcasted_iota(jnp.int32, (SB_BLOCK, SB_BLOCK), 0)
    c = lax.broadcasted_iota(jnp.int32, (SB_BLOCK, SB_BLOCK), 1)
    rq = lax.broadcasted_iota(jnp.int32, (q_rows, SB_BLOCK), 0)
    cq = lax.broadcasted_iota(jnp.int32, (q_rows, SB_BLOCK), 1)
    return (r > c).astype(BF16), cq - rq


def _suffix_sum(vals, tri):
    hi = vals.astype(BF16)
    lo = (vals - hi.astype(F32)).astype(BF16)
    dn = (((1,), (0,)), ((), ()))
    return lax.dot_general(hi, tri, dn, preferred_element_type=F32) + lax.dot_general(lo, tri, dn, preferred_element_type=F32)


def _sb_scores(qi, kj, tri, col_minus_row, mask_off):
    z = _dot(qi, kj, 1, 1) * QK_SCALE
    lb = jnp.minimum(z, 0.0) - jnp.log(1.0 + jnp.exp(-jnp.abs(z)))
    if mask_off is not None:
        valid = col_minus_row < mask_off
        lk = jnp.where(valid, lb - z, 0.0)
        w_loc = jnp.where(valid, jnp.exp(lb + _suffix_sum(lk, tri)), 0.0)
        return lb, lk, valid, w_loc
    lk = lb - z
    return lb, lk, None, jnp.exp(lb + _suffix_sum(lk, tri))


def _sb_walk(i, ratio, tile, st):
    for d in reversed(range(ratio)):
        st = tile(ratio * i + d, -SB_BLOCK * d, st, SB_BLOCK * d)
    n_free = ratio * i

    def pair(p, s):
        j = n_free - 1 - 2 * p
        return tile(j - 1, None, tile(j, None, s))

    st = lax.fori_loop(0, n_free // 2, pair, st)
    return lax.fori_loop(0, n_free % 2, lambda _, s: tile(0, None, s), st)


def _head_norm_rows(src, gain, dst, T):
    for r0, rc in _row_chunks(T):
        a = src[r0:r0 + rc, :]
        r = lax.rsqrt(jnp.mean(a * a, axis=-1, keepdims=True) + EPS)
        dst[r0:r0 + rc, :] = (a * r * gain).astype(BF16)


def _sb_fwd(name, qkv, gq, gk, gather=None):
    T = qkv.shape[0]
    qr = min(SB_QROWS, T)
    nq, ratio = T // qr, qr // SB_BLOCK
    n_g = len(gather[0]) if gather else 0

    def kern(q_ref, k_ref, v_ref, gq_ref, gk_ref, *rest):
        shard_refs, rest = rest[:n_g], rest[2 * n_g:]
        o_ref, o32_ref, rest = rest[0], rest[1], rest[2:]
        w4_refs, rest = rest[:n_g], rest[n_g:]
        qn, kn, vb, sems = rest[0], rest[1], rest[2], rest[3:]
        head = pl.program_id(0)
        if gather:
            @pl.when(head == 0)
            def _():
                _gather_start(shard_refs, w4_refs, sems, gather[2])

        _head_norm_rows(q_ref, gq_ref[...], qn, T)
        _head_norm_rows(k_ref, gk_ref[...], kn, T)
        for r0, rc in _row_chunks(T):
            vb[r0:r0 + rc, :] = v_ref[r0:r0 + rc, :].astype(BF16)
        tri, diag_mask = _strict_upper(qr)

        def qblock(i, _):
            rows_i = pl.ds(pl.multiple_of(i * qr, qr), qr)
            o32_ref[rows_i, :] = jnp.zeros((qr, HEAD), F32)

            def tile(j, mask_off, car, row0=0):
                rows_j = pl.ds(pl.multiple_of(j * SB_BLOCK, SB_BLOCK), SB_BLOCK)
                rows_live = pl.ds(pl.multiple_of(i * qr + row0, SB_BLOCK), qr - row0)
                _, lk, _, w = _sb_scores(qn[rows_live, :], kn[rows_j, :], tri, diag_mask[row0:], mask_off)
                w_hi = w.astype(BF16)
                w_lo = (w - w_hi.astype(F32)).astype(BF16)
                vj = vb[rows_j, :]
                o32_ref[rows_live, :] += jnp.exp(car[row0:]) * (_dot(w_hi, vj, 1, 0) + _dot(w_lo, vj, 1, 0))
                car_new = car[row0:] + jnp.sum(lk, axis=-1, keepdims=True)
                return jnp.concatenate([car[:row0], car_new]) if row0 else car_new

            _sb_walk(i, ratio, tile, jnp.zeros((qr, 1), F32))
            o_ref[rows_i, :] = o32_ref[rows_i, :].astype(BF16)
            return 0

        lax.fori_loop(0, nq, qblock, 0)
        if gather:
            @pl.when(head == SB_HEADS - 2)
            def _():
                _gather_forward(shard_refs, w4_refs, sems, gather[2])

            @pl.when(head == SB_HEADS - 1)
            def _():
                _gather_finish(shard_refs, w4_refs, sems, gather[2])

    col = lambda c0: pl.BlockSpec((T, HEAD), lambda h: (0, c0 + h))
    vec = pl.BlockSpec((1, HEAD), lambda h: (0, 0))
    out = pl.BlockSpec((T, HEAD), lambda h: (0, h))
    extra_in = list(gather[0]) + list(gather[1]) if gather else []
    res = pl.pallas_call(kern, name=name, grid=(SB_HEADS,),
                         in_specs=[col(0), col(SB_HEADS), col(2 * SB_HEADS), vec, vec] + [_ANY] * (2 * n_g),
                         out_specs=[out, out] + [_ANY] * n_g,
                         out_shape=[_sds((T, D_MODEL), BF16), _sds((T, D_MODEL), F32)]
                         + [_sds(w.shape, w.dtype) for w in (gather[1] if gather else [])],
                         scratch_shapes=[pltpu.VMEM((T, HEAD), BF16)] * 3 + (_gather_sems(n_g) if gather else []),
                         input_output_aliases={5 + n_g + t: 2 + t for t in range(n_g)},
                         compiler_params=_cparams("arbitrary" if gather else "parallel"))(qkv, qkv, qkv, gq, gk, *extra_in)
    return res[0], res[1], list(res[2:])


def _sb_bwd(name, qkv, gq, gk, o32, dcat, scatter=None):
    T = qkv.shape[0]
    qr = min(SB_QROWS, T)
    nq, ratio = T // qr, qr // SB_BLOCK
    n_s = len(scatter[0]) if scatter else 0

    def kern(q_ref, k_ref, v_ref, gq_ref, gk_ref, o_ref, do_ref, *rest):
        ps_refs, rest = rest[:n_s], rest[n_s:]
        dq_ref, dk_ref, dv_ref, dgq_ref, dgk_ref = rest[:5]
        slot_refs, rest = rest[5:5 + n_s], rest[5 + n_s:]
        qn, kn, vb, dqn, dkn, dvv, dob_s = rest[:7]
        sems = rest[7:]

        @pl.when(pl.program_id(0) == 0)
        def _():
            dgq_ref[...] = jnp.zeros_like(dgq_ref)
            dgk_ref[...] = jnp.zeros_like(dgk_ref)
            if scatter:
                _scatter_start(ps_refs, slot_refs, sems, scatter[1])

        _head_norm_rows(q_ref, gq_ref[...], qn, T)
        _head_norm_rows(k_ref, gk_ref[...], kn, T)
        for r0, rc in _row_chunks(T):
            vb[r0:r0 + rc, :] = v_ref[r0:r0 + rc, :].astype(BF16)
            dkn[r0:r0 + rc, :] = jnp.zeros((rc, HEAD), F32)
            dvv[r0:r0 + rc, :] = jnp.zeros((rc, HEAD), F32)
        tri, diag_mask = _strict_upper(qr)

        def qblock(i, _):
            rows_i = pl.ds(pl.multiple_of(i * qr, qr), qr)
            dob = do_ref[rows_i, :].astype(BF16)
            dob_s[...] = dob
            dqn[rows_i, :] = jnp.zeros((qr, HEAD), F32)
            etot = jnp.sum(dob.astype(F32) * o_ref[rows_i, :], axis=-1, keepdims=True)

            def tile(j, mask_off, st, row0=0):
                car, ecar = st
                rows_j = pl.ds(pl.multiple_of(j * SB_BLOCK, SB_BLOCK), SB_BLOCK)
                rows_live = pl.ds(pl.multiple_of(i * qr + row0, SB_BLOCK), qr - row0)
                kj = kn[rows_j, :]
                vj = vb[rows_j, :]
                q_live, do_live = qn[rows_live, :], dob_s[row0:, :]
                lb, lk, valid, w_loc = _sb_scores(q_live, kj, tri, diag_mask[row0:], mask_off)
                w = w_loc * jnp.exp(car[row0:])
                e = w * _dot(do_live, vj, 1, 1)
                suff = _suffix_sum(e, tri) + e + ecar[row0:]
                sig = jnp.exp(lb)
                dz = (e * (1.0 - sig) - sig * (etot[row0:] - suff)) * QK_SCALE
                if mask_off is not None:
                    dz = jnp.where(valid, dz, 0.0)
                dzb = dz.astype(BF16)
                dkn[rows_j, :] += _dot(dzb, q_live, 0, 0)
                dvv[rows_j, :] += _dot(w, do_live, 0, 0)
                dqn[rows_live, :] += _dot(dzb, kj, 1, 0)
                new = (car[row0:] + jnp.sum(lk, axis=-1, keepdims=True), ecar[row0:] + jnp.sum(e, axis=-1, keepdims=True))
                if row0:
                    return tuple(jnp.concatenate([old[:row0], upd]) for old, upd in zip(st, new))
                return new

            zcol = jnp.zeros((qr, 1), F32)
            _sb_walk(i, ratio, tile, (zcol, zcol))
            return 0

        lax.fori_loop(0, nq, qblock, 0)

        def norm_bwd(src, gain, dnorm, dst, dgain):
            tot = jnp.zeros((1, HEAD), F32)
            for r0, rc in _row_chunks(T):
                a = src[r0:r0 + rc, :]
                r = lax.rsqrt(jnp.mean(a * a, axis=-1, keepdims=True) + EPS)
                xh = a * r
                dn = dnorm[r0:r0 + rc, :]
                tot = tot + jnp.sum(dn * xh, axis=0, keepdims=True)
                dxh = dn * gain
                dst[r0:r0 + rc, :] = (r * (dxh - xh * jnp.mean(dxh * xh, axis=-1, keepdims=True))).astype(BF16)
            dgain[0:1, :] += tot

        norm_bwd(q_ref, gq_ref[...], dqn, dq_ref, dgq_ref)
        norm_bwd(k_ref, gk_ref[...], dkn, dk_ref, dgk_ref)
        for r0, rc in _row_chunks(T):
            dv_ref[r0:r0 + rc, :] = dvv[r0:r0 + rc, :].astype(BF16)
        if scatter:
            @pl.when(pl.program_id(0) == SB_HEADS - 1)
            def _():
                _scatter_finish(ps_refs, slot_refs, sems, scatter[1])

    col = lambda c0: pl.BlockSpec((T, HEAD), lambda h: (0, c0 + h))
    vec = pl.BlockSpec((1, HEAD), lambda h: (0, 0))
    out = pl.BlockSpec((T, HEAD), lambda h: (0, h))
    st = pl.BlockSpec((8, HEAD), lambda h: (0, 0))
    ps = list(scatter[0]) if scatter else []
    res = pl.pallas_call(kern, name=name, grid=(SB_HEADS,),
                         in_specs=[col(0), col(SB_HEADS), col(2 * SB_HEADS), vec, vec, out, out] + [_ANY] * n_s,
                         out_specs=[out, out, out, st, st] + [_ANY] * n_s,
                         out_shape=[_sds((T, D_MODEL), BF16)] * 3 + [_sds((8, HEAD), F32)] * 2
                         + [_sds(p.shape, p.dtype) for p in ps],
                         scratch_shapes=[pltpu.VMEM((T, HEAD), BF16)] * 3 + [pltpu.VMEM((T, HEAD), F32)] * 3
                         + [pltpu.VMEM((qr, HEAD), BF16)] + (_scatter_sems(n_s) if scatter else []),
                         compiler_params=_cparams("arbitrary"))(qkv, qkv, qkv, gq, gk, o32, dcat, *ps)
    return tuple(res[:5]) + (list(res[5:]),)


def _ada_fwd(c_all, ada_w, ada_b_cols):
    L, K, n = ada_w.shape

    def kern(c_ref, w_ref, b_ref, o_ref):
        cv = c_ref[...]
        o_ref[...] = _dot(cv * _sig(cv), w_ref[...], 1, 0) + b_ref[...]

    return pl.pallas_call(kern, name="ada_fwd", grid=(L,),
                          in_specs=[pl.BlockSpec((N_DEV, K), lambda l: (0, 0)), pl.BlockSpec((None, K, n), lambda l: (l, 0, 0)),
                                    pl.BlockSpec((None, 1, n), lambda l: (l, 0, 0))],
                          out_specs=pl.BlockSpec((None, N_DEV, n), lambda l: (l, 0, 0)),
                          out_shape=_sds((L, N_DEV, n), F32), compiler_params=_cparams("parallel"))(c_all, ada_w, ada_b_cols)


def _ada_wgrad(c_all_t, dmod_cols):
    K = c_all_t.shape[0]
    L, _, n = dmod_cols.shape
    tk = 128

    def kern(c_ref, d_ref, o_ref):
        cv = c_ref[...]
        ca = cv * _sig(cv)
        acc = ca[:, 0:1] * d_ref[0:1, :]
        for b in range(1, N_DEV):
            acc = acc + ca[:, b:b + 1] * d_ref[b:b + 1, :]
        o_ref[...] = acc

    return pl.pallas_call(kern, name="ada_wgrad", grid=(L, K // tk),
                          in_specs=[pl.BlockSpec((tk, N_DEV), lambda l, m: (m, 0)), pl.BlockSpec((None, N_DEV, n), lambda l, m: (l, 0, 0))],
                          out_specs=pl.BlockSpec((None, tk, n), lambda l, m: (l, m, 0)),
                          out_shape=_sds((L, K, n), F32), compiler_params=_cparams("parallel", "parallel"))(c_all_t, dmod_cols)


def _sum_devices(g):
    _, R, n = g.shape

    def kern(g_ref, o_ref):
        acc = g_ref[0]
        for d in range(1, N_DEV):
            acc = acc + g_ref[d]
        o_ref[...] = acc

    return pl.pallas_call(kern, name="sum_devices", out_shape=_sds((R, n), F32))(g)


def _pair_sum(name, g, recv, half):
    n4, L, r, cdim = g.shape
    rows = (L // 2) * r
    tr = _tile(rows, 512)
    gv = g.reshape(n4, 2, rows, cdim)
    rv = recv.reshape(n4, rows, cdim)

    def kern(g_ref, r_ref, o_ref):
        o_ref[...] = (g_ref[...] + r_ref[...]).astype(BF16)

    out = pl.pallas_call(kern, name=name, grid=(n4, rows // tr),
                         in_specs=[pl.BlockSpec((None, None, tr, cdim), lambda j, m: (j, half, m, 0)),
                                   pl.BlockSpec((None, tr, cdim), lambda j, m: (j, m, 0))],
                         out_specs=pl.BlockSpec((None, tr, cdim), lambda j, m: (j, m, 0)),
                         out_shape=_sds((n4, rows, cdim), BF16),
                         compiler_params=_cparams("parallel", "parallel"))(gv, rv)
    return out.reshape(n4, L // 2, r, cdim)


def _chip_sum(name, recv, psum, pos_idx, half, total):
    n4, hl, r, cdim = recv.shape
    rows = hl * r
    tr = _tile(rows, 512)

    def kern(pos_ref, r0, r1, r2, r3, own_ref, t_ref, o_ref):
        me = pos_ref[0]
        own = own_ref[...].astype(F32)
        terms = [jnp.where(me == s, own, rr[...].astype(F32)) for s, rr in enumerate((r0, r1, r2, r3))]
        o_ref[...] = ((terms[0] + terms[1]) + terms[2]) + terms[3]

    def slot(s):
        return pl.BlockSpec((None, tr, cdim), lambda m, pos: (jnp.where(pos[0] == s, (s + 1) % n4, s), m, 0))

    gs = pltpu.PrefetchScalarGridSpec(
        num_scalar_prefetch=1, grid=(rows // tr,),
        in_specs=[slot(s) for s in range(n4)] + [pl.BlockSpec((None, tr, cdim), lambda m, pos: (pos[0], m, 0)), _ANY],
        out_specs=pl.BlockSpec((None, tr, cdim), lambda m, pos: (half, m, 0)))
    rv = recv.reshape(n4, rows, cdim)
    out = pl.pallas_call(kern, name=name, grid_spec=gs, out_shape=_sds((2, rows, cdim), F32),
                         input_output_aliases={6: 0}, compiler_params=_cparams("parallel"))(
        pos_idx, rv, rv, rv, rv, psum.reshape(n4, rows, cdim), total.reshape(2, rows, cdim))
    return out.reshape(2 * hl, r, cdim)


def _adamw(name, w, g, m, v):
    shape = w.shape
    cols = shape[-1]
    rows = int(np.prod(shape[:-1]))
    tr = _tile(rows, 512) if rows % 8 == 0 else rows
    c1 = 1.0 - ADAM_B1 ** ADAM_STEP
    c2 = 1.0 - ADAM_B2 ** ADAM_STEP

    def kern(w_ref, g_ref, m_ref, v_ref, d_ref, mo_ref, vo_ref):
        gv = g_ref[...]
        mn = ADAM_B1 * m_ref[...] + (1.0 - ADAM_B1) * gv
        vn = ADAM_B2 * v_ref[...] + (1.0 - ADAM_B2) * (gv * gv)
        mo_ref[...] = mn
        vo_ref[...] = vn
        d_ref[...] = -ADAM_LR * ((mn / c1) / (jnp.sqrt(vn / c2) + ADAM_EPS) + ADAM_WD * w_ref[...])

    blk = pl.BlockSpec((tr, cols), lambda i: (i, 0))
    outs = pl.pallas_call(kern, name=name, grid=(rows // tr,), in_specs=[blk] * 4, out_specs=[blk] * 3,
                          out_shape=[_sds((rows, cols), F32)] * 3, compiler_params=_cparams("parallel"))(
        *[a.reshape(rows, cols) for a in (w, g, m, v)])
    return tuple(o.reshape(shape) for o in outs)


def _rcopy(src, dst, ssem, rsem, dev):
    return pltpu.make_async_remote_copy(src_ref=src, dst_ref=dst, send_sem=ssem, recv_sem=rsem, device_id=dev,
                                        device_id_type=MESH)


def _gather8(name, blk):
    m_per, n = blk.shape

    def body(x_ref, out_ref, send_sems, recv_sems, local_sem):
        x, y, c = _position()
        me, sibling = (x, y, c), (x, y, 1 - c)
        chips = _other_chips(x, y)

        def rows(px, py, pc):
            return out_ref.at[pl.ds((4 * px + 2 * py + pc) * m_per, m_per), :]

        def copy(k, block, to, src=None):
            return _rcopy(rows(*block) if src is None else src, rows(*block), send_sems.at[k], recv_sems.at[k], to)

        mine = pltpu.make_async_copy(x_ref, rows(*me), local_sem)
        mine.start()
        first = [copy(0, me, sibling, src=x_ref)]
        first += [copy(1 + j, me, (*chip, c), src=x_ref) for j, chip in enumerate(chips)]
        for cp in first:
            cp.start()
        passed = [copy(4 + j, (*chip, c), sibling) for j, chip in enumerate(chips)]
        for j, chip in enumerate(chips):
            copy(1 + j, (*chip, c), me).wait_recv()
            passed[j].start()
        copy(0, sibling, me).wait_recv()
        for j, chip in enumerate(chips):
            copy(4 + j, (*chip, 1 - c), me).wait_recv()
        for cp in first + passed:
            cp.wait_send()
        mine.wait()

    return pl.pallas_call(body, name=name, out_shape=_sds((N_DEV * m_per, n), blk.dtype),
                          in_specs=[pl.BlockSpec(memory_space=pltpu.VMEM)], out_specs=pl.BlockSpec(memory_space=pltpu.VMEM),
                          scratch_shapes=[pltpu.SemaphoreType.DMA((7,)), pltpu.SemaphoreType.DMA((7,)), pltpu.SemaphoreType.DMA],
                          compiler_params=pltpu.CompilerParams(vmem_limit_bytes=VMEM_LIMIT_V7X))(blk)


_ANY = pl.BlockSpec(memory_space=pl.ANY)


def _gather_sems(n):
    return [pltpu.SemaphoreType.DMA((3 * n,))] * 4


def _gather_start(src, out, sems, half):
    send_sems, recv_sems = sems[0], sems[1]
    x, y, c = _position()
    chips = _other_chips(x, y)
    me_chip = 2 * x + y

    @pl.when(c == half)
    def _():
        for t in range(len(src)):
            hl = src[t].shape[0] // 2
            rows = pl.ds(half * hl, hl)
            for j, (px, py) in enumerate(chips):
                _rcopy(src[t].at[rows], out[t].at[me_chip, rows], send_sems.at[3 * t + j], recv_sems.at[3 * t + j],
                       (px, py, half)).start()


def _gather_forward(src, out, sems, half):
    send_sems, recv_sems, fsend_sems, frecv_sems = sems
    x, y, c = _position()
    chips = _other_chips(x, y)

    @pl.when(c == half)
    def _():
        for t in range(len(src)):
            hl = src[t].shape[0] // 2
            rows = pl.ds(half * hl, hl)
            for j, (px, py) in enumerate(chips):
                landed = out[t].at[2 * px + py, rows]
                _rcopy(landed, landed, send_sems.at[3 * t + j], recv_sems.at[3 * t + j], (px, py, half)).wait_recv()
                _rcopy(landed, landed, fsend_sems.at[3 * t + j], frecv_sems.at[3 * t + j], (x, y, 1 - half)).start()


def _gather_finish(src, out, sems, half):
    send_sems, recv_sems, fsend_sems, frecv_sems = sems
    x, y, c = _position()
    chips = _other_chips(x, y)
    me_chip = 2 * x + y

    @pl.when(c == half)
    def _():
        for t in range(len(src)):
            hl = src[t].shape[0] // 2
            rows = pl.ds(half * hl, hl)
            for j, (px, py) in enumerate(chips):
                landed = out[t].at[2 * px + py, rows]
                _rcopy(src[t].at[rows], out[t].at[me_chip, rows], send_sems.at[3 * t + j], recv_sems.at[3 * t + j],
                       (px, py, half)).wait_send()
                _rcopy(landed, landed, fsend_sems.at[3 * t + j], frecv_sems.at[3 * t + j], (x, y, 1 - half)).wait_send()

    @pl.when(c != half)
    def _():
        for t in range(len(src)):
            hl = src[t].shape[0] // 2
            rows = pl.ds(half * hl, hl)
            for j, (px, py) in enumerate(chips):
                landed = out[t].at[2 * px + py, rows]
                _rcopy(landed, landed, fsend_sems.at[3 * t + j], frecv_sems.at[3 * t + j], (x, y, half)).wait_recv()


def _gather_weights(shards, half):
    n = len(shards)

    def body(*refs):
        src, out, sems = refs[:n], refs[n:2 * n], refs[2 * n:]
        _gather_start(src, out, sems, half)
        _gather_forward(src, out, sems, half)
        _gather_finish(src, out, sems, half)

    return pl.pallas_call(body, name="gather_weights", out_shape=[_sds((N_CHIP,) + s.shape, s.dtype) for s in shards],
                          in_specs=[_ANY] * n, out_specs=[_ANY] * n, scratch_shapes=_gather_sems(n))(*shards)


def _place_own(name, w4, shard, pos_idx):
    n4, L, r, cdim = w4.shape
    rows = L * r
    tr = _tile(rows, 1024)

    def kern(pos_ref, s_ref, w_ref, o_ref):
        o_ref[...] = s_ref[...]

    gs = pltpu.PrefetchScalarGridSpec(
        num_scalar_prefetch=1, grid=(rows // tr,),
        in_specs=[pl.BlockSpec((tr, cdim), lambda m, pos: (m, 0)), _ANY],
        out_specs=pl.BlockSpec((None, tr, cdim), lambda m, pos: (pos[0], m, 0)))
    out = pl.pallas_call(kern, name=name, grid_spec=gs, out_shape=_sds((n4, rows, cdim), w4.dtype),
                         input_output_aliases={2: 0}, compiler_params=_cparams("parallel"))(
        pos_idx, shard.reshape(rows, cdim), w4.reshape(n4, rows, cdim))
    return out.reshape(w4.shape)


def _exchange_pair(grads, to_c):
    n = len(grads)

    def body(*refs):
        src, out = refs[:n], refs[n:2 * n]
        send_sems, recv_sems = refs[2 * n:]
        x, y, c = _position()
        sibling = (x, y, 1 - c)

        @pl.when(c != to_c)
        def _():
            cps = []
            for t in range(n):
                hl = src[t].shape[1] // 2
                cp = _rcopy(src[t].at[:, pl.ds(to_c * hl, hl)], out[t], send_sems.at[t], recv_sems.at[t], sibling)
                cp.start()
                cps.append(cp)
            for cp in cps:
                cp.wait_send()

        @pl.when(c == to_c)
        def _():
            for t in range(n):
                _rcopy(out[t], out[t], send_sems.at[t], recv_sems.at[t], sibling).wait_recv()

    sems = [pltpu.SemaphoreType.DMA((n,))] * 2
    return pl.pallas_call(body, name=f"exchange_pair_{to_c}",
                          out_shape=[_sds((g.shape[0], g.shape[1] // 2) + g.shape[2:], g.dtype) for g in grads],
                          in_specs=[_ANY] * n, out_specs=[_ANY] * n, scratch_shapes=sems)(*grads)


def _scatter_sems(n):
    return [pltpu.SemaphoreType.DMA((3 * n,))] * 2


def _scatter_start(src, out, sems, half):
    send_sems, recv_sems = sems
    x, y, c = _position()
    chips = _other_chips(x, y)
    me_chip = 2 * x + y

    @pl.when(c == half)
    def _():
        for t in range(len(src)):
            for j, (px, py) in enumerate(chips):
                _rcopy(src[t].at[2 * px + py], out[t].at[me_chip], send_sems.at[3 * t + j], recv_sems.at[3 * t + j],
                       (px, py, half)).start()


def _scatter_finish(src, out, sems, half):
    send_sems, recv_sems = sems
    x, y, c = _position()
    chips = _other_chips(x, y)
    me_chip = 2 * x + y

    @pl.when(c == half)
    def _():
        for t in range(len(src)):
            for j, (px, py) in enumerate(chips):
                slot = out[t].at[2 * px + py]
                _rcopy(slot, slot, send_sems.at[3 * t + j], recv_sems.at[3 * t + j], (px, py, half)).wait_recv()
        for t in range(len(src)):
            for j, (px, py) in enumerate(chips):
                _rcopy(src[t].at[2 * px + py], out[t].at[me_chip], send_sems.at[3 * t + j], recv_sems.at[3 * t + j],
                       (px, py, half)).wait_send()


def _scatter_chips(psums, half):
    n = len(psums)

    def body(*refs):
        src, out, sems = refs[:n], refs[n:2 * n], refs[2 * n:]
        _scatter_start(src, out, sems, half)
        _scatter_finish(src, out, sems, half)

    return pl.pallas_call(body, name="scatter_chips", out_shape=[_sds(p.shape, p.dtype) for p in psums],
                          in_specs=[_ANY] * n, out_specs=[_ANY] * n, scratch_shapes=_scatter_sems(n))(*psums)


def _share_halves(full):
    n = len(full)

    def body(*refs):
        out = refs[n:2 * n]
        send_sems, recv_sems = refs[2 * n:]
        x, y, c = _position()
        sibling = (x, y, 1 - c)
        cps = []
        for t in range(n):
            hl = out[t].shape[0] // 2
            mine = out[t].at[pl.ds(c * hl, hl)]
            cp = _rcopy(mine, mine, send_sems.at[t], recv_sems.at[t], sibling)
            cp.start()
            cps.append(cp)
        for t in range(n):
            hl = out[t].shape[0] // 2
            theirs = out[t].at[pl.ds((1 - c) * hl, hl)]
            _rcopy(theirs, theirs, send_sems.at[t], recv_sems.at[t], sibling).wait_recv()
        for cp in cps:
            cp.wait_send()

    sems = [pltpu.SemaphoreType.DMA((n,))] * 2
    return pl.pallas_call(body, name="share_halves", out_shape=[_sds(h.shape, h.dtype) for h in full],
                          in_specs=[_ANY] * n, out_specs=[_ANY] * n, scratch_shapes=sems,
                          input_output_aliases={t: t for t in range(n)})(*full)


def _rope_tables(T):
    inv_freq = 1.0 / (ROPE_THETA ** (jnp.arange(0, HEAD, 2, dtype=F32) / HEAD))
    ang = jnp.arange(T, dtype=F32)[:, None] * inv_freq[None, :]
    cos, sin = jnp.cos(ang), jnp.sin(ang)
    return jnp.concatenate([cos, cos], axis=-1), jnp.concatenate([-sin, sin], axis=-1)


def _decay_table():
    lg = np.log1p(-np.exp2(-5.0 - np.arange(RET_HEADS, dtype=np.float32))).astype(np.float32)
    return jnp.asarray(np.broadcast_to(lg[:, None, None], (RET_HEADS, 1, LANE)).copy())


def _local_step(x0, target, mod, W, G, norm_mix_g, norm_ffn_g, conv_full, ev_ret_norm_g, od_q_norm_g, od_k_norm_g,
                fused=None):
    T = x0.shape[0]
    KSH1, KSC1, KG1, KSH2, KSC2, KG2 = range(6)
    gain_mix = norm_mix_g.reshape(DEPTH, 1, D_MODEL)
    gain_ffn = norm_ffn_g.reshape(DEPTH, 1, D_MODEL)
    cosf, sinf = _rope_tables(T)
    lgt = _decay_table()

    saved = []
    xcur = x0
    h = _normmod("norm_mix_0", x0, gain_mix, 0, mod, KSC1, KSH1)
    for l in range(DEPTH):
        j = l // 2
        s = dict(x_in=xcur, h=h)
        to_ffn = (gain_ffn, l, l, KSC2, KSH2)
        if l % 2 == 0:
            proj = _proj_cols(f"ev_in_{l}", h, W["ev_w_in"], j)
            a = _conv_fwd(f"conv_{l}", proj, conv_full[j])
            r, oraw, states = _retention_fwd(f"ret_{l}", proj, cosf, sinf, ev_ret_norm_g[j].reshape(1, RET_DIM), lgt)
            cat = jnp.concatenate([a, r], axis=1)
            s.update(proj=proj, oraw=oraw, states=states, cat=cat)
            y, xmid, h2 = _out_proj(f"ev_out_{l}", cat, False, W["ev_w_out"], j, xcur, mod, l, KG1, to_ffn)
        else:
            qkv = _proj_cols(f"od_in_{l}", h, W["od_w_qkv"], j)
            gather = (fused["shards"], [W[k] for k in fused["names"]], 1) if fused is not None and l == 1 else None
            o, o32, w4s = _sb_fwd(f"sb_{l}", qkv, od_q_norm_g[j].reshape(1, HEAD), od_k_norm_g[j].reshape(1, HEAD), gather)
            if gather:
                W = dict(zip(fused["names"], w4s))
            s.update(qkv=qkv, cat=o, o32=o32)
            y, xmid, h2 = _out_proj(f"od_out_{l}", o, False, W["od_w_out"], j, xcur, mod, l, KG1, to_ffn)
        s.update(y1=y, x_mid=xmid)
        gate, up, act = _ffn_up(f"ffn_up_{l}", h2, W["ffn_w_gate"], W["ffn_w_up"], l)
        to_mix = (gain_mix, l + 1, l + 1, KSC1, KSH1) if l + 1 < DEPTH else None
        y2, xcur, *h_next = _out_proj(f"ffn_down_{l}", act, True, W["ffn_w_down"], l, xmid, mod, l, KG2, to_mix)
        h = h_next[0] if h_next else None
        s.update(h2=h2, gate=gate, up=up, act=act, y2=y2)
        saved.append(s)

    dy, lacc = _loss_head(xcur, target)

    dmod_rows = [None] * DEPTH
    d_mix = [None] * DEPTH
    d_ffn = [None] * DEPTH
    d_conv = [None] * 2
    d_ret = [None] * 2
    d_gq = [None] * 2
    d_gk = [None] * 2
    dx = dy
    reduce_up = None
    dyg, st_top = _gate_bwd("gate2_bwd_top", dx, saved[DEPTH - 1]["y2"], mod, DEPTH - 1, KG2)
    d_gate2 = st_top[0]
    for l in reversed(range(DEPTH)):
        j = l // 2
        s = saved[l]
        G["ffn_w_down"] = _wgrad(f"wg_down_{l}", s["act"], "stack", dyg, "full", G["ffn_w_down"], l)
        dgate, dup = _ffn_down_bwd(f"ffn_down_bwd_{l}", dyg, W["ffn_w_down"], l, s["gate"], s["up"])
        G["ffn_w_gate"] = _wgrad(f"wg_gate_{l}", s["h2"], "full", dgate, "stack", G["ffn_w_gate"], l)
        G["ffn_w_up"] = _wgrad(f"wg_up_{l}", s["h2"], "full", dup, "stack", G["ffn_w_up"], l)
        dh2 = _ffn_up_bwd(f"ffn_up_bwd_{l}", dgate, dup, W["ffn_w_gate"], W["ffn_w_up"], l)
        dxm, st_n2, dyg1 = _normmod_bwd(f"norm_ffn_bwd_{l}", s["x_mid"], dh2, dx, gain_ffn, l, mod, KSC2,
                                        gate=(s["y1"], l, KG1))
        if l % 2 == 0:
            G["ev_w_out"] = _wgrad(f"wg_evout_{l}", s["cat"], "cols", dyg1, "full", G["ev_w_out"], j)
            dcat = _bwd_rows(f"ev_out_bwd_{l}", dyg1, W["ev_w_out"], j)
            db, dcg, du, dwc = _conv_bwd(f"conv_bwd_{l}", s["proj"], conv_full[j], dcat)
            dq, dk, dv, dg, dgr = _retention_bwd(f"ret_bwd_{l}", s["proj"], s["oraw"], s["states"], dcat, cosf, sinf,
                                                 ev_ret_norm_g[j].reshape(1, RET_DIM), lgt)
            dproj = jnp.concatenate([db, dcg, du, dq, dk, dv, dg], axis=1)
            d_conv[j], d_ret[j] = dwc[:CONV_WIDTH], dgr[0]
            G["ev_w_in"] = _wgrad(f"wg_evin_{l}", s["h"], "full", dproj, "cols", G["ev_w_in"], j)
            dh = _bwd_cols(f"ev_in_bwd_{l}", dproj, W["ev_w_in"], j)
        else:
            G["od_w_out"] = _wgrad(f"wg_odout_{l}", s["cat"], "cols", dyg1, "full", G["od_w_out"], j)
            dcat = _bwd_rows(f"od_out_bwd_{l}", dyg1, W["od_w_out"], j)
            scatter = None
            if fused is not None and l == 1:
                g_now = [G[k] for k in fused["names"]]
                recv_up = _exchange_pair(g_now, 1)
                ps_up = [_pair_sum(f"pair_sum_up_{k}", g, r, 1) for k, g, r in zip(fused["names"], g_now, recv_up)]
                scatter = (ps_up, 1)
            dq, dk, dv, dgq, dgk, slots_up = _sb_bwd(f"sb_bwd_{l}", s["qkv"], od_q_norm_g[j].reshape(1, HEAD),
                                                     od_k_norm_g[j].reshape(1, HEAD), s["o32"], dcat, scatter)
            if scatter:
                reduce_up = (ps_up, slots_up)
            dproj = jnp.concatenate([dq, dk, dv], axis=1)
            d_gq[j], d_gk[j] = dgq[0], dgk[0]
            G["od_w_qkv"] = _wgrad(f"wg_odin_{l}", s["h"], "full", dproj, "cols", G["od_w_qkv"], j)
            dh = _bwd_cols(f"od_in_bwd_{l}", dproj, W["od_w_qkv"], j)
        below = (saved[l - 1]["y2"], l - 1, KG2) if l > 0 else None
        dx, st_n1, *dyg_below = _normmod_bwd(f"norm_mix_bwd_{l}", s["x_in"], dh, dxm, gain_mix, l, mod, KSC1, gate=below)
        dmod_rows[l] = jnp.stack([st_n1[0], st_n1[1], st_n2[3], st_n2[0], st_n2[1], d_gate2]).reshape(6 * D_MODEL)
        d_mix[l], d_ffn[l] = st_n1[2], st_n2[2]
        if below:
            dyg, d_gate2 = dyg_below[0], st_n1[3]
    return lacc, dx, G, (dmod_rows, d_mix, d_ffn, d_ret, d_gq, d_gk, d_conv), reduce_up


def kernel(x, c, ada_w, ada_b, norm_mix_g, norm_ffn_g, ev_w_in, ev_conv_w, ev_ret_norm_g, ev_w_out, od_w_qkv, od_q_norm_g, od_k_norm_g, od_w_out, ffn_w_gate, ffn_w_up, ffn_w_down, loss_target, m_ada_w, m_ada_b, m_norm_mix_g, m_norm_ffn_g, m_ev_w_in, m_ev_conv_w, m_ev_ret_norm_g, m_ev_w_out, m_od_w_qkv, m_od_q_norm_g, m_od_k_norm_g, m_od_w_out, m_ffn_w_gate, m_ffn_w_up, m_ffn_w_down, v_ada_w, v_ada_b, v_norm_mix_g, v_norm_ffn_g, v_ev_w_in, v_ev_conv_w, v_ev_ret_norm_g, v_ev_w_out, v_od_w_qkv, v_od_q_norm_g, v_od_k_norm_g, v_od_w_out, v_ffn_w_gate, v_ffn_w_up, v_ffn_w_down):
    xi, yi, ci = _position()
    chip = 2 * xi + yi
    dev = 4 * xi + 2 * yi + ci
    x0 = x[0]
    target = loss_target[0]

    n_small = D_MODEL + 2 * CONV_WIDTH * LANE
    small = jnp.concatenate([c.reshape(1, D_MODEL), ev_conv_w.reshape(1, 2 * CONV_WIDTH * LANE)], axis=1)
    small = jnp.broadcast_to(small, (8, n_small))
    g1 = _gather8("gather_cond", small).reshape(N_DEV, 8, n_small)[:, 0, :]
    c_all = g1[:, :D_MODEL]
    conv_all = g1[0::2, D_MODEL:].reshape(N_CHIP, 2, CONV_WIDTH, LANE)
    conv_full = conv_all.transpose(1, 2, 0, 3).reshape(2, CONV_WIDTH, CONV_DIM)

    n_ada = ada_w.shape[-1]
    ada_b_cols = lax.dynamic_slice_in_dim(ada_b, chip * n_ada, n_ada, axis=1).reshape(DEPTH, 1, n_ada)
    mod_cols = _ada_fwd(c_all, ada_w, ada_b_cols)
    g2 = _gather8("gather_mod", mod_cols.reshape(DEPTH * N_DEV, n_ada)).reshape(N_DEV, DEPTH, N_DEV, n_ada)
    mod_mine = lax.dynamic_index_in_dim(g2[0::2], dev, axis=2, keepdims=False)
    mod = mod_mine.transpose(1, 0, 2).reshape(DEPTH, 6, 1, D_MODEL)

    big_names = ["ev_w_in", "ev_w_out", "od_w_qkv", "od_w_out", "ffn_w_gate", "ffn_w_up", "ffn_w_down"]
    big = dict(ev_w_in=ev_w_in, ev_w_out=ev_w_out, od_w_qkv=od_w_qkv, od_w_out=od_w_out, ffn_w_gate=ffn_w_gate,
               ffn_w_up=ffn_w_up, ffn_w_down=ffn_w_down)
    pos_idx = jnp.stack([chip, ci]).astype(jnp.int32)
    shards = [big[k].astype(BF16) for k in big_names]
    W = {k: _place_own(f"place_{k}", w4, s, pos_idx) for k, w4, s in zip(big_names, _gather_weights(shards, 0), shards)}
    G = {k: lax.empty((N_CHIP,) + big[k].shape, F32) for k in big_names}

    lacc, dx, G, small_grads, (ps_up, slots_up) = _local_step(
        x0, target, mod, W, G, norm_mix_g, norm_ffn_g, conv_full, ev_ret_norm_g, od_q_norm_g, od_k_norm_g,
        fused=dict(names=big_names, shards=shards))
    loss = lax.psum(lacc[0, 0], ("x", "y", "c"))
    grad_x = dx[None]
    dmod_rows, d_mix, d_ffn, d_ret, d_gq, d_gk, d_conv = small_grads

    totals = [_chip_sum(f"chip_sum_up_{k}", r, p, pos_idx, 1, lax.empty(big[k].shape, F32))
              for k, r, p in zip(big_names, slots_up, ps_up)]
    glist = [G[k] for k in big_names]
    recv_lo = _exchange_pair(glist, 0)
    ps_lo = [_pair_sum(f"pair_sum_lo_{k}", g, r, 0) for k, g, r in zip(big_names, glist, recv_lo)]
    slots_lo = _scatter_chips(ps_lo, 0)
    totals = [_chip_sum(f"chip_sum_lo_{k}", r, p, pos_idx, 0, t) for k, r, p, t in zip(big_names, slots_lo, ps_lo, totals)]
    grads = dict(zip(big_names, _share_halves(totals)))

    pieces = [jnp.stack(dmod_rows).reshape(-1), jnp.stack(d_mix).reshape(-1), jnp.stack(d_ffn).reshape(-1),
              jnp.stack(d_ret).reshape(-1), jnp.stack(d_gq).reshape(-1), jnp.stack(d_gk).reshape(-1),
              jnp.stack(d_conv).reshape(-1)]
    sizes = [int(p.shape[0]) for p in pieces]
    n_pack = sum(sizes)
    n_cols = -(-n_pack // (8 * LANE)) * LANE
    packed = jnp.concatenate(pieces + [jnp.zeros((8 * n_cols - n_pack,), F32)]).reshape(8, n_cols)
    g3 = _gather8("gather_small", packed).reshape(N_DEV, 8, n_cols)
    tot = _sum_devices(g3).reshape(-1)
    offs = np.cumsum([0] + sizes)
    part = [tot[offs[i]:offs[i + 1]] for i in range(len(sizes))]
    grads["ada_b"] = part[0].reshape(DEPTH, 6 * D_MODEL)
    grads["norm_mix_g"] = part[1].reshape(DEPTH, D_MODEL)
    grads["norm_ffn_g"] = part[2].reshape(DEPTH, D_MODEL)
    grads["ev_ret_norm_g"] = part[3].reshape(2, RET_DIM)
    grads["od_q_norm_g"] = part[4].reshape(2, HEAD)
    grads["od_k_norm_g"] = part[5].reshape(2, HEAD)
    conv_g = part[6].reshape(2, CONV_WIDTH, CONV_DIM)
    grads["ev_conv_w"] = lax.dynamic_slice_in_dim(conv_g, chip * LANE, LANE, axis=2)
    dmod_all = g3.reshape(N_DEV, -1)[:, :DEPTH * 6 * D_MODEL].reshape(N_DEV, DEPTH, 6 * D_MODEL)
    dmod_cols = lax.dynamic_slice_in_dim(dmod_all, chip * n_ada, n_ada, axis=2).transpose(1, 0, 2)
    grads["ada_w"] = _ada_wgrad(c_all.T, dmod_cols)

    weights = dict(ada_w=ada_w, ada_b=ada_b, norm_mix_g=norm_mix_g, norm_ffn_g=norm_ffn_g, ev_w_in=ev_w_in,
                   ev_conv_w=ev_conv_w, ev_ret_norm_g=ev_ret_norm_g, ev_w_out=ev_w_out, od_w_qkv=od_w_qkv,
                   od_q_norm_g=od_q_norm_g, od_k_norm_g=od_k_norm_g, od_w_out=od_w_out, ffn_w_gate=ffn_w_gate,
                   ffn_w_up=ffn_w_up, ffn_w_down=ffn_w_down)
    m_in = dict(ada_w=m_ada_w, ada_b=m_ada_b, norm_mix_g=m_norm_mix_g, norm_ffn_g=m_norm_ffn_g, ev_w_in=m_ev_w_in,
                ev_conv_w=m_ev_conv_w, ev_ret_norm_g=m_ev_ret_norm_g, ev_w_out=m_ev_w_out, od_w_qkv=m_od_w_qkv,
                od_q_norm_g=m_od_q_norm_g, od_k_norm_g=m_od_k_norm_g, od_w_out=m_od_w_out, ffn_w_gate=m_ffn_w_gate,
                ffn_w_up=m_ffn_w_up, ffn_w_down=m_ffn_w_down)
    v_in = dict(ada_w=v_ada_w, ada_b=v_ada_b, norm_mix_g=v_norm_mix_g, norm_ffn_g=v_norm_ffn_g, ev_w_in=v_ev_w_in,
                ev_conv_w=v_ev_conv_w, ev_ret_norm_g=v_ev_ret_norm_g, ev_w_out=v_ev_w_out, od_w_qkv=v_od_w_qkv,
                od_q_norm_g=v_od_q_norm_g, od_k_norm_g=v_od_k_norm_g, od_w_out=v_od_w_out, ffn_w_gate=v_ffn_w_gate,
                ffn_w_up=v_ffn_w_up, ffn_w_down=v_ffn_w_down)
    order = list(weights)
    deltas, new_m, new_v = {}, {}, {}
    for k in order:
        deltas[k], new_m[k], new_v[k] = _adamw(f"adamw_{k}", weights[k], grads[k], m_in[k], v_in[k])
    return (loss, grad_x, *[grads[k] for k in order], *[deltas[k] for k in order], *[new_m[k] for k in order],
            *[new_v[k] for k in order])
```
